```python
import jax, jax.numpy as jnp
from jax import lax
import numpy as np

D_MODEL = 1024
BATCH = 1
SEQ = 16384
DEPTH = 1

ATTN_HEADS = 8
HEAD_DIM = 64
ATTN_WIDTH = ATTN_HEADS * HEAD_DIM
ROT_DIM = HEAD_DIM // 4
ROPE_THETA = 500000.0
DILATED_PATTERNS = ((128, 1), (512, 4), (2048, 16))
Q_BLOCK = 128
LRU_WIDTH = D_MODEL - ATTN_WIDTH
LRU_BLOCKS = 8
LRU_BLOCK = LRU_WIDTH // LRU_BLOCKS
CONV_WIDTH = 4
LRU_C = 8.0
IN_WIDTH = 3 * ATTN_WIDTH + 2 * LRU_WIDTH
D_FF = 4 * D_MODEL
EPS = 1e-6

kernel_name = "hymba_style_rglru_dilated_attn_block"


def rms_norm(x, g):
    xf = x.astype(jnp.float32)
    y = xf * lax.rsqrt(jnp.mean(xf * xf, axis=-1, keepdims=True) + EPS)
    return (y * g.astype(jnp.float32)).astype(x.dtype)


def partial_rope(x, cos, sin):
    half = ROT_DIM // 2
    xf = x.astype(jnp.float32)
    x1 = xf[..., :half]
    x2 = xf[..., half:ROT_DIM]
    c = cos[None, :, None, :]
    s = sin[None, :, None, :]
    out = jnp.concatenate([x1 * c - x2 * s, x2 * c + x1 * s, xf[..., ROT_DIM:]], axis=-1)
    return out.astype(x.dtype)


def dilated_attention(q, k, v):
    B, S, H, Dh = q.shape
    scale = HEAD_DIM ** -0.5
    n_blocks = S // Q_BLOCK

    def one_block(start):
        qb = lax.dynamic_slice_in_dim(q, start, Q_BLOCK, axis=1)
        t = start + jnp.arange(Q_BLOCK)
        outs, lses = [], []
        for w, d in DILATED_PATTERNS:
            offs = jnp.arange(w // d + 1) * d
            idx = t[:, None] - offs[None, :]
            valid = idx >= 0
            idx = jnp.maximum(idx, 0)
            kg = jnp.take(k, idx, axis=1)
            vg = jnp.take(v, idx, axis=1)
            s = jnp.einsum('bqhd,bqnhd->bhqn', qb, kg,
                           preferred_element_type=jnp.float32) * scale
            s = jnp.where(valid[None, None], s, -jnp.inf)
            lse = jax.nn.logsumexp(s, axis=-1)
            p = jnp.exp(s - lse[..., None])
            o = jnp.einsum('bhqn,bqnhd->bqhd', p, vg.astype(jnp.float32))
            outs.append(o)
            lses.append(lse)
        alpha = jax.nn.softmax(jnp.stack(lses, axis=0), axis=0)
        alpha = jnp.transpose(alpha, (0, 1, 3, 2))[..., None]
        out = jnp.sum(alpha * jnp.stack(outs, axis=0), axis=0)
        return out.astype(q.dtype)

    starts = jnp.arange(n_blocks) * Q_BLOCK
    blocks = lax.map(one_block, starts)
    return jnp.transpose(blocks, (1, 0, 2, 3, 4)).reshape(B, S, H, Dh)


def _lin_comb(left, right):
    a1, b1 = left
    a2, b2 = right
    return a1 * a2, a2 * b1 + b2


def rg_lru_branch(xb, gate_in, conv_w, conv_b, w_r, b_r, w_i, b_i, lam):
    B, S, _ = xb.shape
    xp = jnp.pad(xb, ((0, 0), (CONV_WIDTH - 1, 0), (0, 0)))
    xc = conv_b + sum(xp[:, j:j + S] * conv_w[j] for j in range(CONV_WIDTH))
    xf = xc.astype(jnp.float32)
    xblk = xf.reshape(B, S, LRU_BLOCKS, LRU_BLOCK)
    r = jax.nn.sigmoid(jnp.einsum('bsnc,ncd->bsnd', xblk, w_r.astype(jnp.float32))
                       + b_r.astype(jnp.float32)).reshape(B, S, LRU_WIDTH)
    i = jax.nn.sigmoid(jnp.einsum('bsnc,ncd->bsnd', xblk, w_i.astype(jnp.float32))
                       + b_i.astype(jnp.float32)).reshape(B, S, LRU_WIDTH)
    log_a = -LRU_C * r * jax.nn.softplus(-lam.astype(jnp.float32))
    a = jnp.exp(log_a)
    bx = jnp.sqrt(-jnp.expm1(2.0 * log_a)) * (i * xf)
    _, h = lax.associative_scan(_lin_comb, (a, bx), axis=1)
    y = jax.nn.gelu(gate_in.astype(jnp.float32)) * h
    return y.astype(xb.dtype)


def setup_inputs(seed: int = 0) -> dict:
    key = jax.random.key(seed)
    ks = jax.random.split(key, 20)
    f32 = jnp.float32
    nrm = lambda k, shape, scale: jax.random.normal(k, shape, f32) * scale
    gain = lambda k, shape: 1.0 + 0.02 * jax.random.normal(k, shape, f32)
    x = jax.random.normal(ks[0], (BATCH, SEQ, D_MODEL), f32)
    u = jax.random.uniform(ks[9], (DEPTH, LRU_WIDTH), f32, 0.9, 0.999)
    a0 = u ** (1.0 / LRU_C)
    lru_lambda = jnp.log(a0) - jnp.log1p(-a0)
    return {
        "x": x,
        "norm1_g": gain(ks[1], (DEPTH, D_MODEL)),
        "w_in": nrm(ks[2], (DEPTH, D_MODEL, IN_WIDTH), D_MODEL ** -0.5),
        "conv_w": nrm(ks[3], (DEPTH, CONV_WIDTH, LRU_WIDTH), CONV_WIDTH ** -0.5),
        "conv_b": nrm(ks[4], (DEPTH, LRU_WIDTH), 0.02),
        "w_rgate": nrm(ks[5], (DEPTH, LRU_BLOCKS, LRU_BLOCK, LRU_BLOCK), LRU_BLOCK ** -0.5),
        "b_rgate": nrm(ks[6], (DEPTH, LRU_BLOCKS, LRU_BLOCK), 0.02),
        "w_igate": nrm(ks[7], (DEPTH, LRU_BLOCKS, LRU_BLOCK, LRU_BLOCK), LRU_BLOCK ** -0.5),
        "b_igate": nrm(ks[8], (DEPTH, LRU_BLOCKS, LRU_BLOCK), 0.02),
        "lru_lambda": lru_lambda,
        "attn_out_g": gain(ks[10], (DEPTH, ATTN_WIDTH)),
        "lru_out_g": gain(ks[11], (DEPTH, LRU_WIDTH)),
        "w_out": nrm(ks[12], (DEPTH, D_MODEL, D_MODEL), D_MODEL ** -0.5),
        "norm2_g": gain(ks[13], (DEPTH, D_MODEL)),
        "w_mlp_up": nrm(ks[14], (DEPTH, D_MODEL, D_FF), D_MODEL ** -0.5),
        "w_mlp_down": nrm(ks[15], (DEPTH, D_FF, D_MODEL), D_FF ** -0.5),
        "final_g": gain(ks[16], (D_MODEL,)),
    }


def reference(x, norm1_g, w_in, conv_w, conv_b, w_rgate, b_rgate, w_igate, b_igate,
              lru_lambda, attn_out_g, lru_out_g, w_out, norm2_g, w_mlp_up, w_mlp_down,
              final_g):
    B, S, _ = x.shape
    pos = jnp.arange(S, dtype=jnp.float32)
    inv_freq = ROPE_THETA ** (-jnp.arange(0, ROT_DIM, 2, dtype=jnp.float32) / ROT_DIM)
    ang = pos[:, None] * inv_freq[None, :]
    cos, sin = jnp.cos(ang), jnp.sin(ang)
    splits = [ATTN_WIDTH, 2 * ATTN_WIDTH, 3 * ATTN_WIDTH, 3 * ATTN_WIDTH + LRU_WIDTH]
    h = x
    for l in range(DEPTH):
        u = rms_norm(h, norm1_g[l])
        z = u @ w_in[l]
        q, k, v, xl, gl = jnp.split(z, splits, axis=-1)
        q = partial_rope(q.reshape(B, S, ATTN_HEADS, HEAD_DIM), cos, sin)
        k = partial_rope(k.reshape(B, S, ATTN_HEADS, HEAD_DIM), cos, sin)
        v = v.reshape(B, S, ATTN_HEADS, HEAD_DIM)
        attn = dilated_attention(q, k, v).reshape(B, S, ATTN_WIDTH)
        lru = rg_lru_branch(xl, gl, conv_w[l], conv_b[l], w_rgate[l], b_rgate[l],
                            w_igate[l], b_igate[l], lru_lambda[l])
        mixed = jnp.concatenate([rms_norm(attn, attn_out_g[l]),
                                 rms_norm(lru, lru_out_g[l])], axis=-1)
        h = h + mixed @ w_out[l]
        u = rms_norm(h, norm2_g[l])
        f = jnp.square(jax.nn.relu(u @ w_mlp_up[l]))
        h = h + f @ w_mlp_down[l]
    return rms_norm(h, final_g)
```

```python
import functools

import numpy as np
import jax
import jax.numpy as jnp
from jax import lax
from jax.experimental import pallas as pl
from jax.experimental.pallas import tpu as pltpu

F32 = jnp.float32
BF16 = jnp.bfloat16

D_MODEL = 1024
SEQ = 16384
ATTN_HEADS = 8
HEAD_DIM = 64
ATTN_WIDTH = ATTN_HEADS * HEAD_DIM
ROT_DIM = HEAD_DIM // 4
ROPE_THETA = 500000.0
LRU_WIDTH = D_MODEL - ATTN_WIDTH
LRU_BLOCKS = 8
CONV_WIDTH = 4
LRU_C = 8.0
IN_WIDTH = 3 * ATTN_WIDTH + 2 * LRU_WIDTH
D_FF = 4 * D_MODEL
EPS = 1e-6
WINDOW_STEPS = 128

NS = 16
RA = SEQ // NS
LANES = 128
NSLAB = ATTN_WIDTH // LANES
NEG = -1e30

VMEM_LIMIT = 56 * 1024 * 1024


def _rms(x, g):
    return x * lax.rsqrt(jnp.mean(x * x, axis=-1, keepdims=True) + EPS) * g


TM1 = 512


def _inproj_body(x_ref, g_ref, w_ref, cos_ref, sin_ref,
                 q_ref, k_ref, v_ref, xl_ref, gl_ref):
    u = _rms(x_ref[...], g_ref[...])
    z = jnp.dot(u.astype(BF16), w_ref[...], preferred_element_type=F32)
    c = jnp.concatenate([cos_ref[...]] * NSLAB, axis=1)
    s = jnp.concatenate([sin_ref[...]] * NSLAB, axis=1)
    lane = lax.broadcasted_iota(jnp.int32, (1, ATTN_WIDTH), 1) % HEAD_DIM
    first_half = lane < ROT_DIM // 2

    def rope(t):
        up = pltpu.roll(t, ATTN_WIDTH - ROT_DIM // 2, axis=1)
        dn = pltpu.roll(t, ROT_DIM // 2, axis=1)
        return t * c + jnp.where(first_half, up, dn) * s

    q = rope(z[:, :ATTN_WIDTH]) * (HEAD_DIM ** -0.5)
    k = rope(z[:, ATTN_WIDTH:2 * ATTN_WIDTH])
    v = z[:, 2 * ATTN_WIDTH:3 * ATTN_WIDTH]
    for j in range(NSLAB):
        sl = slice(j * LANES, (j + 1) * LANES)
        q_ref[j] = q[:, sl].astype(BF16)
        k_ref[j] = k[:, sl].astype(BF16)
        v_ref[j] = v[:, sl].astype(BF16)
    xl_ref[...] = z[:, 3 * ATTN_WIDTH:3 * ATTN_WIDTH + LRU_WIDTH]
    gl_ref[...] = z[:, 3 * ATTN_WIDTH + LRU_WIDTH:]


def _inproj(x3, g, w, cos_t, sin_t):
    slab_spec = pl.BlockSpec((NSLAB, None, TM1, LANES), lambda r, i: (0, r, i, 0))
    row_spec = pl.BlockSpec((None, TM1, LRU_WIDTH), lambda r, i: (r, i, 0))
    tab_spec = pl.BlockSpec((None, TM1, LANES), lambda r, i: (r, i, 0))
    slab_shape = jax.ShapeDtypeStruct((NSLAB, NS, RA, LANES), BF16)
    row_shape = jax.ShapeDtypeStruct((NS, RA, LRU_WIDTH), F32)
    return pl.pallas_call(
        _inproj_body,
        grid=(NS, RA // TM1),
        in_specs=[
            pl.BlockSpec((None, TM1, D_MODEL), lambda r, i: (r, i, 0)),
            pl.BlockSpec((1, D_MODEL), lambda r, i: (0, 0)),
            pl.BlockSpec((D_MODEL, IN_WIDTH), lambda r, i: (0, 0)),
            tab_spec, tab_spec,
        ],
        out_specs=[slab_spec, slab_spec, slab_spec, row_spec, row_spec],
        out_shape=[slab_shape, slab_shape, slab_shape, row_shape, row_shape],
        compiler_params=pltpu.CompilerParams(
            dimension_semantics=("arbitrary", "arbitrary"), vmem_limit_bytes=VMEM_LIMIT),
        name="inproj",
    )(x3, g, w, cos_t, sin_t)


TA = 128
Q4, K4 = 32, 64
Q1, K1 = 16, 32


def _attn_masks():
    def both(diff, from_prev):
        ok = (diff >= 0) & (diff <= WINDOW_STEPS)
        normal = np.where(ok, 0.0, NEG).astype(np.float32)
        first = np.where(ok & ~from_prev, 0.0, NEG).astype(np.float32)
        return np.stack([first, normal])

    iq = np.arange(TA)[:, None]
    ck = np.arange(2 * TA)[None, :]
    m16 = both(iq - ck + TA, np.broadcast_to(ck < TA, (TA, 2 * TA)))
    cq, i4 = np.divmod(np.arange(4 * Q4), Q4)
    ckk, j4 = np.divmod(np.arange(4 * K4), K4)
    d4 = 4 * (i4[:, None] - j4[None, :] + Q4) + (cq[:, None] - ckk[None, :])
    m4 = both(d4, np.broadcast_to(j4[None, :] < K4 - Q4, d4.shape))
    rq, i1 = np.divmod(np.arange(NS * Q1), Q1)
    rk, j1 = np.divmod(np.arange(NS * K1), K1)
    d1 = NS * (i1[:, None] - j1[None, :] + Q1) + (rq[:, None] - rk[None, :])
    m1 = both(d1, np.broadcast_to(j1[None, :] < K1 - Q1, d1.shape))
    return m16, m4, m1


def _attn_body(q_ref, kc_ref, kp_ref, vc_ref, vp_ref, m16_ref, m4e_ref, m4_ref, m1e_ref, m1_ref,
               o_ref, qm_scr, vcm_scr, vpm_scr, acc_scr, lse_scr):
    lane = lax.broadcasted_iota(jnp.int32, (1, LANES), 1)
    head0 = lane < HEAD_DIM
    zero = jnp.zeros((), BF16)

    q = q_ref[...]
    qm_scr[0] = jnp.where(head0, q, zero)
    qm_scr[1] = jnp.where(head0, zero, q)
    vc = vc_ref[...]
    vcm_scr[0] = jnp.where(head0, vc, zero)
    vcm_scr[1] = jnp.where(head0, zero, vc)
    vp = vp_ref[...]
    vpm_scr[0] = jnp.where(head0, vp, zero)
    vpm_scr[1] = jnp.where(head0, zero, vp)

    def block(qs, kb, vs, mask):
        o = None
        rl, lse = [], []
        for h in range(2):
            s = lax.dot_general(qs[h], kb, (((1,), (1,)), ((), ())),
                                preferred_element_type=F32) + mask
            m = jnp.max(s, axis=-1, keepdims=True)
            p = jnp.exp(s - m)
            l = jnp.sum(p, axis=-1, keepdims=True)
            pv = jnp.dot(p.astype(BF16), vs[h], preferred_element_type=F32)
            o = pv if o is None else o + pv
            rl.append(1.0 / l)
            lse.append(m + jnp.log(l))
        return o * jnp.where(head0, rl[0], rl[1]), jnp.where(head0, lse[0], lse[1])

    def d16(r, carry):
        kb = jnp.concatenate([kp_ref[r], kc_ref[r]], axis=0)
        qs = [qm_scr[h, r] for h in range(2)]
        vs = [jnp.concatenate([vpm_scr[h, r], vcm_scr[h, r]], axis=0) for h in range(2)]
        o, lse = block(qs, kb, vs, m16_ref[...])
        acc_scr[0, r] = o
        lse_scr[0, r] = lse
        return carry

    lax.fori_loop(0, NS, d16, 0)

    def d4(r4, carry):
        for b in range(TA // Q4):
            q0 = b * Q4

            def rows(cur, prev, c, h=None):
                st = r4 + 4 * c
                ix = (st,) if h is None else (h, st)
                if b == 0:
                    return jnp.concatenate([prev[ix + (slice(TA - (K4 - Q4), TA),)],
                                            cur[ix + (slice(0, Q4),)]], axis=0)
                return cur[ix + (slice(q0 - (K4 - Q4), q0 + Q4),)]

            kb = jnp.concatenate([rows(kc_ref, kp_ref, c) for c in range(4)], axis=0)
            vs = [jnp.concatenate([rows(vcm_scr, vpm_scr, c, h) for c in range(4)], axis=0)
                  for h in range(2)]
            qs = [jnp.concatenate([qm_scr[h, r4 + 4 * c, q0:q0 + Q4] for c in range(4)], axis=0)
                  for h in range(2)]
            o, lse = block(qs, kb, vs, m4e_ref[...] if b == 0 else m4_ref[...])
            for c in range(4):
                acc_scr[1, r4 + 4 * c, q0:q0 + Q4] = o[c * Q4:(c + 1) * Q4]
                lse_scr[1, r4 + 4 * c, q0:q0 + Q4] = lse[c * Q4:(c + 1) * Q4]
        return carry

    lax.fori_loop(0, 4, d4, 0)

    def d1_block(q0, first):
        def rows(cur, prev, r, h=None):
            ix = (r,) if h is None else (h, r)
            if first:
                return jnp.concatenate([prev[ix + (slice(TA - (K1 - Q1), TA),)],
                                        cur[ix + (slice(0, Q1),)]], axis=0)
            return cur[ix + (pl.ds(q0 - (K1 - Q1), K1),)]

        kb = jnp.concatenate([rows(kc_ref, kp_ref, r) for r in range(NS)], axis=0)
        vs = [jnp.concatenate([rows(vcm_scr, vpm_scr, r, h) for r in range(NS)], axis=0)
              for h in range(2)]
        qs = [jnp.concatenate([qm_scr[h, r, pl.ds(q0, Q1)] for r in range(NS)], axis=0)
              for h in range(2)]
        o, lse = block(qs, kb, vs, m1e_ref[...] if first else m1_ref[...])
        for r in range(NS):
            acc_scr[2, r, pl.ds(q0, Q1)] = o[r * Q1:(r + 1) * Q1]
            lse_scr[2, r, pl.ds(q0, Q1)] = lse[r * Q1:(r + 1) * Q1]

    d1_block(0, True)

    def d1(b, carry):
        d1_block(pl.multiple_of(b * Q1, Q1), False)
        return carry

    lax.fori_loop(1, TA // Q1, d1, 0)

    def comb(r, carry):
        l0, l1, l2 = lse_scr[0, r], lse_scr[1, r], lse_scr[2, r]
        m = jnp.maximum(jnp.maximum(l0, l1), l2)
        w0, w1, w2 = jnp.exp(l0 - m), jnp.exp(l1 - m), jnp.exp(l2 - m)
        num = w0 * acc_scr[0, r] + w1 * acc_scr[1, r] + w2 * acc_scr[2, r]
        o_ref[r] = num / (w0 + w1 + w2)
        return carry

    lax.fori_loop(0, NS, comb, 0)


def _attention(q, k, v):
    m16, m4, m1 = _attn_masks()
    cur = pl.BlockSpec((None, NS, TA, LANES), lambda j, i: (j, 0, i, 0))
    prev = pl.BlockSpec((None, NS, TA, LANES), lambda j, i: (j, 0, jnp.maximum(i - 1, 0), 0))

    def edge(m):
        return pl.BlockSpec((None,) + m.shape[1:], lambda j, i: (jnp.minimum(i, 1), 0, 0))

    def const(m):
        return pl.BlockSpec(m.shape[1:], lambda j, i: (0, 0))

    return pl.pallas_call(
        _attn_body,
        grid=(NSLAB, RA // TA),
        in_specs=[cur, cur, prev, cur, prev,
                  edge(m16), edge(m4), const(m4), edge(m1), const(m1)],
        out_specs=pl.BlockSpec((None, NS, TA, LANES), lambda j, i: (j, 0, i, 0)),
        out_shape=jax.ShapeDtypeStruct((NSLAB, NS, RA, LANES), F32),
        scratch_shapes=[
            pltpu.VMEM((2, NS, TA, LANES), BF16),
            pltpu.VMEM((2, NS, TA, LANES), BF16),
            pltpu.VMEM((2, NS, TA, LANES), BF16),
            pltpu.VMEM((3, NS, TA, LANES), F32),
            pltpu.VMEM((3, NS, TA, LANES), F32),
        ],
        compiler_params=pltpu.CompilerParams(
            dimension_semantics=("arbitrary", "arbitrary"), vmem_limit_bytes=VMEM_LIMIT),
        name="dilated_attn",
    )(q, k, k, v, v, m16, m4, m4[1], m1, m1[1])


TL = 64


def _lru_body(xl_ref, gl_ref, cw_ref, cb_ref, wg_ref, bg_ref, lam_ref, y_ref,
              tail_scr, carry_scr, xc_scr, g_scr, pl_scr, hl_scr):
    step = pl.program_id(0)

    @pl.when(step == 0)
    def _():
        tail_scr[...] = jnp.zeros_like(tail_scr)
        carry_scr[...] = jnp.zeros_like(carry_scr)

    row = lax.broadcasted_iota(jnp.int32, (TL, LRU_WIDTH), 0)
    cw = cw_ref[...]
    cb = cb_ref[...]

    def tap(r, k):
        st = r - k
        if st >= 0:
            return xl_ref[st]
        st += NS
        prev_last = tail_scr[st - (NS - CONV_WIDTH + 1), 7:8, :]
        return jnp.where(row == 0, prev_last, pltpu.roll(xl_ref[st], 1, axis=0))

    for r in range(NS):
        xc = cb + sum(cw[CONV_WIDTH - 1 - k:CONV_WIDTH - k] * tap(r, k) for k in range(CONV_WIDTH))
        xc_scr[r] = xc
    for n in range(CONV_WIDTH - 1):
        tail_scr[n] = xl_ref[NS - CONV_WIDTH + 1 + n, TL - 8:TL, :]

    xc_all = xc_scr[...].reshape(NS * TL, LRU_WIDTH)
    g_scr[...] = jnp.dot(xc_all.astype(BF16), wg_ref[...], preferred_element_type=F32) + bg_ref[...]

    neg_lam = -lam_ref[...]
    softplus = jnp.maximum(neg_lam, 0.0) + jnp.log1p(jnp.exp(-jnp.abs(neg_lam)))
    p_run = None
    h_run = None
    for r in range(NS):
        g = g_scr[r * TL:(r + 1) * TL]
        xc = xc_scr[r]
        rg = jax.nn.sigmoid(g[:, :LRU_WIDTH])
        ig = jax.nn.sigmoid(g[:, LRU_WIDTH:])
        log_a = -LRU_C * rg * softplus
        a = jnp.exp(log_a)
        th = jnp.tanh(log_a)
        bx = jnp.sqrt(-2.0 * th / (1.0 - th)) * (ig * xc)
        if r == 0:
            p_run, h_run = a, bx
        else:
            h_run = a * h_run + bx
            p_run = a * p_run
        pl_scr[r] = p_run
        hl_scr[r] = h_run

    pa, hb = p_run, h_run
    sft = 1
    while sft < TL:
        keep = row >= sft
        pa_s = jnp.where(keep, pltpu.roll(pa, sft, axis=0), 1.0)
        hb_s = jnp.where(keep, pltpu.roll(hb, sft, axis=0), 0.0)
        hb = pa * hb_s + hb
        pa = pa * pa_s
        sft *= 2
    carry = carry_scr[0:1, :]
    e = pa * carry + hb
    e_prev = jnp.where(row == 0, carry, pltpu.roll(e, 1, axis=0))
    carry_scr[...] = jnp.broadcast_to(e[TL - 1:TL, :], carry_scr.shape)

    for r in range(NS):
        h = pl_scr[r] * e_prev + hl_scr[r]
        y_ref[r] = jax.nn.gelu(gl_ref[r], approximate=True) * h


def _lru(xl, gl, conv_w, conv_b, w_gate, b_gate, lam):
    blk = pl.BlockSpec((NS, TL, LRU_WIDTH), lambda i: (0, i, 0))

    def const(a):
        return pl.BlockSpec(a.shape, lambda i: (0,) * a.ndim)

    return pl.pallas_call(
        _lru_body,
        grid=(RA // TL,),
        in_specs=[blk, blk, const(conv_w), const(conv_b), const(w_gate), const(b_gate), const(lam)],
        out_specs=blk,
        out_shape=jax.ShapeDtypeStruct((NS, RA, LRU_WIDTH), F32),
        scratch_shapes=[
            pltpu.VMEM((CONV_WIDTH - 1, 8, LRU_WIDTH), F32),
            pltpu.VMEM((8, LRU_WIDTH), F32),
            pltpu.VMEM((NS, TL, LRU_WIDTH), F32),
            pltpu.VMEM((NS * TL, 2 * LRU_WIDTH), F32),
            pltpu.VMEM((NS, TL, LRU_WIDTH), F32),
            pltpu.VMEM((NS, TL, LRU_WIDTH), F32),
        ],
        compiler_params=pltpu.CompilerParams(
            dimension_semantics=("arbitrary",), vmem_limit_bytes=VMEM_LIMIT),
        name="rg_lru",
    )(xl, gl, conv_w, conv_b, w_gate, b_gate, lam)


TM4 = 512
FF_CHUNK = 1024


def _out_body(x_ref, at_ref, lr_ref, ga_ref, gl_ref, wo_ref, g2_ref, wu_ref, wd_ref, gf_ref, o_ref):
    attn = jnp.concatenate([at_ref[j] for j in range(NSLAB)], axis=1)
    mixed = jnp.concatenate([_rms(attn, ga_ref[...]), _rms(lr_ref[...], gl_ref[...])], axis=1)
    h = x_ref[...] + jnp.dot(mixed.astype(BF16), wo_ref[...], preferred_element_type=F32)
    u = _rms(h, g2_ref[...]).astype(BF16)
    acc = h
    for c in range(D_FF // FF_CHUNK):
        sl = slice(c * FF_CHUNK, (c + 1) * FF_CHUNK)
        f = jnp.dot(u, wu_ref[:, sl], preferred_element_type=F32)
        f = jnp.square(jnp.maximum(f, 0.0)).astype(BF16)
        acc = acc + jnp.dot(f, wd_ref[sl, :], preferred_element_type=F32)
    o_ref[...] = _rms(acc, gf_ref[...])


def _out_mlp(x3, attn, lru, ga, gl, wo, g2, wu, wd, gf):
    def const(a):
        return pl.BlockSpec(a.shape, lambda r, i: (0,) * a.ndim, pipeline_mode=pl.Buffered(1))

    xspec = pl.BlockSpec((None, TM4, D_MODEL), lambda r, i: (r, i, 0))
    return pl.pallas_call(
        _out_body,
        grid=(NS, RA // TM4),
        in_specs=[
            xspec,
            pl.BlockSpec((NSLAB, None, TM4, LANES), lambda r, i: (0, r, i, 0)),
            pl.BlockSpec((None, TM4, LRU_WIDTH), lambda r, i: (r, i, 0)),
            const(ga), const(gl), const(wo), const(g2), const(wu), const(wd), const(gf),
        ],
        out_specs=xspec,
        out_shape=jax.ShapeDtypeStruct((NS, RA, D_MODEL), F32),
        compiler_params=pltpu.CompilerParams(
            dimension_semantics=("arbitrary", "arbitrary"), vmem_limit_bytes=VMEM_LIMIT),
        name="out_mlp",
    )(x3, attn, lru, ga, gl, wo, g2, wu, wd, gf)


def _rope_tables():
    pos = jnp.arange(SEQ, dtype=F32)
    inv_freq = ROPE_THETA ** (-jnp.arange(0, ROT_DIM, 2, dtype=F32) / ROT_DIM)
    ang = pos[:, None] * inv_freq[None, :]
    cos, sin = jnp.cos(ang), jnp.sin(ang)
    half = ROT_DIM // 2
    pad = HEAD_DIM - ROT_DIM
    c_head = jnp.concatenate([cos, cos, jnp.ones((SEQ, pad), F32)], axis=1)
    s_head = jnp.concatenate([-sin, sin, jnp.zeros((SEQ, pad), F32)], axis=1)
    reps = LANES // HEAD_DIM

    def lay(t):
        t = jnp.tile(t, (1, reps))
        return t.reshape(RA, NS, LANES).transpose(1, 0, 2)
    del half
    return lay(c_head), lay(s_head)


def _block_diag(w):
    n, b, _ = w.shape
    eye = jnp.eye(n, dtype=w.dtype)
    return (eye[:, None, :, None] * w[:, :, None, :]).reshape(n * b, n * b)


def kernel(x, norm1_g, w_in, conv_w, conv_b, w_rgate, b_rgate, w_igate, b_igate, lru_lambda,
           attn_out_g, lru_out_g, w_out, norm2_g, w_mlp_up, w_mlp_down, final_g):
    assert x.shape == (1, SEQ, D_MODEL) and w_in.shape[0] == 1
    x3 = x.reshape(RA, NS, D_MODEL).transpose(1, 0, 2)
    cos_t, sin_t = _rope_tables()
    q, k, v, xl, gl = _inproj(x3, norm1_g.reshape(1, D_MODEL), w_in[0].astype(BF16), cos_t, sin_t)
    attn = _attention(q, k, v)
    w_gate = jnp.concatenate([_block_diag(w_rgate[0]), _block_diag(w_igate[0])], axis=1).astype(BF16)
    b_gate = jnp.concatenate([b_rgate[0].reshape(1, -1), b_igate[0].reshape(1, -1)], axis=1)
    lru = _lru(xl, gl, conv_w[0], conv_b[0].reshape(1, -1), w_gate, b_gate, lru_lambda[0].reshape(1, -1))
    out = _out_mlp(x3, attn, lru, attn_out_g[0].reshape(1, -1), lru_out_g[0].reshape(1, -1),
                   w_out[0].astype(BF16), norm2_g[0].reshape(1, -1),
                   w_mlp_up[0].astype(BF16), w_mlp_down[0].astype(BF16), final_g.reshape(1, -1))
    return out.transpose(1, 0, 2).reshape(1, SEQ, D_MODEL)
```

```python
import functools

import numpy as np
import jax
import jax.numpy as jnp
from jax import lax
from jax.experimental import pallas as pl
from jax.experimental.pallas import tpu as pltpu

F32 = jnp.float32
BF16 = jnp.bfloat16

D_MODEL = 1024
SEQ = 16384
ATTN_HEADS = 8
HEAD_DIM = 64
ATTN_WIDTH = ATTN_HEADS * HEAD_DIM
ROT_DIM = HEAD_DIM // 4
ROPE_THETA = 500000.0
LRU_WIDTH = D_MODEL - ATTN_WIDTH
LRU_BLOCKS = 8
CONV_WIDTH = 4
LRU_C = 8.0
IN_WIDTH = 3 * ATTN_WIDTH + 2 * LRU_WIDTH
D_FF = 4 * D_MODEL
EPS = 1e-6
WINDOW_STEPS = 128

NS = 16
RA = SEQ // NS
LANES = 128
NSLAB = ATTN_WIDTH // LANES
NEG = -1e30

VMEM_LIMIT = 56 * 1024 * 1024


def _rms(x, g):
    return x * lax.rsqrt(jnp.mean(x * x, axis=-1, keepdims=True) + EPS) * g


TM1 = 512


def _inproj_body(x_ref, g_ref, w_ref, cos_ref, sin_ref,
                 q_ref, k_ref, v_ref, xl_ref, gl_ref):
    u = _rms(x_ref[...], g_ref[...])
    z = jnp.dot(u.astype(BF16), w_ref[...], preferred_element_type=F32)
    c = jnp.concatenate([cos_ref[...]] * NSLAB, axis=1)
    s = jnp.concatenate([sin_ref[...]] * NSLAB, axis=1)
    lane = lax.broadcasted_iota(jnp.int32, (1, ATTN_WIDTH), 1) % HEAD_DIM
    first_half = lane < ROT_DIM // 2

    def rope(t):
        up = pltpu.roll(t, ATTN_WIDTH - ROT_DIM // 2, axis=1)
        dn = pltpu.roll(t, ROT_DIM // 2, axis=1)
        return t * c + jnp.where(first_half, up, dn) * s

    q = rope(z[:, :ATTN_WIDTH]) * (HEAD_DIM ** -0.5)
    k = rope(z[:, ATTN_WIDTH:2 * ATTN_WIDTH])
    v = z[:, 2 * ATTN_WIDTH:3 * ATTN_WIDTH]
    for j in range(NSLAB):
        sl = slice(j * LANES, (j + 1) * LANES)
        q_ref[j] = q[:, sl].astype(BF16)
        k_ref[j] = k[:, sl].astype(BF16)
        v_ref[j] = v[:, sl].astype(BF16)
    xl_ref[...] = z[:, 3 * ATTN_WIDTH:3 * ATTN_WIDTH + LRU_WIDTH]
    gl_ref[...] = z[:, 3 * ATTN_WIDTH + LRU_WIDTH:]


def _inproj(x3, g, w, cos_t, sin_t):
    slab_spec = pl.BlockSpec((NSLAB, None, TM1, LANES), lambda r, i: (0, r, i, 0))
    row_spec = pl.BlockSpec((None, TM1, LRU_WIDTH), lambda r, i: (r, i, 0))
    tab_spec = pl.BlockSpec((None, TM1, LANES), lambda r, i: (r, i, 0))
    slab_shape = jax.ShapeDtypeStruct((NSLAB, NS, RA, LANES), BF16)
    row_shape = jax.ShapeDtypeStruct((NS, RA, LRU_WIDTH), F32)
    return pl.pallas_call(
        _inproj_body,
        grid=(NS, RA // TM1),
        in_specs=[
            pl.BlockSpec((None, TM1, D_MODEL), lambda r, i: (r, i, 0)),
            pl.BlockSpec((1, D_MODEL), lambda r, i: (0, 0)),
            pl.BlockSpec((D_MODEL, IN_WIDTH), lambda r, i: (0, 0)),
            tab_spec, tab_spec,
        ],
        out_specs=[slab_spec, slab_spec, slab_spec, row_spec, row_spec],
        out_shape=[slab_shape, slab_shape, slab_shape, row_shape, row_shape],
        compiler_params=pltpu.CompilerParams(
            dimension_semantics=("arbitrary", "arbitrary"), vmem_limit_bytes=VMEM_LIMIT),
        name="inproj",
    )(x3, g, w, cos_t, sin_t)


TA = 128
Q4, K4 = 32, 64
Q1, K1 = 16, 32
GROUP = 4


@functools.lru_cache(maxsize=None)
def _attn_masks():
    def both(diff, from_prev):
        ok = (diff >= 0) & (diff <= WINDOW_STEPS)
        normal = np.where(ok, 0.0, NEG).astype(np.float32)
        first = np.where(ok & ~from_prev, 0.0, NEG).astype(np.float32)
        return np.stack([first, normal])

    iq = np.arange(TA)[:, None]
    ck = np.arange(2 * TA)[None, :]
    m16 = both(iq - ck + TA, np.broadcast_to(ck < TA, (TA, 2 * TA)))
    cq, i4 = np.divmod(np.arange(4 * Q4), Q4)
    ckk, j4 = np.divmod(np.arange(4 * K4), K4)
    d4 = 4 * (i4[:, None] - j4[None, :] + Q4) + (cq[:, None] - ckk[None, :])
    m4 = both(d4, np.broadcast_to(j4[None, :] < K4 - Q4, d4.shape))
    rq, i1 = np.divmod(np.arange(NS * Q1), Q1)
    rk, j1 = np.divmod(np.arange(NS * K1), K1)
    d1 = NS * (i1[:, None] - j1[None, :] + Q1) + (rq[:, None] - rk[None, :])
    m1 = both(d1, np.broadcast_to(j1[None, :] < K1 - Q1, d1.shape))
    return m16, m4, m1


def _attn_body(q_ref, kc_ref, kp_ref, vc_ref, vp_ref, m16_ref, m4e_ref, m4_ref, m1e_ref, m1_ref,
               o_ref, qm_scr, vcm_scr, vpm_scr, acc_scr, max_scr, sum_scr):
    lane = lax.broadcasted_iota(jnp.int32, (1, LANES), 1)
    head0 = lane < HEAD_DIM
    zero = jnp.zeros((), BF16)

    q = q_ref[...]
    qm_scr[0] = jnp.where(head0, q, zero)
    qm_scr[1] = jnp.where(head0, zero, q)
    vc = vc_ref[...]
    vcm_scr[0] = jnp.where(head0, vc, zero)
    vcm_scr[1] = jnp.where(head0, zero, vc)
    vp = vp_ref[...]
    vpm_scr[0] = jnp.where(head0, vp, zero)
    vpm_scr[1] = jnp.where(head0, zero, vp)

    items = []

    def store_rows(p, pieces):
        def store(o, mx, sm):
            off = 0
            for st, r0, n in pieces:
                acc_scr[p, st, r0:r0 + n] = o[off:off + n]
                max_scr[p, st, r0:r0 + n] = mx[off:off + n]
                sum_scr[p, st, r0:r0 + n] = sm[off:off + n]
                off += n
        return store

    def window(cur, prev, ix, r0, n):
        if r0 < 0:
            return [prev[ix + (slice(TA + r0, TA),)], cur[ix + (slice(0, r0 + n),)]]
        return [cur[ix + (slice(r0, r0 + n),)]]

    def add_item(p, streams, q0, nq, nk, mask_ref):
        k0 = q0 + nq - nk

        def q(h):
            return jnp.concatenate([qm_scr[h, st, q0:q0 + nq] for st in streams], axis=0)

        def k():
            return jnp.concatenate(
                [x for st in streams for x in window(kc_ref, kp_ref, (st,), k0, nk)], axis=0)

        def v(h):
            return jnp.concatenate(
                [x for st in streams for x in window(vcm_scr, vpm_scr, (h, st), k0, nk)], axis=0)

        items.append((q, k, v, mask_ref, store_rows(p, [(st, q0, nq) for st in streams])))

    for r in range(NS):
        add_item(0, [r], 0, TA, 2 * TA, m16_ref)
    for r4 in range(4):
        for b in range(TA // Q4):
            add_item(1, [r4 + 4 * c for c in range(4)], b * Q4, Q4, K4, m4e_ref if b == 0 else m4_ref)
    for b in range(TA // Q1):
        add_item(2, list(range(NS)), b * Q1, Q1, K1, m1e_ref if b == 0 else m1_ref)

    def scores(item):
        q, k, _, mask_ref, _ = item
        kb = k()
        return [lax.dot_general(q(h), kb, (((1,), (1,)), ((), ())),
                                preferred_element_type=F32) + mask_ref[...] for h in range(2)]

    def softmax(s):
        m = jnp.max(s, axis=-1, keepdims=True)
        p = jnp.exp(s - m)
        return p.astype(BF16), m, jnp.sum(p, axis=-1, keepdims=True)

    def finish(item, stats):
        _, _, v, _, store = item
        (p0, m0, l0), (p1, m1, l1) = stats
        o = (jnp.dot(p0, v(0), preferred_element_type=F32)
             + jnp.dot(p1, v(1), preferred_element_type=F32))
        store(o, jnp.where(head0, m0, m1), jnp.where(head0, l0, l1))

    groups = ([items[i:i + GROUP] for i in range(0, 2 * NS, GROUP)]
              + [[it] for it in items[2 * NS:]])
    nxt = [scores(it) for it in groups[0]]
    for g, grp in enumerate(groups):
        stats = [[softmax(s) for s in pair] for pair in nxt]
        if g + 1 < len(groups):
            nxt = [scores(it) for it in groups[g + 1]]
        for it, st in zip(grp, stats):
            finish(it, st)

    def comb(r, carry):
        m0, m1, m2 = max_scr[0, r], max_scr[1, r], max_scr[2, r]
        m = jnp.maximum(jnp.maximum(m0, m1), m2)
        w0, w1, w2 = jnp.exp(m0 - m), jnp.exp(m1 - m), jnp.exp(m2 - m)
        num = w0 * acc_scr[0, r] + w1 * acc_scr[1, r] + w2 * acc_scr[2, r]
        den = w0 * sum_scr[0, r] + w1 * sum_scr[1, r] + w2 * sum_scr[2, r]
        o_ref[r] = num / den
        return carry

    lax.fori_loop(0, NS, comb, 0, unroll=2)


def _attention(q, k, v):
    m16, m4, m1 = _attn_masks()
    cur = pl.BlockSpec((None, NS, TA, LANES), lambda j, i: (j, 0, i, 0))
    prev = pl.BlockSpec((None, NS, TA, LANES), lambda j, i: (j, 0, jnp.maximum(i - 1, 0), 0))

    def edge(m):
        return pl.BlockSpec((None,) + m.shape[1:], lambda j, i: (jnp.minimum(i, 1), 0, 0))

    def const(m):
        return pl.BlockSpec(m.shape[1:], lambda j, i: (0, 0))

    return pl.pallas_call(
        _attn_body,
        grid=(NSLAB, RA // TA),
        in_specs=[cur, cur, prev, cur, prev,
                  edge(m16), edge(m4), const(m4), edge(m1), const(m1)],
        out_specs=pl.BlockSpec((None, NS, TA, LANES), lambda j, i: (j, 0, i, 0)),
        out_shape=jax.ShapeDtypeStruct((NSLAB, NS, RA, LANES), F32),
        scratch_shapes=[
            pltpu.VMEM((2, NS, TA, LANES), BF16),
            pltpu.VMEM((2, NS, TA, LANES), BF16),
            pltpu.VMEM((2, NS, TA, LANES), BF16),
            pltpu.VMEM((3, NS, TA, LANES), F32),
            pltpu.VMEM((3, NS, TA, LANES), F32),
            pltpu.VMEM((3, NS, TA, LANES), F32),
        ],
        compiler_params=pltpu.CompilerParams(
            dimension_semantics=("arbitrary", "arbitrary"), vmem_limit_bytes=VMEM_LIMIT),
        name="dilated_attn",
    )(q, k, k, v, v, m16, m4, m4[1], m1, m1[1])


TL = 64


def _lru_body(xl_ref, gl_ref, cw_ref, cb_ref, wg_ref, bg_ref, lam_ref, y_ref,
              tail_scr, carry_scr, xc_scr, g_scr, pl_scr, hl_scr):
    step = pl.program_id(0)

    @pl.when(step == 0)
    def _():
        tail_scr[...] = jnp.zeros_like(tail_scr)
        carry_scr[...] = jnp.zeros_like(carry_scr)

    row = lax.broadcasted_iota(jnp.int32, (TL, LRU_WIDTH), 0)
    cw = cw_ref[...]
    cb = cb_ref[...]

    def tap(r, k):
        st = r - k
        if st >= 0:
            return xl_ref[st]
        st += NS
        prev_last = tail_scr[st - (NS - CONV_WIDTH + 1), 7:8, :]
        return jnp.where(row == 0, prev_last, pltpu.roll(xl_ref[st], 1, axis=0))

    for r in range(NS):
        xc = cb + sum(cw[CONV_WIDTH - 1 - k:CONV_WIDTH - k] * tap(r, k) for k in range(CONV_WIDTH))
        xc_scr[r] = xc
    for n in range(CONV_WIDTH - 1):
        tail_scr[n] = xl_ref[NS - CONV_WIDTH + 1 + n, TL - 8:TL, :]

    xc_all = xc_scr[...].reshape(NS * TL, LRU_WIDTH)
    g_scr[...] = jnp.dot(xc_all.astype(BF16), wg_ref[...], preferred_element_type=F32) + bg_ref[...]

    neg_lam = -lam_ref[...]
    softplus = jnp.maximum(neg_lam, 0.0) + jnp.log1p(jnp.exp(-jnp.abs(neg_lam)))
    p_run = None
    h_run = None
    for r in range(NS):
        g = g_scr[r * TL:(r + 1) * TL]
        xc = xc_scr[r]
        rg = jax.nn.sigmoid(g[:, :LRU_WIDTH])
        ig = jax.nn.sigmoid(g[:, LRU_WIDTH:])
        log_a = -LRU_C * rg * softplus
        a = jnp.exp(log_a)
        th = jnp.tanh(log_a)
        bx = jnp.sqrt(-2.0 * th / (1.0 - th)) * (ig * xc)
        if r == 0:
            p_run, h_run = a, bx
        else:
            h_run = a * h_run + bx
            p_run = a * p_run
        pl_scr[r] = p_run
        hl_scr[r] = h_run

    pa, hb = p_run, h_run
    sft = 1
    while sft < TL:
        keep = row >= sft
        pa_s = jnp.where(keep, pltpu.roll(pa, sft, axis=0), 1.0)
        hb_s = jnp.where(keep, pltpu.roll(hb, sft, axis=0), 0.0)
        hb = pa * hb_s + hb
        pa = pa * pa_s
        sft *= 2
    carry = carry_scr[0:1, :]
    e = pa * carry + hb
    e_prev = jnp.where(row == 0, carry, pltpu.roll(e, 1, axis=0))
    carry_scr[...] = jnp.broadcast_to(e[TL - 1:TL, :], carry_scr.shape)

    for r in range(NS):
        h = pl_scr[r] * e_prev + hl_scr[r]
        y_ref[r] = jax.nn.gelu(gl_ref[r], approximate=True) * h


def _lru(xl, gl, conv_w, conv_b, w_gate, b_gate, lam):
    blk = pl.BlockSpec((NS, TL, LRU_WIDTH), lambda i: (0, i, 0))

    def const(a):
        return pl.BlockSpec(a.shape, lambda i: (0,) * a.ndim)

    return pl.pallas_call(
        _lru_body,
        grid=(RA // TL,),
        in_specs=[blk, blk, const(conv_w), const(conv_b), const(w_gate), const(b_gate), const(lam)],
        out_specs=blk,
        out_shape=jax.ShapeDtypeStruct((NS, RA, LRU_WIDTH), F32),
        scratch_shapes=[
            pltpu.VMEM((CONV_WIDTH - 1, 8, LRU_WIDTH), F32),
            pltpu.VMEM((8, LRU_WIDTH), F32),
            pltpu.VMEM((NS, TL, LRU_WIDTH), F32),
            pltpu.VMEM((NS * TL, 2 * LRU_WIDTH), F32),
            pltpu.VMEM((NS, TL, LRU_WIDTH), F32),
            pltpu.VMEM((NS, TL, LRU_WIDTH), F32),
        ],
        compiler_params=pltpu.CompilerParams(
            dimension_semantics=("arbitrary",), vmem_limit_bytes=VMEM_LIMIT),
        name="rg_lru",
    )(xl, gl, conv_w, conv_b, w_gate, b_gate, lam)


TM4 = 512
FF_CHUNK = 1024


def _out_body(x_ref, at_ref, lr_ref, ga_ref, gl_ref, wo_ref, g2_ref, wu_ref, wd_ref, gf_ref, o_ref):
    attn = jnp.concatenate([at_ref[j] for j in range(NSLAB)], axis=1)
    mixed = jnp.concatenate([_rms(attn, ga_ref[...]), _rms(lr_ref[...], gl_ref[...])], axis=1)
    h = x_ref[...] + jnp.dot(mixed.astype(BF16), wo_ref[...], preferred_element_type=F32)
    u = _rms(h, g2_ref[...]).astype(BF16)
    acc = h
    for c in range(D_FF // FF_CHUNK):
        sl = slice(c * FF_CHUNK, (c + 1) * FF_CHUNK)
        f = jnp.dot(u, wu_ref[:, sl], preferred_element_type=F32)
        f = jnp.square(jnp.maximum(f, 0.0)).astype(BF16)
        acc = acc + jnp.dot(f, wd_ref[sl, :], preferred_element_type=F32)
    o_ref[...] = _rms(acc, gf_ref[...])


def _out_mlp(x3, attn, lru, ga, gl, wo, g2, wu, wd, gf):
    def const(a):
        return pl.BlockSpec(a.shape, lambda r, i: (0,) * a.ndim, pipeline_mode=pl.Buffered(1))

    xspec = pl.BlockSpec((None, TM4, D_MODEL), lambda r, i: (r, i, 0))
    return pl.pallas_call(
        _out_body,
        grid=(NS, RA // TM4),
        in_specs=[
            xspec,
            pl.BlockSpec((NSLAB, None, TM4, LANES), lambda r, i: (0, r, i, 0)),
            pl.BlockSpec((None, TM4, LRU_WIDTH), lambda r, i: (r, i, 0)),
            const(ga), const(gl), const(wo), const(g2), const(wu), const(wd), const(gf),
        ],
        out_specs=xspec,
        out_shape=jax.ShapeDtypeStruct((NS, RA, D_MODEL), F32),
        compiler_params=pltpu.CompilerParams(
            dimension_semantics=("arbitrary", "arbitrary"), vmem_limit_bytes=VMEM_LIMIT),
        name="out_mlp",
    )(x3, attn, lru, ga, gl, wo, g2, wu, wd, gf)


@functools.lru_cache(maxsize=None)
def _rope_tables():
    pos = np.arange(SEQ, dtype=np.float64)
    inv_freq = ROPE_THETA ** (-np.arange(0, ROT_DIM, 2, dtype=np.float64) / ROT_DIM)
    ang = pos[:, None] * inv_freq[None, :]
    cos, sin = np.cos(ang), np.sin(ang)
    pad = HEAD_DIM - ROT_DIM
    c_head = np.concatenate([cos, cos, np.ones((SEQ, pad))], axis=1)
    s_head = np.concatenate([-sin, sin, np.zeros((SEQ, pad))], axis=1)

    def lay(t):
        t = np.tile(t, (1, LANES // HEAD_DIM)).astype(np.float32)
        return np.ascontiguousarray(t.reshape(RA, NS, LANES).transpose(1, 0, 2))

    return lay(c_head), lay(s_head)


def _block_diag(w):
    n, b, _ = w.shape
    eye = jnp.eye(n, dtype=w.dtype)
    return (eye[:, None, :, None] * w[:, :, None, :]).reshape(n * b, n * b)


def kernel(x, norm1_g, w_in, conv_w, conv_b, w_rgate, b_rgate, w_igate, b_igate, lru_lambda,
           attn_out_g, lru_out_g, w_out, norm2_g, w_mlp_up, w_mlp_down, final_g):
    assert x.shape == (1, SEQ, D_MODEL) and w_in.shape[0] == 1
    x3 = x.reshape(RA, NS, D_MODEL).transpose(1, 0, 2)
    cos_t, sin_t = _rope_tables()
    q, k, v, xl, gl = _inproj(x3, norm1_g.reshape(1, D_MODEL), w_in[0].astype(BF16), cos_t, sin_t)
    attn = _attention(q, k, v)
    w_gate = jnp.concatenate([_block_diag(w_rgate[0]), _block_diag(w_igate[0])], axis=1).astype(BF16)
    b_gate = jnp.concatenate([b_rgate[0].reshape(1, -1), b_igate[0].reshape(1, -1)], axis=1)
    lru = _lru(xl, gl, conv_w[0], conv_b[0].reshape(1, -1), w_gate, b_gate, lru_lambda[0].reshape(1, -1))
    out = _out_mlp(x3, attn, lru, attn_out_g[0].reshape(1, -1), lru_out_g[0].reshape(1, -1),
                   w_out[0].astype(BF16), norm2_g[0].reshape(1, -1),
                   w_mlp_up[0].astype(BF16), w_mlp_down[0].astype(BF16), final_g.reshape(1, -1))
    return out.transpose(1, 0, 2).reshape(1, SEQ, D_MODEL)
```

```python
import functools

import numpy as np
import jax
import jax.numpy as jnp
from jax import lax
from jax.experimental import pallas as pl
from jax.experimental.pallas import tpu as pltpu

F32 = jnp.float32
BF16 = jnp.bfloat16

D_MODEL = 1024
SEQ = 16384
ATTN_HEADS = 8
HEAD_DIM = 64
ATTN_WIDTH = ATTN_HEADS * HEAD_DIM
ROT_DIM = HEAD_DIM // 4
ROPE_THETA = 500000.0
LRU_WIDTH = D_MODEL - ATTN_WIDTH
LRU_BLOCKS = 8
CONV_WIDTH = 4
LRU_C = 8.0
IN_WIDTH = 3 * ATTN_WIDTH + 2 * LRU_WIDTH
D_FF = 4 * D_MODEL
EPS = 1e-6
WINDOW_STEPS = 128

NS = 16
RA = SEQ // NS
LANES = 128
NSLAB = ATTN_WIDTH // LANES
NEG = -1e30

VMEM_LIMIT = 56 * 1024 * 1024


def _rms(x, g):
    return x * lax.rsqrt(jnp.mean(x * x, axis=-1, keepdims=True) + EPS) * g


TM1 = 512
PG = NS * NS
RG = PG // NS
NG1 = TM1 // PG


@functools.lru_cache(maxsize=None)
def _group_perm():
    n = np.arange(PG)
    p = np.zeros((PG, PG), np.float32)
    p[(n % NS) * RG + n // NS, n] = 1.0
    return p


def _inproj_body(x_ref, g_ref, w_ref, perm_ref, cos_ref, sin_ref,
                 q_ref, k_ref, v_ref, xl_ref, gl_ref):
    u = _rms(x_ref[...], g_ref[...]).astype(BF16)
    u = jnp.concatenate(
        [jnp.dot(perm_ref[...], u[g * PG:(g + 1) * PG], preferred_element_type=F32).astype(BF16)
         for g in range(NG1)], axis=0)
    z = jnp.dot(u, w_ref[...], preferred_element_type=F32)
    c = jnp.concatenate([cos_ref[...]] * NSLAB, axis=1)
    s = jnp.concatenate([sin_ref[...]] * NSLAB, axis=1)
    lane = lax.broadcasted_iota(jnp.int32, (1, ATTN_WIDTH), 1) % HEAD_DIM
    first_half = lane < ROT_DIM // 2

    def rope(t):
        up = pltpu.roll(t, ATTN_WIDTH - ROT_DIM // 2, axis=1)
        dn = pltpu.roll(t, ROT_DIM // 2, axis=1)
        return t * c + jnp.where(first_half, up, dn) * s

    q = rope(z[:, :ATTN_WIDTH]) * (HEAD_DIM ** -0.5)
    k = rope(z[:, ATTN_WIDTH:2 * ATTN_WIDTH])
    v = z[:, 2 * ATTN_WIDTH:3 * ATTN_WIDTH]
    xl = z[:, 3 * ATTN_WIDTH:3 * ATTN_WIDTH + LRU_WIDTH]
    gl = z[:, 3 * ATTN_WIDTH + LRU_WIDTH:]
    for g in range(NG1):
        for r in range(NS):
            src = slice(g * PG + r * RG, g * PG + (r + 1) * RG)
            dst = slice(g * RG, (g + 1) * RG)
            for j in range(NSLAB):
                sl = slice(j * LANES, (j + 1) * LANES)
                q_ref[j, r, dst] = q[src, sl].astype(BF16)
                k_ref[j, r, dst] = k[src, sl].astype(BF16)
                v_ref[j, r, dst] = v[src, sl].astype(BF16)
            xl_ref[r, dst] = xl[src]
            gl_ref[r, dst] = gl[src]


def _inproj(x2, g, w, cos_t, sin_t):
    rows = TM1 // NS
    slab_spec = pl.BlockSpec((NSLAB, NS, rows, LANES), lambda i: (0, 0, i, 0))
    row_spec = pl.BlockSpec((NS, rows, LRU_WIDTH), lambda i: (0, i, 0))
    tab_spec = pl.BlockSpec((TM1, LANES), lambda i: (i, 0))
    slab_shape = jax.ShapeDtypeStruct((NSLAB, NS, RA, LANES), BF16)
    row_shape = jax.ShapeDtypeStruct((NS, RA, LRU_WIDTH), F32)
    perm = jnp.asarray(_group_perm(), BF16)
    return pl.pallas_call(
        _inproj_body,
        grid=(SEQ // TM1,),
        in_specs=[
            pl.BlockSpec((TM1, D_MODEL), lambda i: (i, 0)),
            pl.BlockSpec((1, D_MODEL), lambda i: (0, 0)),
            pl.BlockSpec((D_MODEL, IN_WIDTH), lambda i: (0, 0)),
            pl.BlockSpec((PG, PG), lambda i: (0, 0)),
            tab_spec, tab_spec,
        ],
        out_specs=[slab_spec, slab_spec, slab_spec, row_spec, row_spec],
        out_shape=[slab_shape, slab_shape, slab_shape, row_shape, row_shape],
        compiler_params=pltpu.CompilerParams(
            dimension_semantics=("arbitrary",), vmem_limit_bytes=VMEM_LIMIT),
        name="inproj",
    )(x2, g, w, perm, cos_t, sin_t)


TA = 128
Q4, K4 = 32, 64
Q1, K1 = 16, 32
GROUP = 4


@functools.lru_cache(maxsize=None)
def _attn_masks():
    def both(diff, from_prev):
        ok = (diff >= 0) & (diff <= WINDOW_STEPS)
        normal = np.where(ok, 0.0, NEG).astype(np.float32)
        first = np.where(ok & ~from_prev, 0.0, NEG).astype(np.float32)
        return np.stack([first, normal])

    iq = np.arange(TA)[:, None]
    ck = np.arange(2 * TA)[None, :]
    m16 = both(iq - ck + TA, np.broadcast_to(ck < TA, (TA, 2 * TA)))
    cq, i4 = np.divmod(np.arange(4 * Q4), Q4)
    ckk, j4 = np.divmod(np.arange(4 * K4), K4)
    d4 = 4 * (i4[:, None] - j4[None, :] + Q4) + (cq[:, None] - ckk[None, :])
    m4 = both(d4, np.broadcast_to(j4[None, :] < K4 - Q4, d4.shape))
    rq, i1 = np.divmod(np.arange(NS * Q1), Q1)
    rk, j1 = np.divmod(np.arange(NS * K1), K1)
    d1 = NS * (i1[:, None] - j1[None, :] + Q1) + (rq[:, None] - rk[None, :])
    m1 = both(d1, np.broadcast_to(j1[None, :] < K1 - Q1, d1.shape))
    return m16, m4, m1


def _attn_body(q_ref, kc_ref, kp_ref, vc_ref, vp_ref, m16_ref, m4e_ref, m4_ref, m1e_ref, m1_ref,
               o_ref, qm_scr, vcm_scr, vpm_scr, acc_scr, max_scr, sum_scr):
    lane = lax.broadcasted_iota(jnp.int32, (1, LANES), 1)
    head0 = lane < HEAD_DIM
    zero = jnp.zeros((), BF16)

    q = q_ref[...]
    qm_scr[0] = jnp.where(head0, q, zero)
    qm_scr[1] = jnp.where(head0, zero, q)
    vc = vc_ref[...]
    vcm_scr[0] = jnp.where(head0, vc, zero)
    vcm_scr[1] = jnp.where(head0, zero, vc)
    vp = vp_ref[...]
    vpm_scr[0] = jnp.where(head0, vp, zero)
    vpm_scr[1] = jnp.where(head0, zero, vp)

    items = []

    def store_rows(p, pieces):
        def store(o, mx, sm):
            off = 0
            for st, r0, n in pieces:
                acc_scr[p, st, r0:r0 + n] = o[off:off + n]
                max_scr[p, st, r0:r0 + n] = mx[off:off + n]
                sum_scr[p, st, r0:r0 + n] = sm[off:off + n]
                off += n
        return store

    def window(cur, prev, ix, r0, n):
        if r0 < 0:
            return [prev[ix + (slice(TA + r0, TA),)], cur[ix + (slice(0, r0 + n),)]]
        return [cur[ix + (slice(r0, r0 + n),)]]

    def add_item(p, streams, q0, nq, nk, mask_ref):
        k0 = q0 + nq - nk

        def q(h):
            return jnp.concatenate([qm_scr[h, st, q0:q0 + nq] for st in streams], axis=0)

        def k():
            return jnp.concatenate(
                [x for st in streams for x in window(kc_ref, kp_ref, (st,), k0, nk)], axis=0)

        def v(h):
            return jnp.concatenate(
                [x for st in streams for x in window(vcm_scr, vpm_scr, (h, st), k0, nk)], axis=0)

        items.append((q, k, v, mask_ref, store_rows(p, [(st, q0, nq) for st in streams])))

    for r in range(NS):
        add_item(0, [r], 0, TA, 2 * TA, m16_ref)
    for r4 in range(4):
        for b in range(TA // Q4):
            add_item(1, [r4 + 4 * c for c in range(4)], b * Q4, Q4, K4, m4e_ref if b == 0 else m4_ref)
    for b in range(TA // Q1):
        add_item(2, list(range(NS)), b * Q1, Q1, K1, m1e_ref if b == 0 else m1_ref)

    def scores(item):
        q, k, _, mask_ref, _ = item
        kb = k()
        return [lax.dot_general(q(h), kb, (((1,), (1,)), ((), ())),
                                preferred_element_type=F32) + mask_ref[...] for h in range(2)]

    def softmax(s):
        m = jnp.max(s, axis=-1, keepdims=True)
        p = jnp.exp(s - m)
        return p.astype(BF16), m, jnp.sum(p, axis=-1, keepdims=True)

    def finish(item, stats):
        _, _, v, _, store = item
        (p0, m0, l0), (p1, m1, l1) = stats
        o = (jnp.dot(p0, v(0), preferred_element_type=F32)
             + jnp.dot(p1, v(1), preferred_element_type=F32))
        store(o, jnp.where(head0, m0, m1), jnp.where(head0, l0, l1))

    groups = ([items[i:i + GROUP] for i in range(0, 2 * NS, GROUP)]
              + [[it] for it in items[2 * NS:]])
    nxt = [scores(it) for it in groups[0]]
    for g, grp in enumerate(groups):
        stats = [[softmax(s) for s in pair] for pair in nxt]
        if g + 1 < len(groups):
            nxt = [scores(it) for it in groups[g + 1]]
        for it, st in zip(grp, stats):
            finish(it, st)

    def comb(r, carry):
        m0, m1, m2 = max_scr[0, r], max_scr[1, r], max_scr[2, r]
        m = jnp.maximum(jnp.maximum(m0, m1), m2)
        w0, w1, w2 = jnp.exp(m0 - m), jnp.exp(m1 - m), jnp.exp(m2 - m)
        num = w0 * acc_scr[0, r] + w1 * acc_scr[1, r] + w2 * acc_scr[2, r]
        den = w0 * sum_scr[0, r] + w1 * sum_scr[1, r] + w2 * sum_scr[2, r]
        o_ref[r] = num / den
        return carry

    lax.fori_loop(0, NS, comb, 0, unroll=2)


def _attention(q, k, v):
    m16, m4, m1 = _attn_masks()
    cur = pl.BlockSpec((None, NS, TA, LANES), lambda j, i: (j, 0, i, 0))
    prev = pl.BlockSpec((None, NS, TA, LANES), lambda j, i: (j, 0, jnp.maximum(i - 1, 0), 0))

    def edge(m):
        return pl.BlockSpec((None,) + m.shape[1:], lambda j, i: (jnp.minimum(i, 1), 0, 0))

    def const(m):
        return pl.BlockSpec(m.shape[1:], lambda j, i: (0, 0))

    return pl.pallas_call(
        _attn_body,
        grid=(NSLAB, RA // TA),
        in_specs=[cur, cur, prev, cur, prev,
                  edge(m16), edge(m4), const(m4), edge(m1), const(m1)],
        out_specs=pl.BlockSpec((None, NS, TA, LANES), lambda j, i: (j, 0, i, 0)),
        out_shape=jax.ShapeDtypeStruct((NSLAB, NS, RA, LANES), F32),
        scratch_shapes=[
            pltpu.VMEM((2, NS, TA, LANES), BF16),
            pltpu.VMEM((2, NS, TA, LANES), BF16),
            pltpu.VMEM((2, NS, TA, LANES), BF16),
            pltpu.VMEM((3, NS, TA, LANES), F32),
            pltpu.VMEM((3, NS, TA, LANES), F32),
            pltpu.VMEM((3, NS, TA, LANES), F32),
        ],
        compiler_params=pltpu.CompilerParams(
            dimension_semantics=("arbitrary", "arbitrary"), vmem_limit_bytes=VMEM_LIMIT),
        name="dilated_attn",
    )(q, k, k, v, v, m16, m4, m4[1], m1, m1[1])


TL = 64


def _lru_body(xl_ref, gl_ref, cw_ref, cb_ref, wg_ref, bg_ref, lam_ref, y_ref,
              tail_scr, carry_scr, xc_scr, g_scr, pl_scr, hl_scr):
    step = pl.program_id(0)

    @pl.when(step == 0)
    def _():
        tail_scr[...] = jnp.zeros_like(tail_scr)
        carry_scr[...] = jnp.zeros_like(carry_scr)

    row = lax.broadcasted_iota(jnp.int32, (TL, LRU_WIDTH), 0)
    cw = cw_ref[...]
    cb = cb_ref[...]

    def tap(r, k):
        st = r - k
        if st >= 0:
            return xl_ref[st]
        st += NS
        prev_last = tail_scr[st - (NS - CONV_WIDTH + 1), 7:8, :]
        return jnp.where(row == 0, prev_last, pltpu.roll(xl_ref[st], 1, axis=0))

    for r in range(NS):
        xc = cb + sum(cw[CONV_WIDTH - 1 - k:CONV_WIDTH - k] * tap(r, k) for k in range(CONV_WIDTH))
        xc_scr[r] = xc
    for n in range(CONV_WIDTH - 1):
        tail_scr[n] = xl_ref[NS - CONV_WIDTH + 1 + n, TL - 8:TL, :]

    xc_all = xc_scr[...].reshape(NS * TL, LRU_WIDTH)
    g_scr[...] = jnp.dot(xc_all.astype(BF16), wg_ref[...], preferred_element_type=F32) + bg_ref[...]

    neg_lam = -lam_ref[...]
    softplus = jnp.maximum(neg_lam, 0.0) + jnp.log1p(jnp.exp(-jnp.abs(neg_lam)))
    p_run = None
    h_run = None
    for r in range(NS):
        g = g_scr[r * TL:(r + 1) * TL]
        xc = xc_scr[r]
        rg = jax.nn.sigmoid(g[:, :LRU_WIDTH])
        ig = jax.nn.sigmoid(g[:, LRU_WIDTH:])
        log_a = -LRU_C * rg * softplus
        a = jnp.exp(log_a)
        th = jnp.tanh(log_a)
        bx = jnp.sqrt(-2.0 * th / (1.0 - th)) * (ig * xc)
        if r == 0:
            p_run, h_run = a, bx
        else:
            h_run = a * h_run + bx
            p_run = a * p_run
        pl_scr[r] = p_run
        hl_scr[r] = h_run

    pa, hb = p_run, h_run
    sft = 1
    while sft < TL:
        keep = row >= sft
        pa_s = jnp.where(keep, pltpu.roll(pa, sft, axis=0), 1.0)
        hb_s = jnp.where(keep, pltpu.roll(hb, sft, axis=0), 0.0)
        hb = pa * hb_s + hb
        pa = pa * pa_s
        sft *= 2
    carry = carry_scr[0:1, :]
    e = pa * carry + hb
    e_prev = jnp.where(row == 0, carry, pltpu.roll(e, 1, axis=0))
    carry_scr[...] = jnp.broadcast_to(e[TL - 1:TL, :], carry_scr.shape)

    for r in range(NS):
        h = pl_scr[r] * e_prev + hl_scr[r]
        y_ref[r] = jax.nn.gelu(gl_ref[r], approximate=True) * h


def _lru(xl, gl, conv_w, conv_b, w_gate, b_gate, lam):
    blk = pl.BlockSpec((NS, TL, LRU_WIDTH), lambda i: (0, i, 0))

    def const(a):
        return pl.BlockSpec(a.shape, lambda i: (0,) * a.ndim)

    return pl.pallas_call(
        _lru_body,
        grid=(RA // TL,),
        in_specs=[blk, blk, const(conv_w), const(conv_b), const(w_gate), const(b_gate), const(lam)],
        out_specs=blk,
        out_shape=jax.ShapeDtypeStruct((NS, RA, LRU_WIDTH), F32),
        scratch_shapes=[
            pltpu.VMEM((CONV_WIDTH - 1, 8, LRU_WIDTH), F32),
            pltpu.VMEM((8, LRU_WIDTH), F32),
            pltpu.VMEM((NS, TL, LRU_WIDTH), F32),
            pltpu.VMEM((NS * TL, 2 * LRU_WIDTH), F32),
            pltpu.VMEM((NS, TL, LRU_WIDTH), F32),
            pltpu.VMEM((NS, TL, LRU_WIDTH), F32),
        ],
        compiler_params=pltpu.CompilerParams(
            dimension_semantics=("arbitrary",), vmem_limit_bytes=VMEM_LIMIT),
        name="rg_lru",
    )(xl, gl, conv_w, conv_b, w_gate, b_gate, lam)


TM4 = 512
NG4 = TM4 // PG
FF_CHUNK = 1024


def _out_body(x_ref, at_ref, lr_ref, perm_ref, ga_ref, gl_ref, wo_ref, g2_ref, wu_ref, wd_ref,
              gf_ref, o_ref):
    groups = []
    for g in range(NG4):
        rows = slice(g * RG, (g + 1) * RG)
        attn = jnp.concatenate(
            [jnp.concatenate([at_ref[j, r, rows] for j in range(NSLAB)], axis=1) for r in range(NS)],
            axis=0)
        lru = jnp.concatenate([lr_ref[r, rows] for r in range(NS)], axis=0)
        mixed = jnp.concatenate([_rms(attn, ga_ref[...]), _rms(lru, gl_ref[...])], axis=1)
        groups.append(jnp.dot(perm_ref[...], mixed.astype(BF16),
                              preferred_element_type=F32).astype(BF16))
    mixed = jnp.concatenate(groups, axis=0)
    h = x_ref[...] + jnp.dot(mixed, wo_ref[...], preferred_element_type=F32)
    u = _rms(h, g2_ref[...]).astype(BF16)
    acc = h
    for c in range(D_FF // FF_CHUNK):
        sl = slice(c * FF_CHUNK, (c + 1) * FF_CHUNK)
        f = jnp.dot(u, wu_ref[:, sl], preferred_element_type=F32)
        f = jnp.square(jnp.maximum(f, 0.0)).astype(BF16)
        acc = acc + jnp.dot(f, wd_ref[sl, :], preferred_element_type=F32)
    o_ref[...] = _rms(acc, gf_ref[...])


def _out_mlp(x2, attn, lru, ga, gl, wo, g2, wu, wd, gf):
    def const(a):
        return pl.BlockSpec(a.shape, lambda i: (0,) * a.ndim, pipeline_mode=pl.Buffered(1))

    rows = TM4 // NS
    xspec = pl.BlockSpec((TM4, D_MODEL), lambda i: (i, 0))
    perm = jnp.asarray(_group_perm(), BF16)
    return pl.pallas_call(
        _out_body,
        grid=(SEQ // TM4,),
        in_specs=[
            xspec,
            pl.BlockSpec((NSLAB, NS, rows, LANES), lambda i: (0, 0, i, 0)),
            pl.BlockSpec((NS, rows, LRU_WIDTH), lambda i: (0, i, 0)),
            const(perm), const(ga), const(gl), const(wo), const(g2), const(wu), const(wd), const(gf),
        ],
        out_specs=xspec,
        out_shape=jax.ShapeDtypeStruct((SEQ, D_MODEL), F32),
        compiler_params=pltpu.CompilerParams(
            dimension_semantics=("arbitrary",), vmem_limit_bytes=VMEM_LIMIT),
        name="out_mlp",
    )(x2, attn, lru, perm, ga, gl, wo, g2, wu, wd, gf)


@functools.lru_cache(maxsize=None)
def _rope_tables():
    pos = np.arange(SEQ, dtype=np.float64)
    inv_freq = ROPE_THETA ** (-np.arange(0, ROT_DIM, 2, dtype=np.float64) / ROT_DIM)
    ang = pos[:, None] * inv_freq[None, :]
    cos, sin = np.cos(ang), np.sin(ang)
    pad = HEAD_DIM - ROT_DIM
    c_head = np.concatenate([cos, cos, np.ones((SEQ, pad))], axis=1)
    s_head = np.concatenate([-sin, sin, np.zeros((SEQ, pad))], axis=1)

    def lay(t):
        t = np.tile(t, (1, LANES // HEAD_DIM)).astype(np.float32)
        t = t.reshape(SEQ // PG, RG, NS, LANES).transpose(0, 2, 1, 3)
        return np.ascontiguousarray(t.reshape(SEQ, LANES))

    return lay(c_head), lay(s_head)


def _block_diag(w):
    n, b, _ = w.shape
    eye = jnp.eye(n, dtype=w.dtype)
    return (eye[:, None, :, None] * w[:, :, None, :]).reshape(n * b, n * b)


def kernel(x, norm1_g, w_in, conv_w, conv_b, w_rgate, b_rgate, w_igate, b_igate, lru_lambda,
           attn_out_g, lru_out_g, w_out, norm2_g, w_mlp_up, w_mlp_down, final_g):
    assert x.shape == (1, SEQ, D_MODEL) and w_in.shape[0] == 1
    x2 = x.reshape(SEQ, D_MODEL)
    cos_t, sin_t = _rope_tables()
    q, k, v, xl, gl = _inproj(x2, norm1_g.reshape(1, D_MODEL), w_in[0].astype(BF16), cos_t, sin_t)
    attn = _attention(q, k, v)
    w_gate = jnp.concatenate([_block_diag(w_rgate[0]), _block_diag(w_igate[0])], axis=1).astype(BF16)
    b_gate = jnp.concatenate([b_rgate[0].reshape(1, -1), b_igate[0].reshape(1, -1)], axis=1)
    lru = _lru(xl, gl, conv_w[0], conv_b[0].reshape(1, -1), w_gate, b_gate, lru_lambda[0].reshape(1, -1))
    out = _out_mlp(x2, attn, lru, attn_out_g[0].reshape(1, -1), lru_out_g[0].reshape(1, -1),
                   w_out[0].astype(BF16), norm2_g[0].reshape(1, -1),
                   w_mlp_up[0].astype(BF16), w_mlp_down[0].astype(BF16), final_g.reshape(1, -1))
    return out.reshape(1, SEQ, D_MODEL)
```

```python
import functools

import numpy as np
import jax
import jax.numpy as jnp
from jax import lax
from jax.experimental import pallas as pl
from jax.experimental.pallas import tpu as pltpu

F32 = jnp.float32
BF16 = jnp.bfloat16

D_MODEL = 1024
SEQ = 16384
ATTN_HEADS = 8
HEAD_DIM = 64
ATTN_WIDTH = ATTN_HEADS * HEAD_DIM
ROT_DIM = HEAD_DIM // 4
ROPE_THETA = 500000.0
LRU_WIDTH = D_MODEL - ATTN_WIDTH
LRU_BLOCKS = 8
CONV_WIDTH = 4
LRU_C = 8.0
IN_WIDTH = 3 * ATTN_WIDTH + 2 * LRU_WIDTH
D_FF = 4 * D_MODEL
EPS = 1e-6
WINDOW_STEPS = 128

NS = 16
RA = SEQ // NS
LANES = 128
NSLAB = ATTN_WIDTH // LANES
NEG = -1e30

VMEM_LIMIT = 56 * 1024 * 1024


def _rms(x, g):
    return x * lax.rsqrt(jnp.mean(x * x, axis=-1, keepdims=True) + EPS) * g


TM1 = 512
PG = NS * NS
RG = PG // NS
NG1 = TM1 // PG
CH = 8
LOG2E = 1.4426950408889634


@functools.lru_cache(maxsize=None)
def _group_perm():
    n = np.arange(PG)
    p = np.zeros((PG, PG), np.float32)
    p[(n % NS) * RG + n // NS, n] = 1.0
    return p


def _inproj_body(x_ref, g_ref, w_ref, perm_ref, cos_ref, sin_ref,
                 q_ref, k_ref, v_ref, q1_ref, k1_ref, v1_ref, xl_ref, gl_ref):
    u = _rms(x_ref[...], g_ref[...]).astype(BF16)
    u = jnp.concatenate(
        [jnp.dot(perm_ref[...], u[g * PG:(g + 1) * PG], preferred_element_type=F32).astype(BF16)
         for g in range(NG1)], axis=0)
    z = jnp.dot(u, w_ref[...], preferred_element_type=F32)
    c = jnp.concatenate([cos_ref[...]] * NSLAB, axis=1)
    s = jnp.concatenate([sin_ref[...]] * NSLAB, axis=1)
    lane = lax.broadcasted_iota(jnp.int32, (1, ATTN_WIDTH), 1) % HEAD_DIM
    first_half = lane < ROT_DIM // 2

    def rope(t):
        up = pltpu.roll(t, ATTN_WIDTH - ROT_DIM // 2, axis=1)
        dn = pltpu.roll(t, ROT_DIM // 2, axis=1)
        return t * c + jnp.where(first_half, up, dn) * s

    q = rope(z[:, :ATTN_WIDTH]) * (HEAD_DIM ** -0.5 * LOG2E)
    k = rope(z[:, ATTN_WIDTH:2 * ATTN_WIDTH])
    v = z[:, 2 * ATTN_WIDTH:3 * ATTN_WIDTH]
    xl = z[:, 3 * ATTN_WIDTH:3 * ATTN_WIDTH + LRU_WIDTH]
    gl = z[:, 3 * ATTN_WIDTH + LRU_WIDTH:]
    for g in range(NG1):
        for r in range(NS):
            src = slice(g * PG + r * RG, g * PG + (r + 1) * RG)
            dst = slice(g * RG, (g + 1) * RG)
            for j in range(NSLAB):
                sl = slice(j * LANES, (j + 1) * LANES)
                q_ref[j, r, dst] = q[src, sl].astype(BF16)
                k_ref[j, r, dst] = k[src, sl].astype(BF16)
                v_ref[j, r, dst] = v[src, sl].astype(BF16)
            xl_ref[r, dst] = xl[src]
            gl_ref[r, dst] = gl[src]
    for g in range(NG1):
        for cl in range(RG // CH):
            for r in range(0, NS, 2):
                lo = g * PG + r * RG + cl * CH
                hi = lo + RG
                d0 = (g * (RG // CH) + cl) * NS * CH + r * CH
                dst = slice(d0, d0 + 2 * CH)
                for j in range(NSLAB):
                    sl = slice(j * LANES, (j + 1) * LANES)
                    for src, ref in ((q, q1_ref), (k, k1_ref), (v, v1_ref)):
                        ref[j, dst] = jnp.concatenate(
                            [src[lo:lo + CH, sl], src[hi:hi + CH, sl]], axis=0).astype(BF16)


def _inproj(x2, g, w, cos_t, sin_t):
    rows = TM1 // NS
    slab_spec = pl.BlockSpec((NSLAB, NS, rows, LANES), lambda i: (0, 0, i, 0))
    chunk_spec = pl.BlockSpec((NSLAB, TM1, LANES), lambda i: (0, i, 0))
    row_spec = pl.BlockSpec((NS, rows, LRU_WIDTH), lambda i: (0, i, 0))
    tab_spec = pl.BlockSpec((TM1, LANES), lambda i: (i, 0))
    slab_shape = jax.ShapeDtypeStruct((NSLAB, NS, RA, LANES), BF16)
    chunk_shape = jax.ShapeDtypeStruct((NSLAB, SEQ, LANES), BF16)
    row_shape = jax.ShapeDtypeStruct((NS, RA, LRU_WIDTH), F32)
    perm = jnp.asarray(_group_perm(), BF16)
    return pl.pallas_call(
        _inproj_body,
        grid=(SEQ // TM1,),
        in_specs=[
            pl.BlockSpec((TM1, D_MODEL), lambda i: (i, 0)),
            pl.BlockSpec((1, D_MODEL), lambda i: (0, 0)),
            pl.BlockSpec((D_MODEL, IN_WIDTH), lambda i: (0, 0)),
            pl.BlockSpec((PG, PG), lambda i: (0, 0)),
            tab_spec, tab_spec,
        ],
        out_specs=[slab_spec] * 3 + [chunk_spec] * 3 + [row_spec] * 2,
        out_shape=[slab_shape] * 3 + [chunk_shape] * 3 + [row_shape] * 2,
        compiler_params=pltpu.CompilerParams(
            dimension_semantics=("arbitrary",), vmem_limit_bytes=VMEM_LIMIT),
        name="inproj",
    )(x2, g, w, perm, cos_t, sin_t)


TA = 128
UQ, UK = 128, 256
Q4, K4 = UQ // 4, UK // 4
GROUP = 4
NGROUP = NS // GROUP


@functools.lru_cache(maxsize=None)
def _attn_masks():
    def variants(diff, from_prev, per_unit):
        ok = (diff >= 0) & (diff <= WINDOW_STEPS)
        normal = np.where(ok, 0.0, NEG).astype(np.float32)
        first = np.where(ok & ~from_prev, 0.0, NEG).astype(np.float32)
        if per_unit:
            return np.stack([np.stack([first, normal]), np.stack([normal, normal])])
        return np.stack([first, normal])

    iq = np.arange(UQ)[:, None]
    ck = np.arange(UK)[None, :]
    m16 = variants(iq - ck + TA, np.broadcast_to(ck < TA, (UQ, UK)), False)
    cq, i4 = np.divmod(np.arange(UQ), Q4)
    ckk, j4 = np.divmod(np.arange(UK), K4)
    d4 = 4 * (i4[:, None] - j4[None, :] + Q4) + (cq[:, None] - ckk[None, :])
    m4 = variants(d4, np.broadcast_to(j4[None, :] < K4 - Q4, d4.shape), True)
    rq, i1 = np.divmod(np.arange(UQ), CH)
    kc, kin = np.divmod(np.arange(UK), UQ)
    rk, j1 = np.divmod(kin, CH)
    d1 = (NS * i1 + rq)[:, None] - (NS * j1 + rk + UQ * (kc - 1))[None, :]
    m1 = variants(d1, np.broadcast_to(kc[None, :] == 0, d1.shape), True)
    return m16, m4, m1


def _aligned(x, m):
    return x if isinstance(x, int) else pl.multiple_of(x, m)


def _attn_body(q_ref, kc_ref, kp_ref, vc_ref, vp_ref, q1_ref, k1c_ref, k1p_ref, v1c_ref, v1p_ref,
               m16_ref, m4_ref, m1_ref, o_ref,
               qm_scr, vcm_scr, vpm_scr, q1m_scr, k1_scr, v1m_scr, s_scr, acc_scr, max_scr, sum_scr):
    lane = lax.broadcasted_iota(jnp.int32, (1, LANES), 1)
    head0 = lane < HEAD_DIM
    zero = jnp.zeros((), BF16)

    def split(x):
        return jnp.where(head0, x, zero), jnp.where(head0, zero, x)

    qm_scr[0], qm_scr[1] = split(q_ref[...])
    vcm_scr[0], vcm_scr[1] = split(vc_ref[...])
    vpm_scr[0], vpm_scr[1] = split(vp_ref[...])
    q1m_scr[0], q1m_scr[1] = split(q1_ref[...])
    k1_scr[0:UQ] = k1p_ref[...]
    k1_scr[UQ:] = k1c_ref[...]
    v1m_scr[0, 0:UQ], v1m_scr[1, 0:UQ] = split(v1p_ref[...])
    v1m_scr[0, UQ:], v1m_scr[1, UQ:] = split(v1c_ref[...])

    def unit16(r):
        def store(o, mx, sm):
            acc_scr[0, r], max_scr[0, r], sum_scr[0, r] = o, mx, sm

        return (lambda h: qm_scr[h, r],
                lambda: jnp.concatenate([kp_ref[r], kc_ref[r]], axis=0),
                lambda h: jnp.concatenate([vpm_scr[h, r], vcm_scr[h, r]], axis=0),
                lambda: m16_ref[...], store)

    def unit4(r4, b):
        streams = [r4 + 4 * c for c in range(4)]
        q0, k0 = b * Q4, b * Q4 + Q4 - K4

        def window(cur, prev, ix):
            if k0 < 0:
                return [prev[ix + (slice(TA + k0, TA),)], cur[ix + (slice(0, k0 + K4),)]]
            return [cur[ix + (slice(k0, k0 + K4),)]]

        def store(o, mx, sm):
            for c, st in enumerate(streams):
                rows = slice(c * Q4, (c + 1) * Q4)
                acc_scr[1, st, q0:q0 + Q4] = o[rows]
                max_scr[1, st, q0:q0 + Q4] = mx[rows]
                sum_scr[1, st, q0:q0 + Q4] = sm[rows]

        return (lambda h: jnp.concatenate([qm_scr[h, st, q0:q0 + Q4] for st in streams], axis=0),
                lambda: jnp.concatenate(
                    [x for st in streams for x in window(kc_ref, kp_ref, (st,))], axis=0),
                lambda h: jnp.concatenate(
                    [x for st in streams for x in window(vcm_scr, vpm_scr, (h, st))], axis=0),
                lambda: m4_ref[min(b, 1)], store)

    def unit1(u):
        off = _aligned(u * UQ, UQ)
        row = _aligned(u * CH, CH)

        def store(o, mx, sm):
            for r in range(NS):
                rows = slice(r * CH, (r + 1) * CH)
                acc_scr[2, r, pl.ds(row, CH)] = o[rows]
                max_scr[2, r, pl.ds(row, CH)] = mx[rows]
                sum_scr[2, r, pl.ds(row, CH)] = sm[rows]

        return (lambda h: q1m_scr[h, pl.ds(off, UQ)],
                lambda: k1_scr[pl.ds(off, UK)],
                lambda h: v1m_scr[h, pl.ds(off, UK)],
                lambda: m1_ref[min(u, 1) if isinstance(u, int) else jnp.minimum(u, 1)], store)

    def units(p, it):
        if p == 0:
            return [unit16(it * GROUP + g) for g in range(GROUP)]
        if p == 1:
            return [unit4(it, b) for b in range(GROUP)]
        return [unit1(it * GROUP + g) for g in range(GROUP)]

    def score_stage(group, slot):
        for g, (q, k, _, mask, _) in enumerate(group):
            kb = k()
            for h in range(2):
                s_scr[slot, g, h] = lax.dot_general(
                    q(h), kb, (((1,), (1,)), ((), ())), preferred_element_type=F32) + mask()

    def value_stage(group, slot):
        for g, (_, _, v, _, store) in enumerate(group):
            o = None
            mx, sm = [], []
            for h in range(2):
                s = s_scr[slot, g, h]
                m = jnp.max(s, axis=-1, keepdims=True)
                p = jnp.exp2(s - m)
                mx.append(m)
                sm.append(jnp.sum(p, axis=-1, keepdims=True))
                pv = jnp.dot(p.astype(BF16), v(h), preferred_element_type=F32)
                o = pv if o is None else o + pv
            store(o, jnp.where(head0, mx[0], mx[1]), jnp.where(head0, sm[0], sm[1]))

    stages = [units(p, it) for p in range(3) for it in range(NGROUP)]
    score_stage(stages[0], 0)
    for n, group in enumerate(stages):
        if n + 1 < len(stages):
            score_stage(stages[n + 1], (n + 1) % 2)
        value_stage(group, n % 2)

    def comb(r, carry):
        m0, m1, m2 = max_scr[0, r], max_scr[1, r], max_scr[2, r]
        m = jnp.maximum(jnp.maximum(m0, m1), m2)
        w0, w1, w2 = jnp.exp2(m0 - m), jnp.exp2(m1 - m), jnp.exp2(m2 - m)
        num = w0 * acc_scr[0, r] + w1 * acc_scr[1, r] + w2 * acc_scr[2, r]
        den = w0 * sum_scr[0, r] + w1 * sum_scr[1, r] + w2 * sum_scr[2, r]
        o_ref[r] = num / den
        return carry

    lax.fori_loop(0, NS, comb, 0, unroll=2)


def _attention(q, k, v, q1, k1, v1):
    m16, m4, m1 = _attn_masks()
    cur = pl.BlockSpec((None, NS, TA, LANES), lambda j, i: (j, 0, i, 0))
    prev = pl.BlockSpec((None, NS, TA, LANES), lambda j, i: (j, 0, jnp.maximum(i - 1, 0), 0))
    cur1 = pl.BlockSpec((None, NS * TA, LANES), lambda j, i: (j, i, 0))
    prev1 = pl.BlockSpec((None, UQ, LANES),
                         lambda j, i: (j, jnp.maximum(i * (NS * TA // UQ) - 1, 0), 0))

    def tile_kind(m):
        return pl.BlockSpec((None,) + m.shape[1:],
                            lambda j, i: (jnp.minimum(i, 1),) + (0,) * (m.ndim - 1))

    return pl.pallas_call(
        _attn_body,
        grid=(NSLAB, RA // TA),
        in_specs=[cur, cur, prev, cur, prev, cur1, cur1, prev1, cur1, prev1,
                  tile_kind(m16), tile_kind(m4), tile_kind(m1)],
        out_specs=pl.BlockSpec((None, NS, TA, LANES), lambda j, i: (j, 0, i, 0)),
        out_shape=jax.ShapeDtypeStruct((NSLAB, NS, RA, LANES), F32),
        scratch_shapes=[
            pltpu.VMEM((2, NS, TA, LANES), BF16),
            pltpu.VMEM((2, NS, TA, LANES), BF16),
            pltpu.VMEM((2, NS, TA, LANES), BF16),
            pltpu.VMEM((2, NS * TA, LANES), BF16),
            pltpu.VMEM((UQ + NS * TA, LANES), BF16),
            pltpu.VMEM((2, UQ + NS * TA, LANES), BF16),
            pltpu.VMEM((2, GROUP, 2, UQ, UK), F32),
            pltpu.VMEM((3, NS, TA, LANES), F32),
            pltpu.VMEM((3, NS, TA, LANES), F32),
            pltpu.VMEM((3, NS, TA, LANES), F32),
        ],
        compiler_params=pltpu.CompilerParams(
            dimension_semantics=("arbitrary", "arbitrary"), vmem_limit_bytes=VMEM_LIMIT),
        name="dilated_attn",
    )(q, k, k, v, v, q1, k1, k1, v1, v1, m16, m4, m1)


TL = 64


def _lru_body(xl_ref, gl_ref, cw_ref, cb_ref, wg_ref, bg_ref, lam_ref, y_ref,
              tail_scr, carry_scr, xc_scr, g_scr, pl_scr, hl_scr):
    step = pl.program_id(0)

    @pl.when(step == 0)
    def _():
        tail_scr[...] = jnp.zeros_like(tail_scr)
        carry_scr[...] = jnp.zeros_like(carry_scr)

    row = lax.broadcasted_iota(jnp.int32, (TL, LRU_WIDTH), 0)
    cw = cw_ref[...]
    cb = cb_ref[...]

    def tap(r, k):
        st = r - k
        if st >= 0:
            return xl_ref[st]
        st += NS
        prev_last = tail_scr[st - (NS - CONV_WIDTH + 1), 7:8, :]
        return jnp.where(row == 0, prev_last, pltpu.roll(xl_ref[st], 1, axis=0))

    for r in range(NS):
        xc = cb + sum(cw[CONV_WIDTH - 1 - k:CONV_WIDTH - k] * tap(r, k) for k in range(CONV_WIDTH))
        xc_scr[r] = xc
    for n in range(CONV_WIDTH - 1):
        tail_scr[n] = xl_ref[NS - CONV_WIDTH + 1 + n, TL - 8:TL, :]

    xc_all = xc_scr[...].reshape(NS * TL, LRU_WIDTH)
    g_scr[...] = jnp.dot(xc_all.astype(BF16), wg_ref[...], preferred_element_type=F32) + bg_ref[...]

    neg_lam = -lam_ref[...]
    softplus = jnp.maximum(neg_lam, 0.0) + jnp.log1p(jnp.exp(-jnp.abs(neg_lam)))
    p_run = None
    h_run = None
    for r in range(NS):
        g = g_scr[r * TL:(r + 1) * TL]
        xc = xc_scr[r]
        rg = jax.nn.sigmoid(g[:, :LRU_WIDTH])
        ig = jax.nn.sigmoid(g[:, LRU_WIDTH:])
        log_a = -LRU_C * rg * softplus
        a = jnp.exp(log_a)
        th = jnp.tanh(log_a)
        bx = jnp.sqrt(-2.0 * th / (1.0 - th)) * (ig * xc)
        if r == 0:
            p_run, h_run = a, bx
        else:
            h_run = a * h_run + bx
            p_run = a * p_run
        pl_scr[r] = p_run
        hl_scr[r] = h_run

    pa, hb = p_run, h_run
    sft = 1
    while sft < TL:
        keep = row >= sft
        pa_s = jnp.where(keep, pltpu.roll(pa, sft, axis=0), 1.0)
        hb_s = jnp.where(keep, pltpu.roll(hb, sft, axis=0), 0.0)
        hb = pa * hb_s + hb
        pa = pa * pa_s
        sft *= 2
    carry = carry_scr[0:1, :]
    e = pa * carry + hb
    e_prev = jnp.where(row == 0, carry, pltpu.roll(e, 1, axis=0))
    carry_scr[...] = jnp.broadcast_to(e[TL - 1:TL, :], carry_scr.shape)

    for r in range(NS):
        h = pl_scr[r] * e_prev + hl_scr[r]
        y_ref[r] = jax.nn.gelu(gl_ref[r], approximate=True) * h


def _lru(xl, gl, conv_w, conv_b, w_gate, b_gate, lam):
    blk = pl.BlockSpec((NS, TL, LRU_WIDTH), lambda i: (0, i, 0))

    def const(a):
        return pl.BlockSpec(a.shape, lambda i: (0,) * a.ndim)

    return pl.pallas_call(
        _lru_body,
        grid=(RA // TL,),
        in_specs=[blk, blk, const(conv_w), const(conv_b), const(w_gate), const(b_gate), const(lam)],
        out_specs=blk,
        out_shape=jax.ShapeDtypeStruct((NS, RA, LRU_WIDTH), F32),
        scratch_shapes=[
            pltpu.VMEM((CONV_WIDTH - 1, 8, LRU_WIDTH), F32),
            pltpu.VMEM((8, LRU_WIDTH), F32),
            pltpu.VMEM((NS, TL, LRU_WIDTH), F32),
            pltpu.VMEM((NS * TL, 2 * LRU_WIDTH), F32),
            pltpu.VMEM((NS, TL, LRU_WIDTH), F32),
            pltpu.VMEM((NS, TL, LRU_WIDTH), F32),
        ],
        compiler_params=pltpu.CompilerParams(
            dimension_semantics=("arbitrary",), vmem_limit_bytes=VMEM_LIMIT),
        name="rg_lru",
    )(xl, gl, conv_w, conv_b, w_gate, b_gate, lam)


TM4 = 512
NG4 = TM4 // PG
FF_CHUNK = 1024


def _out_body(x_ref, at_ref, lr_ref, perm_ref, ga_ref, gl_ref, wo_ref, g2_ref, wu_ref, wd_ref,
              gf_ref, o_ref):
    groups = []
    for g in range(NG4):
        rows = slice(g * RG, (g + 1) * RG)
        attn = jnp.concatenate(
            [jnp.concatenate([at_ref[j, r, rows] for j in range(NSLAB)], axis=1) for r in range(NS)],
            axis=0)
        lru = jnp.concatenate([lr_ref[r, rows] for r in range(NS)], axis=0)
        mixed = jnp.concatenate([_rms(attn, ga_ref[...]), _rms(lru, gl_ref[...])], axis=1)
        groups.append(jnp.dot(perm_ref[...], mixed.astype(BF16),
                              preferred_element_type=F32).astype(BF16))
    mixed = jnp.concatenate(groups, axis=0)
    h = x_ref[...] + jnp.dot(mixed, wo_ref[...], preferred_element_type=F32)
    u = _rms(h, g2_ref[...]).astype(BF16)
    acc = h
    for c in range(D_FF // FF_CHUNK):
        sl = slice(c * FF_CHUNK, (c + 1) * FF_CHUNK)
        f = jnp.dot(u, wu_ref[:, sl], preferred_element_type=F32)
        f = jnp.square(jnp.maximum(f, 0.0)).astype(BF16)
        acc = acc + jnp.dot(f, wd_ref[sl, :], preferred_element_type=F32)
    o_ref[...] = _rms(acc, gf_ref[...])


def _out_mlp(x2, attn, lru, ga, gl, wo, g2, wu, wd, gf):
    def const(a):
        return pl.BlockSpec(a.shape, lambda i: (0,) * a.ndim, pipeline_mode=pl.Buffered(1))

    rows = TM4 // NS
    xspec = pl.BlockSpec((TM4, D_MODEL), lambda i: (i, 0))
    perm = jnp.asarray(_group_perm(), BF16)
    return pl.pallas_call(
        _out_body,
        grid=(SEQ // TM4,),
        in_specs=[
            xspec,
            pl.BlockSpec((NSLAB, NS, rows, LANES), lambda i: (0, 0, i, 0)),
            pl.BlockSpec((NS, rows, LRU_WIDTH), lambda i: (0, i, 0)),
            const(perm), const(ga), const(gl), const(wo), const(g2), const(wu), const(wd), const(gf),
        ],
        out_specs=xspec,
        out_shape=jax.ShapeDtypeStruct((SEQ, D_MODEL), F32),
        compiler_params=pltpu.CompilerParams(
            dimension_semantics=("arbitrary",), vmem_limit_bytes=VMEM_LIMIT),
        name="out_mlp",
    )(x2, attn, lru, perm, ga, gl, wo, g2, wu, wd, gf)


@functools.lru_cache(maxsize=None)
def _rope_tables():
    pos = np.arange(SEQ, dtype=np.float64)
    inv_freq = ROPE_THETA ** (-np.arange(0, ROT_DIM, 2, dtype=np.float64) / ROT_DIM)
    ang = pos[:, None] * inv_freq[None, :]
    cos, sin = np.cos(ang), np.sin(ang)
    pad = HEAD_DIM - ROT_DIM
    c_head = np.concatenate([cos, cos, np.ones((SEQ, pad))], axis=1)
    s_head = np.concatenate([-sin, sin, np.zeros((SEQ, pad))], axis=1)

    def lay(t):
        t = np.tile(t, (1, LANES // HEAD_DIM)).astype(np.float32)
        t = t.reshape(SEQ // PG, RG, NS, LANES).transpose(0, 2, 1, 3)
        return np.ascontiguousarray(t.reshape(SEQ, LANES))

    return lay(c_head), lay(s_head)


def _block_diag(w):
    n, b, _ = w.shape
    eye = jnp.eye(n, dtype=w.dtype)
    return (eye[:, None, :, None] * w[:, :, None, :]).reshape(n * b, n * b)


def kernel(x, norm1_g, w_in, conv_w, conv_b, w_rgate, b_rgate, w_igate, b_igate, lru_lambda,
           attn_out_g, lru_out_g, w_out, norm2_g, w_mlp_up, w_mlp_down, final_g):
    assert x.shape == (1, SEQ, D_MODEL) and w_in.shape[0] == 1
    x2 = x.reshape(SEQ, D_MODEL)
    cos_t, sin_t = _rope_tables()
    q, k, v, q1, k1, v1, xl, gl = _inproj(x2, norm1_g.reshape(1, D_MODEL), w_in[0].astype(BF16), cos_t, sin_t)
    attn = _attention(q, k, v, q1, k1, v1)
    w_gate = jnp.concatenate([_block_diag(w_rgate[0]), _block_diag(w_igate[0])], axis=1).astype(BF16)
    b_gate = jnp.concatenate([b_rgate[0].reshape(1, -1), b_igate[0].reshape(1, -1)], axis=1)
    lru = _lru(xl, gl, conv_w[0], conv_b[0].reshape(1, -1), w_gate, b_gate, lru_lambda[0].reshape(1, -1))
    out = _out_mlp(x2, attn, lru, attn_out_g[0].reshape(1, -1), lru_out_g[0].reshape(1, -1),
                   w_out[0].astype(BF16), norm2_g[0].reshape(1, -1),
                   w_mlp_up[0].astype(BF16), w_mlp_down[0].astype(BF16), final_g.reshape(1, -1))
    return out.reshape(1, SEQ, D_MODEL)
```

```python
import functools

import numpy as np
import jax
import jax.numpy as jnp
from jax import lax
from jax.experimental import pallas as pl
from jax.experimental.pallas import tpu as pltpu

F32 = jnp.float32
BF16 = jnp.bfloat16

D_MODEL = 1024
SEQ = 16384
ATTN_HEADS = 8
HEAD_DIM = 64
ATTN_WIDTH = ATTN_HEADS * HEAD_DIM
ROT_DIM = HEAD_DIM // 4
ROPE_THETA = 500000.0
LRU_WIDTH = D_MODEL - ATTN_WIDTH
LRU_BLOCKS = 8
CONV_WIDTH = 4
LRU_C = 8.0
IN_WIDTH = 3 * ATTN_WIDTH + 2 * LRU_WIDTH
D_FF = 4 * D_MODEL
EPS = 1e-6
WINDOW_STEPS = 128

NS = 16
RA = SEQ // NS
LANES = 128
NSLAB = ATTN_WIDTH // LANES
NEG = -1e30

VMEM_LIMIT = 56 * 1024 * 1024


def _rms(x, g):
    return x * lax.rsqrt(jnp.mean(x * x, axis=-1, keepdims=True) + EPS) * g


TM1 = 512
PG = NS * NS
RG = PG // NS
NG1 = TM1 // PG
CH = 8
LOG2E = 1.4426950408889634


@functools.lru_cache(maxsize=None)
def _group_perm():
    n = np.arange(PG)
    p = np.zeros((PG, PG), np.float32)
    p[(n % NS) * RG + n // NS, n] = 1.0
    return p


def _inproj_body(x_ref, g_ref, w_ref, perm_ref, cos_ref, sin_ref,
                 q_ref, k_ref, v_ref, q1_ref, k1_ref, v1_ref, xl_ref, gl_ref):
    u = _rms(x_ref[...], g_ref[...]).astype(BF16)
    u = jnp.concatenate(
        [jnp.dot(perm_ref[...], u[g * PG:(g + 1) * PG], preferred_element_type=F32).astype(BF16)
         for g in range(NG1)], axis=0)
    z = jnp.dot(u, w_ref[...], preferred_element_type=F32)
    c = jnp.concatenate([cos_ref[...]] * NSLAB, axis=1)
    s = jnp.concatenate([sin_ref[...]] * NSLAB, axis=1)
    lane = lax.broadcasted_iota(jnp.int32, (1, ATTN_WIDTH), 1) % HEAD_DIM
    first_half = lane < ROT_DIM // 2

    def rope(t):
        up = pltpu.roll(t, ATTN_WIDTH - ROT_DIM // 2, axis=1)
        dn = pltpu.roll(t, ROT_DIM // 2, axis=1)
        return t * c + jnp.where(first_half, up, dn) * s

    q = rope(z[:, :ATTN_WIDTH]) * (HEAD_DIM ** -0.5 * LOG2E)
    k = rope(z[:, ATTN_WIDTH:2 * ATTN_WIDTH])
    v = z[:, 2 * ATTN_WIDTH:3 * ATTN_WIDTH]
    xl = z[:, 3 * ATTN_WIDTH:3 * ATTN_WIDTH + LRU_WIDTH]
    gl = z[:, 3 * ATTN_WIDTH + LRU_WIDTH:]
    for g in range(NG1):
        for r in range(NS):
            src = slice(g * PG + r * RG, g * PG + (r + 1) * RG)
            dst = slice(g * RG, (g + 1) * RG)
            for j in range(NSLAB):
                sl = slice(j * LANES, (j + 1) * LANES)
                q_ref[j, r, dst] = q[src, sl].astype(BF16)
                k_ref[j, r, dst] = k[src, sl].astype(BF16)
                v_ref[j, r, dst] = v[src, sl].astype(BF16)
            xl_ref[r, dst] = xl[src]
            gl_ref[r, dst] = gl[src]
    for g in range(NG1):
        for cl in range(RG // CH):
            for r in range(0, NS, 2):
                lo = g * PG + r * RG + cl * CH
                hi = lo + RG
                d0 = (g * (RG // CH) + cl) * NS * CH + r * CH
                dst = slice(d0, d0 + 2 * CH)
                for j in range(NSLAB):
                    sl = slice(j * LANES, (j + 1) * LANES)
                    for src, ref in ((q, q1_ref), (k, k1_ref), (v, v1_ref)):
                        ref[j, dst] = jnp.concatenate(
                            [src[lo:lo + CH, sl], src[hi:hi + CH, sl]], axis=0).astype(BF16)


def _inproj(x2, g, w, cos_t, sin_t):
    rows = TM1 // NS
    slab_spec = pl.BlockSpec((NSLAB, NS, rows, LANES), lambda i: (0, 0, i, 0))
    chunk_spec = pl.BlockSpec((NSLAB, TM1, LANES), lambda i: (0, i, 0))
    row_spec = pl.BlockSpec((NS, rows, LRU_WIDTH), lambda i: (0, i, 0))
    tab_spec = pl.BlockSpec((TM1, LANES), lambda i: (i, 0))
    slab_shape = jax.ShapeDtypeStruct((NSLAB, NS, RA, LANES), BF16)
    chunk_shape = jax.ShapeDtypeStruct((NSLAB, SEQ, LANES), BF16)
    row_shape = jax.ShapeDtypeStruct((NS, RA, LRU_WIDTH), F32)
    perm = jnp.asarray(_group_perm(), BF16)
    return pl.pallas_call(
        _inproj_body,
        grid=(SEQ // TM1,),
        in_specs=[
            pl.BlockSpec((TM1, D_MODEL), lambda i: (i, 0)),
            pl.BlockSpec((1, D_MODEL), lambda i: (0, 0)),
            pl.BlockSpec((D_MODEL, IN_WIDTH), lambda i: (0, 0)),
            pl.BlockSpec((PG, PG), lambda i: (0, 0)),
            tab_spec, tab_spec,
        ],
        out_specs=[slab_spec] * 3 + [chunk_spec] * 3 + [row_spec] * 2,
        out_shape=[slab_shape] * 3 + [chunk_shape] * 3 + [row_shape] * 2,
        compiler_params=pltpu.CompilerParams(
            dimension_semantics=("arbitrary",), vmem_limit_bytes=VMEM_LIMIT),
        name="inproj",
    )(x2, g, w, perm, cos_t, sin_t)


TA = 128
UQ, UK = 128, 256
Q4, K4 = UQ // 4, UK // 4
GROUP = 2


@functools.lru_cache(maxsize=None)
def _attn_masks():
    def variants(diff, from_prev, per_unit):
        ok = (diff >= 0) & (diff <= WINDOW_STEPS)
        normal = np.where(ok, 0.0, NEG).astype(np.float32)
        first = np.where(ok & ~from_prev, 0.0, NEG).astype(np.float32)
        if per_unit:
            return np.stack([np.stack([first, normal]), np.stack([normal, normal])])
        return np.stack([first, normal])

    iq = np.arange(UQ)[:, None]
    ck = np.arange(UK)[None, :]
    m16 = variants(iq - ck + TA, np.broadcast_to(ck < TA, (UQ, UK)), False)
    cq, i4 = np.divmod(np.arange(UQ), Q4)
    ckk, j4 = np.divmod(np.arange(UK), K4)
    d4 = 4 * (i4[:, None] - j4[None, :] + Q4) + (cq[:, None] - ckk[None, :])
    m4 = variants(d4, np.broadcast_to(j4[None, :] < K4 - Q4, d4.shape), True)
    rq, i1 = np.divmod(np.arange(UQ), CH)
    kc, kin = np.divmod(np.arange(UK), UQ)
    rk, j1 = np.divmod(kin, CH)
    d1 = (NS * i1 + rq)[:, None] - (NS * j1 + rk + UQ * (kc - 1))[None, :]
    m1 = variants(d1, np.broadcast_to(kc[None, :] == 0, d1.shape), True)
    return m16, m4, m1


def _aligned(x, m):
    return x if isinstance(x, int) else pl.multiple_of(x, m)


def _attn_body(q_ref, kc_ref, kp_ref, vc_ref, vp_ref, q1_ref, k1c_ref, k1p_ref, v1c_ref, v1p_ref,
               m16_ref, m4_ref, m1_ref, o_ref,
               qm_scr, vcm_scr, vpm_scr, q1m_scr, k1_scr, v1m_scr, s_scr, acc_scr, max_scr, sum_scr):
    lane = lax.broadcasted_iota(jnp.int32, (1, LANES), 1)
    head0 = lane < HEAD_DIM
    zero = jnp.zeros((), BF16)

    def split(x):
        return jnp.where(head0, x, zero), jnp.where(head0, zero, x)

    qm_scr[0], qm_scr[1] = split(q_ref[...])
    vcm_scr[0], vcm_scr[1] = split(vc_ref[...])
    vpm_scr[0], vpm_scr[1] = split(vp_ref[...])
    q1m_scr[0], q1m_scr[1] = split(q1_ref[...])
    k1_scr[0:UQ] = k1p_ref[...]
    k1_scr[UQ:] = k1c_ref[...]
    v1m_scr[0, 0:UQ], v1m_scr[1, 0:UQ] = split(v1p_ref[...])
    v1m_scr[0, UQ:], v1m_scr[1, UQ:] = split(v1c_ref[...])

    def unit16(r):
        def store(o, mx, sm):
            acc_scr[0, r], max_scr[0, r], sum_scr[0, r] = o, mx, sm

        return (lambda h: qm_scr[h, r],
                lambda: jnp.concatenate([kp_ref[r], kc_ref[r]], axis=0),
                lambda h: jnp.concatenate([vpm_scr[h, r], vcm_scr[h, r]], axis=0),
                lambda: m16_ref[...], store)

    def unit4(r4, b):
        streams = [r4 + 4 * c for c in range(4)]
        q0, k0 = b * Q4, b * Q4 + Q4 - K4

        def window(cur, prev, ix):
            if k0 < 0:
                return [prev[ix + (slice(TA + k0, TA),)], cur[ix + (slice(0, k0 + K4),)]]
            return [cur[ix + (slice(k0, k0 + K4),)]]

        def store(o, mx, sm):
            for c, st in enumerate(streams):
                rows = slice(c * Q4, (c + 1) * Q4)
                acc_scr[1, st, q0:q0 + Q4] = o[rows]
                max_scr[1, st, q0:q0 + Q4] = mx[rows]
                sum_scr[1, st, q0:q0 + Q4] = sm[rows]

        return (lambda h: jnp.concatenate([qm_scr[h, st, q0:q0 + Q4] for st in streams], axis=0),
                lambda: jnp.concatenate(
                    [x for st in streams for x in window(kc_ref, kp_ref, (st,))], axis=0),
                lambda h: jnp.concatenate(
                    [x for st in streams for x in window(vcm_scr, vpm_scr, (h, st))], axis=0),
                lambda: m4_ref[min(b, 1)], store)

    def unit1(u):
        off = _aligned(u * UQ, UQ)
        row = _aligned(u * CH, CH)

        def store(o, mx, sm):
            for r in range(NS):
                rows = slice(r * CH, (r + 1) * CH)
                acc_scr[2, r, pl.ds(row, CH)] = o[rows]
                max_scr[2, r, pl.ds(row, CH)] = mx[rows]
                sum_scr[2, r, pl.ds(row, CH)] = sm[rows]

        return (lambda h: q1m_scr[h, pl.ds(off, UQ)],
                lambda: k1_scr[pl.ds(off, UK)],
                lambda h: v1m_scr[h, pl.ds(off, UK)],
                lambda: m1_ref[min(u, 1) if isinstance(u, int) else jnp.minimum(u, 1)], store)

    all_units = ([unit16(n) for n in range(NS)]
                 + [unit4(n // (TA // Q4), n % (TA // Q4)) for n in range(NS)]
                 + [unit1(n) for n in range(NS)])

    def score_stage(group, slot):
        for g, (q, k, _, _, _) in enumerate(group):
            kb = k()
            for h in range(2):
                s_scr[slot, g, h] = lax.dot_general(
                    q(h), kb, (((1,), (1,)), ((), ())), preferred_element_type=F32)

    def value_stage(group, slot):
        for g, (_, _, v, mask, store) in enumerate(group):
            o = None
            mx, sm = [], []
            for h in range(2):
                s = s_scr[slot, g, h] + mask()
                m = jnp.max(s, axis=-1, keepdims=True)
                p = jnp.exp2(s - m)
                mx.append(m)
                sm.append(jnp.sum(p, axis=-1, keepdims=True))
                pv = jnp.dot(p.astype(BF16), v(h), preferred_element_type=F32)
                o = pv if o is None else o + pv
            store(o, jnp.where(head0, mx[0], mx[1]), jnp.where(head0, sm[0], sm[1]))

    stages = [all_units[n:n + GROUP] for n in range(0, len(all_units), GROUP)]
    score_stage(stages[0], 0)
    for n, group in enumerate(stages):
        if n + 1 < len(stages):
            score_stage(stages[n + 1], (n + 1) % 2)
        value_stage(group, n % 2)

    def comb(r, carry):
        m0, m1, m2 = max_scr[0, r], max_scr[1, r], max_scr[2, r]
        m = jnp.maximum(jnp.maximum(m0, m1), m2)
        w0, w1, w2 = jnp.exp2(m0 - m), jnp.exp2(m1 - m), jnp.exp2(m2 - m)
        num = w0 * acc_scr[0, r] + w1 * acc_scr[1, r] + w2 * acc_scr[2, r]
        den = w0 * sum_scr[0, r] + w1 * sum_scr[1, r] + w2 * sum_scr[2, r]
        o_ref[r] = num / den
        return carry

    lax.fori_loop(0, NS, comb, 0, unroll=2)


def _attention(q, k, v, q1, k1, v1):
    m16, m4, m1 = _attn_masks()
    cur = pl.BlockSpec((None, NS, TA, LANES), lambda j, i: (j, 0, i, 0))
    prev = pl.BlockSpec((None, NS, TA, LANES), lambda j, i: (j, 0, jnp.maximum(i - 1, 0), 0))
    cur1 = pl.BlockSpec((None, NS * TA, LANES), lambda j, i: (j, i, 0))
    prev1 = pl.BlockSpec((None, UQ, LANES),
                         lambda j, i: (j, jnp.maximum(i * (NS * TA // UQ) - 1, 0), 0))

    def tile_kind(m):
        return pl.BlockSpec((None,) + m.shape[1:],
                            lambda j, i: (jnp.minimum(i, 1),) + (0,) * (m.ndim - 1))

    return pl.pallas_call(
        _attn_body,
        grid=(NSLAB, RA // TA),
        in_specs=[cur, cur, prev, cur, prev, cur1, cur1, prev1, cur1, prev1,
                  tile_kind(m16), tile_kind(m4), tile_kind(m1)],
        out_specs=pl.BlockSpec((None, NS, TA, LANES), lambda j, i: (j, 0, i, 0)),
        out_shape=jax.ShapeDtypeStruct((NSLAB, NS, RA, LANES), F32),
        scratch_shapes=[
            pltpu.VMEM((2, NS, TA, LANES), BF16),
            pltpu.VMEM((2, NS, TA, LANES), BF16),
            pltpu.VMEM((2, NS, TA, LANES), BF16),
            pltpu.VMEM((2, NS * TA, LANES), BF16),
            pltpu.VMEM((UQ + NS * TA, LANES), BF16),
            pltpu.VMEM((2, UQ + NS * TA, LANES), BF16),
            pltpu.VMEM((2, GROUP, 2, UQ, UK), F32),
            pltpu.VMEM((3, NS, TA, LANES), F32),
            pltpu.VMEM((3, NS, TA, LANES), F32),
            pltpu.VMEM((3, NS, TA, LANES), F32),
        ],
        compiler_params=pltpu.CompilerParams(
            dimension_semantics=("arbitrary", "arbitrary"), vmem_limit_bytes=VMEM_LIMIT),
        name="dilated_attn",
    )(q, k, k, v, v, q1, k1, k1, v1, v1, m16, m4, m1)


TL = 64


def _lru_body(xl_ref, gl_ref, cw_ref, cb_ref, wg_ref, bg_ref, lam_ref, y_ref,
              tail_scr, carry_scr, xc_scr, g_scr, pl_scr, hl_scr):
    step = pl.program_id(0)

    @pl.when(step == 0)
    def _():
        tail_scr[...] = jnp.zeros_like(tail_scr)
        carry_scr[...] = jnp.zeros_like(carry_scr)

    row = lax.broadcasted_iota(jnp.int32, (TL, LRU_WIDTH), 0)
    cw = cw_ref[...]
    cb = cb_ref[...]

    def tap(r, k):
        st = r - k
        if st >= 0:
            return xl_ref[st]
        st += NS
        prev_last = tail_scr[st - (NS - CONV_WIDTH + 1), 7:8, :]
        return jnp.where(row == 0, prev_last, pltpu.roll(xl_ref[st], 1, axis=0))

    for r in range(NS):
        xc = cb + sum(cw[CONV_WIDTH - 1 - k:CONV_WIDTH - k] * tap(r, k) for k in range(CONV_WIDTH))
        xc_scr[r] = xc
    for n in range(CONV_WIDTH - 1):
        tail_scr[n] = xl_ref[NS - CONV_WIDTH + 1 + n, TL - 8:TL, :]

    xc_all = xc_scr[...].reshape(NS * TL, LRU_WIDTH)
    g_scr[...] = jnp.dot(xc_all.astype(BF16), wg_ref[...], preferred_element_type=F32) + bg_ref[...]

    neg_lam = -lam_ref[...]
    softplus = jnp.maximum(neg_lam, 0.0) + jnp.log1p(jnp.exp(-jnp.abs(neg_lam)))
    p_run = None
    h_run = None
    for r in range(NS):
        g = g_scr[r * TL:(r + 1) * TL]
        xc = xc_scr[r]
        rg = jax.nn.sigmoid(g[:, :LRU_WIDTH])
        ig = jax.nn.sigmoid(g[:, LRU_WIDTH:])
        log_a = -LRU_C * rg * softplus
        a = jnp.exp(log_a)
        th = jnp.tanh(log_a)
        bx = jnp.sqrt(-2.0 * th / (1.0 - th)) * (ig * xc)
        if r == 0:
            p_run, h_run = a, bx
        else:
            h_run = a * h_run + bx
            p_run = a * p_run
        pl_scr[r] = p_run
        hl_scr[r] = h_run

    pa, hb = p_run, h_run
    sft = 1
    while sft < TL:
        keep = row >= sft
        pa_s = jnp.where(keep, pltpu.roll(pa, sft, axis=0), 1.0)
        hb_s = jnp.where(keep, pltpu.roll(hb, sft, axis=0), 0.0)
        hb = pa * hb_s + hb
        pa = pa * pa_s
        sft *= 2
    carry = carry_scr[0:1, :]
    e = pa * carry + hb
    e_prev = jnp.where(row == 0, carry, pltpu.roll(e, 1, axis=0))
    carry_scr[...] = jnp.broadcast_to(e[TL - 1:TL, :], carry_scr.shape)

    for r in range(NS):
        h = pl_scr[r] * e_prev + hl_scr[r]
        y_ref[r] = jax.nn.gelu(gl_ref[r], approximate=True) * h


def _lru(xl, gl, conv_w, conv_b, w_gate, b_gate, lam):
    blk = pl.BlockSpec((NS, TL, LRU_WIDTH), lambda i: (0, i, 0))

    def const(a):
        return pl.BlockSpec(a.shape, lambda i: (0,) * a.ndim)

    return pl.pallas_call(
        _lru_body,
        grid=(RA // TL,),
        in_specs=[blk, blk, const(conv_w), const(conv_b), const(w_gate), const(b_gate), const(lam)],
        out_specs=blk,
        out_shape=jax.ShapeDtypeStruct((NS, RA, LRU_WIDTH), F32),
        scratch_shapes=[
            pltpu.VMEM((CONV_WIDTH - 1, 8, LRU_WIDTH), F32),
            pltpu.VMEM((8, LRU_WIDTH), F32),
            pltpu.VMEM((NS, TL, LRU_WIDTH), F32),
            pltpu.VMEM((NS * TL, 2 * LRU_WIDTH), F32),
            pltpu.VMEM((NS, TL, LRU_WIDTH), F32),
            pltpu.VMEM((NS, TL, LRU_WIDTH), F32),
        ],
        compiler_params=pltpu.CompilerParams(
            dimension_semantics=("arbitrary",), vmem_limit_bytes=VMEM_LIMIT),
        name="rg_lru",
    )(xl, gl, conv_w, conv_b, w_gate, b_gate, lam)


TM4 = 512
NG4 = TM4 // PG
FF_CHUNK = 1024


def _out_body(x_ref, at_ref, lr_ref, perm_ref, ga_ref, gl_ref, wo_ref, g2_ref, wu_ref, wd_ref,
              gf_ref, o_ref):
    groups = []
    for g in range(NG4):
        rows = slice(g * RG, (g + 1) * RG)
        attn = jnp.concatenate(
            [jnp.concatenate([at_ref[j, r, rows] for j in range(NSLAB)], axis=1) for r in range(NS)],
            axis=0)
        lru = jnp.concatenate([lr_ref[r, rows] for r in range(NS)], axis=0)
        mixed = jnp.concatenate([_rms(attn, ga_ref[...]), _rms(lru, gl_ref[...])], axis=1)
        groups.append(jnp.dot(perm_ref[...], mixed.astype(BF16),
                              preferred_element_type=F32).astype(BF16))
    mixed = jnp.concatenate(groups, axis=0)
    h = x_ref[...] + jnp.dot(mixed, wo_ref[...], preferred_element_type=F32)
    u = _rms(h, g2_ref[...]).astype(BF16)
    acc = h
    for c in range(D_FF // FF_CHUNK):
        sl = slice(c * FF_CHUNK, (c + 1) * FF_CHUNK)
        f = jnp.dot(u, wu_ref[:, sl], preferred_element_type=F32)
        f = jnp.square(jnp.maximum(f, 0.0)).astype(BF16)
        acc = acc + jnp.dot(f, wd_ref[sl, :], preferred_element_type=F32)
    o_ref[...] = _rms(acc, gf_ref[...])


def _out_mlp(x2, attn, lru, ga, gl, wo, g2, wu, wd, gf):
    def const(a):
        return pl.BlockSpec(a.shape, lambda i: (0,) * a.ndim, pipeline_mode=pl.Buffered(1))

    rows = TM4 // NS
    xspec = pl.BlockSpec((TM4, D_MODEL), lambda i: (i, 0))
    perm = jnp.asarray(_group_perm(), BF16)
    return pl.pallas_call(
        _out_body,
        grid=(SEQ // TM4,),
        in_specs=[
            xspec,
            pl.BlockSpec((NSLAB, NS, rows, LANES), lambda i: (0, 0, i, 0)),
            pl.BlockSpec((NS, rows, LRU_WIDTH), lambda i: (0, i, 0)),
            const(perm), const(ga), const(gl), const(wo), const(g2), const(wu), const(wd), const(gf),
        ],
        out_specs=xspec,
        out_shape=jax.ShapeDtypeStruct((SEQ, D_MODEL), F32),
        compiler_params=pltpu.CompilerParams(
            dimension_semantics=("arbitrary",), vmem_limit_bytes=VMEM_LIMIT),
        name="out_mlp",
    )(x2, attn, lru, perm, ga, gl, wo, g2, wu, wd, gf)


@functools.lru_cache(maxsize=None)
def _rope_tables():
    pos = np.arange(SEQ, dtype=np.float64)
    inv_freq = ROPE_THETA ** (-np.arange(0, ROT_DIM, 2, dtype=np.float64) / ROT_DIM)
    ang = pos[:, None] * inv_freq[None, :]
    cos, sin = np.cos(ang), np.sin(ang)
    pad = HEAD_DIM - ROT_DIM
    c_head = np.concatenate([cos, cos, np.ones((SEQ, pad))], axis=1)
    s_head = np.concatenate([-sin, sin, np.zeros((SEQ, pad))], axis=1)

    def lay(t):
        t = np.tile(t, (1, LANES // HEAD_DIM)).astype(np.float32)
        t = t.reshape(SEQ // PG, RG, NS, LANES).transpose(0, 2, 1, 3)
        return np.ascontiguousarray(t.reshape(SEQ, LANES))

    return lay(c_head), lay(s_head)


def _block_diag(w):
    n, b, _ = w.shape
    eye = jnp.eye(n, dtype=w.dtype)
    return (eye[:, None, :, None] * w[:, :, None, :]).reshape(n * b, n * b)


def kernel(x, norm1_g, w_in, conv_w, conv_b, w_rgate, b_rgate, w_igate, b_igate, lru_lambda,
           attn_out_g, lru_out_g, w_out, norm2_g, w_mlp_up, w_mlp_down, final_g):
    assert x.shape == (1, SEQ, D_MODEL) and w_in.shape[0] == 1
    x2 = x.reshape(SEQ, D_MODEL)
    cos_t, sin_t = _rope_tables()
    q, k, v, q1, k1, v1, xl, gl = _inproj(x2, norm1_g.reshape(1, D_MODEL), w_in[0].astype(BF16), cos_t, sin_t)
    attn = _attention(q, k, v, q1, k1, v1)
    w_gate = jnp.concatenate([_block_diag(w_rgate[0]), _block_diag(w_igate[0])], axis=1).astype(BF16)
    b_gate = jnp.concatenate([b_rgate[0].reshape(1, -1), b_igate[0].reshape(1, -1)], axis=1)
    lru = _lru(xl, gl, conv_w[0], conv_b[0].reshape(1, -1), w_gate, b_gate, lru_lambda[0].reshape(1, -1))
    out = _out_mlp(x2, attn, lru, attn_out_g[0].reshape(1, -1), lru_out_g[0].reshape(1, -1),
                   w_out[0].astype(BF16), norm2_g[0].reshape(1, -1),
                   w_mlp_up[0].astype(BF16), w_mlp_down[0].astype(BF16), final_g.reshape(1, -1))
    return out.reshape(1, SEQ, D_MODEL)
```

```python
import functools

import numpy as np
import jax
import jax.numpy as jnp
from jax import lax
from jax.experimental import pallas as pl
from jax.experimental.pallas import tpu as pltpu

F32 = jnp.float32
BF16 = jnp.bfloat16

D_MODEL = 1024
SEQ = 16384
ATTN_HEADS = 8
HEAD_DIM = 64
ATTN_WIDTH = ATTN_HEADS * HEAD_DIM
ROT_DIM = HEAD_DIM // 4
ROPE_THETA = 500000.0
LRU_WIDTH = D_MODEL - ATTN_WIDTH
LRU_BLOCKS = 8
CONV_WIDTH = 4
LRU_C = 8.0
IN_WIDTH = 3 * ATTN_WIDTH + 2 * LRU_WIDTH
D_FF = 4 * D_MODEL
EPS = 1e-6
WINDOW_STEPS = 128

NS = 16
RA = SEQ // NS
LANES = 128
NSLAB = ATTN_WIDTH // LANES
NEG = -1e30

VMEM_LIMIT = 56 * 1024 * 1024


def _rms(x, g):
    return x * lax.rsqrt(jnp.mean(x * x, axis=-1, keepdims=True) + EPS) * g


TM1 = 512
PG = NS * NS
RG = PG // NS
NG1 = TM1 // PG
CH = 8
LOG2E = 1.4426950408889634


@functools.lru_cache(maxsize=None)
def _group_perm():
    n = np.arange(PG)
    p = np.zeros((PG, PG), np.float32)
    p[(n % NS) * RG + n // NS, n] = 1.0
    return p


def _inproj_body(x_ref, g_ref, w_ref, perm_ref, cos_ref, sin_ref,
                 q_ref, k_ref, v_ref, q1_ref, k1_ref, v1_ref, xl_ref, gl_ref):
    u = _rms(x_ref[...], g_ref[...]).astype(BF16)
    u = jnp.concatenate(
        [jnp.dot(perm_ref[...], u[g * PG:(g + 1) * PG], preferred_element_type=F32).astype(BF16)
         for g in range(NG1)], axis=0)
    z = jnp.dot(u, w_ref[...], preferred_element_type=F32)
    c = jnp.concatenate([cos_ref[...]] * NSLAB, axis=1)
    s = jnp.concatenate([sin_ref[...]] * NSLAB, axis=1)
    lane = lax.broadcasted_iota(jnp.int32, (1, ATTN_WIDTH), 1) % HEAD_DIM
    first_half = lane < ROT_DIM // 2

    def rope(t):
        up = pltpu.roll(t, ATTN_WIDTH - ROT_DIM // 2, axis=1)
        dn = pltpu.roll(t, ROT_DIM // 2, axis=1)
        return t * c + jnp.where(first_half, up, dn) * s

    q = rope(z[:, :ATTN_WIDTH]) * (HEAD_DIM ** -0.5 * LOG2E)
    k = rope(z[:, ATTN_WIDTH:2 * ATTN_WIDTH])
    v = z[:, 2 * ATTN_WIDTH:3 * ATTN_WIDTH]
    xl = z[:, 3 * ATTN_WIDTH:3 * ATTN_WIDTH + LRU_WIDTH]
    gl = z[:, 3 * ATTN_WIDTH + LRU_WIDTH:]
    for g in range(NG1):
        for r in range(NS):
            src = slice(g * PG + r * RG, g * PG + (r + 1) * RG)
            dst = slice(g * RG, (g + 1) * RG)
            for j in range(NSLAB):
                sl = slice(j * LANES, (j + 1) * LANES)
                q_ref[j, r, dst] = q[src, sl].astype(BF16)
                k_ref[j, r, dst] = k[src, sl].astype(BF16)
                v_ref[j, r, dst] = v[src, sl].astype(BF16)
            xl_ref[r, dst] = xl[src]
            gl_ref[r, dst] = gl[src]
    for g in range(NG1):
        for cl in range(RG // CH):
            for r in range(0, NS, 2):
                lo = g * PG + r * RG + cl * CH
                hi = lo + RG
                d0 = (g * (RG // CH) + cl) * NS * CH + r * CH
                dst = slice(d0, d0 + 2 * CH)
                for j in range(NSLAB):
                    sl = slice(j * LANES, (j + 1) * LANES)
                    for src, ref in ((q, q1_ref), (k, k1_ref), (v, v1_ref)):
                        ref[j, dst] = jnp.concatenate(
                            [src[lo:lo + CH, sl], src[hi:hi + CH, sl]], axis=0).astype(BF16)


def _inproj(x2, g, w, cos_t, sin_t):
    rows = TM1 // NS
    slab_spec = pl.BlockSpec((NSLAB, NS, rows, LANES), lambda i: (0, 0, i, 0))
    chunk_spec = pl.BlockSpec((NSLAB, TM1, LANES), lambda i: (0, i, 0))
    row_spec = pl.BlockSpec((NS, rows, LRU_WIDTH), lambda i: (0, i, 0))
    tab_spec = pl.BlockSpec((TM1, LANES), lambda i: (i, 0))
    slab_shape = jax.ShapeDtypeStruct((NSLAB, NS, RA, LANES), BF16)
    chunk_shape = jax.ShapeDtypeStruct((NSLAB, SEQ, LANES), BF16)
    row_shape = jax.ShapeDtypeStruct((NS, RA, LRU_WIDTH), F32)
    perm = jnp.asarray(_group_perm(), BF16)
    return pl.pallas_call(
        _inproj_body,
        grid=(SEQ // TM1,),
        in_specs=[
            pl.BlockSpec((TM1, D_MODEL), lambda i: (i, 0)),
            pl.BlockSpec((1, D_MODEL), lambda i: (0, 0)),
            pl.BlockSpec((D_MODEL, IN_WIDTH), lambda i: (0, 0)),
            pl.BlockSpec((PG, PG), lambda i: (0, 0)),
            tab_spec, tab_spec,
        ],
        out_specs=[slab_spec] * 3 + [chunk_spec] * 3 + [row_spec] * 2,
        out_shape=[slab_shape] * 3 + [chunk_shape] * 3 + [row_shape] * 2,
        compiler_params=pltpu.CompilerParams(
            dimension_semantics=("arbitrary",), vmem_limit_bytes=VMEM_LIMIT),
        name="inproj",
    )(x2, g, w, perm, cos_t, sin_t)


TA = 128
UQ, UK = 128, 256
Q4, K4 = UQ // 4, UK // 4
GROUP = 2


@functools.lru_cache(maxsize=None)
def _attn_masks():
    def variants(diff, from_prev, per_unit):
        ok = (diff >= 0) & (diff <= WINDOW_STEPS)
        normal = np.where(ok, 0.0, NEG).astype(np.float32)
        first = np.where(ok & ~from_prev, 0.0, NEG).astype(np.float32)
        if per_unit:
            return np.stack([np.stack([first, normal]), np.stack([normal, normal])])
        return np.stack([first, normal])

    iq = np.arange(UQ)[:, None]
    ck = np.arange(UK)[None, :]
    m16 = variants(iq - ck + TA, np.broadcast_to(ck < TA, (UQ, UK)), False)
    cq, i4 = np.divmod(np.arange(UQ), Q4)
    ckk, j4 = np.divmod(np.arange(UK), K4)
    d4 = 4 * (i4[:, None] - j4[None, :] + Q4) + (cq[:, None] - ckk[None, :])
    m4 = variants(d4, np.broadcast_to(j4[None, :] < K4 - Q4, d4.shape), True)
    rq, i1 = np.divmod(np.arange(UQ), CH)
    kc, kin = np.divmod(np.arange(UK), UQ)
    rk, j1 = np.divmod(kin, CH)
    d1 = (NS * i1 + rq)[:, None] - (NS * j1 + rk + UQ * (kc - 1))[None, :]
    m1 = variants(d1, np.broadcast_to(kc[None, :] == 0, d1.shape), True)
    return m16, m4, m1


def _aligned(x, m):
    return x if isinstance(x, int) else pl.multiple_of(x, m)


def _attn_body(q_ref, kc_ref, kp_ref, vc_ref, vp_ref, q1_ref, k1c_ref, k1p_ref, v1c_ref, v1p_ref,
               m16_ref, m4_ref, m1_ref, o_ref,
               qm_scr, vcm_scr, vpm_scr, q1m_scr, k1_scr, v1m_scr, s_scr, acc_scr, max_scr, sum_scr):
    lane = lax.broadcasted_iota(jnp.int32, (1, LANES), 1)
    head0 = lane < HEAD_DIM
    zero = jnp.zeros((), BF16)

    def split(x):
        return jnp.where(head0, x, zero), jnp.where(head0, zero, x)

    qm_scr[0], qm_scr[1] = split(q_ref[...])
    vcm_scr[0], vcm_scr[1] = split(vc_ref[...])
    vpm_scr[0], vpm_scr[1] = split(vp_ref[...])
    q1m_scr[0], q1m_scr[1] = split(q1_ref[...])
    k1_scr[0:UQ] = k1p_ref[...]
    k1_scr[UQ:] = k1c_ref[...]
    v1m_scr[0, 0:UQ], v1m_scr[1, 0:UQ] = split(v1p_ref[...])
    v1m_scr[0, UQ:], v1m_scr[1, UQ:] = split(v1c_ref[...])

    def unit16(r):
        def store(o, mx, sm):
            acc_scr[0, r], max_scr[0, r], sum_scr[0, r] = o, mx, sm

        return (lambda h: qm_scr[h, r],
                lambda: jnp.concatenate([kp_ref[r], kc_ref[r]], axis=0),
                lambda h: jnp.concatenate([vpm_scr[h, r], vcm_scr[h, r]], axis=0),
                lambda: m16_ref[...], store)

    def unit4(r4, b):
        streams = [r4 + 4 * c for c in range(4)]
        q0, k0 = b * Q4, b * Q4 + Q4 - K4

        def window(cur, prev, ix):
            if k0 < 0:
                return [prev[ix + (slice(TA + k0, TA),)], cur[ix + (slice(0, k0 + K4),)]]
            return [cur[ix + (slice(k0, k0 + K4),)]]

        def store(o, mx, sm):
            for c, st in enumerate(streams):
                rows = slice(c * Q4, (c + 1) * Q4)
                acc_scr[1, st, q0:q0 + Q4] = o[rows]
                max_scr[1, st, q0:q0 + Q4] = mx[rows]
                sum_scr[1, st, q0:q0 + Q4] = sm[rows]

        return (lambda h: jnp.concatenate([qm_scr[h, st, q0:q0 + Q4] for st in streams], axis=0),
                lambda: jnp.concatenate(
                    [x for st in streams for x in window(kc_ref, kp_ref, (st,))], axis=0),
                lambda h: jnp.concatenate(
                    [x for st in streams for x in window(vcm_scr, vpm_scr, (h, st))], axis=0),
                lambda: m4_ref[min(b, 1)], store)

    def unit1(u):
        off = _aligned(u * UQ, UQ)
        row = _aligned(u * CH, CH)

        def store(o, mx, sm):
            for r in range(NS):
                rows = slice(r * CH, (r + 1) * CH)
                acc_scr[2, r, pl.ds(row, CH)] = o[rows]
                max_scr[2, r, pl.ds(row, CH)] = mx[rows]
                sum_scr[2, r, pl.ds(row, CH)] = sm[rows]

        return (lambda h: q1m_scr[h, pl.ds(off, UQ)],
                lambda: k1_scr[pl.ds(off, UK)],
                lambda h: v1m_scr[h, pl.ds(off, UK)],
                lambda: m1_ref[min(u, 1) if isinstance(u, int) else jnp.minimum(u, 1)], store)

    all_units = ([unit16(n) for n in range(NS)]
                 + [unit4(n // (TA // Q4), n % (TA // Q4)) for n in range(NS)]
                 + [unit1(n) for n in range(NS)])

    def score_stage(group, slot):
        for g, (q, k, _, _, _) in enumerate(group):
            kb = k()
            for h in range(2):
                s_scr[slot, g, h] = lax.dot_general(
                    q(h), kb, (((1,), (1,)), ((), ())), preferred_element_type=F32)

    def value_stage(group, slot):
        for g, (_, _, v, mask, store) in enumerate(group):
            o = None
            mx, sm = [], []
            for h in range(2):
                s = s_scr[slot, g, h] + mask()
                m = jnp.max(s, axis=-1, keepdims=True)
                p = jnp.exp2(s - m)
                mx.append(m)
                sm.append(jnp.sum(p, axis=-1, keepdims=True))
                pv = jnp.dot(p.astype(BF16), v(h), preferred_element_type=F32)
                o = pv if o is None else o + pv
            store(o, jnp.where(head0, mx[0], mx[1]), jnp.where(head0, sm[0], sm[1]))

    stages = [all_units[n:n + GROUP] for n in range(0, len(all_units), GROUP)]
    score_stage(stages[0], 0)
    for n, group in enumerate(stages):
        if n + 1 < len(stages):
            score_stage(stages[n + 1], (n + 1) % 2)
        value_stage(group, n % 2)

    def comb(r, carry):
        m0, m1, m2 = max_scr[0, r], max_scr[1, r], max_scr[2, r]
        m = jnp.maximum(jnp.maximum(m0, m1), m2)
        w0, w1, w2 = jnp.exp2(m0 - m), jnp.exp2(m1 - m), jnp.exp2(m2 - m)
        num = w0 * acc_scr[0, r] + w1 * acc_scr[1, r] + w2 * acc_scr[2, r]
        den = w0 * sum_scr[0, r] + w1 * sum_scr[1, r] + w2 * sum_scr[2, r]
        o_ref[r] = num / den
        return carry

    lax.fori_loop(0, NS, comb, 0, unroll=2)


def _attention(q, k, v, q1, k1, v1):
    m16, m4, m1 = _attn_masks()
    cur = pl.BlockSpec((None, NS, TA, LANES), lambda j, i: (j, 0, i, 0))
    prev = pl.BlockSpec((None, NS, TA, LANES), lambda j, i: (j, 0, jnp.maximum(i - 1, 0), 0))
    cur1 = pl.BlockSpec((None, NS * TA, LANES), lambda j, i: (j, i, 0))
    prev1 = pl.BlockSpec((None, UQ, LANES),
                         lambda j, i: (j, jnp.maximum(i * (NS * TA // UQ) - 1, 0), 0))

    def tile_kind(m):
        return pl.BlockSpec((None,) + m.shape[1:],
                            lambda j, i: (jnp.minimum(i, 1),) + (0,) * (m.ndim - 1))

    return pl.pallas_call(
        _attn_body,
        grid=(NSLAB, RA // TA),
        in_specs=[cur, cur, prev, cur, prev, cur1, cur1, prev1, cur1, prev1,
                  tile_kind(m16), tile_kind(m4), tile_kind(m1)],
        out_specs=pl.BlockSpec((None, NS, TA, LANES), lambda j, i: (j, 0, i, 0)),
        out_shape=jax.ShapeDtypeStruct((NSLAB, NS, RA, LANES), F32),
        scratch_shapes=[
            pltpu.VMEM((2, NS, TA, LANES), BF16),
            pltpu.VMEM((2, NS, TA, LANES), BF16),
            pltpu.VMEM((2, NS, TA, LANES), BF16),
            pltpu.VMEM((2, NS * TA, LANES), BF16),
            pltpu.VMEM((UQ + NS * TA, LANES), BF16),
            pltpu.VMEM((2, UQ + NS * TA, LANES), BF16),
            pltpu.VMEM((2, GROUP, 2, UQ, UK), F32),
            pltpu.VMEM((3, NS, TA, LANES), F32),
            pltpu.VMEM((3, NS, TA, LANES), F32),
            pltpu.VMEM((3, NS, TA, LANES), F32),
        ],
        compiler_params=pltpu.CompilerParams(
            dimension_semantics=("arbitrary", "arbitrary"), vmem_limit_bytes=VMEM_LIMIT),
        name="dilated_attn",
    )(q, k, k, v, v, q1, k1, k1, v1, v1, m16, m4, m1)


TL = 32


def _lru_phases(xl_ref, gl_ref, cw_ref, cb_ref, wg_ref, bg_ref, lam_ref, y_ref,
                tail_scr, carry_scr, xc_scr, g_scr, pl_scr, hl_scr):
    row = lax.broadcasted_iota(jnp.int32, (TL, LRU_WIDTH), 0)
    state = {}

    def conv(after=0.0):
        cw = cw_ref[...]
        cb = cb_ref[...] + after

        def tap(r, k):
            st = r - k
            if st >= 0:
                return xl_ref[st]
            st += NS
            prev_last = tail_scr[st - (NS - CONV_WIDTH + 1), 7:8, :]
            return jnp.where(row == 0, prev_last, pltpu.roll(xl_ref[st], 1, axis=0))

        for r in range(NS):
            xc_scr[r] = cb + sum(cw[CONV_WIDTH - 1 - k:CONV_WIDTH - k] * tap(r, k)
                                 for k in range(CONV_WIDTH))
        for n in range(CONV_WIDTH - 1):
            tail_scr[n] = xl_ref[NS - CONV_WIDTH + 1 + n, TL - 8:TL, :]

    def gates(after=0.0):
        xc_all = xc_scr[...].reshape(NS * TL, LRU_WIDTH)
        g_scr[...] = (jnp.dot(xc_all.astype(BF16), wg_ref[...], preferred_element_type=F32)
                      + (bg_ref[...] + after))

    def local_scan(streams):
        def run(after=0.0):
            neg_lam = after - lam_ref[...]
            softplus = jnp.maximum(neg_lam, 0.0) + jnp.log1p(jnp.exp(-jnp.abs(neg_lam)))
            p_run, h_run = state.get("run", (None, None))
            for r in streams:
                g = g_scr[r * TL:(r + 1) * TL]
                rg = jax.nn.sigmoid(g[:, :LRU_WIDTH])
                ig = jax.nn.sigmoid(g[:, LRU_WIDTH:])
                log_a = -LRU_C * rg * softplus
                a = jnp.exp(log_a)
                th = jnp.tanh(log_a)
                bx = jnp.sqrt(-2.0 * th / (1.0 - th)) * (ig * xc_scr[r])
                if p_run is None:
                    p_run, h_run = a, bx
                else:
                    h_run = a * h_run + bx
                    p_run = a * p_run
                pl_scr[r] = p_run
                hl_scr[r] = h_run
            state["run"] = (p_run, h_run)
        return run

    def row_scan():
        pa, hb = state["run"]
        sft = 1
        while sft < TL:
            keep = row >= sft
            pa_s = jnp.where(keep, pltpu.roll(pa, sft, axis=0), 1.0)
            hb_s = jnp.where(keep, pltpu.roll(hb, sft, axis=0), 0.0)
            hb = pa * hb_s + hb
            pa = pa * pa_s
            sft *= 2
        carry = carry_scr[0:1, :]
        e = pa * carry + hb
        state["e_prev"] = jnp.where(row == 0, carry, pltpu.roll(e, 1, axis=0))
        carry_scr[...] = jnp.broadcast_to(e[TL - 1:TL, :], carry_scr.shape)

    def finalize(after=0.0):
        e_prev = state["e_prev"] + after
        for r in range(NS):
            h = pl_scr[r] * e_prev + hl_scr[r]
            y_ref[r] = jax.nn.gelu(gl_ref[r], approximate=True) * h

    half = NS // 2

    def second_half(after=0.0):
        local_scan(range(half, NS))(after)
        row_scan()

    return [(conv, LRU_WIDTH), (gates, 2 * LRU_WIDTH), (local_scan(range(half)), LRU_WIDTH),
            (second_half, LRU_WIDTH), (finalize, LRU_WIDTH)]


TM4 = NS * TL
NG4 = TM4 // PG
FF_CHUNK = 1024


def _out_body(x_ref, at_ref, xl0_ref, gl0_ref, xln_ref, gln_ref, zero_ref,
              cw_ref, cb_ref, wg_ref, bg_ref, lam_ref,
              perm_ref, ga_ref, gl_ref, wo_ref, g2_ref, wu_ref, wd_ref, gf_ref,
              o_ref,
              lr_scr, tail_scr, carry_scr, xc_scr, g_scr, pl_scr, hl_scr):
    lru_params = (cw_ref, cb_ref, wg_ref, bg_ref, lam_ref)
    lru_state = (tail_scr, carry_scr, xc_scr, g_scr, pl_scr, hl_scr)

    @pl.when(pl.program_id(0) == 0)
    def _():
        tail_scr[...] = jnp.zeros_like(tail_scr)
        carry_scr[...] = jnp.zeros_like(carry_scr)
        for phase, _ in _lru_phases(xl0_ref, gl0_ref, *lru_params, lr_scr, *lru_state):
            phase()

    groups = []
    for g in range(NG4):
        rows = slice(g * RG, (g + 1) * RG)
        attn = jnp.concatenate(
            [jnp.concatenate([at_ref[j, r, rows] for j in range(NSLAB)], axis=1) for r in range(NS)],
            axis=0)
        lru = jnp.concatenate([lr_scr[r, rows] for r in range(NS)], axis=0)
        mixed = jnp.concatenate([_rms(attn, ga_ref[...]), _rms(lru, gl_ref[...])], axis=1)
        groups.append(jnp.dot(perm_ref[...], mixed.astype(BF16),
                              preferred_element_type=F32).astype(BF16))
    mixed = jnp.concatenate(groups, axis=0)
    lru_next = _lru_phases(xln_ref, gln_ref, *lru_params, lr_scr, *lru_state)
    n_chunks = D_FF // FF_CHUNK
    assert len(lru_next) <= n_chunks + 1

    def zero_after(val, width):
        bits = pltpu.bitcast(val[0:1, :width], jnp.int32) & zero_ref[0:1, :width]
        return pltpu.bitcast(bits, F32)

    h = x_ref[...] + jnp.dot(mixed, wo_ref[...], preferred_element_type=F32)
    u = _rms(h, g2_ref[...]).astype(BF16)
    acc = h
    phase, width = lru_next.pop(0)
    phase(zero_after(h, width))
    for c in range(n_chunks):
        if lru_next:
            phase, width = lru_next.pop(0)
            phase(zero_after(acc, width))
        sl = slice(c * FF_CHUNK, (c + 1) * FF_CHUNK)
        f = jnp.dot(u, wu_ref[:, sl], preferred_element_type=F32)
        f = jnp.square(jnp.maximum(f, 0.0)).astype(BF16)
        acc = acc + jnp.dot(f, wd_ref[sl, :], preferred_element_type=F32)
    o_ref[...] = _rms(acc, gf_ref[...])


def _out_mlp(x2, attn, xl, gl_in, lru_params, ga, gl, wo, g2, wu, wd, gf):
    def const(a):
        return pl.BlockSpec(a.shape, lambda i: (0,) * a.ndim, pipeline_mode=pl.Buffered(1))

    n_tiles = SEQ // TM4
    xspec = pl.BlockSpec((TM4, D_MODEL), lambda i: (i, 0))
    first = pl.BlockSpec((NS, TL, LRU_WIDTH), lambda i: (0, 0, 0), pipeline_mode=pl.Buffered(1))
    ahead = pl.BlockSpec((NS, TL, LRU_WIDTH), lambda i: (0, jnp.minimum(i + 1, n_tiles - 1), 0))
    perm = jnp.asarray(_group_perm(), BF16)
    zero = jnp.zeros((8, D_MODEL), jnp.int32)
    consts = (zero,) + tuple(lru_params) + (perm, ga, gl, wo, g2, wu, wd, gf)
    return pl.pallas_call(
        _out_body,
        grid=(n_tiles,),
        in_specs=[
            xspec,
            pl.BlockSpec((NSLAB, NS, TL, LANES), lambda i: (0, 0, i, 0)),
            first, first, ahead, ahead,
        ] + [const(a) for a in consts],
        out_specs=xspec,
        out_shape=jax.ShapeDtypeStruct((SEQ, D_MODEL), F32),
        scratch_shapes=[
            pltpu.VMEM((NS, TL, LRU_WIDTH), F32),
            pltpu.VMEM((CONV_WIDTH - 1, 8, LRU_WIDTH), F32),
            pltpu.VMEM((8, LRU_WIDTH), F32),
            pltpu.VMEM((NS, TL, LRU_WIDTH), F32),
            pltpu.VMEM((NS * TL, 2 * LRU_WIDTH), F32),
            pltpu.VMEM((NS, TL, LRU_WIDTH), F32),
            pltpu.VMEM((NS, TL, LRU_WIDTH), F32),
        ],
        compiler_params=pltpu.CompilerParams(
            dimension_semantics=("arbitrary",), vmem_limit_bytes=VMEM_LIMIT),
        name="out_mlp",
    )(x2, attn, xl, gl_in, xl, gl_in, *consts)


@functools.lru_cache(maxsize=None)
def _rope_tables():
    pos = np.arange(SEQ, dtype=np.float64)
    inv_freq = ROPE_THETA ** (-np.arange(0, ROT_DIM, 2, dtype=np.float64) / ROT_DIM)
    ang = pos[:, None] * inv_freq[None, :]
    cos, sin = np.cos(ang), np.sin(ang)
    pad = HEAD_DIM - ROT_DIM
    c_head = np.concatenate([cos, cos, np.ones((SEQ, pad))], axis=1)
    s_head = np.concatenate([-sin, sin, np.zeros((SEQ, pad))], axis=1)

    def lay(t):
        t = np.tile(t, (1, LANES // HEAD_DIM)).astype(np.float32)
        t = t.reshape(SEQ // PG, RG, NS, LANES).transpose(0, 2, 1, 3)
        return np.ascontiguousarray(t.reshape(SEQ, LANES))

    return lay(c_head), lay(s_head)


def _block_diag(w):
    n, b, _ = w.shape
    eye = jnp.eye(n, dtype=w.dtype)
    return (eye[:, None, :, None] * w[:, :, None, :]).reshape(n * b, n * b)


def kernel(x, norm1_g, w_in, conv_w, conv_b, w_rgate, b_rgate, w_igate, b_igate, lru_lambda,
           attn_out_g, lru_out_g, w_out, norm2_g, w_mlp_up, w_mlp_down, final_g):
    assert x.shape == (1, SEQ, D_MODEL) and w_in.shape[0] == 1
    x2 = x.reshape(SEQ, D_MODEL)
    cos_t, sin_t = _rope_tables()
    q, k, v, q1, k1, v1, xl, gl = _inproj(x2, norm1_g.reshape(1, D_MODEL), w_in[0].astype(BF16), cos_t, sin_t)
    attn = _attention(q, k, v, q1, k1, v1)
    w_gate = jnp.concatenate([_block_diag(w_rgate[0]), _block_diag(w_igate[0])], axis=1).astype(BF16)
    b_gate = jnp.concatenate([b_rgate[0].reshape(1, -1), b_igate[0].reshape(1, -1)], axis=1)
    lru_params = (conv_w[0], conv_b[0].reshape(1, -1), w_gate, b_gate, lru_lambda[0].reshape(1, -1))
    out = _out_mlp(x2, attn, xl, gl, lru_params,
                   attn_out_g[0].reshape(1, -1), lru_out_g[0].reshape(1, -1),
                   w_out[0].astype(BF16), norm2_g[0].reshape(1, -1),
                   w_mlp_up[0].astype(BF16), w_mlp_down[0].astype(BF16), final_g.reshape(1, -1))
    return out.reshape(1, SEQ, D_MODEL)
```

```python
import functools

import numpy as np
import jax
import jax.numpy as jnp
from jax import lax
from jax.experimental import pallas as pl
from jax.experimental.pallas import tpu as pltpu

F32 = jnp.float32
BF16 = jnp.bfloat16

D_MODEL = 1024
SEQ = 16384
ATTN_HEADS = 8
HEAD_DIM = 64
ATTN_WIDTH = ATTN_HEADS * HEAD_DIM
ROT_DIM = HEAD_DIM // 4
ROPE_THETA = 500000.0
LRU_WIDTH = D_MODEL - ATTN_WIDTH
LRU_BLOCKS = 8
CONV_WIDTH = 4
LRU_C = 8.0
IN_WIDTH = 3 * ATTN_WIDTH + 2 * LRU_WIDTH
D_FF = 4 * D_MODEL
EPS = 1e-6
WINDOW_STEPS = 128

NS = 16
RA = SEQ // NS
LANES = 128
MXU_TILE = 256
NSLAB = ATTN_WIDTH // LANES
NEG = -1e30

VMEM_LIMIT = 56 * 1024 * 1024


def _rms(x, g):
    return x * lax.rsqrt(jnp.mean(x * x, axis=-1, keepdims=True) + EPS) * g


TM1 = 512
PG = NS * NS
RG = PG // NS
NG1 = TM1 // PG
CH = 8
LOG2E = 1.4426950408889634


@functools.lru_cache(maxsize=None)
def _group_perm():
    n = np.arange(PG)
    p = np.zeros((PG, PG), np.float32)
    p[(n % NS) * RG + n // NS, n] = 1.0
    return p


def _inproj_body(x_ref, g_ref, w_ref, perm_ref, cos_ref, sin_ref, wo_ref, wu_ref, wd_ref,
                 q_ref, k_ref, v_ref, q1_ref, k1_ref, v1_ref, xl_ref, gl_ref,
                 wo_bf_ref, wu_bf_ref, wd_bf_ref, w_scr):
    @pl.when(pl.program_id(0) == 0)
    def _():
        w_scr[...] = w_ref[...].astype(BF16)

    wo_bf_ref[...] = wo_ref[...].astype(BF16)
    wu_bf_ref[...] = wu_ref[...].astype(BF16)
    wd_bf_ref[...] = wd_ref[...].astype(BF16)

    u = _rms(x_ref[...], g_ref[...]).astype(BF16)
    u = jnp.concatenate(
        [jnp.dot(perm_ref[...], u[g * PG:(g + 1) * PG], preferred_element_type=F32).astype(BF16)
         for g in range(NG1)], axis=0)
    z = jnp.dot(u, w_scr[...], preferred_element_type=F32)
    c = jnp.concatenate([cos_ref[...]] * NSLAB, axis=1)
    s = jnp.concatenate([sin_ref[...]] * NSLAB, axis=1)
    lane = lax.broadcasted_iota(jnp.int32, (1, ATTN_WIDTH), 1) % HEAD_DIM
    first_half = lane < ROT_DIM // 2

    def rope(t):
        up = pltpu.roll(t, ATTN_WIDTH - ROT_DIM // 2, axis=1)
        dn = pltpu.roll(t, ROT_DIM // 2, axis=1)
        return t * c + jnp.where(first_half, up, dn) * s

    q = rope(z[:, :ATTN_WIDTH]) * (HEAD_DIM ** -0.5 * LOG2E)
    k = rope(z[:, ATTN_WIDTH:2 * ATTN_WIDTH])
    v = z[:, 2 * ATTN_WIDTH:3 * ATTN_WIDTH]
    xl = z[:, 3 * ATTN_WIDTH:3 * ATTN_WIDTH + LRU_WIDTH]
    gl = z[:, 3 * ATTN_WIDTH + LRU_WIDTH:]
    for g in range(NG1):
        for r in range(NS):
            src = slice(g * PG + r * RG, g * PG + (r + 1) * RG)
            dst = slice(g * RG, (g + 1) * RG)
            for j in range(NSLAB):
                sl = slice(j * LANES, (j + 1) * LANES)
                q_ref[j, r, dst] = q[src, sl].astype(BF16)
                k_ref[j, r, dst] = k[src, sl].astype(BF16)
                v_ref[j, r, dst] = v[src, sl].astype(BF16)
            xl_ref[r, dst] = xl[src]
            gl_ref[r, dst] = gl[src]
    for g in range(NG1):
        for cl in range(RG // CH):
            for r in range(0, NS, 2):
                lo = g * PG + r * RG + cl * CH
                hi = lo + RG
                d0 = (g * (RG // CH) + cl) * NS * CH + r * CH
                dst = slice(d0, d0 + 2 * CH)
                for j in range(NSLAB):
                    sl = slice(j * LANES, (j + 1) * LANES)
                    for src, ref in ((q, q1_ref), (k, k1_ref), (v, v1_ref)):
                        ref[j, dst] = jnp.concatenate(
                            [src[lo:lo + CH, sl], src[hi:hi + CH, sl]], axis=0).astype(BF16)


def _inproj(x2, g, w, cos_t, sin_t, later_weights):
    steps = SEQ // TM1
    rows = TM1 // NS
    cast_specs = [pl.BlockSpec((a.shape[0] // steps, a.shape[1]), lambda i: (i, 0))
                  for a in later_weights]
    cast_shapes = [jax.ShapeDtypeStruct(a.shape, BF16) for a in later_weights]
    slab_spec = pl.BlockSpec((NSLAB, NS, rows, LANES), lambda i: (0, 0, i, 0))
    chunk_spec = pl.BlockSpec((NSLAB, TM1, LANES), lambda i: (0, i, 0))
    row_spec = pl.BlockSpec((NS, rows, LRU_WIDTH), lambda i: (0, i, 0))
    tab_spec = pl.BlockSpec((TM1, LANES), lambda i: (i, 0))
    slab_shape = jax.ShapeDtypeStruct((NSLAB, NS, RA, LANES), BF16)
    chunk_shape = jax.ShapeDtypeStruct((NSLAB, SEQ, LANES), BF16)
    row_shape = jax.ShapeDtypeStruct((NS, RA, LRU_WIDTH), F32)
    perm = jnp.asarray(_group_perm(), BF16)
    return pl.pallas_call(
        _inproj_body,
        grid=(steps,),
        in_specs=[
            pl.BlockSpec((TM1, D_MODEL), lambda i: (i, 0)),
            pl.BlockSpec((1, D_MODEL), lambda i: (0, 0)),
            pl.BlockSpec((D_MODEL, IN_WIDTH), lambda i: (0, 0), pipeline_mode=pl.Buffered(1)),
            pl.BlockSpec((PG, PG), lambda i: (0, 0)),
            tab_spec, tab_spec,
        ] + cast_specs,
        out_specs=[slab_spec] * 3 + [chunk_spec] * 3 + [row_spec] * 2 + cast_specs,
        out_shape=[slab_shape] * 3 + [chunk_shape] * 3 + [row_shape] * 2 + cast_shapes,
        scratch_shapes=[pltpu.VMEM((D_MODEL, IN_WIDTH), BF16)],
        compiler_params=pltpu.CompilerParams(
            dimension_semantics=("arbitrary",), vmem_limit_bytes=VMEM_LIMIT),
        name="inproj",
    )(x2, g, w, perm, cos_t, sin_t, *later_weights)


TA = 128
UQ, UK = 128, 256
Q4, K4 = UQ // 4, UK // 4
GROUP = 2


@functools.lru_cache(maxsize=None)
def _attn_masks():
    def variants(diff, from_prev, per_unit):
        ok = (diff >= 0) & (diff <= WINDOW_STEPS)
        normal = np.where(ok, 0.0, NEG).astype(np.float32)
        first = np.where(ok & ~from_prev, 0.0, NEG).astype(np.float32)
        if per_unit:
            return np.stack([np.stack([first, normal]), np.stack([normal, normal])])
        return np.stack([first, normal])

    iq = np.arange(UQ)[:, None]
    ck = np.arange(UK)[None, :]
    m16 = variants(iq - ck + TA, np.broadcast_to(ck < TA, (UQ, UK)), False)
    cq, i4 = np.divmod(np.arange(UQ), Q4)
    ckk, j4 = np.divmod(np.arange(UK), K4)
    d4 = 4 * (i4[:, None] - j4[None, :] + Q4) + (cq[:, None] - ckk[None, :])
    m4 = variants(d4, np.broadcast_to(j4[None, :] < K4 - Q4, d4.shape), True)
    rq, i1 = np.divmod(np.arange(UQ), CH)
    kc, kin = np.divmod(np.arange(UK), UQ)
    rk, j1 = np.divmod(kin, CH)
    d1 = (NS * i1 + rq)[:, None] - (NS * j1 + rk + UQ * (kc - 1))[None, :]
    m1 = variants(d1, np.broadcast_to(kc[None, :] == 0, d1.shape), True)
    return m16, m4, m1


def _aligned(x, m):
    return x if isinstance(x, int) else pl.multiple_of(x, m)


def _attn_body(q_ref, kc_ref, kp_ref, vc_ref, vp_ref, q1_ref, k1c_ref, k1p_ref, v1c_ref, v1p_ref,
               m16_ref, m4_ref, m1_ref, o_ref,
               qm_scr, vcm_scr, vpm_scr, q1m_scr, k1_scr, v1m_scr, s_scr, acc_scr, max_scr, sum_scr):
    lane = lax.broadcasted_iota(jnp.int32, (1, LANES), 1)
    head0 = lane < HEAD_DIM
    zero = jnp.zeros((), BF16)

    def split(x):
        return jnp.where(head0, x, zero), jnp.where(head0, zero, x)

    qm_scr[0], qm_scr[1] = split(q_ref[...])
    vcm_scr[0], vcm_scr[1] = split(vc_ref[...])
    vpm_scr[0], vpm_scr[1] = split(vp_ref[...])
    q1m_scr[0], q1m_scr[1] = split(q1_ref[...])
    k1_scr[0:UQ] = k1p_ref[...]
    k1_scr[UQ:] = k1c_ref[...]
    v1m_scr[0, 0:UQ], v1m_scr[1, 0:UQ] = split(v1p_ref[...])
    v1m_scr[0, UQ:], v1m_scr[1, UQ:] = split(v1c_ref[...])

    def unit16(r):
        def store(o, mx, sm):
            acc_scr[0, r], max_scr[0, r], sum_scr[0, r] = o, mx, sm

        return (lambda h: qm_scr[h, r],
                lambda: jnp.concatenate([kp_ref[r], kc_ref[r]], axis=0),
                lambda h: jnp.concatenate([vpm_scr[h, r], vcm_scr[h, r]], axis=0),
                lambda: m16_ref[...], store)

    def unit4(r4, b):
        streams = [r4 + 4 * c for c in range(4)]
        q0, k0 = b * Q4, b * Q4 + Q4 - K4

        def window(cur, prev, ix):
            if k0 < 0:
                return [prev[ix + (slice(TA + k0, TA),)], cur[ix + (slice(0, k0 + K4),)]]
            return [cur[ix + (slice(k0, k0 + K4),)]]

        def store(o, mx, sm):
            for c, st in enumerate(streams):
                rows = slice(c * Q4, (c + 1) * Q4)
                acc_scr[1, st, q0:q0 + Q4] = o[rows]
                max_scr[1, st, q0:q0 + Q4] = mx[rows]
                sum_scr[1, st, q0:q0 + Q4] = sm[rows]

        return (lambda h: jnp.concatenate([qm_scr[h, st, q0:q0 + Q4] for st in streams], axis=0),
                lambda: jnp.concatenate(
                    [x for st in streams for x in window(kc_ref, kp_ref, (st,))], axis=0),
                lambda h: jnp.concatenate(
                    [x for st in streams for x in window(vcm_scr, vpm_scr, (h, st))], axis=0),
                lambda: m4_ref[min(b, 1)], store)

    def unit1(u):
        off = _aligned(u * UQ, UQ)
        row = _aligned(u * CH, CH)

        def store(o, mx, sm):
            for r in range(NS):
                rows = slice(r * CH, (r + 1) * CH)
                acc_scr[2, r, pl.ds(row, CH)] = o[rows]
                max_scr[2, r, pl.ds(row, CH)] = mx[rows]
                sum_scr[2, r, pl.ds(row, CH)] = sm[rows]

        return (lambda h: q1m_scr[h, pl.ds(off, UQ)],
                lambda: k1_scr[pl.ds(off, UK)],
                lambda h: v1m_scr[h, pl.ds(off, UK)],
                lambda: m1_ref[min(u, 1) if isinstance(u, int) else jnp.minimum(u, 1)], store)

    all_units = ([unit16(n) for n in range(NS)]
                 + [unit4(n // (TA // Q4), n % (TA // Q4)) for n in range(NS)]
                 + [unit1(n) for n in range(NS)])

    def score_stage(group, slot):
        for g, (q, k, _, _, _) in enumerate(group):
            kb = k()
            for h in range(2):
                s_scr[slot, g, h] = lax.dot_general(
                    q(h), kb, (((1,), (1,)), ((), ())), preferred_element_type=F32)

    def value_stage(group, slot):
        for g, (_, _, v, mask, store) in enumerate(group):
            o = None
            mx, sm = [], []
            for h in range(2):
                s = s_scr[slot, g, h] + mask()
                m = jnp.max(s, axis=-1, keepdims=True)
                p = jnp.exp2(s - m)
                mx.append(m)
                sm.append(jnp.sum(p, axis=-1, keepdims=True))
                pv = jnp.dot(p.astype(BF16), v(h), preferred_element_type=F32)
                o = pv if o is None else o + pv
            store(o, jnp.where(head0, mx[0], mx[1]), jnp.where(head0, sm[0], sm[1]))

    stages = [all_units[n:n + GROUP] for n in range(0, len(all_units), GROUP)]
    score_stage(stages[0], 0)
    for n, group in enumerate(stages):
        if n + 1 < len(stages):
            score_stage(stages[n + 1], (n + 1) % 2)
        value_stage(group, n % 2)

    def comb(r, carry):
        m0, m1, m2 = max_scr[0, r], max_scr[1, r], max_scr[2, r]
        m = jnp.maximum(jnp.maximum(m0, m1), m2)
        w0, w1, w2 = jnp.exp2(m0 - m), jnp.exp2(m1 - m), jnp.exp2(m2 - m)
        num = w0 * acc_scr[0, r] + w1 * acc_scr[1, r] + w2 * acc_scr[2, r]
        den = w0 * sum_scr[0, r] + w1 * sum_scr[1, r] + w2 * sum_scr[2, r]
        o_ref[r] = num / den
        return carry

    lax.fori_loop(0, NS, comb, 0, unroll=2)


def _attention(q, k, v, q1, k1, v1):
    m16, m4, m1 = _attn_masks()
    cur = pl.BlockSpec((None, NS, TA, LANES), lambda j, i: (j, 0, i, 0))
    prev = pl.BlockSpec((None, NS, TA, LANES), lambda j, i: (j, 0, jnp.maximum(i - 1, 0), 0))
    cur1 = pl.BlockSpec((None, NS * TA, LANES), lambda j, i: (j, i, 0))
    prev1 = pl.BlockSpec((None, UQ, LANES),
                         lambda j, i: (j, jnp.maximum(i * (NS * TA // UQ) - 1, 0), 0))

    def tile_kind(m):
        return pl.BlockSpec((None,) + m.shape[1:],
                            lambda j, i: (jnp.minimum(i, 1),) + (0,) * (m.ndim - 1))

    return pl.pallas_call(
        _attn_body,
        grid=(NSLAB, RA // TA),
        in_specs=[cur, cur, prev, cur, prev, cur1, cur1, prev1, cur1, prev1,
                  tile_kind(m16), tile_kind(m4), tile_kind(m1)],
        out_specs=pl.BlockSpec((None, NS, TA, LANES), lambda j, i: (j, 0, i, 0)),
        out_shape=jax.ShapeDtypeStruct((NSLAB, NS, RA, LANES), F32),
        scratch_shapes=[
            pltpu.VMEM((2, NS, TA, LANES), BF16),
            pltpu.VMEM((2, NS, TA, LANES), BF16),
            pltpu.VMEM((2, NS, TA, LANES), BF16),
            pltpu.VMEM((2, NS * TA, LANES), BF16),
            pltpu.VMEM((UQ + NS * TA, LANES), BF16),
            pltpu.VMEM((2, UQ + NS * TA, LANES), BF16),
            pltpu.VMEM((2, GROUP, 2, UQ, UK), F32),
            pltpu.VMEM((3, NS, TA, LANES), F32),
            pltpu.VMEM((3, NS, TA, LANES), F32),
            pltpu.VMEM((3, NS, TA, LANES), F32),
        ],
        compiler_params=pltpu.CompilerParams(
            dimension_semantics=("arbitrary", "arbitrary"), vmem_limit_bytes=VMEM_LIMIT),
        name="dilated_attn",
    )(q, k, k, v, v, q1, k1, k1, v1, v1, m16, m4, m1)


TL = 32


def _lru_phases(xl_ref, gl_ref, cw_ref, cb_ref, wg_ref, bg_ref, lam_ref, y_ref,
                tail_scr, carry_scr, xc_scr, g_scr, pl_scr, hl_scr):
    row = lax.broadcasted_iota(jnp.int32, (TL, LRU_WIDTH), 0)
    state = {}

    def conv(after=0.0):
        cw = cw_ref[...]
        cb = cb_ref[...] + after

        def tap(r, k):
            st = r - k
            if st >= 0:
                return xl_ref[st]
            st += NS
            prev_last = tail_scr[st - (NS - CONV_WIDTH + 1), 7:8, :]
            return jnp.where(row == 0, prev_last, pltpu.roll(xl_ref[st], 1, axis=0))

        for r in range(NS):
            xc_scr[r] = cb + sum(cw[CONV_WIDTH - 1 - k:CONV_WIDTH - k] * tap(r, k)
                                 for k in range(CONV_WIDTH))
        for n in range(CONV_WIDTH - 1):
            tail_scr[n] = xl_ref[NS - CONV_WIDTH + 1 + n, TL - 8:TL, :]

    def gates(after=0.0):
        xc_all = xc_scr[...].reshape(NS * TL, LRU_WIDTH).astype(BF16)
        bias = bg_ref[...] + after
        for t in range(LRU_WIDTH // MXU_TILE):
            cols = slice(t * MXU_TILE, (t + 1) * MXU_TILE)
            res = jnp.dot(xc_all[:, cols], wg_ref[t], preferred_element_type=F32)
            for gate in range(2):
                dst = slice(gate * LRU_WIDTH + t * MXU_TILE, gate * LRU_WIDTH + (t + 1) * MXU_TILE)
                g_scr[:, dst] = res[:, gate * MXU_TILE:(gate + 1) * MXU_TILE] + bias[:, dst]

    def local_scan(streams):
        def run(after=0.0):
            neg_lam = after - lam_ref[...]
            softplus = jnp.maximum(neg_lam, 0.0) + jnp.log1p(jnp.exp(-jnp.abs(neg_lam)))
            p_run, h_run = state.get("run", (None, None))
            for r in streams:
                g = g_scr[r * TL:(r + 1) * TL]
                rg = jax.nn.sigmoid(g[:, :LRU_WIDTH])
                ig = jax.nn.sigmoid(g[:, LRU_WIDTH:])
                log_a = -LRU_C * rg * softplus
                a = jnp.exp(log_a)
                th = jnp.tanh(log_a)
                bx = jnp.sqrt(-2.0 * th / (1.0 - th)) * (ig * xc_scr[r])
                if p_run is None:
                    p_run, h_run = a, bx
                else:
                    h_run = a * h_run + bx
                    p_run = a * p_run
                pl_scr[r] = p_run
                hl_scr[r] = h_run
            state["run"] = (p_run, h_run)
        return run

    def row_scan():
        pa, hb = state["run"]
        sft = 1
        while sft < TL:
            keep = row >= sft
            pa_s = jnp.where(keep, pltpu.roll(pa, sft, axis=0), 1.0)
            hb_s = jnp.where(keep, pltpu.roll(hb, sft, axis=0), 0.0)
            hb = pa * hb_s + hb
            pa = pa * pa_s
            sft *= 2
        carry = carry_scr[0:1, :]
        e = pa * carry + hb
        state["e_prev"] = jnp.where(row == 0, carry, pltpu.roll(e, 1, axis=0))
        carry_scr[...] = jnp.broadcast_to(e[TL - 1:TL, :], carry_scr.shape)

    def finalize(after=0.0):
        e_prev = state["e_prev"] + after
        for r in range(NS):
            h = pl_scr[r] * e_prev + hl_scr[r]
            y_ref[r] = jax.nn.gelu(gl_ref[r], approximate=True) * h

    half = NS // 2

    def second_half(after=0.0):
        local_scan(range(half, NS))(after)
        row_scan()

    return [(conv, LRU_WIDTH), (gates, 2 * LRU_WIDTH), (local_scan(range(half)), LRU_WIDTH),
            (second_half, LRU_WIDTH), (finalize, LRU_WIDTH)]


TM4 = NS * TL
NG4 = TM4 // PG
FF_CHUNK = 1024


def _out_body(x_ref, at_ref, xl0_ref, gl0_ref, xln_ref, gln_ref, zero_ref,
              cw_ref, cb_ref, wg_ref, bg_ref, lam_ref,
              perm_ref, ga_ref, gl_ref, wo_ref, g2_ref, wu_ref, wd_ref, gf_ref,
              o_ref,
              lr_scr, tail_scr, carry_scr, xc_scr, g_scr, pl_scr, hl_scr):
    lru_params = (cw_ref, cb_ref, wg_ref, bg_ref, lam_ref)
    lru_state = (tail_scr, carry_scr, xc_scr, g_scr, pl_scr, hl_scr)

    @pl.when(pl.program_id(0) == 0)
    def _():
        tail_scr[...] = jnp.zeros_like(tail_scr)
        carry_scr[...] = jnp.zeros_like(carry_scr)
        for phase, _ in _lru_phases(xl0_ref, gl0_ref, *lru_params, lr_scr, *lru_state):
            phase()

    groups = []
    for g in range(NG4):
        rows = slice(g * RG, (g + 1) * RG)
        attn = jnp.concatenate(
            [jnp.concatenate([at_ref[j, r, rows] for j in range(NSLAB)], axis=1) for r in range(NS)],
            axis=0)
        lru = jnp.concatenate([lr_scr[r, rows] for r in range(NS)], axis=0)
        mixed = jnp.concatenate([_rms(attn, ga_ref[...]), _rms(lru, gl_ref[...])], axis=1)
        groups.append(jnp.dot(perm_ref[...], mixed.astype(BF16),
                              preferred_element_type=F32).astype(BF16))
    mixed = jnp.concatenate(groups, axis=0)
    lru_next = _lru_phases(xln_ref, gln_ref, *lru_params, lr_scr, *lru_state)
    n_chunks = D_FF // FF_CHUNK
    assert len(lru_next) <= n_chunks + 1

    def zero_after(val, width):
        bits = pltpu.bitcast(val[0:1, :width], jnp.int32) & zero_ref[0:1, :width]
        return pltpu.bitcast(bits, F32)

    h = x_ref[...] + jnp.dot(mixed, wo_ref[...], preferred_element_type=F32)
    u = _rms(h, g2_ref[...]).astype(BF16)
    acc = h
    phase, width = lru_next.pop(0)
    phase(zero_after(h, width))
    for c in range(n_chunks):
        if lru_next:
            phase, width = lru_next.pop(0)
            phase(zero_after(acc, width))
        sl = slice(c * FF_CHUNK, (c + 1) * FF_CHUNK)
        f = jnp.dot(u, wu_ref[:, sl], preferred_element_type=F32)
        f = jnp.square(jnp.maximum(f, 0.0)).astype(BF16)
        acc = acc + jnp.dot(f, wd_ref[sl, :], preferred_element_type=F32)
    o_ref[...] = _rms(acc, gf_ref[...])


def _out_mlp(x2, attn, xl, gl_in, lru_params, ga, gl, wo, g2, wu, wd, gf):
    def const(a):
        return pl.BlockSpec(a.shape, lambda i: (0,) * a.ndim, pipeline_mode=pl.Buffered(1))

    n_tiles = SEQ // TM4
    xspec = pl.BlockSpec((TM4, D_MODEL), lambda i: (i, 0))
    first = pl.BlockSpec((NS, TL, LRU_WIDTH), lambda i: (0, 0, 0), pipeline_mode=pl.Buffered(1))
    ahead = pl.BlockSpec((NS, TL, LRU_WIDTH), lambda i: (0, jnp.minimum(i + 1, n_tiles - 1), 0))
    perm = jnp.asarray(_group_perm(), BF16)
    zero = jnp.zeros((8, D_MODEL), jnp.int32)
    consts = (zero,) + tuple(lru_params) + (perm, ga, gl, wo, g2, wu, wd, gf)
    return pl.pallas_call(
        _out_body,
        grid=(n_tiles,),
        in_specs=[
            xspec,
            pl.BlockSpec((NSLAB, NS, TL, LANES), lambda i: (0, 0, i, 0)),
            first, first, ahead, ahead,
        ] + [const(a) for a in consts],
        out_specs=xspec,
        out_shape=jax.ShapeDtypeStruct((SEQ, D_MODEL), F32),
        scratch_shapes=[
            pltpu.VMEM((NS, TL, LRU_WIDTH), F32),
            pltpu.VMEM((CONV_WIDTH - 1, 8, LRU_WIDTH), F32),
            pltpu.VMEM((8, LRU_WIDTH), F32),
            pltpu.VMEM((NS, TL, LRU_WIDTH), F32),
            pltpu.VMEM((NS * TL, 2 * LRU_WIDTH), F32),
            pltpu.VMEM((NS, TL, LRU_WIDTH), F32),
            pltpu.VMEM((NS, TL, LRU_WIDTH), F32),
        ],
        compiler_params=pltpu.CompilerParams(
            dimension_semantics=("arbitrary",), vmem_limit_bytes=VMEM_LIMIT),
        name="out_mlp",
    )(x2, attn, xl, gl_in, xl, gl_in, *consts)


@functools.lru_cache(maxsize=None)
def _rope_tables():
    pos = np.arange(SEQ, dtype=np.float64)
    inv_freq = ROPE_THETA ** (-np.arange(0, ROT_DIM, 2, dtype=np.float64) / ROT_DIM)
    ang = pos[:, None] * inv_freq[None, :]
    cos, sin = np.cos(ang), np.sin(ang)
    pad = HEAD_DIM - ROT_DIM
    c_head = np.concatenate([cos, cos, np.ones((SEQ, pad))], axis=1)
    s_head = np.concatenate([-sin, sin, np.zeros((SEQ, pad))], axis=1)

    def lay(t):
        t = np.tile(t, (1, LANES // HEAD_DIM)).astype(np.float32)
        t = t.reshape(SEQ // PG, RG, NS, LANES).transpose(0, 2, 1, 3)
        return np.ascontiguousarray(t.reshape(SEQ, LANES))

    return lay(c_head), lay(s_head)


def _block_diag(w):
    n, b, _ = w.shape
    eye = jnp.eye(n, dtype=w.dtype)
    return (eye[:, None, :, None] * w[:, :, None, :]).reshape(n * b, n * b)


def _gate_tiles(w_r, w_i):
    per = MXU_TILE // (LRU_WIDTH // LRU_BLOCKS)
    tiles = [jnp.concatenate([_block_diag(w_r[t * per:(t + 1) * per]),
                              _block_diag(w_i[t * per:(t + 1) * per])], axis=1)
             for t in range(LRU_WIDTH // MXU_TILE)]
    return jnp.stack(tiles).astype(BF16)


def kernel(x, norm1_g, w_in, conv_w, conv_b, w_rgate, b_rgate, w_igate, b_igate, lru_lambda,
           attn_out_g, lru_out_g, w_out, norm2_g, w_mlp_up, w_mlp_down, final_g):
    assert x.shape == (1, SEQ, D_MODEL) and w_in.shape[0] == 1
    x2 = x.reshape(SEQ, D_MODEL)
    cos_t, sin_t = _rope_tables()
    q, k, v, q1, k1, v1, xl, gl, wo, wu, wd = _inproj(
        x2, norm1_g.reshape(1, D_MODEL), w_in[0], cos_t, sin_t,
        (w_out[0], w_mlp_up[0], w_mlp_down[0]))
    attn = _attention(q, k, v, q1, k1, v1)
    w_gate = _gate_tiles(w_rgate[0], w_igate[0])
    b_gate = jnp.concatenate([b_rgate[0].reshape(1, -1), b_igate[0].reshape(1, -1)], axis=1)
    lru_params = (conv_w[0], conv_b[0].reshape(1, -1), w_gate, b_gate, lru_lambda[0].reshape(1, -1))
    out = _out_mlp(x2, attn, xl, gl, lru_params,
                   attn_out_g[0].reshape(1, -1), lru_out_g[0].reshape(1, -1),
                   wo, norm2_g[0].reshape(1, -1), wu, wd, final_g.reshape(1, -1))
    return out.reshape(1, SEQ, D_MODEL)
```

```python
import functools

import numpy as np
import jax
import jax.numpy as jnp
from jax import lax
from jax.experimental import pallas as pl
from jax.experimental.pallas import tpu as pltpu

F32 = jnp.float32
BF16 = jnp.bfloat16

D_MODEL = 1024
SEQ = 16384
ATTN_HEADS = 8
HEAD_DIM = 64
ATTN_WIDTH = ATTN_HEADS * HEAD_DIM
ROT_DIM = HEAD_DIM // 4
ROPE_THETA = 500000.0
LRU_WIDTH = D_MODEL - ATTN_WIDTH
LRU_BLOCKS = 8
CONV_WIDTH = 4
LRU_C = 8.0
IN_WIDTH = 3 * ATTN_WIDTH + 2 * LRU_WIDTH
D_FF = 4 * D_MODEL
EPS = 1e-6
WINDOW_STEPS = 128

NS = 16
RA = SEQ // NS
LANES = 128
MXU_TILE = 256
NSLAB = ATTN_WIDTH // LANES
NEG = -1e30

VMEM_LIMIT = 56 * 1024 * 1024


def _rms(x, g):
    return x * lax.rsqrt(jnp.mean(x * x, axis=-1, keepdims=True) + EPS) * g


TM1 = 512
PG = NS * NS
RG = PG // NS
NG1 = TM1 // PG
CH = 8
LOG2E = 1.4426950408889634


@functools.lru_cache(maxsize=None)
def _group_perm():
    n = np.arange(PG)
    p = np.zeros((PG, PG), np.float32)
    p[(n % NS) * RG + n // NS, n] = 1.0
    return p


def _inproj_body(x_ref, g_ref, w_ref, perm_ref, tile_ref, off_ref,
                 q_ref, k_ref, v_ref, q1_ref, k1_ref, v1_ref, xl_ref, gl_ref, w_scr):
    @pl.when(pl.program_id(0) == 0)
    def _():
        w_scr[...] = w_ref[...].astype(BF16)

    u = _rms(x_ref[...], g_ref[...]).astype(BF16)
    u = jnp.concatenate(
        [jnp.dot(perm_ref[...], u[g * PG:(g + 1) * PG], preferred_element_type=F32).astype(BF16)
         for g in range(NG1)], axis=0)
    z = jnp.dot(u, w_scr[...], preferred_element_type=F32)
    tile_cos, tile_sin = tile_ref[:, :LANES], tile_ref[:, LANES:]
    c = tile_cos * off_ref[0] - tile_sin * off_ref[1]
    s = tile_sin * off_ref[2] + tile_cos * off_ref[3]
    c = jnp.concatenate([c] * NSLAB, axis=1)
    s = jnp.concatenate([s] * NSLAB, axis=1)
    lane = lax.broadcasted_iota(jnp.int32, (1, ATTN_WIDTH), 1) % HEAD_DIM
    first_half = lane < ROT_DIM // 2

    def rope(t):
        up = pltpu.roll(t, ATTN_WIDTH - ROT_DIM // 2, axis=1)
        dn = pltpu.roll(t, ROT_DIM // 2, axis=1)
        return t * c + jnp.where(first_half, up, dn) * s

    q = rope(z[:, :ATTN_WIDTH]) * (HEAD_DIM ** -0.5 * LOG2E)
    k = rope(z[:, ATTN_WIDTH:2 * ATTN_WIDTH])
    v = z[:, 2 * ATTN_WIDTH:3 * ATTN_WIDTH]
    xl = z[:, 3 * ATTN_WIDTH:3 * ATTN_WIDTH + LRU_WIDTH]
    gl = z[:, 3 * ATTN_WIDTH + LRU_WIDTH:]
    for g in range(NG1):
        for r in range(NS):
            src = slice(g * PG + r * RG, g * PG + (r + 1) * RG)
            dst = slice(g * RG, (g + 1) * RG)
            for j in range(NSLAB):
                sl = slice(j * LANES, (j + 1) * LANES)
                q_ref[j, r, dst] = q[src, sl].astype(BF16)
                k_ref[j, r, dst] = k[src, sl].astype(BF16)
                v_ref[j, r, dst] = v[src, sl].astype(BF16)
            xl_ref[r, dst] = xl[src]
            gl_ref[r, dst] = gl[src]
    for g in range(NG1):
        for cl in range(RG // CH):
            for r in range(0, NS, 2):
                lo = g * PG + r * RG + cl * CH
                hi = lo + RG
                d0 = (g * (RG // CH) + cl) * NS * CH + r * CH
                dst = slice(d0, d0 + 2 * CH)
                for j in range(NSLAB):
                    sl = slice(j * LANES, (j + 1) * LANES)
                    for src, ref in ((q, q1_ref), (k, k1_ref), (v, v1_ref)):
                        ref[j, dst] = jnp.concatenate(
                            [src[lo:lo + CH, sl], src[hi:hi + CH, sl]], axis=0).astype(BF16)


def _inproj(x2, g, w):
    steps = SEQ // TM1
    rows = TM1 // NS
    slab_spec = pl.BlockSpec((NSLAB, NS, rows, LANES), lambda i: (0, 0, i, 0))
    chunk_spec = pl.BlockSpec((NSLAB, TM1, LANES), lambda i: (0, i, 0))
    row_spec = pl.BlockSpec((NS, rows, LRU_WIDTH), lambda i: (0, i, 0))
    rope_tiles, rope_offs = _rope_tables()
    slab_shape = jax.ShapeDtypeStruct((NSLAB, NS, RA, LANES), BF16)
    chunk_shape = jax.ShapeDtypeStruct((NSLAB, SEQ, LANES), BF16)
    row_shape = jax.ShapeDtypeStruct((NS, RA, LRU_WIDTH), F32)
    perm = jnp.asarray(_group_perm(), BF16)
    return pl.pallas_call(
        _inproj_body,
        grid=(steps,),
        in_specs=[
            pl.BlockSpec((TM1, D_MODEL), lambda i: (i, 0)),
            pl.BlockSpec((1, D_MODEL), lambda i: (0, 0)),
            pl.BlockSpec((D_MODEL, IN_WIDTH), lambda i: (0, 0), pipeline_mode=pl.Buffered(1)),
            pl.BlockSpec((PG, PG), lambda i: (0, 0)),
            pl.BlockSpec((None, 1, 2 * LANES), lambda i: (i, 0, 0)),
            pl.BlockSpec(rope_offs.shape, lambda i: (0, 0, 0)),
        ],
        out_specs=[slab_spec] * 3 + [chunk_spec] * 3 + [row_spec] * 2,
        out_shape=[slab_shape] * 3 + [chunk_shape] * 3 + [row_shape] * 2,
        scratch_shapes=[pltpu.VMEM((D_MODEL, IN_WIDTH), BF16)],
        compiler_params=pltpu.CompilerParams(
            dimension_semantics=("arbitrary",), vmem_limit_bytes=VMEM_LIMIT),
        name="inproj",
    )(x2, g, w, perm, rope_tiles, rope_offs)


TA = 128
UQ, UK = 128, 256
Q4, K4 = UQ // 4, UK // 4
GROUP = 2


@functools.lru_cache(maxsize=None)
def _attn_masks():
    def variants(diff, from_prev, per_unit):
        ok = (diff >= 0) & (diff <= WINDOW_STEPS)
        normal = np.where(ok, 0.0, NEG).astype(np.float32)
        first = np.where(ok & ~from_prev, 0.0, NEG).astype(np.float32)
        if per_unit:
            return np.stack([np.stack([first, normal]), np.stack([normal, normal])])
        return np.stack([first, normal])

    iq = np.arange(UQ)[:, None]
    ck = np.arange(UK)[None, :]
    m16 = variants(iq - ck + TA, np.broadcast_to(ck < TA, (UQ, UK)), False)
    cq, i4 = np.divmod(np.arange(UQ), Q4)
    ckk, j4 = np.divmod(np.arange(UK), K4)
    d4 = 4 * (i4[:, None] - j4[None, :] + Q4) + (cq[:, None] - ckk[None, :])
    m4 = variants(d4, np.broadcast_to(j4[None, :] < K4 - Q4, d4.shape), True)
    rq, i1 = np.divmod(np.arange(UQ), CH)
    kc, kin = np.divmod(np.arange(UK), UQ)
    rk, j1 = np.divmod(kin, CH)
    d1 = (NS * i1 + rq)[:, None] - (NS * j1 + rk + UQ * (kc - 1))[None, :]
    m1 = variants(d1, np.broadcast_to(kc[None, :] == 0, d1.shape), True)
    return m16, m4, m1


def _aligned(x, m):
    return x if isinstance(x, int) else pl.multiple_of(x, m)


def _attn_body(q_ref, kc_ref, kp_ref, vc_ref, vp_ref, q1_ref, k1c_ref, k1p_ref, v1c_ref, v1p_ref,
               m16_ref, m4_ref, m1_ref, wo_ref, wu_ref, wd_ref,
               o_ref, wo_bf_ref, wu_bf_ref, wd_bf_ref,
               qm_scr, vcm_scr, vpm_scr, q1m_scr, k1_scr, v1m_scr, s_scr, acc_scr, max_scr, sum_scr):
    wo_bf_ref[...] = wo_ref[...].astype(BF16)
    wu_bf_ref[...] = wu_ref[...].astype(BF16)
    wd_bf_ref[...] = wd_ref[...].astype(BF16)

    lane = lax.broadcasted_iota(jnp.int32, (1, LANES), 1)
    head0 = lane < HEAD_DIM
    zero = jnp.zeros((), BF16)

    def split(x):
        return jnp.where(head0, x, zero), jnp.where(head0, zero, x)

    qm_scr[0], qm_scr[1] = split(q_ref[...])
    vcm_scr[0], vcm_scr[1] = split(vc_ref[...])
    vpm_scr[0], vpm_scr[1] = split(vp_ref[...])
    q1m_scr[0], q1m_scr[1] = split(q1_ref[...])
    k1_scr[0:UQ] = k1p_ref[...]
    k1_scr[UQ:] = k1c_ref[...]
    v1m_scr[0, 0:UQ], v1m_scr[1, 0:UQ] = split(v1p_ref[...])
    v1m_scr[0, UQ:], v1m_scr[1, UQ:] = split(v1c_ref[...])

    def unit16(r):
        def store(o, mx, sm):
            acc_scr[0, r], max_scr[0, r], sum_scr[0, r] = o, mx, sm

        return (lambda h: qm_scr[h, r],
                lambda: jnp.concatenate([kp_ref[r], kc_ref[r]], axis=0),
                lambda h: jnp.concatenate([vpm_scr[h, r], vcm_scr[h, r]], axis=0),
                lambda: m16_ref[...], store)

    def unit4(r4, b):
        streams = [r4 + 4 * c for c in range(4)]
        q0, k0 = b * Q4, b * Q4 + Q4 - K4

        def window(cur, prev, ix):
            if k0 < 0:
                return [prev[ix + (slice(TA + k0, TA),)], cur[ix + (slice(0, k0 + K4),)]]
            return [cur[ix + (slice(k0, k0 + K4),)]]

        def store(o, mx, sm):
            for c, st in enumerate(streams):
                rows = slice(c * Q4, (c + 1) * Q4)
                acc_scr[1, st, q0:q0 + Q4] = o[rows]
                max_scr[1, st, q0:q0 + Q4] = mx[rows]
                sum_scr[1, st, q0:q0 + Q4] = sm[rows]

        return (lambda h: jnp.concatenate([qm_scr[h, st, q0:q0 + Q4] for st in streams], axis=0),
                lambda: jnp.concatenate(
                    [x for st in streams for x in window(kc_ref, kp_ref, (st,))], axis=0),
                lambda h: jnp.concatenate(
                    [x for st in streams for x in window(vcm_scr, vpm_scr, (h, st))], axis=0),
                lambda: m4_ref[min(b, 1)], store)

    def unit1(u):
        off = _aligned(u * UQ, UQ)
        row = _aligned(u * CH, CH)

        def store(o, mx, sm):
            for r in range(NS):
                rows = slice(r * CH, (r + 1) * CH)
                dst = pl.ds(row, CH)
                m0, m1, m2 = max_scr[0, r, dst], max_scr[1, r, dst], mx[rows]
                m = jnp.maximum(jnp.maximum(m0, m1), m2)
                w0, w1, w2 = jnp.exp2(m0 - m), jnp.exp2(m1 - m), jnp.exp2(m2 - m)
                num = w0 * acc_scr[0, r, dst] + w1 * acc_scr[1, r, dst] + w2 * o[rows]
                den = w0 * sum_scr[0, r, dst] + w1 * sum_scr[1, r, dst] + w2 * sm[rows]
                o_ref[r, dst] = num / den

        return (lambda h: q1m_scr[h, pl.ds(off, UQ)],
                lambda: k1_scr[pl.ds(off, UK)],
                lambda h: v1m_scr[h, pl.ds(off, UK)],
                lambda: m1_ref[min(u, 1) if isinstance(u, int) else jnp.minimum(u, 1)], store)

    all_units = ([unit16(n) for n in range(NS)]
                 + [unit4(n // (TA // Q4), n % (TA // Q4)) for n in range(NS)]
                 + [unit1(n) for n in range(NS)])

    def score_stage(group, slot):
        for g, (q, k, _, _, _) in enumerate(group):
            kb = k()
            for h in range(2):
                s_scr[slot, g, h] = lax.dot_general(
                    q(h), kb, (((1,), (1,)), ((), ())), preferred_element_type=F32)

    def value_stage(group, slot):
        for g, (_, _, v, mask, store) in enumerate(group):
            o = None
            mx, sm = [], []
            for h in range(2):
                s = s_scr[slot, g, h] + mask()
                m = jnp.max(s, axis=-1, keepdims=True)
                p = jnp.exp2(s - m)
                mx.append(m)
                sm.append(jnp.sum(p, axis=-1, keepdims=True))
                pv = jnp.dot(p.astype(BF16), v(h), preferred_element_type=F32)
                o = pv if o is None else o + pv
            store(o, jnp.where(head0, mx[0], mx[1]), jnp.where(head0, sm[0], sm[1]))

    stages = [all_units[n:n + GROUP] for n in range(0, len(all_units), GROUP)]
    score_stage(stages[0], 0)
    for n, group in enumerate(stages):
        if n + 1 < len(stages):
            score_stage(stages[n + 1], (n + 1) % 2)
        value_stage(group, n % 2)


def _attention(q, k, v, q1, k1, v1, later_weights):
    m16, m4, m1 = _attn_masks()
    cur = pl.BlockSpec((None, NS, TA, LANES), lambda j, i: (j, 0, i, 0))
    prev = pl.BlockSpec((None, NS, TA, LANES), lambda j, i: (j, 0, jnp.maximum(i - 1, 0), 0))
    cur1 = pl.BlockSpec((None, NS * TA, LANES), lambda j, i: (j, i, 0))
    prev1 = pl.BlockSpec((None, UQ, LANES),
                         lambda j, i: (j, jnp.maximum(i * (NS * TA // UQ) - 1, 0), 0))

    def tile_kind(m):
        return pl.BlockSpec((None,) + m.shape[1:],
                            lambda j, i: (jnp.minimum(i, 1),) + (0,) * (m.ndim - 1))

    n_tiles = RA // TA
    steps = NSLAB * n_tiles
    cast_specs = [pl.BlockSpec((a.shape[0] // steps, a.shape[1]), lambda j, i: (j * n_tiles + i, 0))
                  for a in later_weights]
    cast_shapes = [jax.ShapeDtypeStruct(a.shape, BF16) for a in later_weights]
    return pl.pallas_call(
        _attn_body,
        grid=(NSLAB, n_tiles),
        in_specs=[cur, cur, prev, cur, prev, cur1, cur1, prev1, cur1, prev1,
                  tile_kind(m16), tile_kind(m4), tile_kind(m1)] + cast_specs,
        out_specs=[pl.BlockSpec((None, NS, TA, LANES), lambda j, i: (j, 0, i, 0))] + cast_specs,
        out_shape=[jax.ShapeDtypeStruct((NSLAB, NS, RA, LANES), F32)] + cast_shapes,
        scratch_shapes=[
            pltpu.VMEM((2, NS, TA, LANES), BF16),
            pltpu.VMEM((2, NS, TA, LANES), BF16),
            pltpu.VMEM((2, NS, TA, LANES), BF16),
            pltpu.VMEM((2, NS * TA, LANES), BF16),
            pltpu.VMEM((UQ + NS * TA, LANES), BF16),
            pltpu.VMEM((2, UQ + NS * TA, LANES), BF16),
            pltpu.VMEM((2, GROUP, 2, UQ, UK), F32),
            pltpu.VMEM((2, NS, TA, LANES), F32),
            pltpu.VMEM((2, NS, TA, LANES), F32),
            pltpu.VMEM((2, NS, TA, LANES), F32),
        ],
        compiler_params=pltpu.CompilerParams(
            dimension_semantics=("arbitrary", "arbitrary"), vmem_limit_bytes=VMEM_LIMIT),
        name="dilated_attn",
    )(q, k, k, v, v, q1, k1, k1, v1, v1, m16, m4, m1, *later_weights)


TL = 32


def _lru_phases(xl_ref, gl_ref, cw_ref, cb_ref, wg_ref, bg_ref, lam_ref, y_ref,
                tail_scr, carry_scr, xc_scr, g_scr, pl_scr, hl_scr):
    row = lax.broadcasted_iota(jnp.int32, (TL, LRU_WIDTH), 0)
    state = {}

    def conv(after=0.0):
        cw = cw_ref[...]
        cb = cb_ref[...] + after

        def tap(r, k):
            st = r - k
            if st >= 0:
                return xl_ref[st]
            st += NS
            prev_last = tail_scr[st - (NS - CONV_WIDTH + 1), 7:8, :]
            return jnp.where(row == 0, prev_last, pltpu.roll(xl_ref[st], 1, axis=0))

        for r in range(NS):
            xc_scr[r] = cb + sum(cw[CONV_WIDTH - 1 - k:CONV_WIDTH - k] * tap(r, k)
                                 for k in range(CONV_WIDTH))
        for n in range(CONV_WIDTH - 1):
            tail_scr[n] = xl_ref[NS - CONV_WIDTH + 1 + n, TL - 8:TL, :]

    def gates(after=0.0):
        xc_all = xc_scr[...].reshape(NS * TL, LRU_WIDTH).astype(BF16)
        bias = bg_ref[...] + after
        for t in range(LRU_WIDTH // MXU_TILE):
            cols = slice(t * MXU_TILE, (t + 1) * MXU_TILE)
            res = jnp.dot(xc_all[:, cols], wg_ref[t], preferred_element_type=F32)
            for gate in range(2):
                dst = slice(gate * LRU_WIDTH + t * MXU_TILE, gate * LRU_WIDTH + (t + 1) * MXU_TILE)
                g_scr[:, dst] = res[:, gate * MXU_TILE:(gate + 1) * MXU_TILE] + bias[:, dst]

    def local_scan(streams):
        def run(after=0.0):
            neg_lam = after - lam_ref[...]
            softplus = jnp.maximum(neg_lam, 0.0) + jnp.log1p(jnp.exp(-jnp.abs(neg_lam)))
            p_run, h_run = state.get("run", (None, None))
            for r in streams:
                g = g_scr[r * TL:(r + 1) * TL]
                rg = jax.nn.sigmoid(g[:, :LRU_WIDTH])
                ig = jax.nn.sigmoid(g[:, LRU_WIDTH:])
                log_a = -LRU_C * rg * softplus
                a = jnp.exp(log_a)
                th = jnp.tanh(log_a)
                bx = jnp.sqrt(-2.0 * th / (1.0 - th)) * (ig * xc_scr[r])
                if p_run is None:
                    p_run, h_run = a, bx
                else:
                    h_run = a * h_run + bx
                    p_run = a * p_run
                pl_scr[r] = p_run
                hl_scr[r] = h_run
            state["run"] = (p_run, h_run)
        return run

    def row_scan():
        pa, hb = state["run"]
        sft = 1
        while sft < TL:
            keep = row >= sft
            pa_s = jnp.where(keep, pltpu.roll(pa, sft, axis=0), 1.0)
            hb_s = jnp.where(keep, pltpu.roll(hb, sft, axis=0), 0.0)
            hb = pa * hb_s + hb
            pa = pa * pa_s
            sft *= 2
        carry = carry_scr[0:1, :]
        e = pa * carry + hb
        state["e_prev"] = jnp.where(row == 0, carry, pltpu.roll(e, 1, axis=0))
        carry_scr[...] = jnp.broadcast_to(e[TL - 1:TL, :], carry_scr.shape)

    def finalize(after=0.0):
        e_prev = state["e_prev"] + after
        for r in range(NS):
            h = pl_scr[r] * e_prev + hl_scr[r]
            y_ref[r] = jax.nn.gelu(gl_ref[r], approximate=True) * h

    half = NS // 2

    def second_half(after=0.0):
        local_scan(range(half, NS))(after)
        row_scan()

    return [(conv, LRU_WIDTH), (gates, 2 * LRU_WIDTH), (local_scan(range(half)), LRU_WIDTH),
            (second_half, LRU_WIDTH), (finalize, LRU_WIDTH)]


TM4 = NS * TL
NG4 = TM4 // PG
FF_CHUNK = 1024


def _out_body(x_ref, at_ref, xl0_ref, gl0_ref, xln_ref, gln_ref, zero_ref,
              cw_ref, cb_ref, wg_ref, bg_ref, lam_ref,
              perm_ref, ga_ref, gl_ref, wo_ref, g2_ref, wu_ref, wd_ref, gf_ref,
              o_ref,
              lr_scr, tail_scr, carry_scr, xc_scr, g_scr, pl_scr, hl_scr):
    lru_params = (cw_ref, cb_ref, wg_ref, bg_ref, lam_ref)
    lru_state = (tail_scr, carry_scr, xc_scr, g_scr, pl_scr, hl_scr)

    @pl.when(pl.program_id(0) == 0)
    def _():
        tail_scr[...] = jnp.zeros_like(tail_scr)
        carry_scr[...] = jnp.zeros_like(carry_scr)
        for phase, _ in _lru_phases(xl0_ref, gl0_ref, *lru_params, lr_scr, *lru_state):
            phase()

    groups = []
    for g in range(NG4):
        rows = slice(g * RG, (g + 1) * RG)
        attn = jnp.concatenate(
            [jnp.concatenate([at_ref[j, r, rows] for j in range(NSLAB)], axis=1) for r in range(NS)],
            axis=0)
        lru = jnp.concatenate([lr_scr[r, rows] for r in range(NS)], axis=0)
        mixed = jnp.concatenate([_rms(attn, ga_ref[...]), _rms(lru, gl_ref[...])], axis=1)
        groups.append(jnp.dot(perm_ref[...], mixed.astype(BF16),
                              preferred_element_type=F32).astype(BF16))
    mixed = jnp.concatenate(groups, axis=0)
    lru_next = _lru_phases(xln_ref, gln_ref, *lru_params, lr_scr, *lru_state)
    n_chunks = D_FF // FF_CHUNK
    assert len(lru_next) <= n_chunks + 1

    def zero_after(val, width):
        bits = pltpu.bitcast(val[0:1, :width], jnp.int32) & zero_ref[0:1, :width]
        return pltpu.bitcast(bits, F32)

    h = x_ref[...] + jnp.dot(mixed, wo_ref[...], preferred_element_type=F32)
    u = _rms(h, g2_ref[...]).astype(BF16)
    acc = h
    phase, width = lru_next.pop(0)
    phase(zero_after(h, width))
    for c in range(n_chunks):
        if lru_next:
            phase, width = lru_next.pop(0)
            phase(zero_after(acc, width))
        sl = slice(c * FF_CHUNK, (c + 1) * FF_CHUNK)
        f = jnp.dot(u, wu_ref[:, sl], preferred_element_type=F32)
        f = jnp.square(jnp.maximum(f, 0.0)).astype(BF16)
        acc = acc + jnp.dot(f, wd_ref[sl, :], preferred_element_type=F32)
    o_ref[...] = _rms(acc, gf_ref[...])


def _out_mlp(x2, attn, xl, gl_in, lru_params, ga, gl, wo, g2, wu, wd, gf):
    def const(a):
        return pl.BlockSpec(a.shape, lambda i: (0,) * a.ndim, pipeline_mode=pl.Buffered(1))

    n_tiles = SEQ // TM4
    xspec = pl.BlockSpec((TM4, D_MODEL), lambda i: (i, 0))
    first = pl.BlockSpec((NS, TL, LRU_WIDTH), lambda i: (0, 0, 0), pipeline_mode=pl.Buffered(1))
    ahead = pl.BlockSpec((NS, TL, LRU_WIDTH), lambda i: (0, jnp.minimum(i + 1, n_tiles - 1), 0))
    perm = jnp.asarray(_group_perm(), BF16)
    zero = jnp.zeros((8, D_MODEL), jnp.int32)
    consts = (zero,) + tuple(lru_params) + (perm, ga, gl, wo, g2, wu, wd, gf)
    return pl.pallas_call(
        _out_body,
        grid=(n_tiles,),
        in_specs=[
            xspec,
            pl.BlockSpec((NSLAB, NS, TL, LANES), lambda i: (0, 0, i, 0)),
            first, first, ahead, ahead,
        ] + [const(a) for a in consts],
        out_specs=xspec,
        out_shape=jax.ShapeDtypeStruct((SEQ, D_MODEL), F32),
        scratch_shapes=[
            pltpu.VMEM((NS, TL, LRU_WIDTH), F32),
            pltpu.VMEM((CONV_WIDTH - 1, 8, LRU_WIDTH), F32),
            pltpu.VMEM((8, LRU_WIDTH), F32),
            pltpu.VMEM((NS, TL, LRU_WIDTH), F32),
            pltpu.VMEM((NS * TL, 2 * LRU_WIDTH), F32),
            pltpu.VMEM((NS, TL, LRU_WIDTH), F32),
            pltpu.VMEM((NS, TL, LRU_WIDTH), F32),
        ],
        compiler_params=pltpu.CompilerParams(
            dimension_semantics=("arbitrary",), vmem_limit_bytes=VMEM_LIMIT),
        name="out_mlp",
    )(x2, attn, xl, gl_in, xl, gl_in, *consts)


@functools.lru_cache(maxsize=None)
def _rope_tables():
    inv_freq = ROPE_THETA ** (-np.arange(0, ROT_DIM, 2, dtype=np.float64) / ROT_DIM)
    pad = HEAD_DIM - ROT_DIM
    reps = LANES // HEAD_DIM

    def lanes(rot_first, rot_second, rest, n):
        head = np.concatenate([rot_first, rot_second, np.full((n, pad), rest)], axis=1)
        return np.tile(head, (1, reps))

    tile_ang = (TM1 * np.arange(SEQ // TM1, dtype=np.float64))[:, None] * inv_freq[None, :]
    tc, ts = np.cos(tile_ang), np.sin(tile_ang)
    tiles = np.concatenate([lanes(tc, tc, 1.0, len(tc)), lanes(ts, ts, 0.0, len(ts))], axis=1)
    off_ang = np.arange(TM1, dtype=np.float64)[:, None] * inv_freq[None, :]
    oc, osn = np.cos(off_ang), np.sin(off_ang)
    offs = np.stack([lanes(oc, oc, 1.0, TM1), lanes(osn, osn, 0.0, TM1),
                     lanes(-oc, oc, 0.0, TM1), lanes(-osn, osn, 0.0, TM1)])
    offs = offs.reshape(4, NG1, RG, NS, LANES).transpose(0, 1, 3, 2, 4).reshape(4, TM1, LANES)
    return (tiles.astype(np.float32)[:, None, :], np.ascontiguousarray(offs.astype(np.float32)))


def _block_diag(w):
    n, b, _ = w.shape
    eye = jnp.eye(n, dtype=w.dtype)
    return (eye[:, None, :, None] * w[:, :, None, :]).reshape(n * b, n * b)


def _gate_tiles(w_r, w_i):
    per = MXU_TILE // (LRU_WIDTH // LRU_BLOCKS)
    tiles = [jnp.concatenate([_block_diag(w_r[t * per:(t + 1) * per]),
                              _block_diag(w_i[t * per:(t + 1) * per])], axis=1)
             for t in range(LRU_WIDTH // MXU_TILE)]
    return jnp.stack(tiles).astype(BF16)


def kernel(x, norm1_g, w_in, conv_w, conv_b, w_rgate, b_rgate, w_igate, b_igate, lru_lambda,
           attn_out_g, lru_out_g, w_out, norm2_g, w_mlp_up, w_mlp_down, final_g):
    assert x.shape == (1, SEQ, D_MODEL) and w_in.shape[0] == 1
    x2 = x.reshape(SEQ, D_MODEL)
    q, k, v, q1, k1, v1, xl, gl = _inproj(x2, norm1_g.reshape(1, D_MODEL), w_in[0])
    attn, wo, wu, wd = _attention(q, k, v, q1, k1, v1, (w_out[0], w_mlp_up[0], w_mlp_down[0]))
    w_gate = _gate_tiles(w_rgate[0], w_igate[0])
    b_gate = jnp.concatenate([b_rgate[0].reshape(1, -1), b_igate[0].reshape(1, -1)], axis=1)
    lru_params = (conv_w[0], conv_b[0].reshape(1, -1), w_gate, b_gate, lru_lambda[0].reshape(1, -1))
    out = _out_mlp(x2, attn, xl, gl, lru_params,
                   attn_out_g[0].reshape(1, -1), lru_out_g[0].reshape(1, -1),
                   wo, norm2_g[0].reshape(1, -1), wu, wd, final_g.reshape(1, -1))
    return out.reshape(1, SEQ, D_MODEL)
```

```python
import functools

import numpy as np
import jax
import jax.numpy as jnp
from jax import lax
from jax.experimental import pallas as pl
from jax.experimental.pallas import tpu as pltpu

F32 = jnp.float32
BF16 = jnp.bfloat16

D_MODEL = 1024
SEQ = 16384
ATTN_HEADS = 8
HEAD_DIM = 64
ATTN_WIDTH = ATTN_HEADS * HEAD_DIM
ROT_DIM = HEAD_DIM // 4
ROPE_THETA = 500000.0
LRU_WIDTH = D_MODEL - ATTN_WIDTH
LRU_BLOCKS = 8
CONV_WIDTH = 4
LRU_C = 8.0
IN_WIDTH = 3 * ATTN_WIDTH + 2 * LRU_WIDTH
D_FF = 4 * D_MODEL
EPS = 1e-6
WINDOW_STEPS = 128

NS = 16
RA = SEQ // NS
LANES = 128
MXU_TILE = 256
NSLAB = ATTN_WIDTH // LANES
NEG = -1e30

VMEM_LIMIT = 56 * 1024 * 1024


def _rms(x, g):
    return x * lax.rsqrt(jnp.mean(x * x, axis=-1, keepdims=True) + EPS) * g


TM1 = 512
PG = NS * NS
RG = PG // NS
NG1 = TM1 // PG
CH = 8
LOG2E = 1.4426950408889634


@functools.lru_cache(maxsize=None)
def _group_perm():
    n = np.arange(PG)
    p = np.zeros((PG, PG), np.float32)
    p[(n % NS) * RG + n // NS, n] = 1.0
    return p


def _inproj_body(x_ref, g_ref, w_ref, perm_ref, tile_ref, off_ref,
                 q_ref, k_ref, v_ref, q1_ref, k1_ref, v1_ref, xl_ref, gl_ref, w_scr):
    @pl.when(pl.program_id(0) == 0)
    def _():
        w_scr[...] = w_ref[...].astype(BF16)

    u = _rms(x_ref[...], g_ref[...]).astype(BF16)
    u = jnp.concatenate(
        [jnp.dot(perm_ref[...], u[g * PG:(g + 1) * PG], preferred_element_type=F32).astype(BF16)
         for g in range(NG1)], axis=0)
    z = jnp.dot(u, w_scr[...], preferred_element_type=F32)
    tile_cos, tile_sin = tile_ref[:, :LANES], tile_ref[:, LANES:]
    c = tile_cos * off_ref[0] - tile_sin * off_ref[1]
    s = tile_sin * off_ref[2] + tile_cos * off_ref[3]
    c = jnp.concatenate([c] * NSLAB, axis=1)
    s = jnp.concatenate([s] * NSLAB, axis=1)
    lane = lax.broadcasted_iota(jnp.int32, (1, ATTN_WIDTH), 1) % HEAD_DIM
    first_half = lane < ROT_DIM // 2

    def rope(t):
        up = pltpu.roll(t, ATTN_WIDTH - ROT_DIM // 2, axis=1)
        dn = pltpu.roll(t, ROT_DIM // 2, axis=1)
        return t * c + jnp.where(first_half, up, dn) * s

    q = rope(z[:, :ATTN_WIDTH]) * (HEAD_DIM ** -0.5 * LOG2E)
    k = rope(z[:, ATTN_WIDTH:2 * ATTN_WIDTH])
    v = z[:, 2 * ATTN_WIDTH:3 * ATTN_WIDTH]
    xl = z[:, 3 * ATTN_WIDTH:3 * ATTN_WIDTH + LRU_WIDTH]
    gl = z[:, 3 * ATTN_WIDTH + LRU_WIDTH:]
    for g in range(NG1):
        for r in range(NS):
            src = slice(g * PG + r * RG, g * PG + (r + 1) * RG)
            dst = slice(g * RG, (g + 1) * RG)
            for j in range(NSLAB):
                sl = slice(j * LANES, (j + 1) * LANES)
                q_ref[j, r, dst] = q[src, sl].astype(BF16)
                k_ref[j, r, dst] = k[src, sl].astype(BF16)
                v_ref[j, r, dst] = v[src, sl].astype(BF16)
            xl_ref[r, dst] = xl[src]
            gl_ref[r, dst] = gl[src]
    for g in range(NG1):
        for cl in range(RG // CH):
            for r in range(0, NS, 2):
                lo = g * PG + r * RG + cl * CH
                hi = lo + RG
                d0 = (g * (RG // CH) + cl) * NS * CH + r * CH
                dst = slice(d0, d0 + 2 * CH)
                for j in range(NSLAB):
                    sl = slice(j * LANES, (j + 1) * LANES)
                    for src, ref in ((q, q1_ref), (k, k1_ref), (v, v1_ref)):
                        ref[j, dst] = jnp.concatenate(
                            [src[lo:lo + CH, sl], src[hi:hi + CH, sl]], axis=0).astype(BF16)


def _inproj(x2, g, w):
    steps = SEQ // TM1
    rows = TM1 // NS
    slab_spec = pl.BlockSpec((NSLAB, NS, rows, LANES), lambda i: (0, 0, i, 0))
    chunk_spec = pl.BlockSpec((NSLAB, TM1, LANES), lambda i: (0, i, 0))
    row_spec = pl.BlockSpec((NS, rows, LRU_WIDTH), lambda i: (0, i, 0))
    rope_tiles, rope_offs = _rope_tables()
    slab_shape = jax.ShapeDtypeStruct((NSLAB, NS, RA, LANES), BF16)
    chunk_shape = jax.ShapeDtypeStruct((NSLAB, SEQ, LANES), BF16)
    row_shape = jax.ShapeDtypeStruct((NS, RA, LRU_WIDTH), F32)
    perm = jnp.asarray(_group_perm(), BF16)
    return pl.pallas_call(
        _inproj_body,
        grid=(steps,),
        in_specs=[
            pl.BlockSpec((TM1, D_MODEL), lambda i: (i, 0)),
            pl.BlockSpec((1, D_MODEL), lambda i: (0, 0)),
            pl.BlockSpec((D_MODEL, IN_WIDTH), lambda i: (0, 0), pipeline_mode=pl.Buffered(1)),
            pl.BlockSpec((PG, PG), lambda i: (0, 0)),
            pl.BlockSpec((None, 1, 2 * LANES), lambda i: (i, 0, 0)),
            pl.BlockSpec(rope_offs.shape, lambda i: (0, 0, 0)),
        ],
        out_specs=[slab_spec] * 3 + [chunk_spec] * 3 + [row_spec] * 2,
        out_shape=[slab_shape] * 3 + [chunk_shape] * 3 + [row_shape] * 2,
        scratch_shapes=[pltpu.VMEM((D_MODEL, IN_WIDTH), BF16)],
        compiler_params=pltpu.CompilerParams(
            dimension_semantics=("arbitrary",), vmem_limit_bytes=VMEM_LIMIT),
        name="inproj",
    )(x2, g, w, perm, rope_tiles, rope_offs)


TA = 128
UQ, UK = 128, 256
Q4, K4 = UQ // 4, UK // 4
GROUP = 4


@functools.lru_cache(maxsize=None)
def _attn_masks():
    def variants(diff, from_prev, per_unit):
        ok = (diff >= 0) & (diff <= WINDOW_STEPS)
        normal = np.where(ok, 0.0, NEG).astype(np.float32)
        first = np.where(ok & ~from_prev, 0.0, NEG).astype(np.float32)
        if per_unit:
            return np.stack([np.stack([first, normal]), np.stack([normal, normal])])
        return np.stack([first, normal])

    iq = np.arange(UQ)[:, None]
    ck = np.arange(UK)[None, :]
    m16 = variants(iq - ck + TA, np.broadcast_to(ck < TA, (UQ, UK)), False)
    cq, i4 = np.divmod(np.arange(UQ), Q4)
    ckk, j4 = np.divmod(np.arange(UK), K4)
    d4 = 4 * (i4[:, None] - j4[None, :] + Q4) + (cq[:, None] - ckk[None, :])
    m4 = variants(d4, np.broadcast_to(j4[None, :] < K4 - Q4, d4.shape), True)
    rq, i1 = np.divmod(np.arange(UQ), CH)
    kc, kin = np.divmod(np.arange(UK), UQ)
    rk, j1 = np.divmod(kin, CH)
    d1 = (NS * i1 + rq)[:, None] - (NS * j1 + rk + UQ * (kc - 1))[None, :]
    m1 = variants(d1, np.broadcast_to(kc[None, :] == 0, d1.shape), True)
    return m16, m4, m1


def _attn_body(q_ref, kc_ref, kp_ref, vc_ref, vp_ref, q1_ref, k1c_ref, k1p_ref, v1c_ref, v1p_ref,
               m16_ref, m4_ref, m1_ref, wo_ref, wu_ref, wd_ref,
               o_ref, wo_bf_ref, wu_bf_ref, wd_bf_ref,
               qm_scr, q1m_scr, k1_scr, v1_scr, s_scr, acc_scr, max_scr, sum_scr):
    wo_bf_ref[...] = wo_ref[...].astype(BF16)
    wu_bf_ref[...] = wu_ref[...].astype(BF16)
    wd_bf_ref[...] = wd_ref[...].astype(BF16)

    lane = lax.broadcasted_iota(jnp.int32, (1, LANES), 1)
    head0 = lane < HEAD_DIM
    zero = jnp.zeros((), BF16)

    def split(x):
        return jnp.where(head0, x, zero), jnp.where(head0, zero, x)

    qm_scr[0], qm_scr[1] = split(q_ref[...])
    q1m_scr[0], q1m_scr[1] = split(q1_ref[...])
    k1_scr[0:UQ] = k1p_ref[...]
    k1_scr[UQ:] = k1c_ref[...]
    v1_scr[0:UQ] = v1p_ref[...]
    v1_scr[UQ:] = v1c_ref[...]
    ones = jnp.ones((UK, LANES), BF16)

    def unit16(r):
        def store(o, mx, sm):
            acc_scr[0, r], max_scr[0, r], sum_scr[0, r] = o, mx, sm

        return (lambda h: qm_scr[h, r],
                lambda: jnp.concatenate([kp_ref[r], kc_ref[r]], axis=0),
                lambda: jnp.concatenate([vp_ref[r], vc_ref[r]], axis=0),
                lambda: m16_ref[...], store)

    def unit4(r4, b):
        streams = [r4 + 4 * c for c in range(4)]
        q0, k0 = b * Q4, b * Q4 + Q4 - K4

        def window(cur, prev, st):
            if k0 < 0:
                return [prev[st, TA + k0:TA], cur[st, 0:k0 + K4]]
            return [cur[st, k0:k0 + K4]]

        def store(o, mx, sm):
            for c, st in enumerate(streams):
                rows = slice(c * Q4, (c + 1) * Q4)
                acc_scr[1, st, q0:q0 + Q4] = o[rows]
                max_scr[1, st, q0:q0 + Q4] = mx[rows]
                sum_scr[1, st, q0:q0 + Q4] = sm[rows]

        return (lambda h: jnp.concatenate([qm_scr[h, st, q0:q0 + Q4] for st in streams], axis=0),
                lambda: jnp.concatenate(
                    [x for st in streams for x in window(kc_ref, kp_ref, st)], axis=0),
                lambda: jnp.concatenate(
                    [x for st in streams for x in window(vc_ref, vp_ref, st)], axis=0),
                lambda: m4_ref[min(b, 1)], store)

    def unit1(u):
        off, row = u * UQ, u * CH

        def store(o, mx, sm):
            for r in range(NS):
                rows = slice(r * CH, (r + 1) * CH)
                acc_scr[2, r, row:row + CH] = o[rows]
                max_scr[2, r, row:row + CH] = mx[rows]
                sum_scr[2, r, row:row + CH] = sm[rows]

        return (lambda h: q1m_scr[h, off:off + UQ],
                lambda: k1_scr[off:off + UK],
                lambda: v1_scr[off:off + UK],
                lambda: m1_ref[min(u, 1)], store)

    all_units = ([unit16(n) for n in range(NS)]
                 + [unit4(n // (TA // Q4), n % (TA // Q4)) for n in range(NS)]
                 + [unit1(n) for n in range(NS)])

    def score_stage(group, slot):
        for g, (q, k, _, _, _) in enumerate(group):
            s_scr[slot, g] = lax.dot_general(
                jnp.concatenate([q(0), q(1)], axis=0), k(), (((1,), (1,)), ((), ())),
                preferred_element_type=F32)

    def value_stage(group, slot):
        for g, (_, _, v, mask, store) in enumerate(group):
            ps, mx = [], []
            for h in range(2):
                s = s_scr[slot, g, h * UQ:(h + 1) * UQ] + mask()
                m = jnp.max(s, axis=-1, keepdims=True)
                ps.append(jnp.exp2(s - m).astype(BF16))
                mx.append(m)
            res = jnp.dot(jnp.concatenate(ps, axis=0), jnp.concatenate([v(), ones], axis=1),
                          preferred_element_type=F32)
            top, bot = res[:UQ], res[UQ:]
            store(jnp.where(head0, top[:, :LANES], bot[:, :LANES]),
                  jnp.where(head0, mx[0], mx[1]),
                  jnp.where(head0, top[:, LANES:], bot[:, LANES:]))

    stages = [all_units[n:n + GROUP] for n in range(0, len(all_units), GROUP)]
    score_stage(stages[0], 0)
    for n, group in enumerate(stages):
        if n + 1 < len(stages):
            score_stage(stages[n + 1], (n + 1) % 2)
        value_stage(group, n % 2)

    def comb(r, carry):
        m0, m1, m2 = max_scr[0, r], max_scr[1, r], max_scr[2, r]
        m = jnp.maximum(jnp.maximum(m0, m1), m2)
        w0, w1, w2 = jnp.exp2(m0 - m), jnp.exp2(m1 - m), jnp.exp2(m2 - m)
        num = w0 * acc_scr[0, r] + w1 * acc_scr[1, r] + w2 * acc_scr[2, r]
        den = w0 * sum_scr[0, r] + w1 * sum_scr[1, r] + w2 * sum_scr[2, r]
        o_ref[r] = num / den
        return carry

    lax.fori_loop(0, NS, comb, 0, unroll=2)


def _attention(q, k, v, q1, k1, v1, later_weights):
    m16, m4, m1 = _attn_masks()
    cur = pl.BlockSpec((None, NS, TA, LANES), lambda j, i: (j, 0, i, 0))
    prev = pl.BlockSpec((None, NS, TA, LANES), lambda j, i: (j, 0, jnp.maximum(i - 1, 0), 0))
    cur1 = pl.BlockSpec((None, NS * TA, LANES), lambda j, i: (j, i, 0))
    prev1 = pl.BlockSpec((None, UQ, LANES),
                         lambda j, i: (j, jnp.maximum(i * (NS * TA // UQ) - 1, 0), 0))

    def tile_kind(m):
        return pl.BlockSpec((None,) + m.shape[1:],
                            lambda j, i: (jnp.minimum(i, 1),) + (0,) * (m.ndim - 1))

    n_tiles = RA // TA
    steps = NSLAB * n_tiles
    cast_specs = [pl.BlockSpec((a.shape[0] // steps, a.shape[1]), lambda j, i: (j * n_tiles + i, 0))
                  for a in later_weights]
    cast_shapes = [jax.ShapeDtypeStruct(a.shape, BF16) for a in later_weights]
    return pl.pallas_call(
        _attn_body,
        grid=(NSLAB, n_tiles),
        in_specs=[cur, cur, prev, cur, prev, cur1, cur1, prev1, cur1, prev1,
                  tile_kind(m16), tile_kind(m4), tile_kind(m1)] + cast_specs,
        out_specs=[pl.BlockSpec((None, NS, TA, LANES), lambda j, i: (j, 0, i, 0))] + cast_specs,
        out_shape=[jax.ShapeDtypeStruct((NSLAB, NS, RA, LANES), F32)] + cast_shapes,
        scratch_shapes=[
            pltpu.VMEM((2, NS, TA, LANES), BF16),
            pltpu.VMEM((2, NS * TA, LANES), BF16),
            pltpu.VMEM((UQ + NS * TA, LANES), BF16),
            pltpu.VMEM((UQ + NS * TA, LANES), BF16),
            pltpu.VMEM((2, GROUP, 2 * UQ, UK), F32),
            pltpu.VMEM((3, NS, TA, LANES), F32),
            pltpu.VMEM((3, NS, TA, LANES), F32),
            pltpu.VMEM((3, NS, TA, LANES), F32),
        ],
        compiler_params=pltpu.CompilerParams(
            dimension_semantics=("arbitrary", "arbitrary"), vmem_limit_bytes=VMEM_LIMIT),
        name="dilated_attn",
    )(q, k, k, v, v, q1, k1, k1, v1, v1, m16, m4, m1, *later_weights)


TL = 32


def _lru_phases(xl_ref, gl_ref, cw_ref, cb_ref, wg_ref, bg_ref, lam_ref, y_ref,
                tail_scr, carry_scr, xc_scr, g_scr, pl_scr, hl_scr):
    row = lax.broadcasted_iota(jnp.int32, (TL, LRU_WIDTH), 0)
    state = {}

    def conv(after=0.0):
        cw = cw_ref[...]
        cb = cb_ref[...] + after

        def tap(r, k):
            st = r - k
            if st >= 0:
                return xl_ref[st]
            st += NS
            prev_last = tail_scr[st - (NS - CONV_WIDTH + 1), 7:8, :]
            return jnp.where(row == 0, prev_last, pltpu.roll(xl_ref[st], 1, axis=0))

        for r in range(NS):
            xc_scr[r] = cb + sum(cw[CONV_WIDTH - 1 - k:CONV_WIDTH - k] * tap(r, k)
                                 for k in range(CONV_WIDTH))
        for n in range(CONV_WIDTH - 1):
            tail_scr[n] = xl_ref[NS - CONV_WIDTH + 1 + n, TL - 8:TL, :]

    def gates(after=0.0):
        xc_all = xc_scr[...].reshape(NS * TL, LRU_WIDTH).astype(BF16)
        bias = bg_ref[...] + after
        for t in range(LRU_WIDTH // MXU_TILE):
            cols = slice(t * MXU_TILE, (t + 1) * MXU_TILE)
            res = jnp.dot(xc_all[:, cols], wg_ref[t], preferred_element_type=F32)
            for gate in range(2):
                dst = slice(gate * LRU_WIDTH + t * MXU_TILE, gate * LRU_WIDTH + (t + 1) * MXU_TILE)
                g_scr[:, dst] = res[:, gate * MXU_TILE:(gate + 1) * MXU_TILE] + bias[:, dst]

    def local_scan(streams):
        def run(after=0.0):
            neg_lam = after - lam_ref[...]
            softplus = jnp.maximum(neg_lam, 0.0) + jnp.log1p(jnp.exp(-jnp.abs(neg_lam)))
            p_run, h_run = state.get("run", (None, None))
            for r in streams:
                g = g_scr[r * TL:(r + 1) * TL]
                rg = jax.nn.sigmoid(g[:, :LRU_WIDTH])
                ig = jax.nn.sigmoid(g[:, LRU_WIDTH:])
                log_a = -LRU_C * rg * softplus
                a = jnp.exp(log_a)
                th = jnp.tanh(log_a)
                bx = jnp.sqrt(-2.0 * th / (1.0 - th)) * (ig * xc_scr[r])
                if p_run is None:
                    p_run, h_run = a, bx
                else:
                    h_run = a * h_run + bx
                    p_run = a * p_run
                pl_scr[r] = p_run
                hl_scr[r] = h_run
            state["run"] = (p_run, h_run)
        return run

    def row_scan():
        pa, hb = state["run"]
        sft = 1
        while sft < TL:
            keep = row >= sft
            pa_s = jnp.where(keep, pltpu.roll(pa, sft, axis=0), 1.0)
            hb_s = jnp.where(keep, pltpu.roll(hb, sft, axis=0), 0.0)
            hb = pa * hb_s + hb
            pa = pa * pa_s
            sft *= 2
        carry = carry_scr[0:1, :]
        e = pa * carry + hb
        state["e_prev"] = jnp.where(row == 0, carry, pltpu.roll(e, 1, axis=0))
        carry_scr[...] = jnp.broadcast_to(e[TL - 1:TL, :], carry_scr.shape)

    def finalize(after=0.0):
        e_prev = state["e_prev"] + after
        for r in range(NS):
            h = pl_scr[r] * e_prev + hl_scr[r]
            y_ref[r] = jax.nn.gelu(gl_ref[r], approximate=True) * h

    half = NS // 2

    def second_half(after=0.0):
        local_scan(range(half, NS))(after)
        row_scan()

    return [(conv, LRU_WIDTH), (gates, 2 * LRU_WIDTH), (local_scan(range(half)), LRU_WIDTH),
            (second_half, LRU_WIDTH), (finalize, LRU_WIDTH)]


TM4 = NS * TL
NG4 = TM4 // PG
FF_CHUNK = 1024


def _out_body(x_ref, at_ref, xl0_ref, gl0_ref, xln_ref, gln_ref, zero_ref,
              cw_ref, cb_ref, wg_ref, bg_ref, lam_ref,
              perm_ref, ga_ref, gl_ref, wo_ref, g2_ref, wu_ref, wd_ref, gf_ref,
              o_ref,
              lr_scr, tail_scr, carry_scr, xc_scr, g_scr, pl_scr, hl_scr):
    lru_params = (cw_ref, cb_ref, wg_ref, bg_ref, lam_ref)
    lru_state = (tail_scr, carry_scr, xc_scr, g_scr, pl_scr, hl_scr)

    @pl.when(pl.program_id(0) == 0)
    def _():
        tail_scr[...] = jnp.zeros_like(tail_scr)
        carry_scr[...] = jnp.zeros_like(carry_scr)
        for phase, _ in _lru_phases(xl0_ref, gl0_ref, *lru_params, lr_scr, *lru_state):
            phase()

    groups = []
    for g in range(NG4):
        rows = slice(g * RG, (g + 1) * RG)
        attn = jnp.concatenate(
            [jnp.concatenate([at_ref[j, r, rows] for j in range(NSLAB)], axis=1) for r in range(NS)],
            axis=0)
        lru = jnp.concatenate([lr_scr[r, rows] for r in range(NS)], axis=0)
        mixed = jnp.concatenate([_rms(attn, ga_ref[...]), _rms(lru, gl_ref[...])], axis=1)
        groups.append(jnp.dot(perm_ref[...], mixed.astype(BF16),
                              preferred_element_type=F32).astype(BF16))
    mixed = jnp.concatenate(groups, axis=0)
    lru_next = _lru_phases(xln_ref, gln_ref, *lru_params, lr_scr, *lru_state)
    n_chunks = D_FF // FF_CHUNK
    assert len(lru_next) <= n_chunks + 1

    def zero_after(val, width):
        bits = pltpu.bitcast(val[0:1, :width], jnp.int32) & zero_ref[0:1, :width]
        return pltpu.bitcast(bits, F32)

    h = x_ref[...] + jnp.dot(mixed, wo_ref[...], preferred_element_type=F32)
    u = _rms(h, g2_ref[...]).astype(BF16)
    acc = h
    phase, width = lru_next.pop(0)
    phase(zero_after(h, width))
    for c in range(n_chunks):
        if lru_next:
            phase, width = lru_next.pop(0)
            phase(zero_after(acc, width))
        sl = slice(c * FF_CHUNK, (c + 1) * FF_CHUNK)
        f = jnp.dot(u, wu_ref[:, sl], preferred_element_type=F32)
        f = jnp.square(jnp.maximum(f, 0.0)).astype(BF16)
        acc = acc + jnp.dot(f, wd_ref[sl, :], preferred_element_type=F32)
    o_ref[...] = _rms(acc, gf_ref[...])


def _out_mlp(x2, attn, xl, gl_in, lru_params, ga, gl, wo, g2, wu, wd, gf):
    def const(a):
        return pl.BlockSpec(a.shape, lambda i: (0,) * a.ndim, pipeline_mode=pl.Buffered(1))

    n_tiles = SEQ // TM4
    xspec = pl.BlockSpec((TM4, D_MODEL), lambda i: (i, 0))
    first = pl.BlockSpec((NS, TL, LRU_WIDTH), lambda i: (0, 0, 0), pipeline_mode=pl.Buffered(1))
    ahead = pl.BlockSpec((NS, TL, LRU_WIDTH), lambda i: (0, jnp.minimum(i + 1, n_tiles - 1), 0))
    perm = jnp.asarray(_group_perm(), BF16)
    zero = jnp.zeros((8, D_MODEL), jnp.int32)
    consts = (zero,) + tuple(lru_params) + (perm, ga, gl, wo, g2, wu, wd, gf)
    return pl.pallas_call(
        _out_body,
        grid=(n_tiles,),
        in_specs=[
            xspec,
            pl.BlockSpec((NSLAB, NS, TL, LANES), lambda i: (0, 0, i, 0)),
            first, first, ahead, ahead,
        ] + [const(a) for a in consts],
        out_specs=xspec,
        out_shape=jax.ShapeDtypeStruct((SEQ, D_MODEL), F32),
        scratch_shapes=[
            pltpu.VMEM((NS, TL, LRU_WIDTH), F32),
            pltpu.VMEM((CONV_WIDTH - 1, 8, LRU_WIDTH), F32),
            pltpu.VMEM((8, LRU_WIDTH), F32),
            pltpu.VMEM((NS, TL, LRU_WIDTH), F32),
            pltpu.VMEM((NS * TL, 2 * LRU_WIDTH), F32),
            pltpu.VMEM((NS, TL, LRU_WIDTH), F32),
            pltpu.VMEM((NS, TL, LRU_WIDTH), F32),
        ],
        compiler_params=pltpu.CompilerParams(
            dimension_semantics=("arbitrary",), vmem_limit_bytes=VMEM_LIMIT),
        name="out_mlp",
    )(x2, attn, xl, gl_in, xl, gl_in, *consts)


@functools.lru_cache(maxsize=None)
def _rope_tables():
    inv_freq = ROPE_THETA ** (-np.arange(0, ROT_DIM, 2, dtype=np.float64) / ROT_DIM)
    pad = HEAD_DIM - ROT_DIM
    reps = LANES // HEAD_DIM

    def lanes(rot_first, rot_second, rest, n):
        head = np.concatenate([rot_first, rot_second, np.full((n, pad), rest)], axis=1)
        return np.tile(head, (1, reps))

    tile_ang = (TM1 * np.arange(SEQ // TM1, dtype=np.float64))[:, None] * inv_freq[None, :]
    tc, ts = np.cos(tile_ang), np.sin(tile_ang)
    tiles = np.concatenate([lanes(tc, tc, 1.0, len(tc)), lanes(ts, ts, 0.0, len(ts))], axis=1)
    off_ang = np.arange(TM1, dtype=np.float64)[:, None] * inv_freq[None, :]
    oc, osn = np.cos(off_ang), np.sin(off_ang)
    offs = np.stack([lanes(oc, oc, 1.0, TM1), lanes(osn, osn, 0.0, TM1),
                     lanes(-oc, oc, 0.0, TM1), lanes(-osn, osn, 0.0, TM1)])
    offs = offs.reshape(4, NG1, RG, NS, LANES).transpose(0, 1, 3, 2, 4).reshape(4, TM1, LANES)
    return (tiles.astype(np.float32)[:, None, :], np.ascontiguousarray(offs.astype(np.float32)))


def _block_diag(w):
    n, b, _ = w.shape
    eye = jnp.eye(n, dtype=w.dtype)
    return (eye[:, None, :, None] * w[:, :, None, :]).reshape(n * b, n * b)


def _gate_tiles(w_r, w_i):
    per = MXU_TILE // (LRU_WIDTH // LRU_BLOCKS)
    tiles = [jnp.concatenate([_block_diag(w_r[t * per:(t + 1) * per]),
                              _block_diag(w_i[t * per:(t + 1) * per])], axis=1)
             for t in range(LRU_WIDTH // MXU_TILE)]
    return jnp.stack(tiles).astype(BF16)


def kernel(x, norm1_g, w_in, conv_w, conv_b, w_rgate, b_rgate, w_igate, b_igate, lru_lambda,
           attn_out_g, lru_out_g, w_out, norm2_g, w_mlp_up, w_mlp_down, final_g):
    assert x.shape == (1, SEQ, D_MODEL) and w_in.shape[0] == 1
    x2 = x.reshape(SEQ, D_MODEL)
    q, k, v, q1, k1, v1, xl, gl = _inproj(x2, norm1_g.reshape(1, D_MODEL), w_in[0])
    attn, wo, wu, wd = _attention(q, k, v, q1, k1, v1, (w_out[0], w_mlp_up[0], w_mlp_down[0]))
    w_gate = _gate_tiles(w_rgate[0], w_igate[0])
    b_gate = jnp.concatenate([b_rgate[0].reshape(1, -1), b_igate[0].reshape(1, -1)], axis=1)
    lru_params = (conv_w[0], conv_b[0].reshape(1, -1), w_gate, b_gate, lru_lambda[0].reshape(1, -1))
    out = _out_mlp(x2, attn, xl, gl, lru_params,
                   attn_out_g[0].reshape(1, -1), lru_out_g[0].reshape(1, -1),
                   wo, norm2_g[0].reshape(1, -1), wu, wd, final_g.reshape(1, -1))
    return out.reshape(1, SEQ, D_MODEL)
```

```python
import functools

import numpy as np
import jax
import jax.numpy as jnp
from jax import lax
from jax.experimental import pallas as pl
from jax.experimental.pallas import tpu as pltpu

F32 = jnp.float32
BF16 = jnp.bfloat16

D_MODEL = 1024
SEQ = 16384
ATTN_HEADS = 8
HEAD_DIM = 64
ATTN_WIDTH = ATTN_HEADS * HEAD_DIM
ROT_DIM = HEAD_DIM // 4
ROPE_THETA = 500000.0
LRU_WIDTH = D_MODEL - ATTN_WIDTH
LRU_BLOCKS = 8
CONV_WIDTH = 4
LRU_C = 8.0
IN_WIDTH = 3 * ATTN_WIDTH + 2 * LRU_WIDTH
D_FF = 4 * D_MODEL
EPS = 1e-6
WINDOW_STEPS = 128

NS = 16
RA = SEQ // NS
LANES = 128
MXU_TILE = 256
NSLAB = ATTN_WIDTH // LANES
NEG = -1e30

VMEM_LIMIT = 56 * 1024 * 1024


def _rms(x, g):
    return x * lax.rsqrt(jnp.mean(x * x, axis=-1, keepdims=True) + EPS) * g


TM1 = 512
PG = NS * NS
RG = PG // NS
NG1 = TM1 // PG
CH = 8
LOG2E = 1.4426950408889634


@functools.lru_cache(maxsize=None)
def _group_perm():
    n = np.arange(PG)
    p = np.zeros((PG, PG), np.float32)
    p[(n % NS) * RG + n // NS, n] = 1.0
    return p


def _inproj_body(x_ref, g_ref, w_ref, perm_ref, tile_ref, off_ref,
                 q_ref, k_ref, v_ref, q1_ref, k1_ref, v1_ref, xl_ref, gl_ref, w_scr):
    @pl.when(pl.program_id(0) == 0)
    def _():
        w_scr[...] = w_ref[...].astype(BF16)

    u = _rms(x_ref[...], g_ref[...]).astype(BF16)
    u = jnp.concatenate(
        [jnp.dot(perm_ref[...], u[g * PG:(g + 1) * PG], preferred_element_type=F32).astype(BF16)
         for g in range(NG1)], axis=0)
    z = jnp.dot(u, w_scr[...], preferred_element_type=F32)
    tile_cos, tile_sin = tile_ref[:, :LANES], tile_ref[:, LANES:]
    c = tile_cos * off_ref[0] - tile_sin * off_ref[1]
    s = tile_sin * off_ref[2] + tile_cos * off_ref[3]
    c = jnp.concatenate([c] * NSLAB, axis=1)
    s = jnp.concatenate([s] * NSLAB, axis=1)
    lane = lax.broadcasted_iota(jnp.int32, (1, ATTN_WIDTH), 1) % HEAD_DIM
    first_half = lane < ROT_DIM // 2

    def rope(t):
        up = pltpu.roll(t, ATTN_WIDTH - ROT_DIM // 2, axis=1)
        dn = pltpu.roll(t, ROT_DIM // 2, axis=1)
        return t * c + jnp.where(first_half, up, dn) * s

    q = rope(z[:, :ATTN_WIDTH]) * (HEAD_DIM ** -0.5 * LOG2E)
    k = rope(z[:, ATTN_WIDTH:2 * ATTN_WIDTH])
    v = z[:, 2 * ATTN_WIDTH:3 * ATTN_WIDTH]
    xl = z[:, 3 * ATTN_WIDTH:3 * ATTN_WIDTH + LRU_WIDTH]
    gl = z[:, 3 * ATTN_WIDTH + LRU_WIDTH:]
    for g in range(NG1):
        for r in range(NS):
            src = slice(g * PG + r * RG, g * PG + (r + 1) * RG)
            dst = slice(g * RG, (g + 1) * RG)
            for j in range(NSLAB):
                sl = slice(j * LANES, (j + 1) * LANES)
                q_ref[j, r, dst] = q[src, sl].astype(BF16)
                k_ref[j, r, dst] = k[src, sl].astype(BF16)
                v_ref[j, r, dst] = v[src, sl].astype(BF16)
            xl_ref[r, dst] = xl[src]
            gl_ref[r, dst] = gl[src]
    for g in range(NG1):
        for cl in range(RG // CH):
            for r in range(0, NS, 2):
                lo = g * PG + r * RG + cl * CH
                hi = lo + RG
                d0 = (g * (RG // CH) + cl) * NS * CH + r * CH
                dst = slice(d0, d0 + 2 * CH)
                for j in range(NSLAB):
                    sl = slice(j * LANES, (j + 1) * LANES)
                    for src, ref in ((q, q1_ref), (k, k1_ref), (v, v1_ref)):
                        ref[j, dst] = jnp.concatenate(
                            [src[lo:lo + CH, sl], src[hi:hi + CH, sl]], axis=0).astype(BF16)


def _inproj(x2, g, w):
    steps = SEQ // TM1
    rows = TM1 // NS
    slab_spec = pl.BlockSpec((NSLAB, NS, rows, LANES), lambda i: (0, 0, i, 0))
    chunk_spec = pl.BlockSpec((NSLAB, TM1, LANES), lambda i: (0, i, 0))
    row_spec = pl.BlockSpec((NS, rows, LRU_WIDTH), lambda i: (0, i, 0))
    rope_tiles, rope_offs = _rope_tables()
    slab_shape = jax.ShapeDtypeStruct((NSLAB, NS, RA, LANES), BF16)
    chunk_shape = jax.ShapeDtypeStruct((NSLAB, SEQ, LANES), BF16)
    row_shape = jax.ShapeDtypeStruct((NS, RA, LRU_WIDTH), F32)
    perm = jnp.asarray(_group_perm(), BF16)
    return pl.pallas_call(
        _inproj_body,
        grid=(steps,),
        in_specs=[
            pl.BlockSpec((TM1, D_MODEL), lambda i: (i, 0)),
            pl.BlockSpec((1, D_MODEL), lambda i: (0, 0)),
            pl.BlockSpec((D_MODEL, IN_WIDTH), lambda i: (0, 0), pipeline_mode=pl.Buffered(1)),
            pl.BlockSpec((PG, PG), lambda i: (0, 0)),
            pl.BlockSpec((None, 1, 2 * LANES), lambda i: (i, 0, 0)),
            pl.BlockSpec(rope_offs.shape, lambda i: (0, 0, 0)),
        ],
        out_specs=[slab_spec] * 3 + [chunk_spec] * 3 + [row_spec] * 2,
        out_shape=[slab_shape] * 3 + [chunk_shape] * 3 + [row_shape] * 2,
        scratch_shapes=[pltpu.VMEM((D_MODEL, IN_WIDTH), BF16)],
        compiler_params=pltpu.CompilerParams(
            dimension_semantics=("arbitrary",), vmem_limit_bytes=VMEM_LIMIT),
        name="inproj",
    )(x2, g, w, perm, rope_tiles, rope_offs)


TA = 128
UQ, UK = 128, 256
Q4, K4 = UQ // 4, UK // 4
AHEAD = 4
RING = 2 * AHEAD


@functools.lru_cache(maxsize=None)
def _attn_masks():
    def variants(diff, from_prev, per_unit):
        ok = (diff >= 0) & (diff <= WINDOW_STEPS)
        normal = np.where(ok, 0.0, NEG).astype(np.float32)
        first = np.where(ok & ~from_prev, 0.0, NEG).astype(np.float32)
        if per_unit:
            return np.stack([np.stack([first, normal]), np.stack([normal, normal])])
        return np.stack([first, normal])

    iq = np.arange(UQ)[:, None]
    ck = np.arange(UK)[None, :]
    m16 = variants(iq - ck + TA, np.broadcast_to(ck < TA, (UQ, UK)), False)
    cq, i4 = np.divmod(np.arange(UQ), Q4)
    ckk, j4 = np.divmod(np.arange(UK), K4)
    d4 = 4 * (i4[:, None] - j4[None, :] + Q4) + (cq[:, None] - ckk[None, :])
    m4 = variants(d4, np.broadcast_to(j4[None, :] < K4 - Q4, d4.shape), True)
    rq, i1 = np.divmod(np.arange(UQ), CH)
    kc, kin = np.divmod(np.arange(UK), UQ)
    rk, j1 = np.divmod(kin, CH)
    d1 = (NS * i1 + rq)[:, None] - (NS * j1 + rk + UQ * (kc - 1))[None, :]
    m1 = variants(d1, np.broadcast_to(kc[None, :] == 0, d1.shape), True)
    return m16, m4, m1


def _attn_body(q_ref, kc_ref, kp_ref, vc_ref, vp_ref, q1_ref, k1c_ref, k1p_ref, v1c_ref, v1p_ref,
               m16_ref, m4_ref, m1_ref, wo_ref, wu_ref, wd_ref,
               o_ref, wo_bf_ref, wu_bf_ref, wd_bf_ref,
               qm_scr, q1m_scr, k1_scr, v1_scr, s_scr, acc_scr, max_scr, sum_scr):
    wo_bf_ref[...] = wo_ref[...].astype(BF16)
    wu_bf_ref[...] = wu_ref[...].astype(BF16)
    wd_bf_ref[...] = wd_ref[...].astype(BF16)

    lane = lax.broadcasted_iota(jnp.int32, (1, LANES), 1)
    head0 = lane < HEAD_DIM
    zero = jnp.zeros((), BF16)

    def split(x):
        return jnp.where(head0, x, zero), jnp.where(head0, zero, x)

    qm_scr[0], qm_scr[1] = split(q_ref[...])
    q1m_scr[0], q1m_scr[1] = split(q1_ref[...])
    k1_scr[0:UQ] = k1p_ref[...]
    k1_scr[UQ:] = k1c_ref[...]
    v1_scr[0:UQ] = v1p_ref[...]
    v1_scr[UQ:] = v1c_ref[...]
    ones = jnp.ones((UK, LANES), BF16)

    def unit16(r):
        def store(o, mx, sm):
            acc_scr[0, r], max_scr[0, r], sum_scr[0, r] = o, mx, sm

        return (lambda h: qm_scr[h, r],
                lambda: jnp.concatenate([kp_ref[r], kc_ref[r]], axis=0),
                lambda: jnp.concatenate([vp_ref[r], vc_ref[r]], axis=0),
                lambda: m16_ref[...], store)

    def unit4(r4, b):
        streams = [r4 + 4 * c for c in range(4)]
        q0, k0 = b * Q4, b * Q4 + Q4 - K4

        def window(cur, prev, st):
            if k0 < 0:
                return [prev[st, TA + k0:TA], cur[st, 0:k0 + K4]]
            return [cur[st, k0:k0 + K4]]

        def store(o, mx, sm):
            for c, st in enumerate(streams):
                rows = slice(c * Q4, (c + 1) * Q4)
                acc_scr[1, st, q0:q0 + Q4] = o[rows]
                max_scr[1, st, q0:q0 + Q4] = mx[rows]
                sum_scr[1, st, q0:q0 + Q4] = sm[rows]

        return (lambda h: jnp.concatenate([qm_scr[h, st, q0:q0 + Q4] for st in streams], axis=0),
                lambda: jnp.concatenate(
                    [x for st in streams for x in window(kc_ref, kp_ref, st)], axis=0),
                lambda: jnp.concatenate(
                    [x for st in streams for x in window(vc_ref, vp_ref, st)], axis=0),
                lambda: m4_ref[min(b, 1)], store)

    def unit1(u):
        off, row = u * UQ, u * CH

        def store(o, mx, sm):
            for r in range(NS):
                rows = slice(r * CH, (r + 1) * CH)
                acc_scr[2, r, row:row + CH] = o[rows]
                max_scr[2, r, row:row + CH] = mx[rows]
                sum_scr[2, r, row:row + CH] = sm[rows]

        return (lambda h: q1m_scr[h, off:off + UQ],
                lambda: k1_scr[off:off + UK],
                lambda: v1_scr[off:off + UK],
                lambda: m1_ref[min(u, 1)], store)

    all_units = ([unit16(n) for n in range(NS)]
                 + [unit4(n // (TA // Q4), n % (TA // Q4)) for n in range(NS)]
                 + [unit1(n) for n in range(NS)])

    def score_unit(n):
        q, k, _, _, _ = all_units[n]
        s_scr[n % RING] = lax.dot_general(
            jnp.concatenate([q(0), q(1)], axis=0), k(), (((1,), (1,)), ((), ())),
            preferred_element_type=F32)

    def value_unit(n):
        _, _, v, mask, store = all_units[n]
        ps, mx = [], []
        for h in range(2):
            s = s_scr[n % RING, h * UQ:(h + 1) * UQ] + mask()
            m = jnp.max(s, axis=-1, keepdims=True)
            ps.append(jnp.exp2(s - m).astype(BF16))
            mx.append(m)
        res = jnp.dot(jnp.concatenate(ps, axis=0), jnp.concatenate([v(), ones], axis=1),
                      preferred_element_type=F32)
        top, bot = res[:UQ], res[UQ:]
        store(jnp.where(head0, top[:, :LANES], bot[:, :LANES]),
              jnp.where(head0, mx[0], mx[1]),
              jnp.where(head0, top[:, LANES:], bot[:, LANES:]))

    for n in range(AHEAD):
        score_unit(n)
    for n in range(len(all_units)):
        if n + AHEAD < len(all_units):
            score_unit(n + AHEAD)
        value_unit(n)

    def comb(r, carry):
        m0, m1, m2 = max_scr[0, r], max_scr[1, r], max_scr[2, r]
        m = jnp.maximum(jnp.maximum(m0, m1), m2)
        w0, w1, w2 = jnp.exp2(m0 - m), jnp.exp2(m1 - m), jnp.exp2(m2 - m)
        num = w0 * acc_scr[0, r] + w1 * acc_scr[1, r] + w2 * acc_scr[2, r]
        den = w0 * sum_scr[0, r] + w1 * sum_scr[1, r] + w2 * sum_scr[2, r]
        o_ref[r] = num / den
        return carry

    lax.fori_loop(0, NS, comb, 0, unroll=2)


def _attention(q, k, v, q1, k1, v1, later_weights):
    m16, m4, m1 = _attn_masks()
    cur = pl.BlockSpec((None, NS, TA, LANES), lambda j, i: (j, 0, i, 0))
    prev = pl.BlockSpec((None, NS, TA, LANES), lambda j, i: (j, 0, jnp.maximum(i - 1, 0), 0))
    cur1 = pl.BlockSpec((None, NS * TA, LANES), lambda j, i: (j, i, 0))
    prev1 = pl.BlockSpec((None, UQ, LANES),
                         lambda j, i: (j, jnp.maximum(i * (NS * TA // UQ) - 1, 0), 0))

    def tile_kind(m):
        return pl.BlockSpec((None,) + m.shape[1:],
                            lambda j, i: (jnp.minimum(i, 1),) + (0,) * (m.ndim - 1))

    n_tiles = RA // TA
    steps = NSLAB * n_tiles
    cast_specs = [pl.BlockSpec((a.shape[0] // steps, a.shape[1]), lambda j, i: (j * n_tiles + i, 0))
                  for a in later_weights]
    cast_shapes = [jax.ShapeDtypeStruct(a.shape, BF16) for a in later_weights]
    return pl.pallas_call(
        _attn_body,
        grid=(NSLAB, n_tiles),
        in_specs=[cur, cur, prev, cur, prev, cur1, cur1, prev1, cur1, prev1,
                  tile_kind(m16), tile_kind(m4), tile_kind(m1)] + cast_specs,
        out_specs=[pl.BlockSpec((None, NS, TA, LANES), lambda j, i: (j, 0, i, 0))] + cast_specs,
        out_shape=[jax.ShapeDtypeStruct((NSLAB, NS, RA, LANES), F32)] + cast_shapes,
        scratch_shapes=[
            pltpu.VMEM((2, NS, TA, LANES), BF16),
            pltpu.VMEM((2, NS * TA, LANES), BF16),
            pltpu.VMEM((UQ + NS * TA, LANES), BF16),
            pltpu.VMEM((UQ + NS * TA, LANES), BF16),
            pltpu.VMEM((RING, 2 * UQ, UK), F32),
            pltpu.VMEM((3, NS, TA, LANES), F32),
            pltpu.VMEM((3, NS, TA, LANES), F32),
            pltpu.VMEM((3, NS, TA, LANES), F32),
        ],
        compiler_params=pltpu.CompilerParams(
            dimension_semantics=("arbitrary", "arbitrary"), vmem_limit_bytes=VMEM_LIMIT),
        name="dilated_attn",
    )(q, k, k, v, v, q1, k1, k1, v1, v1, m16, m4, m1, *later_weights)


TL = 32


def _lru_phases(xl_ref, gl_ref, cw_ref, cb_ref, wg_ref, bg_ref, lam_ref, y_ref,
                tail_scr, carry_scr, xc_scr, g_scr, pl_scr, hl_scr):
    row = lax.broadcasted_iota(jnp.int32, (TL, LRU_WIDTH), 0)
    state = {}

    def conv(after=0.0):
        cw = cw_ref[...]
        cb = cb_ref[...] + after

        def tap(r, k):
            st = r - k
            if st >= 0:
                return xl_ref[st]
            st += NS
            prev_last = tail_scr[st - (NS - CONV_WIDTH + 1), 7:8, :]
            return jnp.where(row == 0, prev_last, pltpu.roll(xl_ref[st], 1, axis=0))

        for r in range(NS):
            xc_scr[r] = cb + sum(cw[CONV_WIDTH - 1 - k:CONV_WIDTH - k] * tap(r, k)
                                 for k in range(CONV_WIDTH))
        for n in range(CONV_WIDTH - 1):
            tail_scr[n] = xl_ref[NS - CONV_WIDTH + 1 + n, TL - 8:TL, :]

    def gates(after=0.0):
        xc_all = xc_scr[...].reshape(NS * TL, LRU_WIDTH).astype(BF16)
        bias = bg_ref[...] + after
        for t in range(LRU_WIDTH // MXU_TILE):
            cols = slice(t * MXU_TILE, (t + 1) * MXU_TILE)
            res = jnp.dot(xc_all[:, cols], wg_ref[t], preferred_element_type=F32)
            for gate in range(2):
                dst = slice(gate * LRU_WIDTH + t * MXU_TILE, gate * LRU_WIDTH + (t + 1) * MXU_TILE)
                g_scr[:, dst] = res[:, gate * MXU_TILE:(gate + 1) * MXU_TILE] + bias[:, dst]

    def local_scan(streams):
        def run(after=0.0):
            neg_lam = after - lam_ref[...]
            softplus = jnp.maximum(neg_lam, 0.0) + jnp.log1p(jnp.exp(-jnp.abs(neg_lam)))
            p_run, h_run = state.get("run", (None, None))
            for r in streams:
                g = g_scr[r * TL:(r + 1) * TL]
                rg = jax.nn.sigmoid(g[:, :LRU_WIDTH])
                ig = jax.nn.sigmoid(g[:, LRU_WIDTH:])
                log_a = -LRU_C * rg * softplus
                a = jnp.exp(log_a)
                th = jnp.tanh(log_a)
                bx = jnp.sqrt(-2.0 * th / (1.0 - th)) * (ig * xc_scr[r])
                if p_run is None:
                    p_run, h_run = a, bx
                else:
                    h_run = a * h_run + bx
                    p_run = a * p_run
                pl_scr[r] = p_run
                hl_scr[r] = h_run
            state["run"] = (p_run, h_run)
        return run

    def row_scan():
        pa, hb = state["run"]
        sft = 1
        while sft < TL:
            keep = row >= sft
            pa_s = jnp.where(keep, pltpu.roll(pa, sft, axis=0), 1.0)
            hb_s = jnp.where(keep, pltpu.roll(hb, sft, axis=0), 0.0)
            hb = pa * hb_s + hb
            pa = pa * pa_s
            sft *= 2
        carry = carry_scr[0:1, :]
        e = pa * carry + hb
        state["e_prev"] = jnp.where(row == 0, carry, pltpu.roll(e, 1, axis=0))
        carry_scr[...] = jnp.broadcast_to(e[TL - 1:TL, :], carry_scr.shape)

    def finalize(after=0.0):
        e_prev = state["e_prev"] + after
        for r in range(NS):
            h = pl_scr[r] * e_prev + hl_scr[r]
            y_ref[r] = jax.nn.gelu(gl_ref[r], approximate=True) * h

    half = NS // 2

    def second_half(after=0.0):
        local_scan(range(half, NS))(after)
        row_scan()

    return [(conv, LRU_WIDTH), (gates, 2 * LRU_WIDTH), (local_scan(range(half)), LRU_WIDTH),
            (second_half, LRU_WIDTH), (finalize, LRU_WIDTH)]


TM4 = NS * TL
NG4 = TM4 // PG
FF_CHUNK = 1024


def _out_body(x_ref, at_ref, xl0_ref, gl0_ref, xln_ref, gln_ref, zero_ref,
              cw_ref, cb_ref, wg_ref, bg_ref, lam_ref,
              perm_ref, ga_ref, gl_ref, wo_ref, g2_ref, wu_ref, wd_ref, gf_ref,
              o_ref,
              lr_scr, tail_scr, carry_scr, xc_scr, g_scr, pl_scr, hl_scr):
    lru_params = (cw_ref, cb_ref, wg_ref, bg_ref, lam_ref)
    lru_state = (tail_scr, carry_scr, xc_scr, g_scr, pl_scr, hl_scr)

    @pl.when(pl.program_id(0) == 0)
    def _():
        tail_scr[...] = jnp.zeros_like(tail_scr)
        carry_scr[...] = jnp.zeros_like(carry_scr)
        for phase, _ in _lru_phases(xl0_ref, gl0_ref, *lru_params, lr_scr, *lru_state):
            phase()

    groups = []
    for g in range(NG4):
        rows = slice(g * RG, (g + 1) * RG)
        attn = jnp.concatenate(
            [jnp.concatenate([at_ref[j, r, rows] for j in range(NSLAB)], axis=1) for r in range(NS)],
            axis=0)
        lru = jnp.concatenate([lr_scr[r, rows] for r in range(NS)], axis=0)
        mixed = jnp.concatenate([_rms(attn, ga_ref[...]), _rms(lru, gl_ref[...])], axis=1)
        groups.append(jnp.dot(perm_ref[...], mixed.astype(BF16),
                              preferred_element_type=F32).astype(BF16))
    mixed = jnp.concatenate(groups, axis=0)
    lru_next = _lru_phases(xln_ref, gln_ref, *lru_params, lr_scr, *lru_state)
    n_chunks = D_FF // FF_CHUNK
    assert len(lru_next) <= n_chunks + 1

    def zero_after(val, width):
        bits = pltpu.bitcast(val[0:1, :width], jnp.int32) & zero_ref[0:1, :width]
        return pltpu.bitcast(bits, F32)

    h = x_ref[...] + jnp.dot(mixed, wo_ref[...], preferred_element_type=F32)
    u = _rms(h, g2_ref[...]).astype(BF16)
    acc = h
    phase, width = lru_next.pop(0)
    phase(zero_after(h, width))
    for c in range(n_chunks):
        if lru_next:
            phase, width = lru_next.pop(0)
            phase(zero_after(acc, width))
        sl = slice(c * FF_CHUNK, (c + 1) * FF_CHUNK)
        f = jnp.dot(u, wu_ref[:, sl], preferred_element_type=F32)
        f = jnp.square(jnp.maximum(f, 0.0)).astype(BF16)
        acc = acc + jnp.dot(f, wd_ref[sl, :], preferred_element_type=F32)
    o_ref[...] = _rms(acc, gf_ref[...])


def _out_mlp(x2, attn, xl, gl_in, lru_params, ga, gl, wo, g2, wu, wd, gf):
    def const(a):
        return pl.BlockSpec(a.shape, lambda i: (0,) * a.ndim, pipeline_mode=pl.Buffered(1))

    n_tiles = SEQ // TM4
    xspec = pl.BlockSpec((TM4, D_MODEL), lambda i: (i, 0))
    first = pl.BlockSpec((NS, TL, LRU_WIDTH), lambda i: (0, 0, 0), pipeline_mode=pl.Buffered(1))
    ahead = pl.BlockSpec((NS, TL, LRU_WIDTH), lambda i: (0, jnp.minimum(i + 1, n_tiles - 1), 0))
    perm = jnp.asarray(_group_perm(), BF16)
    zero = jnp.zeros((8, D_MODEL), jnp.int32)
    consts = (zero,) + tuple(lru_params) + (perm, ga, gl, wo, g2, wu, wd, gf)
    return pl.pallas_call(
        _out_body,
        grid=(n_tiles,),
        in_specs=[
            xspec,
            pl.BlockSpec((NSLAB, NS, TL, LANES), lambda i: (0, 0, i, 0)),
            first, first, ahead, ahead,
        ] + [const(a) for a in consts],
        out_specs=xspec,
        out_shape=jax.ShapeDtypeStruct((SEQ, D_MODEL), F32),
        scratch_shapes=[
            pltpu.VMEM((NS, TL, LRU_WIDTH), F32),
            pltpu.VMEM((CONV_WIDTH - 1, 8, LRU_WIDTH), F32),
            pltpu.VMEM((8, LRU_WIDTH), F32),
            pltpu.VMEM((NS, TL, LRU_WIDTH), F32),
            pltpu.VMEM((NS * TL, 2 * LRU_WIDTH), F32),
            pltpu.VMEM((NS, TL, LRU_WIDTH), F32),
            pltpu.VMEM((NS, TL, LRU_WIDTH), F32),
        ],
        compiler_params=pltpu.CompilerParams(
            dimension_semantics=("arbitrary",), vmem_limit_bytes=VMEM_LIMIT),
        name="out_mlp",
    )(x2, attn, xl, gl_in, xl, gl_in, *consts)


@functools.lru_cache(maxsize=None)
def _rope_tables():
    inv_freq = ROPE_THETA ** (-np.arange(0, ROT_DIM, 2, dtype=np.float64) / ROT_DIM)
    pad = HEAD_DIM - ROT_DIM
    reps = LANES // HEAD_DIM

    def lanes(rot_first, rot_second, rest, n):
        head = np.concatenate([rot_first, rot_second, np.full((n, pad), rest)], axis=1)
        return np.tile(head, (1, reps))

    tile_ang = (TM1 * np.arange(SEQ // TM1, dtype=np.float64))[:, None] * inv_freq[None, :]
    tc, ts = np.cos(tile_ang), np.sin(tile_ang)
    tiles = np.concatenate([lanes(tc, tc, 1.0, len(tc)), lanes(ts, ts, 0.0, len(ts))], axis=1)
    off_ang = np.arange(TM1, dtype=np.float64)[:, None] * inv_freq[None, :]
    oc, osn = np.cos(off_ang), np.sin(off_ang)
    offs = np.stack([lanes(oc, oc, 1.0, TM1), lanes(osn, osn, 0.0, TM1),
                     lanes(-oc, oc, 0.0, TM1), lanes(-osn, osn, 0.0, TM1)])
    offs = offs.reshape(4, NG1, RG, NS, LANES).transpose(0, 1, 3, 2, 4).reshape(4, TM1, LANES)
    return (tiles.astype(np.float32)[:, None, :], np.ascontiguousarray(offs.astype(np.float32)))


def _block_diag(w):
    n, b, _ = w.shape
    eye = jnp.eye(n, dtype=w.dtype)
    return (eye[:, None, :, None] * w[:, :, None, :]).reshape(n * b, n * b)


def _gate_tiles(w_r, w_i):
    per = MXU_TILE // (LRU_WIDTH // LRU_BLOCKS)
    tiles = [jnp.concatenate([_block_diag(w_r[t * per:(t + 1) * per]),
                              _block_diag(w_i[t * per:(t + 1) * per])], axis=1)
             for t in range(LRU_WIDTH // MXU_TILE)]
    return jnp.stack(tiles).astype(BF16)


def kernel(x, norm1_g, w_in, conv_w, conv_b, w_rgate, b_rgate, w_igate, b_igate, lru_lambda,
           attn_out_g, lru_out_g, w_out, norm2_g, w_mlp_up, w_mlp_down, final_g):
    assert x.shape == (1, SEQ, D_MODEL) and w_in.shape[0] == 1
    x2 = x.reshape(SEQ, D_MODEL)
    q, k, v, q1, k1, v1, xl, gl = _inproj(x2, norm1_g.reshape(1, D_MODEL), w_in[0])
    attn, wo, wu, wd = _attention(q, k, v, q1, k1, v1, (w_out[0], w_mlp_up[0], w_mlp_down[0]))
    w_gate = _gate_tiles(w_rgate[0], w_igate[0])
    b_gate = jnp.concatenate([b_rgate[0].reshape(1, -1), b_igate[0].reshape(1, -1)], axis=1)
    lru_params = (conv_w[0], conv_b[0].reshape(1, -1), w_gate, b_gate, lru_lambda[0].reshape(1, -1))
    out = _out_mlp(x2, attn, xl, gl, lru_params,
                   attn_out_g[0].reshape(1, -1), lru_out_g[0].reshape(1, -1),
                   wo, norm2_g[0].reshape(1, -1), wu, wd, final_g.reshape(1, -1))
    return out.reshape(1, SEQ, D_MODEL)
```

```python
import functools

import numpy as np
import jax
import jax.numpy as jnp
from jax import lax
from jax.experimental import pallas as pl
from jax.experimental.pallas import tpu as pltpu

F32 = jnp.float32
BF16 = jnp.bfloat16

D_MODEL = 1024
SEQ = 16384
ATTN_HEADS = 8
HEAD_DIM = 64
ATTN_WIDTH = ATTN_HEADS * HEAD_DIM
ROT_DIM = HEAD_DIM // 4
ROPE_THETA = 500000.0
LRU_WIDTH = D_MODEL - ATTN_WIDTH
LRU_BLOCKS = 8
CONV_WIDTH = 4
LRU_C = 8.0
IN_WIDTH = 3 * ATTN_WIDTH + 2 * LRU_WIDTH
D_FF = 4 * D_MODEL
EPS = 1e-6
WINDOW_STEPS = 128

NS = 16
RA = SEQ // NS
LANES = 128
MXU_TILE = 256
NSLAB = ATTN_WIDTH // LANES
NEG = -np.inf

VMEM_LIMIT = 56 * 1024 * 1024


def _rms(x, g):
    return x * lax.rsqrt(jnp.mean(x * x, axis=-1, keepdims=True) + EPS) * g


TM1 = 512
PG = NS * NS
RG = PG // NS
NG1 = TM1 // PG
CH = 8
LOG2E = 1.4426950408889634


@functools.lru_cache(maxsize=None)
def _group_perm():
    n = np.arange(PG)
    p = np.zeros((PG, PG), np.float32)
    p[(n % NS) * RG + n // NS, n] = 1.0
    return p


def _inproj_body(x_ref, g_ref, w_ref, perm_ref, tile_ref, off_ref,
                 q_ref, k_ref, v_ref, q1_ref, k1_ref, v1_ref, xl_ref, gl_ref, w_scr):
    @pl.when(pl.program_id(0) == 0)
    def _():
        w_scr[...] = w_ref[...].astype(BF16)

    u = _rms(x_ref[...], g_ref[...]).astype(BF16)
    u = jnp.concatenate(
        [jnp.dot(perm_ref[...], u[g * PG:(g + 1) * PG], preferred_element_type=F32).astype(BF16)
         for g in range(NG1)], axis=0)
    z = jnp.dot(u, w_scr[...], preferred_element_type=F32)
    tile_cos, tile_sin = tile_ref[:, :LANES], tile_ref[:, LANES:]
    c = tile_cos * off_ref[0] - tile_sin * off_ref[1]
    s = tile_sin * off_ref[2] + tile_cos * off_ref[3]
    c = jnp.concatenate([c] * NSLAB, axis=1)
    s = jnp.concatenate([s] * NSLAB, axis=1)
    lane = lax.broadcasted_iota(jnp.int32, (1, ATTN_WIDTH), 1) % HEAD_DIM
    first_half = lane < ROT_DIM // 2

    def rope(t):
        up = pltpu.roll(t, ATTN_WIDTH - ROT_DIM // 2, axis=1)
        dn = pltpu.roll(t, ROT_DIM // 2, axis=1)
        return t * c + jnp.where(first_half, up, dn) * s

    q = rope(z[:, :ATTN_WIDTH]) * (HEAD_DIM ** -0.5 * LOG2E)
    k = rope(z[:, ATTN_WIDTH:2 * ATTN_WIDTH])
    v = z[:, 2 * ATTN_WIDTH:3 * ATTN_WIDTH]
    xl = z[:, 3 * ATTN_WIDTH:3 * ATTN_WIDTH + LRU_WIDTH]
    gl = z[:, 3 * ATTN_WIDTH + LRU_WIDTH:]
    for g in range(NG1):
        for r in range(NS):
            src = slice(g * PG + r * RG, g * PG + (r + 1) * RG)
            dst = slice(g * RG, (g + 1) * RG)
            for j in range(NSLAB):
                sl = slice(j * LANES, (j + 1) * LANES)
                q_ref[j, r, dst] = q[src, sl].astype(BF16)
                k_ref[j, r, dst] = k[src, sl].astype(BF16)
                v_ref[j, r, dst] = v[src, sl].astype(BF16)
            xl_ref[r, dst] = xl[src]
            gl_ref[r, dst] = gl[src]
    for g in range(NG1):
        for cl in range(RG // CH):
            for r in range(0, NS, 2):
                lo = g * PG + r * RG + cl * CH
                hi = lo + RG
                d0 = (g * (RG // CH) + cl) * NS * CH + r * CH
                dst = slice(d0, d0 + 2 * CH)
                for j in range(NSLAB):
                    sl = slice(j * LANES, (j + 1) * LANES)
                    for src, ref in ((q, q1_ref), (k, k1_ref), (v, v1_ref)):
                        ref[j, dst] = jnp.concatenate(
                            [src[lo:lo + CH, sl], src[hi:hi + CH, sl]], axis=0).astype(BF16)


def _inproj(x2, g, w):
    steps = SEQ // TM1
    rows = TM1 // NS
    slab_spec = pl.BlockSpec((NSLAB, NS, rows, LANES), lambda i: (0, 0, i, 0))
    chunk_spec = pl.BlockSpec((NSLAB, TM1, LANES), lambda i: (0, i, 0))
    row_spec = pl.BlockSpec((NS, rows, LRU_WIDTH), lambda i: (0, i, 0))
    rope_tiles, rope_offs = _rope_tables()
    slab_shape = jax.ShapeDtypeStruct((NSLAB, NS, RA, LANES), BF16)
    chunk_shape = jax.ShapeDtypeStruct((NSLAB, SEQ, LANES), BF16)
    row_shape = jax.ShapeDtypeStruct((NS, RA, LRU_WIDTH), F32)
    perm = jnp.asarray(_group_perm(), BF16)
    return pl.pallas_call(
        _inproj_body,
        grid=(steps,),
        in_specs=[
            pl.BlockSpec((TM1, D_MODEL), lambda i: (i, 0)),
            pl.BlockSpec((1, D_MODEL), lambda i: (0, 0)),
            pl.BlockSpec((D_MODEL, IN_WIDTH), lambda i: (0, 0), pipeline_mode=pl.Buffered(1)),
            pl.BlockSpec((PG, PG), lambda i: (0, 0)),
            pl.BlockSpec((None, 1, 2 * LANES), lambda i: (i, 0, 0)),
            pl.BlockSpec(rope_offs.shape, lambda i: (0, 0, 0)),
        ],
        out_specs=[slab_spec] * 3 + [chunk_spec] * 3 + [row_spec] * 2,
        out_shape=[slab_shape] * 3 + [chunk_shape] * 3 + [row_shape] * 2,
        scratch_shapes=[pltpu.VMEM((D_MODEL, IN_WIDTH), BF16)],
        compiler_params=pltpu.CompilerParams(
            dimension_semantics=("arbitrary",), vmem_limit_bytes=VMEM_LIMIT),
        name="inproj",
    )(x2, g, w, perm, rope_tiles, rope_offs)


TA = 128
UQ, UK = 128, 256
Q4, K4 = UQ // 4, UK // 4
AHEAD = 4
RING = 2 * AHEAD


@functools.lru_cache(maxsize=None)
def _attn_masks():
    def variants(diff, from_prev, per_unit):
        ok = (diff >= 0) & (diff <= WINDOW_STEPS)
        normal = np.where(ok, 0.0, NEG).astype(np.float32)
        first = np.where(ok & ~from_prev, 0.0, NEG).astype(np.float32)
        if per_unit:
            return np.stack([np.stack([first, normal]), np.stack([normal, normal])])
        return np.stack([first, normal])

    iq = np.arange(UQ)[:, None]
    ck = np.arange(UK)[None, :]
    m16 = variants(iq - ck + TA, np.broadcast_to(ck < TA, (UQ, UK)), False)
    cq, i4 = np.divmod(np.arange(UQ), Q4)
    ckk, j4 = np.divmod(np.arange(UK), K4)
    d4 = 4 * (i4[:, None] - j4[None, :] + Q4) + (cq[:, None] - ckk[None, :])
    m4 = variants(d4, np.broadcast_to(j4[None, :] < K4 - Q4, d4.shape), True)
    rq, i1 = np.divmod(np.arange(UQ), CH)
    kc, kin = np.divmod(np.arange(UK), UQ)
    rk, j1 = np.divmod(kin, CH)
    d1 = (NS * i1 + rq)[:, None] - (NS * j1 + rk + UQ * (kc - 1))[None, :]
    m1 = variants(d1, np.broadcast_to(kc[None, :] == 0, d1.shape), True)
    return m16, m4, m1


def _attn_body(q_ref, kc_ref, kp_ref, vc_ref, vp_ref, q1_ref, k1c_ref, k1p_ref, v1c_ref, v1p_ref,
               m16_ref, m4_ref, m1_ref, wo_ref, wu_ref, wd_ref,
               o_ref, wo_bf_ref, wu_bf_ref, wd_bf_ref,
               qm_scr, q1m_scr, k1_scr, v1_scr, s_scr, acc_scr, max_scr, sum_scr):
    wo_bf_ref[...] = wo_ref[...].astype(BF16)
    wu_bf_ref[...] = wu_ref[...].astype(BF16)
    wd_bf_ref[...] = wd_ref[...].astype(BF16)

    lane = lax.broadcasted_iota(jnp.int32, (1, LANES), 1)
    head0 = lane < HEAD_DIM
    zero = jnp.zeros((), BF16)

    def split(x):
        return jnp.where(head0, x, zero), jnp.where(head0, zero, x)

    qm_scr[0], qm_scr[1] = split(q_ref[...])
    q1m_scr[0], q1m_scr[1] = split(q1_ref[...])
    k1_scr[0:UQ] = k1p_ref[...]
    k1_scr[UQ:] = k1c_ref[...]
    v1_scr[0:UQ] = v1p_ref[...]
    v1_scr[UQ:] = v1c_ref[...]
    ones = jnp.ones((UK, LANES), BF16)

    def unit16(r):
        def store(o, mx, sm):
            acc_scr[0, r], max_scr[0, r], sum_scr[0, r] = o, mx, sm

        return (lambda h: qm_scr[h, r],
                lambda: jnp.concatenate([kp_ref[r], kc_ref[r]], axis=0),
                lambda: jnp.concatenate([vp_ref[r], vc_ref[r]], axis=0),
                lambda: m16_ref[...], store)

    def unit4(r4, b):
        streams = [r4 + 4 * c for c in range(4)]
        q0, k0 = b * Q4, b * Q4 + Q4 - K4

        def window(cur, prev, st):
            if k0 < 0:
                return [prev[st, TA + k0:TA], cur[st, 0:k0 + K4]]
            return [cur[st, k0:k0 + K4]]

        def store(o, mx, sm):
            for c, st in enumerate(streams):
                rows = slice(c * Q4, (c + 1) * Q4)
                acc_scr[1, st, q0:q0 + Q4] = o[rows]
                max_scr[1, st, q0:q0 + Q4] = mx[rows]
                sum_scr[1, st, q0:q0 + Q4] = sm[rows]

        return (lambda h: jnp.concatenate([qm_scr[h, st, q0:q0 + Q4] for st in streams], axis=0),
                lambda: jnp.concatenate(
                    [x for st in streams for x in window(kc_ref, kp_ref, st)], axis=0),
                lambda: jnp.concatenate(
                    [x for st in streams for x in window(vc_ref, vp_ref, st)], axis=0),
                lambda: m4_ref[min(b, 1)], store)

    def unit1(u):
        off, row = u * UQ, u * CH

        def store(o, mx, sm):
            for r in range(NS):
                rows = slice(r * CH, (r + 1) * CH)
                acc_scr[2, r, row:row + CH] = o[rows]
                max_scr[2, r, row:row + CH] = mx[rows]
                sum_scr[2, r, row:row + CH] = sm[rows]

        return (lambda h: q1m_scr[h, off:off + UQ],
                lambda: k1_scr[off:off + UK],
                lambda: v1_scr[off:off + UK],
                lambda: m1_ref[min(u, 1)], store)

    all_units = ([unit16(n) for n in range(NS)]
                 + [unit4(n // (TA // Q4), n % (TA // Q4)) for n in range(NS)]
                 + [unit1(n) for n in range(NS)])

    def score_unit(n):
        q, k, _, _, _ = all_units[n]
        s_scr[n % RING] = lax.dot_general(
            jnp.concatenate([q(0), q(1)], axis=0), k(), (((1,), (1,)), ((), ())),
            preferred_element_type=F32)

    def value_unit(n):
        _, _, v, mask, store = all_units[n]
        ps, mx = [], []
        for h in range(2):
            s = s_scr[n % RING, h * UQ:(h + 1) * UQ] + mask()
            m = jnp.max(s, axis=-1, keepdims=True)
            ps.append(jnp.exp2(s - m).astype(BF16))
            mx.append(m)
        res = jnp.dot(jnp.concatenate(ps, axis=0), jnp.concatenate([v(), ones], axis=1),
                      preferred_element_type=F32)
        top, bot = res[:UQ], res[UQ:]
        store(jnp.where(head0, top[:, :LANES], bot[:, :LANES]),
              jnp.where(head0, mx[0], mx[1]),
              jnp.where(head0, top[:, LANES:], bot[:, LANES:]))

    for n in range(AHEAD):
        score_unit(n)
    for n in range(len(all_units)):
        if n + AHEAD < len(all_units):
            score_unit(n + AHEAD)
        value_unit(n)

    def comb(r, carry):
        m0, m1, m2 = max_scr[0, r], max_scr[1, r], max_scr[2, r]
        m = jnp.maximum(jnp.maximum(m0, m1), m2)
        w0, w1, w2 = jnp.exp2(m0 - m), jnp.exp2(m1 - m), jnp.exp2(m2 - m)
        num = w0 * acc_scr[0, r] + w1 * acc_scr[1, r] + w2 * acc_scr[2, r]
        den = w0 * sum_scr[0, r] + w1 * sum_scr[1, r] + w2 * sum_scr[2, r]
        o_ref[r] = num / den
        return carry

    lax.fori_loop(0, NS, comb, 0, unroll=4)


def _attention(q, k, v, q1, k1, v1, later_weights):
    m16, m4, m1 = _attn_masks()
    cur = pl.BlockSpec((None, NS, TA, LANES), lambda j, i: (j, 0, i, 0))
    prev = pl.BlockSpec((None, NS, TA, LANES), lambda j, i: (j, 0, jnp.maximum(i - 1, 0), 0))
    cur1 = pl.BlockSpec((None, NS * TA, LANES), lambda j, i: (j, i, 0))
    prev1 = pl.BlockSpec((None, UQ, LANES),
                         lambda j, i: (j, jnp.maximum(i * (NS * TA // UQ) - 1, 0), 0))

    def tile_kind(m):
        return pl.BlockSpec((None,) + m.shape[1:],
                            lambda j, i: (jnp.minimum(i, 1),) + (0,) * (m.ndim - 1))

    n_tiles = RA // TA
    steps = NSLAB * n_tiles
    cast_specs = [pl.BlockSpec((a.shape[0] // steps, a.shape[1]), lambda j, i: (j * n_tiles + i, 0))
                  for a in later_weights]
    cast_shapes = [jax.ShapeDtypeStruct(a.shape, BF16) for a in later_weights]
    return pl.pallas_call(
        _attn_body,
        grid=(NSLAB, n_tiles),
        in_specs=[cur, cur, prev, cur, prev, cur1, cur1, prev1, cur1, prev1,
                  tile_kind(m16), tile_kind(m4), tile_kind(m1)] + cast_specs,
        out_specs=[pl.BlockSpec((None, NS, TA, LANES), lambda j, i: (j, 0, i, 0))] + cast_specs,
        out_shape=[jax.ShapeDtypeStruct((NSLAB, NS, RA, LANES), F32)] + cast_shapes,
        scratch_shapes=[
            pltpu.VMEM((2, NS, TA, LANES), BF16),
            pltpu.VMEM((2, NS * TA, LANES), BF16),
            pltpu.VMEM((UQ + NS * TA, LANES), BF16),
            pltpu.VMEM((UQ + NS * TA, LANES), BF16),
            pltpu.VMEM((RING, 2 * UQ, UK), F32),
            pltpu.VMEM((3, NS, TA, LANES), F32),
            pltpu.VMEM((3, NS, TA, LANES), F32),
            pltpu.VMEM((3, NS, TA, LANES), F32),
        ],
        compiler_params=pltpu.CompilerParams(
            dimension_semantics=("arbitrary", "arbitrary"), vmem_limit_bytes=VMEM_LIMIT),
        name="dilated_attn",
    )(q, k, k, v, v, q1, k1, k1, v1, v1, m16, m4, m1, *later_weights)


TL = 32


def _lru_phases(xl_ref, gl_ref, cw_ref, cb_ref, wg_ref, bg_ref, lam_ref, y_ref,
                tail_scr, carry_scr, xc_scr, g_scr, pl_scr, hl_scr):
    row = lax.broadcasted_iota(jnp.int32, (TL, LRU_WIDTH), 0)
    state = {}

    def conv(after=0.0):
        cw = cw_ref[...]
        cb = cb_ref[...] + after

        def tap(r, k):
            st = r - k
            if st >= 0:
                return xl_ref[st]
            st += NS
            prev_last = tail_scr[st - (NS - CONV_WIDTH + 1), 7:8, :]
            return jnp.where(row == 0, prev_last, pltpu.roll(xl_ref[st], 1, axis=0))

        for r in range(NS):
            xc_scr[r] = cb + sum(cw[CONV_WIDTH - 1 - k:CONV_WIDTH - k] * tap(r, k)
                                 for k in range(CONV_WIDTH))
        for n in range(CONV_WIDTH - 1):
            tail_scr[n] = xl_ref[NS - CONV_WIDTH + 1 + n, TL - 8:TL, :]

    def gates(after=0.0):
        xc_all = xc_scr[...].reshape(NS * TL, LRU_WIDTH).astype(BF16)
        bias = bg_ref[...] + after
        for t in range(LRU_WIDTH // MXU_TILE):
            cols = slice(t * MXU_TILE, (t + 1) * MXU_TILE)
            res = jnp.dot(xc_all[:, cols], wg_ref[t], preferred_element_type=F32)
            for gate in range(2):
                dst = slice(gate * LRU_WIDTH + t * MXU_TILE, gate * LRU_WIDTH + (t + 1) * MXU_TILE)
                g_scr[:, dst] = res[:, gate * MXU_TILE:(gate + 1) * MXU_TILE] + bias[:, dst]

    def local_scan(streams):
        def run(after=0.0):
            neg_lam = after - lam_ref[...]
            softplus = jnp.maximum(neg_lam, 0.0) + jnp.log1p(jnp.exp(-jnp.abs(neg_lam)))
            p_run, h_run = state.get("run", (None, None))
            for r in streams:
                g = g_scr[r * TL:(r + 1) * TL]
                rg = jax.nn.sigmoid(g[:, :LRU_WIDTH])
                ig = jax.nn.sigmoid(g[:, LRU_WIDTH:])
                log_a = -LRU_C * rg * softplus
                a = jnp.exp(log_a)
                th = jnp.tanh(log_a)
                bx = jnp.sqrt(-2.0 * th / (1.0 - th)) * (ig * xc_scr[r])
                if p_run is None:
                    p_run, h_run = a, bx
                else:
                    h_run = a * h_run + bx
                    p_run = a * p_run
                pl_scr[r] = p_run
                hl_scr[r] = h_run
            state["run"] = (p_run, h_run)
        return run

    def row_scan():
        pa, hb = state["run"]
        sft = 1
        while sft < TL:
            keep = row >= sft
            pa_s = jnp.where(keep, pltpu.roll(pa, sft, axis=0), 1.0)
            hb_s = jnp.where(keep, pltpu.roll(hb, sft, axis=0), 0.0)
            hb = pa * hb_s + hb
            pa = pa * pa_s
            sft *= 2
        carry = carry_scr[0:1, :]
        e = pa * carry + hb
        state["e_prev"] = jnp.where(row == 0, carry, pltpu.roll(e, 1, axis=0))
        carry_scr[...] = jnp.broadcast_to(e[TL - 1:TL, :], carry_scr.shape)

    def finalize(after=0.0):
        e_prev = state["e_prev"] + after
        for r in range(NS):
            h = pl_scr[r] * e_prev + hl_scr[r]
            y_ref[r] = jax.nn.gelu(gl_ref[r], approximate=True) * h

    half = NS // 2

    def second_half(after=0.0):
        local_scan(range(half, NS))(after)
        row_scan()

    return [(conv, LRU_WIDTH), (gates, 2 * LRU_WIDTH), (local_scan(range(half)), LRU_WIDTH),
            (second_half, LRU_WIDTH), (finalize, LRU_WIDTH)]


TM4 = NS * TL
NG4 = TM4 // PG
FF_CHUNK = 1024


def _out_body(x_ref, at_ref, xl0_ref, gl0_ref, xln_ref, gln_ref, zero_ref,
              cw_ref, cb_ref, wg_ref, bg_ref, lam_ref,
              perm_ref, ga_ref, gl_ref, wo_ref, g2_ref, wu_ref, wd_ref, gf_ref,
              o_ref,
              lr_scr, tail_scr, carry_scr, xc_scr, g_scr, pl_scr, hl_scr):
    lru_params = (cw_ref, cb_ref, wg_ref, bg_ref, lam_ref)
    lru_state = (tail_scr, carry_scr, xc_scr, g_scr, pl_scr, hl_scr)

    @pl.when(pl.program_id(0) == 0)
    def _():
        tail_scr[...] = jnp.zeros_like(tail_scr)
        carry_scr[...] = jnp.zeros_like(carry_scr)
        for phase, _ in _lru_phases(xl0_ref, gl0_ref, *lru_params, lr_scr, *lru_state):
            phase()

    groups = []
    for g in range(NG4):
        rows = slice(g * RG, (g + 1) * RG)
        attn = jnp.concatenate(
            [jnp.concatenate([at_ref[j, r, rows] for j in range(NSLAB)], axis=1) for r in range(NS)],
            axis=0)
        lru = jnp.concatenate([lr_scr[r, rows] for r in range(NS)], axis=0)
        mixed = jnp.concatenate([_rms(attn, ga_ref[...]), _rms(lru, gl_ref[...])], axis=1)
        groups.append(jnp.dot(perm_ref[...], mixed.astype(BF16),
                              preferred_element_type=F32).astype(BF16))
    mixed = jnp.concatenate(groups, axis=0)
    lru_next = _lru_phases(xln_ref, gln_ref, *lru_params, lr_scr, *lru_state)
    n_chunks = D_FF // FF_CHUNK
    assert len(lru_next) <= n_chunks + 1

    def zero_after(val, width):
        bits = pltpu.bitcast(val[0:1, :width], jnp.int32) & zero_ref[0:1, :width]
        return pltpu.bitcast(bits, F32)

    h = x_ref[...] + jnp.dot(mixed, wo_ref[...], preferred_element_type=F32)
    u = _rms(h, g2_ref[...]).astype(BF16)
    acc = h
    phase, width = lru_next.pop(0)
    phase(zero_after(h, width))
    for c in range(n_chunks):
        if lru_next:
            phase, width = lru_next.pop(0)
            phase(zero_after(acc, width))
        sl = slice(c * FF_CHUNK, (c + 1) * FF_CHUNK)
        f = jnp.dot(u, wu_ref[:, sl], preferred_element_type=F32)
        f = jnp.square(jnp.maximum(f, 0.0)).astype(BF16)
        acc = acc + jnp.dot(f, wd_ref[sl, :], preferred_element_type=F32)
    o_ref[...] = _rms(acc, gf_ref[...])


def _out_mlp(x2, attn, xl, gl_in, lru_params, ga, gl, wo, g2, wu, wd, gf):
    def const(a):
        return pl.BlockSpec(a.shape, lambda i: (0,) * a.ndim, pipeline_mode=pl.Buffered(1))

    n_tiles = SEQ // TM4
    xspec = pl.BlockSpec((TM4, D_MODEL), lambda i: (i, 0))
    first = pl.BlockSpec((NS, TL, LRU_WIDTH), lambda i: (0, 0, 0), pipeline_mode=pl.Buffered(1))
    ahead = pl.BlockSpec((NS, TL, LRU_WIDTH), lambda i: (0, jnp.minimum(i + 1, n_tiles - 1), 0))
    perm = jnp.asarray(_group_perm(), BF16)
    zero = jnp.zeros((8, D_MODEL), jnp.int32)
    consts = (zero,) + tuple(lru_params) + (perm, ga, gl, wo, g2, wu, wd, gf)
    return pl.pallas_call(
        _out_body,
        grid=(n_tiles,),
        in_specs=[
            xspec,
            pl.BlockSpec((NSLAB, NS, TL, LANES), lambda i: (0, 0, i, 0)),
            first, first, ahead, ahead,
        ] + [const(a) for a in consts],
        out_specs=xspec,
        out_shape=jax.ShapeDtypeStruct((SEQ, D_MODEL), F32),
        scratch_shapes=[
            pltpu.VMEM((NS, TL, LRU_WIDTH), F32),
            pltpu.VMEM((CONV_WIDTH - 1, 8, LRU_WIDTH), F32),
            pltpu.VMEM((8, LRU_WIDTH), F32),
            pltpu.VMEM((NS, TL, LRU_WIDTH), F32),
            pltpu.VMEM((NS * TL, 2 * LRU_WIDTH), F32),
            pltpu.VMEM((NS, TL, LRU_WIDTH), F32),
            pltpu.VMEM((NS, TL, LRU_WIDTH), F32),
        ],
        compiler_params=pltpu.CompilerParams(
            dimension_semantics=("arbitrary",), vmem_limit_bytes=VMEM_LIMIT),
        name="out_mlp",
    )(x2, attn, xl, gl_in, xl, gl_in, *consts)


@functools.lru_cache(maxsize=None)
def _rope_tables():
    inv_freq = ROPE_THETA ** (-np.arange(0, ROT_DIM, 2, dtype=np.float64) / ROT_DIM)
    pad = HEAD_DIM - ROT_DIM
    reps = LANES // HEAD_DIM

    def lanes(rot_first, rot_second, rest, n):
        head = np.concatenate([rot_first, rot_second, np.full((n, pad), rest)], axis=1)
        return np.tile(head, (1, reps))

    tile_ang = (TM1 * np.arange(SEQ // TM1, dtype=np.float64))[:, None] * inv_freq[None, :]
    tc, ts = np.cos(tile_ang), np.sin(tile_ang)
    tiles = np.concatenate([lanes(tc, tc, 1.0, len(tc)), lanes(ts, ts, 0.0, len(ts))], axis=1)
    off_ang = np.arange(TM1, dtype=np.float64)[:, None] * inv_freq[None, :]
    oc, osn = np.cos(off_ang), np.sin(off_ang)
    offs = np.stack([lanes(oc, oc, 1.0, TM1), lanes(osn, osn, 0.0, TM1),
                     lanes(-oc, oc, 0.0, TM1), lanes(-osn, osn, 0.0, TM1)])
    offs = offs.reshape(4, NG1, RG, NS, LANES).transpose(0, 1, 3, 2, 4).reshape(4, TM1, LANES)
    return (tiles.astype(np.float32)[:, None, :], np.ascontiguousarray(offs.astype(np.float32)))


def _block_diag(w):
    n, b, _ = w.shape
    eye = jnp.eye(n, dtype=w.dtype)
    return (eye[:, None, :, None] * w[:, :, None, :]).reshape(n * b, n * b)


def _gate_tiles(w_r, w_i):
    per = MXU_TILE // (LRU_WIDTH // LRU_BLOCKS)
    tiles = [jnp.concatenate([_block_diag(w_r[t * per:(t + 1) * per]),
                              _block_diag(w_i[t * per:(t + 1) * per])], axis=1)
             for t in range(LRU_WIDTH // MXU_TILE)]
    return jnp.stack(tiles).astype(BF16)


def kernel(x, norm1_g, w_in, conv_w, conv_b, w_rgate, b_rgate, w_igate, b_igate, lru_lambda,
           attn_out_g, lru_out_g, w_out, norm2_g, w_mlp_up, w_mlp_down, final_g):
    assert x.shape == (1, SEQ, D_MODEL) and w_in.shape[0] == 1
    x2 = x.reshape(SEQ, D_MODEL)
    q, k, v, q1, k1, v1, xl, gl = _inproj(x2, norm1_g.reshape(1, D_MODEL), w_in[0])
    attn, wo, wu, wd = _attention(q, k, v, q1, k1, v1, (w_out[0], w_mlp_up[0], w_mlp_down[0]))
    w_gate = _gate_tiles(w_rgate[0], w_igate[0])
    b_gate = jnp.concatenate([b_rgate[0].reshape(1, -1), b_igate[0].reshape(1, -1)], axis=1)
    lru_params = (conv_w[0], conv_b[0].reshape(1, -1), w_gate, b_gate, lru_lambda[0].reshape(1, -1))
    out = _out_mlp(x2, attn, xl, gl, lru_params,
                   attn_out_g[0].reshape(1, -1), lru_out_g[0].reshape(1, -1),
                   wo, norm2_g[0].reshape(1, -1), wu, wd, final_g.reshape(1, -1))
    return out.reshape(1, SEQ, D_MODEL)
```

```python
import functools

import numpy as np
import jax
import jax.numpy as jnp
from jax import lax
from jax.experimental import pallas as pl
from jax.experimental.pallas import tpu as pltpu

F32 = jnp.float32
BF16 = jnp.bfloat16

D_MODEL = 1024
SEQ = 16384
ATTN_HEADS = 8
HEAD_DIM = 64
ATTN_WIDTH = ATTN_HEADS * HEAD_DIM
ROT_DIM = HEAD_DIM // 4
ROPE_THETA = 500000.0
LRU_WIDTH = D_MODEL - ATTN_WIDTH
LRU_BLOCKS = 8
CONV_WIDTH = 4
LRU_C = 8.0
IN_WIDTH = 3 * ATTN_WIDTH + 2 * LRU_WIDTH
D_FF = 4 * D_MODEL
EPS = 1e-6
WINDOW_STEPS = 128

NS = 16
RA = SEQ // NS
LANES = 128
MXU_TILE = 256
NSLAB = ATTN_WIDTH // LANES
NEG = -np.inf

VMEM_LIMIT = 56 * 1024 * 1024


def _rms(x, g):
    return x * lax.rsqrt(jnp.mean(x * x, axis=-1, keepdims=True) + EPS) * g


TM1 = 512
PG = NS * NS
RG = PG // NS
NG1 = TM1 // PG
CH = 8
LOG2E = 1.4426950408889634


@functools.lru_cache(maxsize=None)
def _group_perm():
    n = np.arange(PG)
    p = np.zeros((PG, PG), np.float32)
    p[(n % NS) * RG + n // NS, n] = 1.0
    return p


def _inproj_body(x_ref, g_ref, w_ref, perm_ref, tile_ref, off_ref,
                 q_ref, k_ref, v_ref, q1_ref, k1_ref, v1_ref, xl_ref, gl_ref, w_scr):
    @pl.when(pl.program_id(0) == 0)
    def _():
        w_scr[...] = w_ref[...].astype(BF16)

    u = _rms(x_ref[...], g_ref[...]).astype(BF16)
    u = jnp.concatenate(
        [jnp.dot(perm_ref[...], u[g * PG:(g + 1) * PG], preferred_element_type=F32).astype(BF16)
         for g in range(NG1)], axis=0)
    z = jnp.dot(u, w_scr[...], preferred_element_type=F32)
    tile_cos, tile_sin = tile_ref[:, :LANES], tile_ref[:, LANES:]
    c = tile_cos * off_ref[0] - tile_sin * off_ref[1]
    s = tile_sin * off_ref[2] + tile_cos * off_ref[3]
    c = jnp.concatenate([c] * NSLAB, axis=1)
    s = jnp.concatenate([s] * NSLAB, axis=1)
    lane = lax.broadcasted_iota(jnp.int32, (1, ATTN_WIDTH), 1) % HEAD_DIM
    first_half = lane < ROT_DIM // 2

    def rope(t):
        up = pltpu.roll(t, ATTN_WIDTH - ROT_DIM // 2, axis=1)
        dn = pltpu.roll(t, ROT_DIM // 2, axis=1)
        return t * c + jnp.where(first_half, up, dn) * s

    q = rope(z[:, :ATTN_WIDTH]) * (HEAD_DIM ** -0.5 * LOG2E)
    k = rope(z[:, ATTN_WIDTH:2 * ATTN_WIDTH])
    v = z[:, 2 * ATTN_WIDTH:3 * ATTN_WIDTH]
    xl = z[:, 3 * ATTN_WIDTH:3 * ATTN_WIDTH + LRU_WIDTH]
    gl = z[:, 3 * ATTN_WIDTH + LRU_WIDTH:]
    for g in range(NG1):
        for r in range(NS):
            src = slice(g * PG + r * RG, g * PG + (r + 1) * RG)
            dst = slice(g * RG, (g + 1) * RG)
            for j in range(NSLAB):
                sl = slice(j * LANES, (j + 1) * LANES)
                q_ref[j, r, dst] = q[src, sl].astype(BF16)
                k_ref[j, r, dst] = k[src, sl].astype(BF16)
                v_ref[j, r, dst] = v[src, sl].astype(BF16)
            xl_ref[r, dst] = xl[src]
            gl_ref[r, dst] = gl[src]
    for g in range(NG1):
        for cl in range(RG // CH):
            for r in range(0, NS, 2):
                lo = g * PG + r * RG + cl * CH
                hi = lo + RG
                d0 = (g * (RG // CH) + cl) * NS * CH + r * CH
                dst = slice(d0, d0 + 2 * CH)
                for j in range(NSLAB):
                    sl = slice(j * LANES, (j + 1) * LANES)
                    for src, ref in ((q, q1_ref), (k, k1_ref), (v, v1_ref)):
                        ref[j, dst] = jnp.concatenate(
                            [src[lo:lo + CH, sl], src[hi:hi + CH, sl]], axis=0).astype(BF16)


def _inproj(x2, g, w):
    steps = SEQ // TM1
    rows = TM1 // NS
    slab_spec = pl.BlockSpec((NSLAB, NS, rows, LANES), lambda i: (0, 0, i, 0))
    chunk_spec = pl.BlockSpec((NSLAB, TM1, LANES), lambda i: (0, i, 0))
    row_spec = pl.BlockSpec((NS, rows, LRU_WIDTH), lambda i: (0, i, 0))
    rope_tiles, rope_offs = _rope_tables()
    slab_shape = jax.ShapeDtypeStruct((NSLAB, NS, RA, LANES), BF16)
    chunk_shape = jax.ShapeDtypeStruct((NSLAB, SEQ, LANES), BF16)
    row_shape = jax.ShapeDtypeStruct((NS, RA, LRU_WIDTH), F32)
    perm = jnp.asarray(_group_perm(), BF16)
    return pl.pallas_call(
        _inproj_body,
        grid=(steps,),
        in_specs=[
            pl.BlockSpec((TM1, D_MODEL), lambda i: (i, 0)),
            pl.BlockSpec((1, D_MODEL), lambda i: (0, 0)),
            pl.BlockSpec((D_MODEL, IN_WIDTH), lambda i: (0, 0), pipeline_mode=pl.Buffered(1)),
            pl.BlockSpec((PG, PG), lambda i: (0, 0)),
            pl.BlockSpec((None, 1, 2 * LANES), lambda i: (i, 0, 0)),
            pl.BlockSpec(rope_offs.shape, lambda i: (0, 0, 0)),
        ],
        out_specs=[slab_spec] * 3 + [chunk_spec] * 3 + [row_spec] * 2,
        out_shape=[slab_shape] * 3 + [chunk_shape] * 3 + [row_shape] * 2,
        scratch_shapes=[pltpu.VMEM((D_MODEL, IN_WIDTH), BF16)],
        compiler_params=pltpu.CompilerParams(
            dimension_semantics=("arbitrary",), vmem_limit_bytes=VMEM_LIMIT),
        name="inproj",
    )(x2, g, w, perm, rope_tiles, rope_offs)


TA = 128
UQ, UK = 128, 256
Q4, K4 = UQ // 4, UK // 4
AHEAD = 4
RING = 2 * AHEAD


@functools.lru_cache(maxsize=None)
def _attn_masks():
    def variants(diff, from_prev, per_unit):
        ok = (diff >= 0) & (diff <= WINDOW_STEPS)
        normal = np.where(ok, 0.0, NEG).astype(np.float32)
        first = np.where(ok & ~from_prev, 0.0, NEG).astype(np.float32)
        if per_unit:
            return np.stack([np.stack([first, normal]), np.stack([normal, normal])])
        return np.stack([first, normal])

    iq = np.arange(UQ)[:, None]
    ck = np.arange(UK)[None, :]
    m16 = variants(iq - ck + TA, np.broadcast_to(ck < TA, (UQ, UK)), False)
    cq, i4 = np.divmod(np.arange(UQ), Q4)
    ckk, j4 = np.divmod(np.arange(UK), K4)
    d4 = 4 * (i4[:, None] - j4[None, :] + Q4) + (cq[:, None] - ckk[None, :])
    m4 = variants(d4, np.broadcast_to(j4[None, :] < K4 - Q4, d4.shape), True)
    rq, i1 = np.divmod(np.arange(UQ), CH)
    kc, kin = np.divmod(np.arange(UK), UQ)
    rk, j1 = np.divmod(kin, CH)
    d1 = (NS * i1 + rq)[:, None] - (NS * j1 + rk + UQ * (kc - 1))[None, :]
    m1 = variants(d1, np.broadcast_to(kc[None, :] == 0, d1.shape), True)
    return m16, m4, m1


def _attn_body(q_ref, kc_ref, kp_ref, vc_ref, vp_ref, q1_ref, k1c_ref, k1p_ref, v1c_ref, v1p_ref,
               m16_ref, m4_ref, m1_ref, wo_ref, wu_ref, wd_ref,
               o_ref, wo_bf_ref, wu_bf_ref, wd_bf_ref,
               qm_scr, q1m_scr, k1_scr, v1_scr, s_scr, acc_scr, max_scr, sum_scr):
    wo_bf_ref[...] = wo_ref[...].astype(BF16)
    wu_bf_ref[...] = wu_ref[...].astype(BF16)
    wd_bf_ref[...] = wd_ref[...].astype(BF16)

    lane = lax.broadcasted_iota(jnp.int32, (1, LANES), 1)
    head0 = lane < HEAD_DIM
    zero = jnp.zeros((), BF16)

    def split(x):
        return jnp.where(head0, x, zero), jnp.where(head0, zero, x)

    qm_scr[0], qm_scr[1] = split(q_ref[...])
    q1m_scr[0], q1m_scr[1] = split(q1_ref[...])
    k1_scr[0:UQ] = k1p_ref[...]
    k1_scr[UQ:] = k1c_ref[...]
    v1_scr[0:UQ] = v1p_ref[...]
    v1_scr[UQ:] = v1c_ref[...]
    ones = jnp.ones((UK, LANES), BF16)

    def unit16(r):
        def store(o, mx, sm):
            acc_scr[0, r], max_scr[0, r], sum_scr[0, r] = o, mx, sm

        return (lambda h: qm_scr[h, r],
                lambda: jnp.concatenate([kp_ref[r], kc_ref[r]], axis=0),
                lambda: jnp.concatenate([vp_ref[r], vc_ref[r]], axis=0),
                lambda: m16_ref[...], store)

    def unit4(r4, b):
        streams = [r4 + 4 * c for c in range(4)]
        q0, k0 = b * Q4, b * Q4 + Q4 - K4

        def window(cur, prev, st):
            if k0 < 0:
                return [prev[st, TA + k0:TA], cur[st, 0:k0 + K4]]
            return [cur[st, k0:k0 + K4]]

        def store(o, mx, sm):
            for c, st in enumerate(streams):
                rows = slice(c * Q4, (c + 1) * Q4)
                acc_scr[1, st, q0:q0 + Q4] = o[rows]
                max_scr[1, st, q0:q0 + Q4] = mx[rows]
                sum_scr[1, st, q0:q0 + Q4] = sm[rows]

        return (lambda h: jnp.concatenate([qm_scr[h, st, q0:q0 + Q4] for st in streams], axis=0),
                lambda: jnp.concatenate(
                    [x for st in streams for x in window(kc_ref, kp_ref, st)], axis=0),
                lambda: jnp.concatenate(
                    [x for st in streams for x in window(vc_ref, vp_ref, st)], axis=0),
                lambda: m4_ref[min(b, 1)], store)

    def unit1(u):
        off, row = u * UQ, u * CH

        def store(o, mx, sm):
            for r in range(NS):
                rows = slice(r * CH, (r + 1) * CH)
                acc_scr[2, r, row:row + CH] = o[rows]
                max_scr[2, r, row:row + CH] = mx[rows]
                sum_scr[2, r, row:row + CH] = sm[rows]

        return (lambda h: q1m_scr[h, off:off + UQ],
                lambda: k1_scr[off:off + UK],
                lambda: v1_scr[off:off + UK],
                lambda: m1_ref[min(u, 1)], store)

    all_units = ([unit16(n) for n in range(NS)]
                 + [unit4(n // (TA // Q4), n % (TA // Q4)) for n in range(NS)]
                 + [unit1(n) for n in range(NS)])

    def score_unit(n):
        q, k, _, _, _ = all_units[n]
        s_scr[n % RING] = lax.dot_general(
            jnp.concatenate([q(0), q(1)], axis=0), k(), (((1,), (1,)), ((), ())),
            preferred_element_type=F32)

    def value_unit(n):
        _, _, v, mask, store = all_units[n]
        ps, mx = [], []
        for h in range(2):
            s = s_scr[n % RING, h * UQ:(h + 1) * UQ] + mask()
            m = jnp.max(s, axis=-1, keepdims=True)
            ps.append(jnp.exp2(s - m).astype(BF16))
            mx.append(m)
        res = jnp.dot(jnp.concatenate(ps, axis=0), jnp.concatenate([v(), ones], axis=1),
                      preferred_element_type=F32)
        top, bot = res[:UQ], res[UQ:]
        store(jnp.where(head0, top[:, :LANES], bot[:, :LANES]),
              jnp.where(head0, mx[0], mx[1]),
              jnp.where(head0, top[:, LANES:], bot[:, LANES:]))

    for n in range(AHEAD):
        score_unit(n)
    for n in range(len(all_units)):
        if n + AHEAD < len(all_units):
            score_unit(n + AHEAD)
        value_unit(n)

    def comb(r, carry):
        m0, m1, m2 = max_scr[0, r], max_scr[1, r], max_scr[2, r]
        m = jnp.maximum(jnp.maximum(m0, m1), m2)
        w0, w1, w2 = jnp.exp2(m0 - m), jnp.exp2(m1 - m), jnp.exp2(m2 - m)
        num = w0 * acc_scr[0, r] + w1 * acc_scr[1, r] + w2 * acc_scr[2, r]
        den = w0 * sum_scr[0, r] + w1 * sum_scr[1, r] + w2 * sum_scr[2, r]
        o_ref[pl.ds(r, TA, stride=NS), :] = num / den
        return carry

    lax.fori_loop(0, NS, comb, 0, unroll=4)


def _attention(q, k, v, q1, k1, v1, later_weights):
    m16, m4, m1 = _attn_masks()
    cur = pl.BlockSpec((None, NS, TA, LANES), lambda j, i: (j, 0, i, 0))
    prev = pl.BlockSpec((None, NS, TA, LANES), lambda j, i: (j, 0, jnp.maximum(i - 1, 0), 0))
    cur1 = pl.BlockSpec((None, NS * TA, LANES), lambda j, i: (j, i, 0))
    prev1 = pl.BlockSpec((None, UQ, LANES),
                         lambda j, i: (j, jnp.maximum(i * (NS * TA // UQ) - 1, 0), 0))

    def tile_kind(m):
        return pl.BlockSpec((None,) + m.shape[1:],
                            lambda j, i: (jnp.minimum(i, 1),) + (0,) * (m.ndim - 1))

    n_tiles = RA // TA
    steps = NSLAB * n_tiles
    cast_specs = [pl.BlockSpec((a.shape[0] // steps, a.shape[1]), lambda j, i: (j * n_tiles + i, 0))
                  for a in later_weights]
    cast_shapes = [jax.ShapeDtypeStruct(a.shape, BF16) for a in later_weights]
    return pl.pallas_call(
        _attn_body,
        grid=(NSLAB, n_tiles),
        in_specs=[cur, cur, prev, cur, prev, cur1, cur1, prev1, cur1, prev1,
                  tile_kind(m16), tile_kind(m4), tile_kind(m1)] + cast_specs,
        out_specs=[pl.BlockSpec((None, NS * TA, LANES), lambda j, i: (j, i, 0))] + cast_specs,
        out_shape=[jax.ShapeDtypeStruct((NSLAB, SEQ, LANES), F32)] + cast_shapes,
        scratch_shapes=[
            pltpu.VMEM((2, NS, TA, LANES), BF16),
            pltpu.VMEM((2, NS * TA, LANES), BF16),
            pltpu.VMEM((UQ + NS * TA, LANES), BF16),
            pltpu.VMEM((UQ + NS * TA, LANES), BF16),
            pltpu.VMEM((RING, 2 * UQ, UK), F32),
            pltpu.VMEM((3, NS, TA, LANES), F32),
            pltpu.VMEM((3, NS, TA, LANES), F32),
            pltpu.VMEM((3, NS, TA, LANES), F32),
        ],
        compiler_params=pltpu.CompilerParams(
            dimension_semantics=("arbitrary", "arbitrary"), vmem_limit_bytes=VMEM_LIMIT),
        name="dilated_attn",
    )(q, k, k, v, v, q1, k1, k1, v1, v1, m16, m4, m1, *later_weights)


TL = 32


def _lru_phases(xl_ref, gl_ref, cw_ref, cb_ref, wg_ref, bg_ref, lam_ref, y_ref,
                tail_scr, carry_scr, xc_scr, g_scr, pl_scr, hl_scr):
    row = lax.broadcasted_iota(jnp.int32, (TL, LRU_WIDTH), 0)
    state = {}

    def conv(after=0.0):
        cw = cw_ref[...]
        cb = cb_ref[...] + after

        def tap(r, k):
            st = r - k
            if st >= 0:
                return xl_ref[st]
            st += NS
            prev_last = tail_scr[st - (NS - CONV_WIDTH + 1), 7:8, :]
            return jnp.where(row == 0, prev_last, pltpu.roll(xl_ref[st], 1, axis=0))

        for r in range(NS):
            xc_scr[r] = cb + sum(cw[CONV_WIDTH - 1 - k:CONV_WIDTH - k] * tap(r, k)
                                 for k in range(CONV_WIDTH))
        for n in range(CONV_WIDTH - 1):
            tail_scr[n] = xl_ref[NS - CONV_WIDTH + 1 + n, TL - 8:TL, :]

    def gates(after=0.0):
        xc_all = xc_scr[...].reshape(NS * TL, LRU_WIDTH).astype(BF16)
        bias = bg_ref[...] + after
        for t in range(LRU_WIDTH // MXU_TILE):
            cols = slice(t * MXU_TILE, (t + 1) * MXU_TILE)
            res = jnp.dot(xc_all[:, cols], wg_ref[t], preferred_element_type=F32)
            for gate in range(2):
                dst = slice(gate * LRU_WIDTH + t * MXU_TILE, gate * LRU_WIDTH + (t + 1) * MXU_TILE)
                g_scr[:, dst] = res[:, gate * MXU_TILE:(gate + 1) * MXU_TILE] + bias[:, dst]

    def local_scan(streams):
        def run(after=0.0):
            neg_lam = after - lam_ref[...]
            softplus = jnp.maximum(neg_lam, 0.0) + jnp.log1p(jnp.exp(-jnp.abs(neg_lam)))
            p_run, h_run = state.get("run", (None, None))
            for r in streams:
                g = g_scr[r * TL:(r + 1) * TL]
                rg = jax.nn.sigmoid(g[:, :LRU_WIDTH])
                ig = jax.nn.sigmoid(g[:, LRU_WIDTH:])
                log_a = -LRU_C * rg * softplus
                a = jnp.exp(log_a)
                th = jnp.tanh(log_a)
                bx = jnp.sqrt(-2.0 * th / (1.0 - th)) * (ig * xc_scr[r])
                if p_run is None:
                    p_run, h_run = a, bx
                else:
                    h_run = a * h_run + bx
                    p_run = a * p_run
                pl_scr[r] = p_run
                hl_scr[r] = h_run
            state["run"] = (p_run, h_run)
        return run

    def row_scan():
        pa, hb = state["run"]
        sft = 1
        while sft < TL:
            keep = row >= sft
            pa_s = jnp.where(keep, pltpu.roll(pa, sft, axis=0), 1.0)
            hb_s = jnp.where(keep, pltpu.roll(hb, sft, axis=0), 0.0)
            hb = pa * hb_s + hb
            pa = pa * pa_s
            sft *= 2
        carry = carry_scr[0:1, :]
        e = pa * carry + hb
        state["e_prev"] = jnp.where(row == 0, carry, pltpu.roll(e, 1, axis=0))
        carry_scr[...] = jnp.broadcast_to(e[TL - 1:TL, :], carry_scr.shape)

    def finalize(after=0.0):
        e_prev = state["e_prev"] + after
        for r in range(NS):
            h = pl_scr[r] * e_prev + hl_scr[r]
            y = jax.nn.gelu(gl_ref[r], approximate=True) * h
            for j in range(LRU_WIDTH // LANES):
                y_ref[j, pl.ds(r, TL, stride=NS), :] = y[:, j * LANES:(j + 1) * LANES]

    half = NS // 2

    def second_half(after=0.0):
        local_scan(range(half, NS))(after)
        row_scan()

    return [(conv, LRU_WIDTH), (gates, 2 * LRU_WIDTH), (local_scan(range(half)), LRU_WIDTH),
            (second_half, LRU_WIDTH), (finalize, LRU_WIDTH)]


TM4 = NS * TL
FF_CHUNK = 1024


def _out_body(x_ref, at_ref, xl0_ref, gl0_ref, xln_ref, gln_ref, zero_ref,
              cw_ref, cb_ref, wg_ref, bg_ref, lam_ref,
              ga_ref, gl_ref, wo_ref, g2_ref, wu_ref, wd_ref, gf_ref,
              o_ref,
              lr_scr, tail_scr, carry_scr, xc_scr, g_scr, pl_scr, hl_scr):
    lru_params = (cw_ref, cb_ref, wg_ref, bg_ref, lam_ref)
    lru_state = (tail_scr, carry_scr, xc_scr, g_scr, pl_scr, hl_scr)

    @pl.when(pl.program_id(0) == 0)
    def _():
        tail_scr[...] = jnp.zeros_like(tail_scr)
        carry_scr[...] = jnp.zeros_like(carry_scr)
        for phase, _ in _lru_phases(xl0_ref, gl0_ref, *lru_params, lr_scr, *lru_state):
            phase()

    attn = jnp.concatenate([at_ref[j] for j in range(NSLAB)], axis=1)
    lru = jnp.concatenate([lr_scr[j] for j in range(LRU_WIDTH // LANES)], axis=1)
    mixed = jnp.concatenate([_rms(attn, ga_ref[...]), _rms(lru, gl_ref[...])], axis=1).astype(BF16)
    lru_next = _lru_phases(xln_ref, gln_ref, *lru_params, lr_scr, *lru_state)
    n_chunks = D_FF // FF_CHUNK
    assert len(lru_next) <= n_chunks + 1

    def zero_after(val, width):
        bits = pltpu.bitcast(val[0:1, :width], jnp.int32) & zero_ref[0:1, :width]
        return pltpu.bitcast(bits, F32)

    h = x_ref[...] + jnp.dot(mixed, wo_ref[...], preferred_element_type=F32)
    u = _rms(h, g2_ref[...]).astype(BF16)
    acc = h
    phase, width = lru_next.pop(0)
    phase(zero_after(h, width))
    for c in range(n_chunks):
        if lru_next:
            phase, width = lru_next.pop(0)
            phase(zero_after(acc, width))
        sl = slice(c * FF_CHUNK, (c + 1) * FF_CHUNK)
        f = jnp.dot(u, wu_ref[:, sl], preferred_element_type=F32)
        f = jnp.square(jnp.maximum(f, 0.0)).astype(BF16)
        acc = acc + jnp.dot(f, wd_ref[sl, :], preferred_element_type=F32)
    o_ref[...] = _rms(acc, gf_ref[...])


def _out_mlp(x2, attn, xl, gl_in, lru_params, ga, gl, wo, g2, wu, wd, gf):
    def const(a):
        return pl.BlockSpec(a.shape, lambda i: (0,) * a.ndim, pipeline_mode=pl.Buffered(1))

    n_tiles = SEQ // TM4
    xspec = pl.BlockSpec((TM4, D_MODEL), lambda i: (i, 0))
    first = pl.BlockSpec((NS, TL, LRU_WIDTH), lambda i: (0, 0, 0), pipeline_mode=pl.Buffered(1))
    ahead = pl.BlockSpec((NS, TL, LRU_WIDTH), lambda i: (0, jnp.minimum(i + 1, n_tiles - 1), 0))
    zero = jnp.zeros((8, D_MODEL), jnp.int32)
    consts = (zero,) + tuple(lru_params) + (ga, gl, wo, g2, wu, wd, gf)
    return pl.pallas_call(
        _out_body,
        grid=(n_tiles,),
        in_specs=[
            xspec,
            pl.BlockSpec((NSLAB, TM4, LANES), lambda i: (0, i, 0)),
            first, first, ahead, ahead,
        ] + [const(a) for a in consts],
        out_specs=xspec,
        out_shape=jax.ShapeDtypeStruct((SEQ, D_MODEL), F32),
        scratch_shapes=[
            pltpu.VMEM((LRU_WIDTH // LANES, TM4, LANES), F32),
            pltpu.VMEM((CONV_WIDTH - 1, 8, LRU_WIDTH), F32),
            pltpu.VMEM((8, LRU_WIDTH), F32),
            pltpu.VMEM((NS, TL, LRU_WIDTH), F32),
            pltpu.VMEM((NS * TL, 2 * LRU_WIDTH), F32),
            pltpu.VMEM((NS, TL, LRU_WIDTH), F32),
            pltpu.VMEM((NS, TL, LRU_WIDTH), F32),
        ],
        compiler_params=pltpu.CompilerParams(
            dimension_semantics=("arbitrary",), vmem_limit_bytes=VMEM_LIMIT),
        name="out_mlp",
    )(x2, attn, xl, gl_in, xl, gl_in, *consts)


@functools.lru_cache(maxsize=None)
def _rope_tables():
    inv_freq = ROPE_THETA ** (-np.arange(0, ROT_DIM, 2, dtype=np.float64) / ROT_DIM)
    pad = HEAD_DIM - ROT_DIM
    reps = LANES // HEAD_DIM

    def lanes(rot_first, rot_second, rest, n):
        head = np.concatenate([rot_first, rot_second, np.full((n, pad), rest)], axis=1)
        return np.tile(head, (1, reps))

    tile_ang = (TM1 * np.arange(SEQ // TM1, dtype=np.float64))[:, None] * inv_freq[None, :]
    tc, ts = np.cos(tile_ang), np.sin(tile_ang)
    tiles = np.concatenate([lanes(tc, tc, 1.0, len(tc)), lanes(ts, ts, 0.0, len(ts))], axis=1)
    off_ang = np.arange(TM1, dtype=np.float64)[:, None] * inv_freq[None, :]
    oc, osn = np.cos(off_ang), np.sin(off_ang)
    offs = np.stack([lanes(oc, oc, 1.0, TM1), lanes(osn, osn, 0.0, TM1),
                     lanes(-oc, oc, 0.0, TM1), lanes(-osn, osn, 0.0, TM1)])
    offs = offs.reshape(4, NG1, RG, NS, LANES).transpose(0, 1, 3, 2, 4).reshape(4, TM1, LANES)
    return (tiles.astype(np.float32)[:, None, :], np.ascontiguousarray(offs.astype(np.float32)))


def _block_diag(w):
    n, b, _ = w.shape
    eye = jnp.eye(n, dtype=w.dtype)
    return (eye[:, None, :, None] * w[:, :, None, :]).reshape(n * b, n * b)


def _gate_tiles(w_r, w_i):
    per = MXU_TILE // (LRU_WIDTH // LRU_BLOCKS)
    tiles = [jnp.concatenate([_block_diag(w_r[t * per:(t + 1) * per]),
                              _block_diag(w_i[t * per:(t + 1) * per])], axis=1)
             for t in range(LRU_WIDTH // MXU_TILE)]
    return jnp.stack(tiles).astype(BF16)


def kernel(x, norm1_g, w_in, conv_w, conv_b, w_rgate, b_rgate, w_igate, b_igate, lru_lambda,
           attn_out_g, lru_out_g, w_out, norm2_g, w_mlp_up, w_mlp_down, final_g):
    assert x.shape == (1, SEQ, D_MODEL) and w_in.shape[0] == 1
    x2 = x.reshape(SEQ, D_MODEL)
    q, k, v, q1, k1, v1, xl, gl = _inproj(x2, norm1_g.reshape(1, D_MODEL), w_in[0])
    attn, wo, wu, wd = _attention(q, k, v, q1, k1, v1, (w_out[0], w_mlp_up[0], w_mlp_down[0]))
    w_gate = _gate_tiles(w_rgate[0], w_igate[0])
    b_gate = jnp.concatenate([b_rgate[0].reshape(1, -1), b_igate[0].reshape(1, -1)], axis=1)
    lru_params = (conv_w[0], conv_b[0].reshape(1, -1), w_gate, b_gate, lru_lambda[0].reshape(1, -1))
    out = _out_mlp(x2, attn, xl, gl, lru_params,
                   attn_out_g[0].reshape(1, -1), lru_out_g[0].reshape(1, -1),
                   wo, norm2_g[0].reshape(1, -1), wu, wd, final_g.reshape(1, -1))
    return out.reshape(1, SEQ, D_MODEL)
```

```python
import functools

import numpy as np
import jax
import jax.numpy as jnp
from jax import lax
from jax.experimental import pallas as pl
from jax.experimental.pallas import tpu as pltpu

F32 = jnp.float32
BF16 = jnp.bfloat16

D_MODEL = 1024
SEQ = 16384
ATTN_HEADS = 8
HEAD_DIM = 64
ATTN_WIDTH = ATTN_HEADS * HEAD_DIM
ROT_DIM = HEAD_DIM // 4
ROPE_THETA = 500000.0
LRU_WIDTH = D_MODEL - ATTN_WIDTH
LRU_BLOCKS = 8
CONV_WIDTH = 4
LRU_C = 8.0
IN_WIDTH = 3 * ATTN_WIDTH + 2 * LRU_WIDTH
D_FF = 4 * D_MODEL
EPS = 1e-6
WINDOW_STEPS = 128

NS = 16
RA = SEQ // NS
LANES = 128
MXU_TILE = 256
NSLAB = ATTN_WIDTH // LANES
NEG = -np.inf

VMEM_LIMIT = 56 * 1024 * 1024


def _rms(x, g):
    return x * lax.rsqrt(jnp.mean(x * x, axis=-1, keepdims=True) + EPS) * g


TM1 = 1024
PG = NS * NS
RG = PG // NS
NG1 = TM1 // PG
CH = 8
LOG2E = 1.4426950408889634


@functools.lru_cache(maxsize=None)
def _group_perm():
    n = np.arange(PG)
    p = np.zeros((PG, PG), np.float32)
    p[(n % NS) * RG + n // NS, n] = 1.0
    return p


def _inproj_body(x_ref, g_ref, w_ref, perm_ref, tile_ref, off_ref,
                 q_ref, k_ref, v_ref, q1_ref, k1_ref, v1_ref, xl_ref, gl_ref, w_scr):
    @pl.when(pl.program_id(0) == 0)
    def _():
        w_scr[...] = w_ref[...].astype(BF16)

    u = _rms(x_ref[...], g_ref[...]).astype(BF16)
    u = jnp.concatenate(
        [jnp.dot(perm_ref[...], u[g * PG:(g + 1) * PG], preferred_element_type=F32).astype(BF16)
         for g in range(NG1)], axis=0)
    z = jnp.dot(u, w_scr[...], preferred_element_type=F32)
    tile_cos, tile_sin = tile_ref[:, :LANES], tile_ref[:, LANES:]
    c = tile_cos * off_ref[0] - tile_sin * off_ref[1]
    s = tile_sin * off_ref[2] + tile_cos * off_ref[3]
    c = jnp.concatenate([c] * NSLAB, axis=1)
    s = jnp.concatenate([s] * NSLAB, axis=1)
    lane = lax.broadcasted_iota(jnp.int32, (1, ATTN_WIDTH), 1) % HEAD_DIM
    first_half = lane < ROT_DIM // 2

    def rope(t):
        up = pltpu.roll(t, ATTN_WIDTH - ROT_DIM // 2, axis=1)
        dn = pltpu.roll(t, ROT_DIM // 2, axis=1)
        return t * c + jnp.where(first_half, up, dn) * s

    q = rope(z[:, :ATTN_WIDTH]) * (HEAD_DIM ** -0.5 * LOG2E)
    k = rope(z[:, ATTN_WIDTH:2 * ATTN_WIDTH])
    v = z[:, 2 * ATTN_WIDTH:3 * ATTN_WIDTH]
    xl = z[:, 3 * ATTN_WIDTH:3 * ATTN_WIDTH + LRU_WIDTH]
    gl = z[:, 3 * ATTN_WIDTH + LRU_WIDTH:]
    for g in range(NG1):
        for r in range(NS):
            src = slice(g * PG + r * RG, g * PG + (r + 1) * RG)
            dst = slice(g * RG, (g + 1) * RG)
            for j in range(NSLAB):
                sl = slice(j * LANES, (j + 1) * LANES)
                q_ref[j, r, dst] = q[src, sl].astype(BF16)
                k_ref[j, r, dst] = k[src, sl].astype(BF16)
                v_ref[j, r, dst] = v[src, sl].astype(BF16)
            xl_ref[r, dst] = xl[src]
            gl_ref[r, dst] = gl[src]
    for g in range(NG1):
        for cl in range(RG // CH):
            for r in range(0, NS, 2):
                lo = g * PG + r * RG + cl * CH
                hi = lo + RG
                d0 = (g * (RG // CH) + cl) * NS * CH + r * CH
                dst = slice(d0, d0 + 2 * CH)
                for j in range(NSLAB):
                    sl = slice(j * LANES, (j + 1) * LANES)
                    for src, ref in ((q, q1_ref), (k, k1_ref), (v, v1_ref)):
                        ref[j, dst] = jnp.concatenate(
                            [src[lo:lo + CH, sl], src[hi:hi + CH, sl]], axis=0).astype(BF16)


def _inproj(x2, g, w):
    steps = SEQ // TM1
    rows = TM1 // NS
    slab_spec = pl.BlockSpec((NSLAB, NS, rows, LANES), lambda i: (0, 0, i, 0))
    chunk_spec = pl.BlockSpec((NSLAB, TM1, LANES), lambda i: (0, i, 0))
    row_spec = pl.BlockSpec((NS, rows, LRU_WIDTH), lambda i: (0, i, 0))
    rope_tiles, rope_offs = _rope_tables()
    slab_shape = jax.ShapeDtypeStruct((NSLAB, NS, RA, LANES), BF16)
    chunk_shape = jax.ShapeDtypeStruct((NSLAB, SEQ, LANES), BF16)
    row_shape = jax.ShapeDtypeStruct((NS, RA, LRU_WIDTH), F32)
    perm = jnp.asarray(_group_perm(), BF16)
    return pl.pallas_call(
        _inproj_body,
        grid=(steps,),
        in_specs=[
            pl.BlockSpec((TM1, D_MODEL), lambda i: (i, 0)),
            pl.BlockSpec((1, D_MODEL), lambda i: (0, 0)),
            pl.BlockSpec((D_MODEL, IN_WIDTH), lambda i: (0, 0), pipeline_mode=pl.Buffered(1)),
            pl.BlockSpec((PG, PG), lambda i: (0, 0)),
            pl.BlockSpec((None, 1, 2 * LANES), lambda i: (i, 0, 0)),
            pl.BlockSpec(rope_offs.shape, lambda i: (0, 0, 0)),
        ],
        out_specs=[slab_spec] * 3 + [chunk_spec] * 3 + [row_spec] * 2,
        out_shape=[slab_shape] * 3 + [chunk_shape] * 3 + [row_shape] * 2,
        scratch_shapes=[pltpu.VMEM((D_MODEL, IN_WIDTH), BF16)],
        compiler_params=pltpu.CompilerParams(
            dimension_semantics=("arbitrary",), vmem_limit_bytes=VMEM_LIMIT),
        name="inproj",
    )(x2, g, w, perm, rope_tiles, rope_offs)


TA = 128
UQ, UK = 128, 256
Q4, K4 = UQ // 4, UK // 4
AHEAD = 4
RING = 2 * AHEAD


@functools.lru_cache(maxsize=None)
def _attn_masks():
    def variants(diff, from_prev, per_unit):
        ok = (diff >= 0) & (diff <= WINDOW_STEPS)
        normal = np.where(ok, 0.0, NEG).astype(np.float32)
        first = np.where(ok & ~from_prev, 0.0, NEG).astype(np.float32)
        if per_unit:
            return np.stack([np.stack([first, normal]), np.stack([normal, normal])])
        return np.stack([first, normal])

    iq = np.arange(UQ)[:, None]
    ck = np.arange(UK)[None, :]
    m16 = variants(iq - ck + TA, np.broadcast_to(ck < TA, (UQ, UK)), False)
    cq, i4 = np.divmod(np.arange(UQ), Q4)
    ckk, j4 = np.divmod(np.arange(UK), K4)
    d4 = 4 * (i4[:, None] - j4[None, :] + Q4) + (cq[:, None] - ckk[None, :])
    m4 = variants(d4, np.broadcast_to(j4[None, :] < K4 - Q4, d4.shape), True)
    rq, i1 = np.divmod(np.arange(UQ), CH)
    kc, kin = np.divmod(np.arange(UK), UQ)
    rk, j1 = np.divmod(kin, CH)
    d1 = (NS * i1 + rq)[:, None] - (NS * j1 + rk + UQ * (kc - 1))[None, :]
    m1 = variants(d1, np.broadcast_to(kc[None, :] == 0, d1.shape), True)
    return m16, m4, m1


def _attn_body(q_ref, kc_ref, kp_ref, vc_ref, vp_ref, q1_ref, k1c_ref, k1p_ref, v1c_ref, v1p_ref,
               m16_ref, m4_ref, m1_ref, wo_ref, wu_ref, wd_ref,
               o_ref, wo_bf_ref, wu_bf_ref, wd_bf_ref,
               qm_scr, q1m_scr, k1_scr, v1_scr, s_scr, acc_scr, max_scr, sum_scr):
    wo_bf_ref[...] = wo_ref[...].astype(BF16)
    wu_bf_ref[...] = wu_ref[...].astype(BF16)
    wd_bf_ref[...] = wd_ref[...].astype(BF16)

    lane = lax.broadcasted_iota(jnp.int32, (1, LANES), 1)
    head0 = lane < HEAD_DIM
    zero = jnp.zeros((), BF16)

    def split(x):
        return jnp.where(head0, x, zero), jnp.where(head0, zero, x)

    qm_scr[0], qm_scr[1] = split(q_ref[...])
    q1m_scr[0], q1m_scr[1] = split(q1_ref[...])
    k1_scr[0:UQ] = k1p_ref[...]
    k1_scr[UQ:] = k1c_ref[...]
    v1_scr[0:UQ] = v1p_ref[...]
    v1_scr[UQ:] = v1c_ref[...]
    ones = jnp.ones((UK, LANES), BF16)

    def unit16(r):
        def store(o, mx, sm):
            acc_scr[0, r], max_scr[0, r], sum_scr[0, r] = o, mx, sm

        return (lambda h: qm_scr[h, r],
                lambda: jnp.concatenate([kp_ref[r], kc_ref[r]], axis=0),
                lambda: jnp.concatenate([vp_ref[r], vc_ref[r]], axis=0),
                lambda: m16_ref[...], store)

    def unit4(r4, b):
        streams = [r4 + 4 * c for c in range(4)]
        q0, k0 = b * Q4, b * Q4 + Q4 - K4

        def window(cur, prev, st):
            if k0 < 0:
                return [prev[st, TA + k0:TA], cur[st, 0:k0 + K4]]
            return [cur[st, k0:k0 + K4]]

        def store(o, mx, sm):
            for c, st in enumerate(streams):
                rows = slice(c * Q4, (c + 1) * Q4)
                acc_scr[1, st, q0:q0 + Q4] = o[rows]
                max_scr[1, st, q0:q0 + Q4] = mx[rows]
                sum_scr[1, st, q0:q0 + Q4] = sm[rows]

        return (lambda h: jnp.concatenate([qm_scr[h, st, q0:q0 + Q4] for st in streams], axis=0),
                lambda: jnp.concatenate(
                    [x for st in streams for x in window(kc_ref, kp_ref, st)], axis=0),
                lambda: jnp.concatenate(
                    [x for st in streams for x in window(vc_ref, vp_ref, st)], axis=0),
                lambda: m4_ref[min(b, 1)], store)

    def unit1(u):
        off, row = u * UQ, u * CH

        def store(o, mx, sm):
            for r in range(NS):
                rows = slice(r * CH, (r + 1) * CH)
                acc_scr[2, r, row:row + CH] = o[rows]
                max_scr[2, r, row:row + CH] = mx[rows]
                sum_scr[2, r, row:row + CH] = sm[rows]

        return (lambda h: q1m_scr[h, off:off + UQ],
                lambda: k1_scr[off:off + UK],
                lambda: v1_scr[off:off + UK],
                lambda: m1_ref[min(u, 1)], store)

    all_units = ([unit16(n) for n in range(NS)]
                 + [unit4(n // (TA // Q4), n % (TA // Q4)) for n in range(NS)]
                 + [unit1(n) for n in range(NS)])

    def score_unit(n):
        q, k, _, _, _ = all_units[n]
        s_scr[n % RING] = lax.dot_general(
            jnp.concatenate([q(0), q(1)], axis=0), k(), (((1,), (1,)), ((), ())),
            preferred_element_type=F32)

    def value_unit(n):
        _, _, v, mask, store = all_units[n]
        ps, mx = [], []
        for h in range(2):
            s = s_scr[n % RING, h * UQ:(h + 1) * UQ] + mask()
            m = jnp.max(s, axis=-1, keepdims=True)
            ps.append(jnp.exp2(s - m).astype(BF16))
            mx.append(m)
        res = jnp.dot(jnp.concatenate(ps, axis=0), jnp.concatenate([v(), ones], axis=1),
                      preferred_element_type=F32)
        top, bot = res[:UQ], res[UQ:]
        store(jnp.where(head0, top[:, :LANES], bot[:, :LANES]),
              jnp.where(head0, mx[0], mx[1]),
              jnp.where(head0, top[:, LANES:], bot[:, LANES:]))

    for n in range(AHEAD):
        score_unit(n)
    for n in range(len(all_units)):
        if n + AHEAD < len(all_units):
            score_unit(n + AHEAD)
        value_unit(n)

    def comb(r, carry):
        m0, m1, m2 = max_scr[0, r], max_scr[1, r], max_scr[2, r]
        m = jnp.maximum(jnp.maximum(m0, m1), m2)
        w0, w1, w2 = jnp.exp2(m0 - m), jnp.exp2(m1 - m), jnp.exp2(m2 - m)
        num = w0 * acc_scr[0, r] + w1 * acc_scr[1, r] + w2 * acc_scr[2, r]
        den = w0 * sum_scr[0, r] + w1 * sum_scr[1, r] + w2 * sum_scr[2, r]
        o_ref[r] = num / den
        return carry

    lax.fori_loop(0, NS, comb, 0, unroll=4)


def _attention(q, k, v, q1, k1, v1, later_weights):
    m16, m4, m1 = _attn_masks()
    cur = pl.BlockSpec((None, NS, TA, LANES), lambda j, i: (j, 0, i, 0))
    prev = pl.BlockSpec((None, NS, TA, LANES), lambda j, i: (j, 0, jnp.maximum(i - 1, 0), 0))
    cur1 = pl.BlockSpec((None, NS * TA, LANES), lambda j, i: (j, i, 0))
    prev1 = pl.BlockSpec((None, UQ, LANES),
                         lambda j, i: (j, jnp.maximum(i * (NS * TA // UQ) - 1, 0), 0))

    def tile_kind(m):
        return pl.BlockSpec((None,) + m.shape[1:],
                            lambda j, i: (jnp.minimum(i, 1),) + (0,) * (m.ndim - 1))

    n_tiles = RA // TA
    steps = NSLAB * n_tiles
    cast_specs = [pl.BlockSpec((a.shape[0] // steps, a.shape[1]), lambda j, i: (j * n_tiles + i, 0))
                  for a in later_weights]
    cast_shapes = [jax.ShapeDtypeStruct(a.shape, BF16) for a in later_weights]
    return pl.pallas_call(
        _attn_body,
        grid=(NSLAB, n_tiles),
        in_specs=[cur, cur, prev, cur, prev, cur1, cur1, prev1, cur1, prev1,
                  tile_kind(m16), tile_kind(m4), tile_kind(m1)] + cast_specs,
        out_specs=[pl.BlockSpec((None, NS, TA, LANES), lambda j, i: (j, 0, i, 0))] + cast_specs,
        out_shape=[jax.ShapeDtypeStruct((NSLAB, NS, RA, LANES), F32)] + cast_shapes,
        scratch_shapes=[
            pltpu.VMEM((2, NS, TA, LANES), BF16),
            pltpu.VMEM((2, NS * TA, LANES), BF16),
            pltpu.VMEM((UQ + NS * TA, LANES), BF16),
            pltpu.VMEM((UQ + NS * TA, LANES), BF16),
            pltpu.VMEM((RING, 2 * UQ, UK), F32),
            pltpu.VMEM((3, NS, TA, LANES), F32),
            pltpu.VMEM((3, NS, TA, LANES), F32),
            pltpu.VMEM((3, NS, TA, LANES), F32),
        ],
        compiler_params=pltpu.CompilerParams(
            dimension_semantics=("arbitrary", "arbitrary"), vmem_limit_bytes=VMEM_LIMIT),
        name="dilated_attn",
    )(q, k, k, v, v, q1, k1, k1, v1, v1, m16, m4, m1, *later_weights)


TL = 32


def _lru_phases(xl_ref, gl_ref, cw_ref, cb_ref, wg_ref, bg_ref, lam_ref, y_ref,
                tail_scr, carry_scr, xc_scr, g_scr, pl_scr, hl_scr):
    row = lax.broadcasted_iota(jnp.int32, (TL, LRU_WIDTH), 0)
    state = {}

    def conv(after=0.0):
        cw = cw_ref[...]
        cb = cb_ref[...] + after

        def tap(r, k):
            st = r - k
            if st >= 0:
                return xl_ref[st]
            st += NS
            prev_last = tail_scr[st - (NS - CONV_WIDTH + 1), 7:8, :]
            return jnp.where(row == 0, prev_last, pltpu.roll(xl_ref[st], 1, axis=0))

        for r in range(NS):
            xc_scr[r] = cb + sum(cw[CONV_WIDTH - 1 - k:CONV_WIDTH - k] * tap(r, k)
                                 for k in range(CONV_WIDTH))
        for n in range(CONV_WIDTH - 1):
            tail_scr[n] = xl_ref[NS - CONV_WIDTH + 1 + n, TL - 8:TL, :]

    def gates(after=0.0):
        xc_all = xc_scr[...].reshape(NS * TL, LRU_WIDTH).astype(BF16)
        bias = bg_ref[...] + after
        for t in range(LRU_WIDTH // MXU_TILE):
            cols = slice(t * MXU_TILE, (t + 1) * MXU_TILE)
            res = jnp.dot(xc_all[:, cols], wg_ref[t], preferred_element_type=F32)
            for gate in range(2):
                dst = slice(gate * LRU_WIDTH + t * MXU_TILE, gate * LRU_WIDTH + (t + 1) * MXU_TILE)
                g_scr[:, dst] = res[:, gate * MXU_TILE:(gate + 1) * MXU_TILE] + bias[:, dst]

    def local_scan(streams):
        def run(after=0.0):
            neg_lam = after - lam_ref[...]
            softplus = jnp.maximum(neg_lam, 0.0) + jnp.log1p(jnp.exp(-jnp.abs(neg_lam)))
            p_run, h_run = state.get("run", (None, None))
            for r in streams:
                g = g_scr[r * TL:(r + 1) * TL]
                rg = jax.nn.sigmoid(g[:, :LRU_WIDTH])
                ig = jax.nn.sigmoid(g[:, LRU_WIDTH:])
                log_a = -LRU_C * rg * softplus
                a = jnp.exp(log_a)
                th = jnp.tanh(log_a)
                bx = jnp.sqrt(-2.0 * th / (1.0 - th)) * (ig * xc_scr[r])
                if p_run is None:
                    p_run, h_run = a, bx
                else:
                    h_run = a * h_run + bx
                    p_run = a * p_run
                pl_scr[r] = p_run
                hl_scr[r] = h_run
            state["run"] = (p_run, h_run)
        return run

    def row_scan():
        pa, hb = state["run"]
        sft = 1
        while sft < TL:
            keep = row >= sft
            pa_s = jnp.where(keep, pltpu.roll(pa, sft, axis=0), 1.0)
            hb_s = jnp.where(keep, pltpu.roll(hb, sft, axis=0), 0.0)
            hb = pa * hb_s + hb
            pa = pa * pa_s
            sft *= 2
        carry = carry_scr[0:1, :]
        e = pa * carry + hb
        state["e_prev"] = jnp.where(row == 0, carry, pltpu.roll(e, 1, axis=0))
        carry_scr[...] = jnp.broadcast_to(e[TL - 1:TL, :], carry_scr.shape)

    def finalize(after=0.0):
        e_prev = state["e_prev"] + after
        for r in range(NS):
            h = pl_scr[r] * e_prev + hl_scr[r]
            y_ref[r] = jax.nn.gelu(gl_ref[r], approximate=True) * h

    half = NS // 2

    def second_half(after=0.0):
        local_scan(range(half, NS))(after)
        row_scan()

    return [(conv, LRU_WIDTH), (gates, 2 * LRU_WIDTH), (local_scan(range(half)), LRU_WIDTH),
            (second_half, LRU_WIDTH), (finalize, LRU_WIDTH)]


TM4 = NS * TL
NG4 = TM4 // PG
FF_CHUNK = 1024


def _out_body(x_ref, at_ref, xl0_ref, gl0_ref, xln_ref, gln_ref, zero_ref,
              cw_ref, cb_ref, wg_ref, bg_ref, lam_ref,
              perm_ref, ga_ref, gl_ref, wo_ref, g2_ref, wu_ref, wd_ref, gf_ref,
              o_ref,
              lr_scr, tail_scr, carry_scr, xc_scr, g_scr, pl_scr, hl_scr):
    lru_params = (cw_ref, cb_ref, wg_ref, bg_ref, lam_ref)
    lru_state = (tail_scr, carry_scr, xc_scr, g_scr, pl_scr, hl_scr)

    @pl.when(pl.program_id(0) == 0)
    def _():
        tail_scr[...] = jnp.zeros_like(tail_scr)
        carry_scr[...] = jnp.zeros_like(carry_scr)
        for phase, _ in _lru_phases(xl0_ref, gl0_ref, *lru_params, lr_scr, *lru_state):
            phase()

    groups = []
    for g in range(NG4):
        rows = slice(g * RG, (g + 1) * RG)
        attn = jnp.concatenate(
            [jnp.concatenate([at_ref[j, r, rows] for j in range(NSLAB)], axis=1) for r in range(NS)],
            axis=0)
        lru = jnp.concatenate([lr_scr[r, rows] for r in range(NS)], axis=0)
        mixed = jnp.concatenate([_rms(attn, ga_ref[...]), _rms(lru, gl_ref[...])], axis=1)
        groups.append(jnp.dot(perm_ref[...], mixed.astype(BF16),
                              preferred_element_type=F32).astype(BF16))
    mixed = jnp.concatenate(groups, axis=0)
    lru_next = _lru_phases(xln_ref, gln_ref, *lru_params, lr_scr, *lru_state)
    n_chunks = D_FF // FF_CHUNK
    assert len(lru_next) <= n_chunks + 1

    def zero_after(val, width):
        bits = pltpu.bitcast(val[0:1, :width], jnp.int32) & zero_ref[0:1, :width]
        return pltpu.bitcast(bits, F32)

    h = x_ref[...] + jnp.dot(mixed, wo_ref[...], preferred_element_type=F32)
    u = _rms(h, g2_ref[...]).astype(BF16)
    acc = h
    phase, width = lru_next.pop(0)
    phase(zero_after(h, width))
    for c in range(n_chunks):
        if lru_next:
            phase, width = lru_next.pop(0)
            phase(zero_after(acc, width))
        sl = slice(c * FF_CHUNK, (c + 1) * FF_CHUNK)
        f = jnp.dot(u, wu_ref[:, sl], preferred_element_type=F32)
        f = jnp.square(jnp.maximum(f, 0.0)).astype(BF16)
        acc = acc + jnp.dot(f, wd_ref[sl, :], preferred_element_type=F32)
    o_ref[...] = _rms(acc, gf_ref[...])


def _out_mlp(x2, attn, xl, gl_in, lru_params, ga, gl, wo, g2, wu, wd, gf):
    def const(a):
        return pl.BlockSpec(a.shape, lambda i: (0,) * a.ndim, pipeline_mode=pl.Buffered(1))

    n_tiles = SEQ // TM4
    xspec = pl.BlockSpec((TM4, D_MODEL), lambda i: (i, 0))
    first = pl.BlockSpec((NS, TL, LRU_WIDTH), lambda i: (0, 0, 0), pipeline_mode=pl.Buffered(1))
    ahead = pl.BlockSpec((NS, TL, LRU_WIDTH), lambda i: (0, jnp.minimum(i + 1, n_tiles - 1), 0))
    perm = jnp.asarray(_group_perm(), BF16)
    zero = jnp.zeros((8, D_MODEL), jnp.int32)
    consts = (zero,) + tuple(lru_params) + (perm, ga, gl, wo, g2, wu, wd, gf)
    return pl.pallas_call(
        _out_body,
        grid=(n_tiles,),
        in_specs=[
            xspec,
            pl.BlockSpec((NSLAB, NS, TL, LANES), lambda i: (0, 0, i, 0)),
            first, first, ahead, ahead,
        ] + [const(a) for a in consts],
        out_specs=xspec,
        out_shape=jax.ShapeDtypeStruct((SEQ, D_MODEL), F32),
        scratch_shapes=[
            pltpu.VMEM((NS, TL, LRU_WIDTH), F32),
            pltpu.VMEM((CONV_WIDTH - 1, 8, LRU_WIDTH), F32),
            pltpu.VMEM((8, LRU_WIDTH), F32),
            pltpu.VMEM((NS, TL, LRU_WIDTH), F32),
            pltpu.VMEM((NS * TL, 2 * LRU_WIDTH), F32),
            pltpu.VMEM((NS, TL, LRU_WIDTH), F32),
            pltpu.VMEM((NS, TL, LRU_WIDTH), F32),
        ],
        compiler_params=pltpu.CompilerParams(
            dimension_semantics=("arbitrary",), vmem_limit_bytes=VMEM_LIMIT),
        name="out_mlp",
    )(x2, attn, xl, gl_in, xl, gl_in, *consts)


@functools.lru_cache(maxsize=None)
def _rope_tables():
    inv_freq = ROPE_THETA ** (-np.arange(0, ROT_DIM, 2, dtype=np.float64) / ROT_DIM)
    pad = HEAD_DIM - ROT_DIM
    reps = LANES // HEAD_DIM

    def lanes(rot_first, rot_second, rest, n):
        head = np.concatenate([rot_first, rot_second, np.full((n, pad), rest)], axis=1)
        return np.tile(head, (1, reps))

    tile_ang = (TM1 * np.arange(SEQ // TM1, dtype=np.float64))[:, None] * inv_freq[None, :]
    tc, ts = np.cos(tile_ang), np.sin(tile_ang)
    tiles = np.concatenate([lanes(tc, tc, 1.0, len(tc)), lanes(ts, ts, 0.0, len(ts))], axis=1)
    off_ang = np.arange(TM1, dtype=np.float64)[:, None] * inv_freq[None, :]
    oc, osn = np.cos(off_ang), np.sin(off_ang)
    offs = np.stack([lanes(oc, oc, 1.0, TM1), lanes(osn, osn, 0.0, TM1),
                     lanes(-oc, oc, 0.0, TM1), lanes(-osn, osn, 0.0, TM1)])
    offs = offs.reshape(4, NG1, RG, NS, LANES).transpose(0, 1, 3, 2, 4).reshape(4, TM1, LANES)
    return (tiles.astype(np.float32)[:, None, :], np.ascontiguousarray(offs.astype(np.float32)))


def _block_diag(w):
    n, b, _ = w.shape
    eye = jnp.eye(n, dtype=w.dtype)
    return (eye[:, None, :, None] * w[:, :, None, :]).reshape(n * b, n * b)


def _gate_tiles(w_r, w_i):
    per = MXU_TILE // (LRU_WIDTH // LRU_BLOCKS)
    tiles = [jnp.concatenate([_block_diag(w_r[t * per:(t + 1) * per]),
                              _block_diag(w_i[t * per:(t + 1) * per])], axis=1)
             for t in range(LRU_WIDTH // MXU_TILE)]
    return jnp.stack(tiles).astype(BF16)


def kernel(x, norm1_g, w_in, conv_w, conv_b, w_rgate, b_rgate, w_igate, b_igate, lru_lambda,
           attn_out_g, lru_out_g, w_out, norm2_g, w_mlp_up, w_mlp_down, final_g):
    assert x.shape == (1, SEQ, D_MODEL) and w_in.shape[0] == 1
    x2 = x.reshape(SEQ, D_MODEL)
    q, k, v, q1, k1, v1, xl, gl = _inproj(x2, norm1_g.reshape(1, D_MODEL), w_in[0])
    attn, wo, wu, wd = _attention(q, k, v, q1, k1, v1, (w_out[0], w_mlp_up[0], w_mlp_down[0]))
    w_gate = _gate_tiles(w_rgate[0], w_igate[0])
    b_gate = jnp.concatenate([b_rgate[0].reshape(1, -1), b_igate[0].reshape(1, -1)], axis=1)
    lru_params = (conv_w[0], conv_b[0].reshape(1, -1), w_gate, b_gate, lru_lambda[0].reshape(1, -1))
    out = _out_mlp(x2, attn, xl, gl, lru_params,
                   attn_out_g[0].reshape(1, -1), lru_out_g[0].reshape(1, -1),
                   wo, norm2_g[0].reshape(1, -1), wu, wd, final_g.reshape(1, -1))
    return out.reshape(1, SEQ, D_MODEL)
```

```python
import functools

import numpy as np
import jax
import jax.numpy as jnp
from jax import lax
from jax.experimental import pallas as pl
from jax.experimental.pallas import tpu as pltpu

F32 = jnp.float32
BF16 = jnp.bfloat16

D_MODEL = 1024
SEQ = 16384
ATTN_HEADS = 8
HEAD_DIM = 64
ATTN_WIDTH = ATTN_HEADS * HEAD_DIM
ROT_DIM = HEAD_DIM // 4
ROPE_THETA = 500000.0
LRU_WIDTH = D_MODEL - ATTN_WIDTH
LRU_BLOCKS = 8
CONV_WIDTH = 4
LRU_C = 8.0
IN_WIDTH = 3 * ATTN_WIDTH + 2 * LRU_WIDTH
D_FF = 4 * D_MODEL
EPS = 1e-6
WINDOW_STEPS = 128

NS = 16
RA = SEQ // NS
LANES = 128
SUBLANES = 8
MXU_TILE = 256
NSLAB = ATTN_WIDTH // LANES
NEG = -np.inf

VMEM_LIMIT = 56 * 1024 * 1024


def _rms(x, g):
    return x * lax.rsqrt(jnp.mean(x * x, axis=-1, keepdims=True) + EPS) * g


TM1 = 1024
PG = NS * NS
RG = PG // NS
NG1 = TM1 // PG
CH = SUBLANES
LOG2E = 1.4426950408889634


@functools.lru_cache(maxsize=None)
def _group_perm():
    n = np.arange(PG)
    p = np.zeros((PG, PG), np.float32)
    p[(n % NS) * RG + n // NS, n] = 1.0
    return p


def _inproj_body(x_ref, g_ref, w_ref, perm_ref, tile_ref, off_ref,
                 q_ref, k_ref, v_ref, q1_ref, k1_ref, v1_ref, xl_ref, gl_ref, w_scr):
    @pl.when(pl.program_id(0) == 0)
    def _():
        w_scr[...] = w_ref[...].astype(BF16)

    u = _rms(x_ref[...], g_ref[...]).astype(BF16)
    u = jnp.concatenate(
        [jnp.dot(perm_ref[...], u[g * PG:(g + 1) * PG], preferred_element_type=F32).astype(BF16)
         for g in range(NG1)], axis=0)
    z = jnp.dot(u, w_scr[...], preferred_element_type=F32)
    tile_cos, tile_sin = tile_ref[:, :LANES], tile_ref[:, LANES:]
    c = tile_cos * off_ref[0] - tile_sin * off_ref[1]
    s = tile_sin * off_ref[2] + tile_cos * off_ref[3]
    c = jnp.concatenate([c] * NSLAB, axis=1)
    s = jnp.concatenate([s] * NSLAB, axis=1)
    lane = lax.broadcasted_iota(jnp.int32, (1, ATTN_WIDTH), 1) % HEAD_DIM
    first_half = lane < ROT_DIM // 2

    def rope(t):
        up = pltpu.roll(t, ATTN_WIDTH - ROT_DIM // 2, axis=1)
        dn = pltpu.roll(t, ROT_DIM // 2, axis=1)
        return t * c + jnp.where(first_half, up, dn) * s

    q = rope(z[:, :ATTN_WIDTH]) * (HEAD_DIM ** -0.5 * LOG2E)
    k = rope(z[:, ATTN_WIDTH:2 * ATTN_WIDTH])
    v = z[:, 2 * ATTN_WIDTH:3 * ATTN_WIDTH]
    xl = z[:, 3 * ATTN_WIDTH:3 * ATTN_WIDTH + LRU_WIDTH]
    gl = z[:, 3 * ATTN_WIDTH + LRU_WIDTH:]
    for g in range(NG1):
        for r in range(NS):
            src = slice(g * PG + r * RG, g * PG + (r + 1) * RG)
            dst = slice(g * RG, (g + 1) * RG)
            for j in range(NSLAB):
                sl = slice(j * LANES, (j + 1) * LANES)
                q_ref[j, r, dst] = q[src, sl].astype(BF16)
                k_ref[j, r, dst] = k[src, sl].astype(BF16)
                v_ref[j, r, dst] = v[src, sl].astype(BF16)
            xl_ref[r, dst] = xl[src]
            gl_ref[r, dst] = gl[src]
    for g in range(NG1):
        for cl in range(RG // CH):
            for r in range(0, NS, 2):
                lo = g * PG + r * RG + cl * CH
                hi = lo + RG
                d0 = (g * (RG // CH) + cl) * NS * CH + r * CH
                dst = slice(d0, d0 + 2 * CH)
                for j in range(NSLAB):
                    sl = slice(j * LANES, (j + 1) * LANES)
                    for src, ref in ((q, q1_ref), (k, k1_ref), (v, v1_ref)):
                        ref[j, dst] = jnp.concatenate(
                            [src[lo:lo + CH, sl], src[hi:hi + CH, sl]], axis=0).astype(BF16)


def _inproj(x2, g, w):
    steps = SEQ // TM1
    rows = TM1 // NS
    slab_spec = pl.BlockSpec((NSLAB, NS, rows, LANES), lambda i: (0, 0, i, 0))
    chunk_spec = pl.BlockSpec((NSLAB, TM1, LANES), lambda i: (0, i, 0))
    row_spec = pl.BlockSpec((NS, rows, LRU_WIDTH), lambda i: (0, i, 0))
    rope_tiles, rope_offs = _rope_tables()
    slab_shape = jax.ShapeDtypeStruct((NSLAB, NS, RA, LANES), BF16)
    chunk_shape = jax.ShapeDtypeStruct((NSLAB, SEQ, LANES), BF16)
    row_shape = jax.ShapeDtypeStruct((NS, RA, LRU_WIDTH), F32)
    perm = jnp.asarray(_group_perm(), BF16)
    return pl.pallas_call(
        _inproj_body,
        grid=(steps,),
        in_specs=[
            pl.BlockSpec((TM1, D_MODEL), lambda i: (i, 0)),
            pl.BlockSpec((1, D_MODEL), lambda i: (0, 0)),
            pl.BlockSpec((D_MODEL, IN_WIDTH), lambda i: (0, 0), pipeline_mode=pl.Buffered(1)),
            pl.BlockSpec((PG, PG), lambda i: (0, 0)),
            pl.BlockSpec((None, 1, 2 * LANES), lambda i: (i, 0, 0)),
            pl.BlockSpec(rope_offs.shape, lambda i: (0, 0, 0)),
        ],
        out_specs=[slab_spec] * 3 + [chunk_spec] * 3 + [row_spec] * 2,
        out_shape=[slab_shape] * 3 + [chunk_shape] * 3 + [row_shape] * 2,
        scratch_shapes=[pltpu.VMEM((D_MODEL, IN_WIDTH), BF16)],
        compiler_params=pltpu.CompilerParams(
            dimension_semantics=("arbitrary",), vmem_limit_bytes=VMEM_LIMIT),
        name="inproj",
    )(x2, g, w, perm, rope_tiles, rope_offs)


TA = WINDOW_STEPS
UQ, UK = WINDOW_STEPS, 2 * WINDOW_STEPS
Q4, K4 = UQ // 4, UK // 4
AHEAD = 4
RING = 2 * AHEAD


@functools.lru_cache(maxsize=None)
def _attn_masks():
    def variants(diff, from_prev, per_unit):
        ok = (diff >= 0) & (diff <= WINDOW_STEPS)
        normal = np.where(ok, 0.0, NEG).astype(np.float32)
        first = np.where(ok & ~from_prev, 0.0, NEG).astype(np.float32)
        if per_unit:
            return np.stack([np.stack([first, normal]), np.stack([normal, normal])])
        return np.stack([first, normal])

    iq = np.arange(UQ)[:, None]
    ck = np.arange(UK)[None, :]
    m16 = variants(iq - ck + TA, np.broadcast_to(ck < TA, (UQ, UK)), False)
    cq, i4 = np.divmod(np.arange(UQ), Q4)
    ckk, j4 = np.divmod(np.arange(UK), K4)
    d4 = 4 * (i4[:, None] - j4[None, :] + Q4) + (cq[:, None] - ckk[None, :])
    m4 = variants(d4, np.broadcast_to(j4[None, :] < K4 - Q4, d4.shape), True)
    rq, i1 = np.divmod(np.arange(UQ), CH)
    kc, kin = np.divmod(np.arange(UK), UQ)
    rk, j1 = np.divmod(kin, CH)
    d1 = (NS * i1 + rq)[:, None] - (NS * j1 + rk + UQ * (kc - 1))[None, :]
    m1 = variants(d1, np.broadcast_to(kc[None, :] == 0, d1.shape), True)
    return m16, m4, m1


def _attn_body(q_ref, kc_ref, kp_ref, vc_ref, vp_ref, q1_ref, k1c_ref, k1p_ref, v1c_ref, v1p_ref,
               m16_ref, m4_ref, m1_ref, wo_ref, wu_ref, wd_ref,
               o_ref, wo_bf_ref, wu_bf_ref, wd_bf_ref,
               qm_scr, q1m_scr, k1_scr, v1_scr, s_scr, acc_scr, max_scr, sum_scr):
    wo_bf_ref[...] = wo_ref[...].astype(BF16)
    wu_bf_ref[...] = wu_ref[...].astype(BF16)
    wd_bf_ref[...] = wd_ref[...].astype(BF16)

    lane = lax.broadcasted_iota(jnp.int32, (1, LANES), 1)
    head0 = lane < HEAD_DIM
    zero = jnp.zeros((), BF16)

    def split(x):
        return jnp.where(head0, x, zero), jnp.where(head0, zero, x)

    qm_scr[0], qm_scr[1] = split(q_ref[...])
    q1m_scr[0], q1m_scr[1] = split(q1_ref[...])
    k1_scr[0:UQ] = k1p_ref[...]
    k1_scr[UQ:] = k1c_ref[...]
    v1_scr[0:UQ] = v1p_ref[...]
    v1_scr[UQ:] = v1c_ref[...]
    ones = jnp.ones((UK, LANES), BF16)

    def unit16(r):
        def store(o, mx, sm):
            acc_scr[0, r], max_scr[0, r], sum_scr[0, r] = o, mx, sm

        return (lambda h: qm_scr[h, r],
                lambda: jnp.concatenate([kp_ref[r], kc_ref[r]], axis=0),
                lambda: jnp.concatenate([vp_ref[r], vc_ref[r]], axis=0),
                lambda: m16_ref[...], store)

    def unit4(r4, b):
        streams = [r4 + 4 * c for c in range(4)]
        q0, k0 = b * Q4, b * Q4 + Q4 - K4

        def window(cur, prev, st):
            if k0 < 0:
                return [prev[st, TA + k0:TA], cur[st, 0:k0 + K4]]
            return [cur[st, k0:k0 + K4]]

        def store(o, mx, sm):
            for c, st in enumerate(streams):
                rows = slice(c * Q4, (c + 1) * Q4)
                acc_scr[1, st, q0:q0 + Q4] = o[rows]
                max_scr[1, st, q0:q0 + Q4] = mx[rows]
                sum_scr[1, st, q0:q0 + Q4] = sm[rows]

        return (lambda h: jnp.concatenate([qm_scr[h, st, q0:q0 + Q4] for st in streams], axis=0),
                lambda: jnp.concatenate(
                    [x for st in streams for x in window(kc_ref, kp_ref, st)], axis=0),
                lambda: jnp.concatenate(
                    [x for st in streams for x in window(vc_ref, vp_ref, st)], axis=0),
                lambda: m4_ref[min(b, 1)], store)

    def unit1(u):
        off, row = u * UQ, u * CH

        def store(o, mx, sm):
            for r in range(NS):
                rows = slice(r * CH, (r + 1) * CH)
                acc_scr[2, r, row:row + CH] = o[rows]
                max_scr[2, r, row:row + CH] = mx[rows]
                sum_scr[2, r, row:row + CH] = sm[rows]

        return (lambda h: q1m_scr[h, off:off + UQ],
                lambda: k1_scr[off:off + UK],
                lambda: v1_scr[off:off + UK],
                lambda: m1_ref[min(u, 1)], store)

    all_units = ([unit16(n) for n in range(NS)]
                 + [unit4(n // (TA // Q4), n % (TA // Q4)) for n in range(NS)]
                 + [unit1(n) for n in range(NS)])

    def score_unit(n):
        q, k, _, _, _ = all_units[n]
        s_scr[n % RING] = lax.dot_general(
            jnp.concatenate([q(0), q(1)], axis=0), k(), (((1,), (1,)), ((), ())),
            preferred_element_type=F32)

    def value_unit(n):
        _, _, v, mask, store = all_units[n]
        ps, mx = [], []
        for h in range(2):
            s = s_scr[n % RING, h * UQ:(h + 1) * UQ] + mask()
            m = jnp.max(s, axis=-1, keepdims=True)
            ps.append(jnp.exp2(s - m).astype(BF16))
            mx.append(m)
        res = jnp.dot(jnp.concatenate(ps, axis=0), jnp.concatenate([v(), ones], axis=1),
                      preferred_element_type=F32)
        top, bot = res[:UQ], res[UQ:]
        store(jnp.where(head0, top[:, :LANES], bot[:, :LANES]),
              jnp.where(head0, mx[0], mx[1]),
              jnp.where(head0, top[:, LANES:], bot[:, LANES:]))

    for n in range(AHEAD):
        score_unit(n)
    for n in range(len(all_units)):
        if n + AHEAD < len(all_units):
            score_unit(n + AHEAD)
        value_unit(n)

    def comb(r, carry):
        m0, m1, m2 = max_scr[0, r], max_scr[1, r], max_scr[2, r]
        m = jnp.maximum(jnp.maximum(m0, m1), m2)
        w0, w1, w2 = jnp.exp2(m0 - m), jnp.exp2(m1 - m), jnp.exp2(m2 - m)
        num = w0 * acc_scr[0, r] + w1 * acc_scr[1, r] + w2 * acc_scr[2, r]
        den = w0 * sum_scr[0, r] + w1 * sum_scr[1, r] + w2 * sum_scr[2, r]
        o_ref[r] = num / den
        return carry

    lax.fori_loop(0, NS, comb, 0, unroll=4)


def _attention(q, k, v, q1, k1, v1, later_weights):
    m16, m4, m1 = _attn_masks()
    cur = pl.BlockSpec((None, NS, TA, LANES), lambda j, i: (j, 0, i, 0))
    prev = pl.BlockSpec((None, NS, TA, LANES), lambda j, i: (j, 0, jnp.maximum(i - 1, 0), 0))
    cur1 = pl.BlockSpec((None, NS * TA, LANES), lambda j, i: (j, i, 0))
    prev1 = pl.BlockSpec((None, UQ, LANES),
                         lambda j, i: (j, jnp.maximum(i * (NS * TA // UQ) - 1, 0), 0))

    def tile_kind(m):
        return pl.BlockSpec((None,) + m.shape[1:],
                            lambda j, i: (jnp.minimum(i, 1),) + (0,) * (m.ndim - 1))

    n_tiles = RA // TA
    steps = NSLAB * n_tiles
    cast_specs = [pl.BlockSpec((a.shape[0] // steps, a.shape[1]), lambda j, i: (j * n_tiles + i, 0))
                  for a in later_weights]
    cast_shapes = [jax.ShapeDtypeStruct(a.shape, BF16) for a in later_weights]
    return pl.pallas_call(
        _attn_body,
        grid=(NSLAB, n_tiles),
        in_specs=[cur, cur, prev, cur, prev, cur1, cur1, prev1, cur1, prev1,
                  tile_kind(m16), tile_kind(m4), tile_kind(m1)] + cast_specs,
        out_specs=[pl.BlockSpec((None, NS, TA, LANES), lambda j, i: (j, 0, i, 0))] + cast_specs,
        out_shape=[jax.ShapeDtypeStruct((NSLAB, NS, RA, LANES), F32)] + cast_shapes,
        scratch_shapes=[
            pltpu.VMEM((2, NS, TA, LANES), BF16),
            pltpu.VMEM((2, NS * TA, LANES), BF16),
            pltpu.VMEM((UQ + NS * TA, LANES), BF16),
            pltpu.VMEM((UQ + NS * TA, LANES), BF16),
            pltpu.VMEM((RING, 2 * UQ, UK), F32),
            pltpu.VMEM((3, NS, TA, LANES), F32),
            pltpu.VMEM((3, NS, TA, LANES), F32),
            pltpu.VMEM((3, NS, TA, LANES), F32),
        ],
        compiler_params=pltpu.CompilerParams(
            dimension_semantics=("arbitrary", "arbitrary"), vmem_limit_bytes=VMEM_LIMIT),
        name="dilated_attn",
    )(q, k, k, v, v, q1, k1, k1, v1, v1, m16, m4, m1, *later_weights)


TL = 32


def _lru_phases(xl_ref, gl_ref, cw_ref, cb_ref, wg_ref, bg_ref, lam_ref, y_ref,
                tail_scr, carry_scr, xc_scr, g_scr, pl_scr, hl_scr):
    row = lax.broadcasted_iota(jnp.int32, (TL, LRU_WIDTH), 0)
    state = {}

    def conv(after=0.0):
        cw = cw_ref[...]
        cb = cb_ref[...] + after

        def tap(r, k):
            st = r - k
            if st >= 0:
                return xl_ref[st]
            st += NS
            prev_last = tail_scr[st - (NS - CONV_WIDTH + 1), SUBLANES - 1:SUBLANES, :]
            return jnp.where(row == 0, prev_last, pltpu.roll(xl_ref[st], 1, axis=0))

        for r in range(NS):
            xc_scr[r] = cb + sum(cw[CONV_WIDTH - 1 - k:CONV_WIDTH - k] * tap(r, k)
                                 for k in range(CONV_WIDTH))
        for n in range(CONV_WIDTH - 1):
            tail_scr[n] = xl_ref[NS - CONV_WIDTH + 1 + n, TL - SUBLANES:TL, :]

    def gates(after=0.0):
        xc_all = xc_scr[...].reshape(NS * TL, LRU_WIDTH).astype(BF16)
        bias = bg_ref[...] + after
        for t in range(LRU_WIDTH // MXU_TILE):
            cols = slice(t * MXU_TILE, (t + 1) * MXU_TILE)
            res = jnp.dot(xc_all[:, cols], wg_ref[t], preferred_element_type=F32)
            for gate in range(2):
                dst = slice(gate * LRU_WIDTH + t * MXU_TILE, gate * LRU_WIDTH + (t + 1) * MXU_TILE)
                g_scr[:, dst] = res[:, gate * MXU_TILE:(gate + 1) * MXU_TILE] + bias[:, dst]

    def local_scan(streams):
        def run(after=0.0):
            neg_lam = after - lam_ref[...]
            softplus = jnp.maximum(neg_lam, 0.0) + jnp.log1p(jnp.exp(-jnp.abs(neg_lam)))
            p_run, h_run = state.get("run", (None, None))
            for r in streams:
                g = g_scr[r * TL:(r + 1) * TL]
                rg = jax.nn.sigmoid(g[:, :LRU_WIDTH])
                ig = jax.nn.sigmoid(g[:, LRU_WIDTH:])
                log_a = -LRU_C * rg * softplus
                a = jnp.exp(log_a)
                th = jnp.tanh(log_a)
                bx = jnp.sqrt(-2.0 * th / (1.0 - th)) * (ig * xc_scr[r])
                if p_run is None:
                    p_run, h_run = a, bx
                else:
                    h_run = a * h_run + bx
                    p_run = a * p_run
                pl_scr[r] = p_run
                hl_scr[r] = h_run
            state["run"] = (p_run, h_run)
        return run

    def row_scan():
        pa, hb = state["run"]
        sft = 1
        while sft < TL:
            keep = row >= sft
            pa_s = jnp.where(keep, pltpu.roll(pa, sft, axis=0), 1.0)
            hb_s = jnp.where(keep, pltpu.roll(hb, sft, axis=0), 0.0)
            hb = pa * hb_s + hb
            pa = pa * pa_s
            sft *= 2
        carry = carry_scr[0:1, :]
        e = pa * carry + hb
        state["e_prev"] = jnp.where(row == 0, carry, pltpu.roll(e, 1, axis=0))
        carry_scr[...] = jnp.broadcast_to(e[TL - 1:TL, :], carry_scr.shape)

    def finalize(after=0.0):
        e_prev = state["e_prev"] + after
        for r in range(NS):
            h = pl_scr[r] * e_prev + hl_scr[r]
            y_ref[r] = jax.nn.gelu(gl_ref[r], approximate=True) * h

    half = NS // 2

    def second_half(after=0.0):
        local_scan(range(half, NS))(after)
        row_scan()

    return [(conv, LRU_WIDTH), (gates, 2 * LRU_WIDTH), (local_scan(range(half)), LRU_WIDTH),
            (second_half, LRU_WIDTH), (finalize, LRU_WIDTH)]


TM4 = NS * TL
NG4 = TM4 // PG
FF_CHUNK = 1024


def _out_body(x_ref, at_ref, xl0_ref, gl0_ref, xln_ref, gln_ref, zero_ref,
              cw_ref, cb_ref, wg_ref, bg_ref, lam_ref,
              perm_ref, ga_ref, gl_ref, wo_ref, g2_ref, wu_ref, wd_ref, gf_ref,
              o_ref,
              lr_scr, tail_scr, carry_scr, xc_scr, g_scr, pl_scr, hl_scr):
    lru_params = (cw_ref, cb_ref, wg_ref, bg_ref, lam_ref)
    lru_state = (tail_scr, carry_scr, xc_scr, g_scr, pl_scr, hl_scr)

    @pl.when(pl.program_id(0) == 0)
    def _():
        tail_scr[...] = jnp.zeros_like(tail_scr)
        carry_scr[...] = jnp.zeros_like(carry_scr)
        for phase, _ in _lru_phases(xl0_ref, gl0_ref, *lru_params, lr_scr, *lru_state):
            phase()

    groups = []
    for g in range(NG4):
        rows = slice(g * RG, (g + 1) * RG)
        attn = jnp.concatenate(
            [jnp.concatenate([at_ref[j, r, rows] for j in range(NSLAB)], axis=1) for r in range(NS)],
            axis=0)
        lru = jnp.concatenate([lr_scr[r, rows] for r in range(NS)], axis=0)
        mixed = jnp.concatenate([_rms(attn, ga_ref[...]), _rms(lru, gl_ref[...])], axis=1)
        groups.append(jnp.dot(perm_ref[...], mixed.astype(BF16),
                              preferred_element_type=F32).astype(BF16))
    mixed = jnp.concatenate(groups, axis=0)
    lru_next = _lru_phases(xln_ref, gln_ref, *lru_params, lr_scr, *lru_state)
    n_chunks = D_FF // FF_CHUNK
    assert len(lru_next) <= n_chunks + 1

    def zero_after(val, width):
        bits = pltpu.bitcast(val[0:1, :width], jnp.int32) & zero_ref[0:1, :width]
        return pltpu.bitcast(bits, F32)

    h = x_ref[...] + jnp.dot(mixed, wo_ref[...], preferred_element_type=F32)
    u = _rms(h, g2_ref[...]).astype(BF16)
    acc = h
    phase, width = lru_next.pop(0)
    phase(zero_after(h, width))
    for c in range(n_chunks):
        if lru_next:
            phase, width = lru_next.pop(0)
            phase(zero_after(acc, width))
        sl = slice(c * FF_CHUNK, (c + 1) * FF_CHUNK)
        f = jnp.dot(u, wu_ref[:, sl], preferred_element_type=F32)
        f = jnp.square(jnp.maximum(f, 0.0)).astype(BF16)
        acc = acc + jnp.dot(f, wd_ref[sl, :], preferred_element_type=F32)
    o_ref[...] = _rms(acc, gf_ref[...])


def _out_mlp(x2, attn, xl, gl_in, lru_params, ga, gl, wo, g2, wu, wd, gf):
    def const(a):
        return pl.BlockSpec(a.shape, lambda i: (0,) * a.ndim, pipeline_mode=pl.Buffered(1))

    n_tiles = SEQ // TM4
    xspec = pl.BlockSpec((TM4, D_MODEL), lambda i: (i, 0))
    first = pl.BlockSpec((NS, TL, LRU_WIDTH), lambda i: (0, 0, 0), pipeline_mode=pl.Buffered(1))
    ahead = pl.BlockSpec((NS, TL, LRU_WIDTH), lambda i: (0, jnp.minimum(i + 1, n_tiles - 1), 0))
    perm = jnp.asarray(_group_perm(), BF16)
    zero = jnp.zeros((SUBLANES, D_MODEL), jnp.int32)
    consts = (zero,) + tuple(lru_params) + (perm, ga, gl, wo, g2, wu, wd, gf)
    return pl.pallas_call(
        _out_body,
        grid=(n_tiles,),
        in_specs=[
            xspec,
            pl.BlockSpec((NSLAB, NS, TL, LANES), lambda i: (0, 0, i, 0)),
            first, first, ahead, ahead,
        ] + [const(a) for a in consts],
        out_specs=xspec,
        out_shape=jax.ShapeDtypeStruct((SEQ, D_MODEL), F32),
        scratch_shapes=[
            pltpu.VMEM((NS, TL, LRU_WIDTH), F32),
            pltpu.VMEM((CONV_WIDTH - 1, SUBLANES, LRU_WIDTH), F32),
            pltpu.VMEM((SUBLANES, LRU_WIDTH), F32),
            pltpu.VMEM((NS, TL, LRU_WIDTH), F32),
            pltpu.VMEM((NS * TL, 2 * LRU_WIDTH), F32),
            pltpu.VMEM((NS, TL, LRU_WIDTH), F32),
            pltpu.VMEM((NS, TL, LRU_WIDTH), F32),
        ],
        compiler_params=pltpu.CompilerParams(
            dimension_semantics=("arbitrary",), vmem_limit_bytes=VMEM_LIMIT),
        name="out_mlp",
    )(x2, attn, xl, gl_in, xl, gl_in, *consts)


@functools.lru_cache(maxsize=None)
def _rope_tables():
    inv_freq = ROPE_THETA ** (-np.arange(0, ROT_DIM, 2, dtype=np.float64) / ROT_DIM)
    pad = HEAD_DIM - ROT_DIM
    reps = LANES // HEAD_DIM

    def lanes(rot_first, rot_second, rest, n):
        head = np.concatenate([rot_first, rot_second, np.full((n, pad), rest)], axis=1)
        return np.tile(head, (1, reps))

    tile_ang = (TM1 * np.arange(SEQ // TM1, dtype=np.float64))[:, None] * inv_freq[None, :]
    tc, ts = np.cos(tile_ang), np.sin(tile_ang)
    tiles = np.concatenate([lanes(tc, tc, 1.0, len(tc)), lanes(ts, ts, 0.0, len(ts))], axis=1)
    off_ang = np.arange(TM1, dtype=np.float64)[:, None] * inv_freq[None, :]
    oc, osn = np.cos(off_ang), np.sin(off_ang)
    offs = np.stack([lanes(oc, oc, 1.0, TM1), lanes(osn, osn, 0.0, TM1),
                     lanes(-oc, oc, 0.0, TM1), lanes(-osn, osn, 0.0, TM1)])
    offs = offs.reshape(4, NG1, RG, NS, LANES).transpose(0, 1, 3, 2, 4).reshape(4, TM1, LANES)
    return (tiles.astype(np.float32)[:, None, :], np.ascontiguousarray(offs.astype(np.float32)))


def _block_diag(w):
    n, b, _ = w.shape
    eye = jnp.eye(n, dtype=w.dtype)
    return (eye[:, None, :, None] * w[:, :, None, :]).reshape(n * b, n * b)


def _gate_tiles(w_r, w_i):
    per = MXU_TILE // (LRU_WIDTH // LRU_BLOCKS)
    tiles = [jnp.concatenate([_block_diag(w_r[t * per:(t + 1) * per]),
                              _block_diag(w_i[t * per:(t + 1) * per])], axis=1)
             for t in range(LRU_WIDTH // MXU_TILE)]
    return jnp.stack(tiles).astype(BF16)


def kernel(x, norm1_g, w_in, conv_w, conv_b, w_rgate, b_rgate, w_igate, b_igate, lru_lambda,
           attn_out_g, lru_out_g, w_out, norm2_g, w_mlp_up, w_mlp_down, final_g):
    assert x.shape == (1, SEQ, D_MODEL) and w_in.shape[0] == 1
    x2 = x.reshape(SEQ, D_MODEL)
    q, k, v, q1, k1, v1, xl, gl = _inproj(x2, norm1_g.reshape(1, D_MODEL), w_in[0])
    attn, wo, wu, wd = _attention(q, k, v, q1, k1, v1, (w_out[0], w_mlp_up[0], w_mlp_down[0]))
    w_gate = _gate_tiles(w_rgate[0], w_igate[0])
    b_gate = jnp.concatenate([b_rgate[0].reshape(1, -1), b_igate[0].reshape(1, -1)], axis=1)
    lru_params = (conv_w[0], conv_b[0].reshape(1, -1), w_gate, b_gate, lru_lambda[0].reshape(1, -1))
    out = _out_mlp(x2, attn, xl, gl, lru_params,
                   attn_out_g[0].reshape(1, -1), lru_out_g[0].reshape(1, -1),
                   wo, norm2_g[0].reshape(1, -1), wu, wd, final_g.reshape(1, -1))
    return out.reshape(1, SEQ, D_MODEL)
```

```python
import functools

import numpy as np
import jax
import jax.numpy as jnp
from jax import lax
from jax.experimental import pallas as pl
from jax.experimental.pallas import tpu as pltpu

F32 = jnp.float32
BF16 = jnp.bfloat16

D_MODEL = 1024
SEQ = 16384
ATTN_HEADS = 8
HEAD_DIM = 64
ATTN_WIDTH = ATTN_HEADS * HEAD_DIM
ROT_DIM = HEAD_DIM // 4
ROPE_THETA = 500000.0
LRU_WIDTH = D_MODEL - ATTN_WIDTH
LRU_BLOCKS = 8
CONV_WIDTH = 4
LRU_C = 8.0
IN_WIDTH = 3 * ATTN_WIDTH + 2 * LRU_WIDTH
D_FF = 4 * D_MODEL
EPS = 1e-6
WINDOW_STEPS = 128

NS = 16
RA = SEQ // NS
LANES = 128
SUBLANES = 8
MXU_TILE = 256
NSLAB = ATTN_WIDTH // LANES
NEG = -np.inf

VMEM_LIMIT = 56 * 1024 * 1024


def _rms(x, g):
    return x * lax.rsqrt(jnp.mean(x * x, axis=-1, keepdims=True) + EPS) * g


TM1 = 1024
PG = NS * NS
RG = PG // NS
NG1 = TM1 // PG
CH = SUBLANES
LOG2E = 1.4426950408889634


@functools.lru_cache(maxsize=None)
def _group_perm():
    n = np.arange(PG)
    p = np.zeros((PG, PG), np.float32)
    p[(n % NS) * RG + n // NS, n] = 1.0
    return p


def _inproj_body(x_ref, g_ref, w_ref, perm_ref, tile_ref, off_ref,
                 q_ref, k_ref, v_ref, q1_ref, k1_ref, v1_ref, xl_ref, gl_ref, w_scr):
    @pl.when(pl.program_id(0) == 0)
    def _():
        w_scr[...] = w_ref[...].astype(BF16)

    u = _rms(x_ref[...], g_ref[...]).astype(BF16)
    u = jnp.concatenate(
        [jnp.dot(perm_ref[...], u[g * PG:(g + 1) * PG], preferred_element_type=F32).astype(BF16)
         for g in range(NG1)], axis=0)
    z = jnp.dot(u, w_scr[...], preferred_element_type=F32)
    tile_cos, tile_sin = tile_ref[:, :LANES], tile_ref[:, LANES:]
    c = tile_cos * off_ref[0] - tile_sin * off_ref[1]
    s = tile_sin * off_ref[2] + tile_cos * off_ref[3]
    c = jnp.concatenate([c] * NSLAB, axis=1)
    s = jnp.concatenate([s] * NSLAB, axis=1)
    lane = lax.broadcasted_iota(jnp.int32, (1, ATTN_WIDTH), 1) % HEAD_DIM
    first_half = lane < ROT_DIM // 2

    def rope(t):
        up = pltpu.roll(t, ATTN_WIDTH - ROT_DIM // 2, axis=1)
        dn = pltpu.roll(t, ROT_DIM // 2, axis=1)
        return t * c + jnp.where(first_half, up, dn) * s

    q = rope(z[:, :ATTN_WIDTH]) * (HEAD_DIM ** -0.5 * LOG2E)
    k = rope(z[:, ATTN_WIDTH:2 * ATTN_WIDTH])
    v = z[:, 2 * ATTN_WIDTH:3 * ATTN_WIDTH]
    xl = z[:, 3 * ATTN_WIDTH:3 * ATTN_WIDTH + LRU_WIDTH]
    gl = z[:, 3 * ATTN_WIDTH + LRU_WIDTH:]
    for g in range(NG1):
        for r in range(NS):
            src = slice(g * PG + r * RG, g * PG + (r + 1) * RG)
            dst = slice(g * RG, (g + 1) * RG)
            for j in range(NSLAB):
                sl = slice(j * LANES, (j + 1) * LANES)
                q_ref[j, r, dst] = q[src, sl].astype(BF16)
                k_ref[j, r, dst] = k[src, sl].astype(BF16)
                v_ref[j, r, dst] = v[src, sl].astype(BF16)
            xl_ref[r, dst] = xl[src]
            gl_ref[r, dst] = gl[src]
    for g in range(NG1):
        for cl in range(RG // CH):
            for r in range(0, NS, 2):
                lo = g * PG + r * RG + cl * CH
                hi = lo + RG
                d0 = (g * (RG // CH) + cl) * NS * CH + r * CH
                dst = slice(d0, d0 + 2 * CH)
                for j in range(NSLAB):
                    sl = slice(j * LANES, (j + 1) * LANES)
                    for src, ref in ((q, q1_ref), (k, k1_ref), (v, v1_ref)):
                        ref[j, dst] = jnp.concatenate(
                            [src[lo:lo + CH, sl], src[hi:hi + CH, sl]], axis=0).astype(BF16)


def _inproj(x2, g, w):
    steps = SEQ // TM1
    rows = TM1 // NS
    slab_spec = pl.BlockSpec((NSLAB, NS, rows, LANES), lambda i: (0, 0, i, 0))
    chunk_spec = pl.BlockSpec((NSLAB, TM1, LANES), lambda i: (0, i, 0))
    row_spec = pl.BlockSpec((NS, rows, LRU_WIDTH), lambda i: (0, i, 0))
    rope_tiles, rope_offs = _rope_tables()
    slab_shape = jax.ShapeDtypeStruct((NSLAB, NS, RA, LANES), BF16)
    chunk_shape = jax.ShapeDtypeStruct((NSLAB, SEQ, LANES), BF16)
    row_shape = jax.ShapeDtypeStruct((NS, RA, LRU_WIDTH), F32)
    perm = jnp.asarray(_group_perm(), BF16)
    return pl.pallas_call(
        _inproj_body,
        grid=(steps,),
        in_specs=[
            pl.BlockSpec((TM1, D_MODEL), lambda i: (i, 0)),
            pl.BlockSpec((1, D_MODEL), lambda i: (0, 0)),
            pl.BlockSpec((D_MODEL, IN_WIDTH), lambda i: (0, 0), pipeline_mode=pl.Buffered(1)),
            pl.BlockSpec((PG, PG), lambda i: (0, 0)),
            pl.BlockSpec((None, 1, 2 * LANES), lambda i: (i, 0, 0)),
            pl.BlockSpec(rope_offs.shape, lambda i: (0, 0, 0)),
        ],
        out_specs=[slab_spec] * 3 + [chunk_spec] * 3 + [row_spec] * 2,
        out_shape=[slab_shape] * 3 + [chunk_shape] * 3 + [row_shape] * 2,
        scratch_shapes=[pltpu.VMEM((D_MODEL, IN_WIDTH), BF16)],
        compiler_params=pltpu.CompilerParams(
            dimension_semantics=("arbitrary",), vmem_limit_bytes=VMEM_LIMIT),
        name="inproj",
    )(x2, g, w, perm, rope_tiles, rope_offs)


TA = WINDOW_STEPS
UQ, UK = WINDOW_STEPS, 2 * WINDOW_STEPS
Q4, K4 = UQ // 4, UK // 4
TPS = 2
AHEAD = 4
RING = 2 * AHEAD


@functools.lru_cache(maxsize=None)
def _attn_masks():
    def variants(diff, from_prev, per_unit):
        ok = (diff >= 0) & (diff <= WINDOW_STEPS)
        normal = np.where(ok, 0.0, NEG).astype(np.float32)
        first = np.where(ok & ~from_prev, 0.0, NEG).astype(np.float32)
        if per_unit:
            return np.stack([np.stack([first, normal]), np.stack([normal, normal])])
        return np.stack([first, normal])

    iq = np.arange(UQ)[:, None]
    ck = np.arange(UK)[None, :]
    m16 = variants(iq - ck + TA, np.broadcast_to(ck < TA, (UQ, UK)), False)
    cq, i4 = np.divmod(np.arange(UQ), Q4)
    ckk, j4 = np.divmod(np.arange(UK), K4)
    d4 = 4 * (i4[:, None] - j4[None, :] + Q4) + (cq[:, None] - ckk[None, :])
    m4 = variants(d4, np.broadcast_to(j4[None, :] < K4 - Q4, d4.shape), True)
    rq, i1 = np.divmod(np.arange(UQ), CH)
    kc, kin = np.divmod(np.arange(UK), UQ)
    rk, j1 = np.divmod(kin, CH)
    d1 = (NS * i1 + rq)[:, None] - (NS * j1 + rk + UQ * (kc - 1))[None, :]
    m1 = variants(d1, np.broadcast_to(kc[None, :] == 0, d1.shape), True)
    return m16, m4, m1


def _attn_body(q_ref, kc_ref, kp_ref, vc_ref, vp_ref, q1_ref, k1c_ref, k1p_ref, v1c_ref, v1p_ref,
               m16_ref, m4_ref, m1_ref, zero_ref, wo_ref, wu_ref, wd_ref,
               o_ref, wo_bf_ref, wu_bf_ref, wd_bf_ref,
               qm_scr, q1m_scr, k1_scr, v1_scr, s_scr, acc_scr, max_scr, sum_scr):
    kind = jnp.minimum(pl.program_id(1), 1)

    wo_bf_ref[...] = wo_ref[...].astype(BF16)
    wu_bf_ref[...] = wu_ref[...].astype(BF16)
    wd_bf_ref[...] = wd_ref[...].astype(BF16)

    lane = lax.broadcasted_iota(jnp.int32, (1, LANES), 1)
    head0 = lane < HEAD_DIM
    zero = jnp.zeros((), BF16)

    def split(x):
        return jnp.where(head0, x, zero), jnp.where(head0, zero, x)

    qm_scr[0], qm_scr[1] = split(q_ref[...])
    q1m_scr[0], q1m_scr[1] = split(q1_ref[...])
    k1_scr[0:UQ] = k1p_ref[...]
    k1_scr[UQ:] = k1c_ref[...]
    v1_scr[0:UQ] = v1p_ref[...]
    v1_scr[UQ:] = v1c_ref[...]
    ones = jnp.ones((UK, LANES), BF16)

    def window(cur, prev, st, start, n):
        if start < 0:
            return [prev[st, TA + start:TA], cur[st, 0:start + n]]
        return [cur[st, start:start + n]]

    def unit16(t, r):
        def store(o, mx, sm):
            acc_scr[t, 0, r], max_scr[t, 0, r], sum_scr[t, 0, r] = o, mx, sm

        return (lambda h: qm_scr[h, r, t * TA:(t + 1) * TA],
                lambda: jnp.concatenate(window(kc_ref, kp_ref, r, (t - 1) * TA, UK), axis=0),
                lambda: jnp.concatenate(window(vc_ref, vp_ref, r, (t - 1) * TA, UK), axis=0),
                lambda: m16_ref[kind] if t == 0 else m16_ref[1], store)

    def unit4(t, r4, b):
        streams = [r4 + 4 * c for c in range(4)]
        q0 = b * Q4
        g0, k0 = t * TA + q0, t * TA + q0 + Q4 - K4

        def store(o, mx, sm):
            for c, st in enumerate(streams):
                rows = slice(c * Q4, (c + 1) * Q4)
                acc_scr[t, 1, st, q0:q0 + Q4] = o[rows]
                max_scr[t, 1, st, q0:q0 + Q4] = mx[rows]
                sum_scr[t, 1, st, q0:q0 + Q4] = sm[rows]

        return (lambda h: jnp.concatenate([qm_scr[h, st, g0:g0 + Q4] for st in streams], axis=0),
                lambda: jnp.concatenate(
                    [x for st in streams for x in window(kc_ref, kp_ref, st, k0, K4)], axis=0),
                lambda: jnp.concatenate(
                    [x for st in streams for x in window(vc_ref, vp_ref, st, k0, K4)], axis=0),
                lambda: m4_ref[kind, min(b, 1)] if t == 0 else m4_ref[1, 1], store)

    def unit1(t, u):
        off, row = (t * NS + u) * UQ, u * CH

        def store(o, mx, sm):
            for r in range(NS):
                rows = slice(r * CH, (r + 1) * CH)
                acc_scr[t, 2, r, row:row + CH] = o[rows]
                max_scr[t, 2, r, row:row + CH] = mx[rows]
                sum_scr[t, 2, r, row:row + CH] = sm[rows]

        return (lambda h: q1m_scr[h, off:off + UQ],
                lambda: k1_scr[off:off + UK],
                lambda: v1_scr[off:off + UK],
                lambda: m1_ref[kind, min(u, 1)] if t == 0 else m1_ref[1, 1], store)

    per_tile = 3 * NS
    all_units = []
    for t in range(TPS):
        all_units += ([unit16(t, n) for n in range(NS)]
                      + [unit4(t, n // (TA // Q4), n % (TA // Q4)) for n in range(NS)]
                      + [unit1(t, n) for n in range(NS)])

    def score_unit(n):
        q, k, _, _, _ = all_units[n]
        s_scr[n % RING] = lax.dot_general(
            jnp.concatenate([q(0), q(1)], axis=0), k(), (((1,), (1,)), ((), ())),
            preferred_element_type=F32)

    def value_unit(n):
        _, _, v, mask, store = all_units[n]
        ps, mx = [], []
        for h in range(2):
            s = s_scr[n % RING, h * UQ:(h + 1) * UQ] + mask()
            m = jnp.max(s, axis=-1, keepdims=True)
            ps.append(jnp.exp2(s - m).astype(BF16))
            mx.append(m)
        res = jnp.dot(jnp.concatenate(ps, axis=0), jnp.concatenate([v(), ones], axis=1),
                      preferred_element_type=F32)
        top, bot = res[:UQ], res[UQ:]
        store(jnp.where(head0, top[:, :LANES], bot[:, :LANES]),
              jnp.where(head0, mx[0], mx[1]),
              jnp.where(head0, top[:, LANES:], bot[:, LANES:]))
        bits = pltpu.bitcast(top[0:1, :LANES], jnp.int32) & zero_ref[0:1, :]
        return pltpu.bitcast(bits, F32)

    def merge(t, r, after=0.0):
        m0, m1, m2 = max_scr[t, 0, r], max_scr[t, 1, r], max_scr[t, 2, r]
        m = jnp.maximum(jnp.maximum(m0, m1), m2) + after
        w0, w1, w2 = jnp.exp2(m0 - m), jnp.exp2(m1 - m), jnp.exp2(m2 - m)
        num = w0 * acc_scr[t, 0, r] + w1 * acc_scr[t, 1, r] + w2 * acc_scr[t, 2, r]
        den = w0 * sum_scr[t, 0, r] + w1 * sum_scr[t, 1, r] + w2 * sum_scr[t, 2, r]
        o_ref[r, t * TA:(t + 1) * TA] = num / den

    for n in range(AHEAD):
        score_unit(n)
    for n in range(len(all_units)):
        if n + AHEAD < len(all_units):
            score_unit(n + AHEAD)
        after = value_unit(n)
        t, local = divmod(n, per_tile)
        if t > 0 and local % (per_tile // NS) == per_tile // NS - 1:
            merge(t - 1, local // (per_tile // NS), after)

    def last_merge(r, carry):
        merge(TPS - 1, r)
        return carry

    lax.fori_loop(0, NS, last_merge, 0, unroll=4)


def _attention(q, k, v, q1, k1, v1, later_weights):
    m16, m4, m1 = _attn_masks()
    rows = TPS * TA
    cur = pl.BlockSpec((None, NS, rows, LANES), lambda j, i: (j, 0, i, 0))
    prev = pl.BlockSpec((None, NS, TA, LANES), lambda j, i: (j, 0, jnp.maximum(TPS * i - 1, 0), 0))
    cur1 = pl.BlockSpec((None, NS * rows, LANES), lambda j, i: (j, i, 0))
    prev1 = pl.BlockSpec((None, UQ, LANES),
                         lambda j, i: (j, jnp.maximum(i * (NS * rows // UQ) - 1, 0), 0))

    def whole(a):
        return pl.BlockSpec(a.shape, lambda j, i: (0,) * a.ndim)

    zero = jnp.zeros((SUBLANES, LANES), jnp.int32)
    n_steps = RA // rows
    steps = NSLAB * n_steps
    cast_specs = [pl.BlockSpec((a.shape[0] // steps, a.shape[1]), lambda j, i: (j * n_steps + i, 0))
                  for a in later_weights]
    cast_shapes = [jax.ShapeDtypeStruct(a.shape, BF16) for a in later_weights]
    return pl.pallas_call(
        _attn_body,
        grid=(NSLAB, n_steps),
        in_specs=[cur, cur, prev, cur, prev, cur1, cur1, prev1, cur1, prev1,
                  whole(m16), whole(m4), whole(m1), whole(zero)] + cast_specs,
        out_specs=[pl.BlockSpec((None, NS, rows, LANES), lambda j, i: (j, 0, i, 0))] + cast_specs,
        out_shape=[jax.ShapeDtypeStruct((NSLAB, NS, RA, LANES), F32)] + cast_shapes,
        scratch_shapes=[
            pltpu.VMEM((2, NS, rows, LANES), BF16),
            pltpu.VMEM((2, NS * rows, LANES), BF16),
            pltpu.VMEM((UQ + NS * rows, LANES), BF16),
            pltpu.VMEM((UQ + NS * rows, LANES), BF16),
            pltpu.VMEM((RING, 2 * UQ, UK), F32),
            pltpu.VMEM((TPS, 3, NS, TA, LANES), F32),
            pltpu.VMEM((TPS, 3, NS, TA, LANES), F32),
            pltpu.VMEM((TPS, 3, NS, TA, LANES), F32),
        ],
        compiler_params=pltpu.CompilerParams(
            dimension_semantics=("arbitrary", "arbitrary"), vmem_limit_bytes=VMEM_LIMIT),
        name="dilated_attn",
    )(q, k, k, v, v, q1, k1, k1, v1, v1, m16, m4, m1, zero, *later_weights)


TL = 32


def _lru_phases(xl_ref, gl_ref, cw_ref, cb_ref, wg_ref, bg_ref, lam_ref, y_ref,
                tail_scr, carry_scr, xc_scr, g_scr, pl_scr, hl_scr):
    row = lax.broadcasted_iota(jnp.int32, (TL, LRU_WIDTH), 0)
    state = {}

    def conv(after=0.0):
        cw = cw_ref[...]
        cb = cb_ref[...] + after

        def tap(r, k):
            st = r - k
            if st >= 0:
                return xl_ref[st]
            st += NS
            prev_last = tail_scr[st - (NS - CONV_WIDTH + 1), SUBLANES - 1:SUBLANES, :]
            return jnp.where(row == 0, prev_last, pltpu.roll(xl_ref[st], 1, axis=0))

        for r in range(NS):
            xc_scr[r] = cb + sum(cw[CONV_WIDTH - 1 - k:CONV_WIDTH - k] * tap(r, k)
                                 for k in range(CONV_WIDTH))
        for n in range(CONV_WIDTH - 1):
            tail_scr[n] = xl_ref[NS - CONV_WIDTH + 1 + n, TL - SUBLANES:TL, :]

    def gates(after=0.0):
        xc_all = xc_scr[...].reshape(NS * TL, LRU_WIDTH).astype(BF16)
        bias = bg_ref[...] + after
        for t in range(LRU_WIDTH // MXU_TILE):
            cols = slice(t * MXU_TILE, (t + 1) * MXU_TILE)
            res = jnp.dot(xc_all[:, cols], wg_ref[t], preferred_element_type=F32)
            for gate in range(2):
                dst = slice(gate * LRU_WIDTH + t * MXU_TILE, gate * LRU_WIDTH + (t + 1) * MXU_TILE)
                g_scr[:, dst] = res[:, gate * MXU_TILE:(gate + 1) * MXU_TILE] + bias[:, dst]

    def local_scan(streams):
        def run(after=0.0):
            neg_lam = after - lam_ref[...]
            softplus = jnp.maximum(neg_lam, 0.0) + jnp.log1p(jnp.exp(-jnp.abs(neg_lam)))
            p_run, h_run = state.get("run", (None, None))
            for r in streams:
                g = g_scr[r * TL:(r + 1) * TL]
                rg = jax.nn.sigmoid(g[:, :LRU_WIDTH])
                ig = jax.nn.sigmoid(g[:, LRU_WIDTH:])
                log_a = -LRU_C * rg * softplus
                a = jnp.exp(log_a)
                th = jnp.tanh(log_a)
                bx = jnp.sqrt(-2.0 * th / (1.0 - th)) * (ig * xc_scr[r])
                if p_run is None:
                    p_run, h_run = a, bx
                else:
                    h_run = a * h_run + bx
                    p_run = a * p_run
                pl_scr[r] = p_run
                hl_scr[r] = h_run
            state["run"] = (p_run, h_run)
        return run

    def row_scan():
        pa, hb = state["run"]
        sft = 1
        while sft < TL:
            keep = row >= sft
            pa_s = jnp.where(keep, pltpu.roll(pa, sft, axis=0), 1.0)
            hb_s = jnp.where(keep, pltpu.roll(hb, sft, axis=0), 0.0)
            hb = pa * hb_s + hb
            pa = pa * pa_s
            sft *= 2
        carry = carry_scr[0:1, :]
        e = pa * carry + hb
        state["e_prev"] = jnp.where(row == 0, carry, pltpu.roll(e, 1, axis=0))
        carry_scr[...] = jnp.broadcast_to(e[TL - 1:TL, :], carry_scr.shape)

    def finalize(after=0.0):
        e_prev = state["e_prev"] + after
        for r in range(NS):
            h = pl_scr[r] * e_prev + hl_scr[r]
            y_ref[r] = jax.nn.gelu(gl_ref[r], approximate=True) * h

    half = NS // 2

    def second_half(after=0.0):
        local_scan(range(half, NS))(after)
        row_scan()

    return [(conv, LRU_WIDTH), (gates, 2 * LRU_WIDTH), (local_scan(range(half)), LRU_WIDTH),
            (second_half, LRU_WIDTH), (finalize, LRU_WIDTH)]


TM4 = NS * TL
NG4 = TM4 // PG
FF_CHUNK = 1024


def _out_body(x_ref, at_ref, xl0_ref, gl0_ref, xln_ref, gln_ref, zero_ref,
              cw_ref, cb_ref, wg_ref, bg_ref, lam_ref,
              perm_ref, ga_ref, gl_ref, wo_ref, g2_ref, wu_ref, wd_ref, gf_ref,
              o_ref,
              lr_scr, tail_scr, carry_scr, xc_scr, g_scr, pl_scr, hl_scr):
    lru_params = (cw_ref, cb_ref, wg_ref, bg_ref, lam_ref)
    lru_state = (tail_scr, carry_scr, xc_scr, g_scr, pl_scr, hl_scr)

    @pl.when(pl.program_id(0) == 0)
    def _():
        tail_scr[...] = jnp.zeros_like(tail_scr)
        carry_scr[...] = jnp.zeros_like(carry_scr)
        for phase, _ in _lru_phases(xl0_ref, gl0_ref, *lru_params, lr_scr, *lru_state):
            phase()

    groups = []
    for g in range(NG4):
        rows = slice(g * RG, (g + 1) * RG)
        attn = jnp.concatenate(
            [jnp.concatenate([at_ref[j, r, rows] for j in range(NSLAB)], axis=1) for r in range(NS)],
            axis=0)
        lru = jnp.concatenate([lr_scr[r, rows] for r in range(NS)], axis=0)
        mixed = jnp.concatenate([_rms(attn, ga_ref[...]), _rms(lru, gl_ref[...])], axis=1)
        groups.append(jnp.dot(perm_ref[...], mixed.astype(BF16),
                              preferred_element_type=F32).astype(BF16))
    mixed = jnp.concatenate(groups, axis=0)
    lru_next = _lru_phases(xln_ref, gln_ref, *lru_params, lr_scr, *lru_state)
    n_chunks = D_FF // FF_CHUNK
    assert len(lru_next) <= n_chunks + 1

    def zero_after(val, width):
        bits = pltpu.bitcast(val[0:1, :width], jnp.int32) & zero_ref[0:1, :width]
        return pltpu.bitcast(bits, F32)

    h = x_ref[...] + jnp.dot(mixed, wo_ref[...], preferred_element_type=F32)
    u = _rms(h, g2_ref[...]).astype(BF16)
    acc = h
    phase, width = lru_next.pop(0)
    phase(zero_after(h, width))
    for c in range(n_chunks):
        if lru_next:
            phase, width = lru_next.pop(0)
            phase(zero_after(acc, width))
        sl = slice(c * FF_CHUNK, (c + 1) * FF_CHUNK)
        f = jnp.dot(u, wu_ref[:, sl], preferred_element_type=F32)
        f = jnp.square(jnp.maximum(f, 0.0)).astype(BF16)
        acc = acc + jnp.dot(f, wd_ref[sl, :], preferred_element_type=F32)
    o_ref[...] = _rms(acc, gf_ref[...])


def _out_mlp(x2, attn, xl, gl_in, lru_params, ga, gl, wo, g2, wu, wd, gf):
    def const(a):
        return pl.BlockSpec(a.shape, lambda i: (0,) * a.ndim, pipeline_mode=pl.Buffered(1))

    n_tiles = SEQ // TM4
    xspec = pl.BlockSpec((TM4, D_MODEL), lambda i: (i, 0))
    first = pl.BlockSpec((NS, TL, LRU_WIDTH), lambda i: (0, 0, 0), pipeline_mode=pl.Buffered(1))
    ahead = pl.BlockSpec((NS, TL, LRU_WIDTH), lambda i: (0, jnp.minimum(i + 1, n_tiles - 1), 0))
    perm = jnp.asarray(_group_perm(), BF16)
    zero = jnp.zeros((SUBLANES, D_MODEL), jnp.int32)
    consts = (zero,) + tuple(lru_params) + (perm, ga, gl, wo, g2, wu, wd, gf)
    return pl.pallas_call(
        _out_body,
        grid=(n_tiles,),
        in_specs=[
            xspec,
            pl.BlockSpec((NSLAB, NS, TL, LANES), lambda i: (0, 0, i, 0)),
            first, first, ahead, ahead,
        ] + [const(a) for a in consts],
        out_specs=xspec,
        out_shape=jax.ShapeDtypeStruct((SEQ, D_MODEL), F32),
        scratch_shapes=[
            pltpu.VMEM((NS, TL, LRU_WIDTH), F32),
            pltpu.VMEM((CONV_WIDTH - 1, SUBLANES, LRU_WIDTH), F32),
            pltpu.VMEM((SUBLANES, LRU_WIDTH), F32),
            pltpu.VMEM((NS, TL, LRU_WIDTH), F32),
            pltpu.VMEM((NS * TL, 2 * LRU_WIDTH), F32),
            pltpu.VMEM((NS, TL, LRU_WIDTH), F32),
            pltpu.VMEM((NS, TL, LRU_WIDTH), F32),
        ],
        compiler_params=pltpu.CompilerParams(
            dimension_semantics=("arbitrary",), vmem_limit_bytes=VMEM_LIMIT),
        name="out_mlp",
    )(x2, attn, xl, gl_in, xl, gl_in, *consts)


@functools.lru_cache(maxsize=None)
def _rope_tables():
    inv_freq = ROPE_THETA ** (-np.arange(0, ROT_DIM, 2, dtype=np.float64) / ROT_DIM)
    pad = HEAD_DIM - ROT_DIM
    reps = LANES // HEAD_DIM

    def lanes(rot_first, rot_second, rest, n):
        head = np.concatenate([rot_first, rot_second, np.full((n, pad), rest)], axis=1)
        return np.tile(head, (1, reps))

    tile_ang = (TM1 * np.arange(SEQ // TM1, dtype=np.float64))[:, None] * inv_freq[None, :]
    tc, ts = np.cos(tile_ang), np.sin(tile_ang)
    tiles = np.concatenate([lanes(tc, tc, 1.0, len(tc)), lanes(ts, ts, 0.0, len(ts))], axis=1)
    off_ang = np.arange(TM1, dtype=np.float64)[:, None] * inv_freq[None, :]
    oc, osn = np.cos(off_ang), np.sin(off_ang)
    offs = np.stack([lanes(oc, oc, 1.0, TM1), lanes(osn, osn, 0.0, TM1),
                     lanes(-oc, oc, 0.0, TM1), lanes(-osn, osn, 0.0, TM1)])
    offs = offs.reshape(4, NG1, RG, NS, LANES).transpose(0, 1, 3, 2, 4).reshape(4, TM1, LANES)
    return (tiles.astype(np.float32)[:, None, :], np.ascontiguousarray(offs.astype(np.float32)))


def _block_diag(w):
    n, b, _ = w.shape
    eye = jnp.eye(n, dtype=w.dtype)
    return (eye[:, None, :, None] * w[:, :, None, :]).reshape(n * b, n * b)


def _gate_tiles(w_r, w_i):
    per = MXU_TILE // (LRU_WIDTH // LRU_BLOCKS)
    tiles = [jnp.concatenate([_block_diag(w_r[t * per:(t + 1) * per]),
                              _block_diag(w_i[t * per:(t + 1) * per])], axis=1)
             for t in range(LRU_WIDTH // MXU_TILE)]
    return jnp.stack(tiles).astype(BF16)


def kernel(x, norm1_g, w_in, conv_w, conv_b, w_rgate, b_rgate, w_igate, b_igate, lru_lambda,
           attn_out_g, lru_out_g, w_out, norm2_g, w_mlp_up, w_mlp_down, final_g):
    assert x.shape == (1, SEQ, D_MODEL) and w_in.shape[0] == 1
    x2 = x.reshape(SEQ, D_MODEL)
    q, k, v, q1, k1, v1, xl, gl = _inproj(x2, norm1_g.reshape(1, D_MODEL), w_in[0])
    attn, wo, wu, wd = _attention(q, k, v, q1, k1, v1, (w_out[0], w_mlp_up[0], w_mlp_down[0]))
    w_gate = _gate_tiles(w_rgate[0], w_igate[0])
    b_gate = jnp.concatenate([b_rgate[0].reshape(1, -1), b_igate[0].reshape(1, -1)], axis=1)
    lru_params = (conv_w[0], conv_b[0].reshape(1, -1), w_gate, b_gate, lru_lambda[0].reshape(1, -1))
    out = _out_mlp(x2, attn, xl, gl, lru_params,
                   attn_out_g[0].reshape(1, -1), lru_out_g[0].reshape(1, -1),
                   wo, norm2_g[0].reshape(1, -1), wu, wd, final_g.reshape(1, -1))
    return out.reshape(1, SEQ, D_MODEL)
```

```python
import functools

import numpy as np
import jax
import jax.numpy as jnp
from jax import lax
from jax.experimental import pallas as pl
from jax.experimental.pallas import tpu as pltpu

F32 = jnp.float32
BF16 = jnp.bfloat16

D_MODEL = 1024
SEQ = 16384
ATTN_HEADS = 8
HEAD_DIM = 64
ATTN_WIDTH = ATTN_HEADS * HEAD_DIM
ROT_DIM = HEAD_DIM // 4
ROPE_THETA = 500000.0
LRU_WIDTH = D_MODEL - ATTN_WIDTH
LRU_BLOCKS = 8
CONV_WIDTH = 4
LRU_C = 8.0
IN_WIDTH = 3 * ATTN_WIDTH + 2 * LRU_WIDTH
D_FF = 4 * D_MODEL
EPS = 1e-6
WINDOW_STEPS = 128

NS = 16
RA = SEQ // NS
LANES = 128
SUBLANES = 8
MXU_TILE = 256
NSLAB = ATTN_WIDTH // LANES
NEG = -np.inf

VMEM_LIMIT = 56 * 1024 * 1024


def _rms(x, g):
    return x * lax.rsqrt(jnp.mean(x * x, axis=-1, keepdims=True) + EPS) * g


TM1 = 1024
PG = NS * NS
RG = PG // NS
NG1 = TM1 // PG
CH = SUBLANES
LOG2E = 1.4426950408889634


@functools.lru_cache(maxsize=None)
def _group_perm():
    n = np.arange(PG)
    p = np.zeros((PG, PG), np.float32)
    p[(n % NS) * RG + n // NS, n] = 1.0
    return p


def _inproj_body(x_ref, g_ref, w_ref, perm_ref, tile_ref, off_ref,
                 q_ref, k_ref, v_ref, q1_ref, k1_ref, v1_ref, xl_ref, gl_ref, w_scr):
    @pl.when(pl.program_id(0) == 0)
    def _():
        w_scr[...] = w_ref[...].astype(BF16)

    u = _rms(x_ref[...], g_ref[...]).astype(BF16)
    u = jnp.concatenate(
        [jnp.dot(perm_ref[...], u[g * PG:(g + 1) * PG], preferred_element_type=F32).astype(BF16)
         for g in range(NG1)], axis=0)
    z = jnp.dot(u, w_scr[...], preferred_element_type=F32)
    tile_cos, tile_sin = tile_ref[:, :LANES], tile_ref[:, LANES:]
    c = tile_cos * off_ref[0] - tile_sin * off_ref[1]
    s = tile_sin * off_ref[2] + tile_cos * off_ref[3]
    c = jnp.concatenate([c] * NSLAB, axis=1)
    s = jnp.concatenate([s] * NSLAB, axis=1)
    lane = lax.broadcasted_iota(jnp.int32, (1, ATTN_WIDTH), 1) % HEAD_DIM
    first_half = lane < ROT_DIM // 2

    def rope(t):
        up = pltpu.roll(t, ATTN_WIDTH - ROT_DIM // 2, axis=1)
        dn = pltpu.roll(t, ROT_DIM // 2, axis=1)
        return t * c + jnp.where(first_half, up, dn) * s

    q = rope(z[:, :ATTN_WIDTH]) * (HEAD_DIM ** -0.5 * LOG2E)
    k = rope(z[:, ATTN_WIDTH:2 * ATTN_WIDTH])
    v = z[:, 2 * ATTN_WIDTH:3 * ATTN_WIDTH]
    xl = z[:, 3 * ATTN_WIDTH:3 * ATTN_WIDTH + LRU_WIDTH]
    gl = z[:, 3 * ATTN_WIDTH + LRU_WIDTH:]
    for g in range(NG1):
        for r in range(NS):
            src = slice(g * PG + r * RG, g * PG + (r + 1) * RG)
            dst = slice(g * RG, (g + 1) * RG)
            for j in range(NSLAB):
                sl = slice(j * LANES, (j + 1) * LANES)
                q_ref[j, r, dst] = q[src, sl].astype(BF16)
                k_ref[j, r, dst] = k[src, sl].astype(BF16)
                v_ref[j, r, dst] = v[src, sl].astype(BF16)
            xl_ref[r, dst] = xl[src]
            gl_ref[r, dst] = gl[src]
    for g in range(NG1):
        for cl in range(RG // CH):
            for r in range(0, NS, 2):
                lo = g * PG + r * RG + cl * CH
                hi = lo + RG
                d0 = (g * (RG // CH) + cl) * NS * CH + r * CH
                dst = slice(d0, d0 + 2 * CH)
                for j in range(NSLAB):
                    sl = slice(j * LANES, (j + 1) * LANES)
                    for src, ref in ((q, q1_ref), (k, k1_ref), (v, v1_ref)):
                        ref[j, dst] = jnp.concatenate(
                            [src[lo:lo + CH, sl], src[hi:hi + CH, sl]], axis=0).astype(BF16)


def _inproj(x2, g, w):
    steps = SEQ // TM1
    rows = TM1 // NS
    slab_spec = pl.BlockSpec((NSLAB, NS, rows, LANES), lambda i: (0, 0, i, 0))
    chunk_spec = pl.BlockSpec((NSLAB, TM1, LANES), lambda i: (0, i, 0))
    row_spec = pl.BlockSpec((NS, rows, LRU_WIDTH), lambda i: (0, i, 0))
    rope_tiles, rope_offs = _rope_tables()
    slab_shape = jax.ShapeDtypeStruct((NSLAB, NS, RA, LANES), BF16)
    chunk_shape = jax.ShapeDtypeStruct((NSLAB, SEQ, LANES), BF16)
    row_shape = jax.ShapeDtypeStruct((NS, RA, LRU_WIDTH), F32)
    perm = jnp.asarray(_group_perm(), BF16)
    return pl.pallas_call(
        _inproj_body,
        grid=(steps,),
        in_specs=[
            pl.BlockSpec((TM1, D_MODEL), lambda i: (i, 0)),
            pl.BlockSpec((1, D_MODEL), lambda i: (0, 0)),
            pl.BlockSpec((D_MODEL, IN_WIDTH), lambda i: (0, 0), pipeline_mode=pl.Buffered(1)),
            pl.BlockSpec((PG, PG), lambda i: (0, 0)),
            pl.BlockSpec((None, 1, 2 * LANES), lambda i: (i, 0, 0)),
            pl.BlockSpec(rope_offs.shape, lambda i: (0, 0, 0)),
        ],
        out_specs=[slab_spec] * 3 + [chunk_spec] * 3 + [row_spec] * 2,
        out_shape=[slab_shape] * 3 + [chunk_shape] * 3 + [row_shape] * 2,
        scratch_shapes=[pltpu.VMEM((D_MODEL, IN_WIDTH), BF16)],
        compiler_params=pltpu.CompilerParams(
            dimension_semantics=("arbitrary",), vmem_limit_bytes=VMEM_LIMIT),
        name="inproj",
    )(x2, g, w, perm, rope_tiles, rope_offs)


TA = WINDOW_STEPS
UQ, UK = WINDOW_STEPS, 2 * WINDOW_STEPS
Q4, K4 = UQ // 4, UK // 4
AHEAD = 2
RING = 2 * AHEAD


@functools.lru_cache(maxsize=None)
def _attn_masks():
    def variants(diff, from_prev, per_unit):
        ok = (diff >= 0) & (diff <= WINDOW_STEPS)
        normal = np.where(ok, 0.0, NEG).astype(np.float32)
        first = np.where(ok & ~from_prev, 0.0, NEG).astype(np.float32)
        if per_unit:
            return np.stack([np.stack([first, normal]), np.stack([normal, normal])])
        return np.stack([first, normal])

    iq = np.arange(UQ)[:, None]
    ck = np.arange(UK)[None, :]
    m16 = variants(iq - ck + TA, np.broadcast_to(ck < TA, (UQ, UK)), False)
    cq, i4 = np.divmod(np.arange(UQ), Q4)
    ckk, j4 = np.divmod(np.arange(UK), K4)
    d4 = 4 * (i4[:, None] - j4[None, :] + Q4) + (cq[:, None] - ckk[None, :])
    m4 = variants(d4, np.broadcast_to(j4[None, :] < K4 - Q4, d4.shape), True)
    rq, i1 = np.divmod(np.arange(UQ), CH)
    kc, kin = np.divmod(np.arange(UK), UQ)
    rk, j1 = np.divmod(kin, CH)
    d1 = (NS * i1 + rq)[:, None] - (NS * j1 + rk + UQ * (kc - 1))[None, :]
    m1 = variants(d1, np.broadcast_to(kc[None, :] == 0, d1.shape), True)
    return m16, m4, m1


def _attn_body(q_ref, kc_ref, kp_ref, vc_ref, vp_ref, q1_ref, k1c_ref, k1p_ref, v1c_ref, v1p_ref,
               m16_ref, m4_ref, m1_ref, wo_ref, wu_ref, wd_ref,
               o_ref, wo_bf_ref, wu_bf_ref, wd_bf_ref,
               s_scr, acc_scr, max_scr, sum_scr):
    wo_bf_ref[...] = wo_ref[...].astype(BF16)
    wu_bf_ref[...] = wu_ref[...].astype(BF16)
    wd_bf_ref[...] = wd_ref[...].astype(BF16)

    lane = lax.broadcasted_iota(jnp.int32, (1, LANES), 1)
    head0 = lane < HEAD_DIM
    zero = jnp.zeros((), BF16)

    def only_head(x, h):
        return jnp.where(head0, x, zero) if h == 0 else jnp.where(head0, zero, x)

    ones = jnp.ones((UK, LANES), BF16)

    def unit16(r):
        def store(o, mx, sm):
            acc_scr[0, r], max_scr[0, r], sum_scr[0, r] = o, mx, sm

        return (lambda h: only_head(q_ref[r], h),
                lambda: jnp.concatenate([kp_ref[r], kc_ref[r]], axis=0),
                lambda: jnp.concatenate([vp_ref[r], vc_ref[r]], axis=0),
                lambda: m16_ref[...], store)

    def unit4(r4, b):
        streams = [r4 + 4 * c for c in range(4)]
        q0, k0 = b * Q4, b * Q4 + Q4 - K4

        def window(cur, prev, st):
            if k0 < 0:
                return [prev[st, TA + k0:TA], cur[st, 0:k0 + K4]]
            return [cur[st, k0:k0 + K4]]

        def store(o, mx, sm):
            for c, st in enumerate(streams):
                rows = slice(c * Q4, (c + 1) * Q4)
                acc_scr[1, st, q0:q0 + Q4] = o[rows]
                max_scr[1, st, q0:q0 + Q4] = mx[rows]
                sum_scr[1, st, q0:q0 + Q4] = sm[rows]

        return (lambda h: only_head(
                    jnp.concatenate([q_ref[st, q0:q0 + Q4] for st in streams], axis=0), h),
                lambda: jnp.concatenate(
                    [x for st in streams for x in window(kc_ref, kp_ref, st)], axis=0),
                lambda: jnp.concatenate(
                    [x for st in streams for x in window(vc_ref, vp_ref, st)], axis=0),
                lambda: m4_ref[min(b, 1)], store)

    def unit1(u):
        off, row = u * UQ, u * CH

        def store(o, mx, sm):
            for r in range(NS):
                rows = slice(r * CH, (r + 1) * CH)
                acc_scr[2, r, row:row + CH] = o[rows]
                max_scr[2, r, row:row + CH] = mx[rows]
                sum_scr[2, r, row:row + CH] = sm[rows]

        def chunks(cur, prev):
            if u == 0:
                return jnp.concatenate([prev[...], cur[0:UQ]], axis=0)
            return cur[off - UQ:off + UQ]

        return (lambda h: only_head(q1_ref[off:off + UQ], h),
                lambda: chunks(k1c_ref, k1p_ref),
                lambda: chunks(v1c_ref, v1p_ref),
                lambda: m1_ref[min(u, 1)], store)

    all_units = ([unit16(n) for n in range(NS)]
                 + [unit4(n // (TA // Q4), n % (TA // Q4)) for n in range(NS)]
                 + [unit1(n) for n in range(NS)])

    def score_unit(n):
        q, k, _, _, _ = all_units[n]
        s_scr[n % RING] = lax.dot_general(
            jnp.concatenate([q(0), q(1)], axis=0), k(), (((1,), (1,)), ((), ())),
            preferred_element_type=F32)

    def value_unit(n):
        _, _, v, mask, store = all_units[n]
        ps, mx = [], []
        for h in range(2):
            s = s_scr[n % RING, h * UQ:(h + 1) * UQ] + mask()
            m = jnp.max(s, axis=-1, keepdims=True)
            ps.append(jnp.exp2(s - m).astype(BF16))
            mx.append(m)
        res = jnp.dot(jnp.concatenate(ps, axis=0), jnp.concatenate([v(), ones], axis=1),
                      preferred_element_type=F32)
        top, bot = res[:UQ], res[UQ:]
        store(jnp.where(head0, top[:, :LANES], bot[:, :LANES]),
              jnp.where(head0, mx[0], mx[1]),
              jnp.where(head0, top[:, LANES:], bot[:, LANES:]))

    for n in range(AHEAD):
        score_unit(n)
    for n in range(len(all_units)):
        if n + AHEAD < len(all_units):
            score_unit(n + AHEAD)
        value_unit(n)

    def comb(r, carry):
        m0, m1, m2 = max_scr[0, r], max_scr[1, r], max_scr[2, r]
        m = jnp.maximum(jnp.maximum(m0, m1), m2)
        w0, w1, w2 = jnp.exp2(m0 - m), jnp.exp2(m1 - m), jnp.exp2(m2 - m)
        num = w0 * acc_scr[0, r] + w1 * acc_scr[1, r] + w2 * acc_scr[2, r]
        den = w0 * sum_scr[0, r] + w1 * sum_scr[1, r] + w2 * sum_scr[2, r]
        o_ref[r] = num / den
        return carry

    lax.fori_loop(0, NS, comb, 0, unroll=4)


def _attention(q, k, v, q1, k1, v1, later_weights):
    m16, m4, m1 = _attn_masks()
    cur = pl.BlockSpec((None, NS, TA, LANES), lambda j, i: (j, 0, i, 0))
    prev = pl.BlockSpec((None, NS, TA, LANES), lambda j, i: (j, 0, jnp.maximum(i - 1, 0), 0))
    cur1 = pl.BlockSpec((None, NS * TA, LANES), lambda j, i: (j, i, 0))
    prev1 = pl.BlockSpec((None, UQ, LANES),
                         lambda j, i: (j, jnp.maximum(i * (NS * TA // UQ) - 1, 0), 0))

    def tile_kind(m):
        return pl.BlockSpec((None,) + m.shape[1:],
                            lambda j, i: (jnp.minimum(i, 1),) + (0,) * (m.ndim - 1))

    n_tiles = RA // TA
    steps = NSLAB * n_tiles
    cast_specs = [pl.BlockSpec((a.shape[0] // steps, a.shape[1]), lambda j, i: (j * n_tiles + i, 0))
                  for a in later_weights]
    cast_shapes = [jax.ShapeDtypeStruct(a.shape, BF16) for a in later_weights]
    return pl.pallas_call(
        _attn_body,
        grid=(NSLAB, n_tiles),
        in_specs=[cur, cur, prev, cur, prev, cur1, cur1, prev1, cur1, prev1,
                  tile_kind(m16), tile_kind(m4), tile_kind(m1)] + cast_specs,
        out_specs=[pl.BlockSpec((None, NS, TA, LANES), lambda j, i: (j, 0, i, 0))] + cast_specs,
        out_shape=[jax.ShapeDtypeStruct((NSLAB, NS, RA, LANES), F32)] + cast_shapes,
        scratch_shapes=[
            pltpu.VMEM((RING, 2 * UQ, UK), F32),
            pltpu.VMEM((3, NS, TA, LANES), F32),
            pltpu.VMEM((3, NS, TA, LANES), F32),
            pltpu.VMEM((3, NS, TA, LANES), F32),
        ],
        compiler_params=pltpu.CompilerParams(
            dimension_semantics=("arbitrary", "arbitrary"), vmem_limit_bytes=VMEM_LIMIT),
        name="dilated_attn",
    )(q, k, k, v, v, q1, k1, k1, v1, v1, m16, m4, m1, *later_weights)


TL = 32


def _lru_phases(xl_ref, gl_ref, cw_ref, cb_ref, wg_ref, bg_ref, lam_ref, y_ref,
                tail_scr, carry_scr, xc_scr, g_scr, pl_scr, hl_scr):
    row = lax.broadcasted_iota(jnp.int32, (TL, LRU_WIDTH), 0)
    state = {}

    def conv(after=0.0):
        cw = cw_ref[...]
        cb = cb_ref[...] + after

        def tap(r, k):
            st = r - k
            if st >= 0:
                return xl_ref[st]
            st += NS
            prev_last = tail_scr[st - (NS - CONV_WIDTH + 1), SUBLANES - 1:SUBLANES, :]
            return jnp.where(row == 0, prev_last, pltpu.roll(xl_ref[st], 1, axis=0))

        for r in range(NS):
            xc_scr[r] = cb + sum(cw[CONV_WIDTH - 1 - k:CONV_WIDTH - k] * tap(r, k)
                                 for k in range(CONV_WIDTH))
        for n in range(CONV_WIDTH - 1):
            tail_scr[n] = xl_ref[NS - CONV_WIDTH + 1 + n, TL - SUBLANES:TL, :]

    def gates(after=0.0):
        xc_all = xc_scr[...].reshape(NS * TL, LRU_WIDTH).astype(BF16)
        bias = bg_ref[...] + after
        for t in range(LRU_WIDTH // MXU_TILE):
            cols = slice(t * MXU_TILE, (t + 1) * MXU_TILE)
            res = jnp.dot(xc_all[:, cols], wg_ref[t], preferred_element_type=F32)
            for gate in range(2):
                dst = slice(gate * LRU_WIDTH + t * MXU_TILE, gate * LRU_WIDTH + (t + 1) * MXU_TILE)
                g_scr[:, dst] = res[:, gate * MXU_TILE:(gate + 1) * MXU_TILE] + bias[:, dst]

    def local_scan(streams):
        def run(after=0.0):
            neg_lam = after - lam_ref[...]
            softplus = jnp.maximum(neg_lam, 0.0) + jnp.log1p(jnp.exp(-jnp.abs(neg_lam)))
            p_run, h_run = state.get("run", (None, None))
            for r in streams:
                g = g_scr[r * TL:(r + 1) * TL]
                rg = jax.nn.sigmoid(g[:, :LRU_WIDTH])
                ig = jax.nn.sigmoid(g[:, LRU_WIDTH:])
                log_a = -LRU_C * rg * softplus
                a = jnp.exp(log_a)
                th = jnp.tanh(log_a)
                bx = jnp.sqrt(-2.0 * th / (1.0 - th)) * (ig * xc_scr[r])
                if p_run is None:
                    p_run, h_run = a, bx
                else:
                    h_run = a * h_run + bx
                    p_run = a * p_run
                pl_scr[r] = p_run
                hl_scr[r] = h_run
            state["run"] = (p_run, h_run)
        return run

    def row_scan():
        pa, hb = state["run"]
        sft = 1
        while sft < TL:
            keep = row >= sft
            pa_s = jnp.where(keep, pltpu.roll(pa, sft, axis=0), 1.0)
            hb_s = jnp.where(keep, pltpu.roll(hb, sft, axis=0), 0.0)
            hb = pa * hb_s + hb
            pa = pa * pa_s
            sft *= 2
        carry = carry_scr[0:1, :]
        e = pa * carry + hb
        state["e_prev"] = jnp.where(row == 0, carry, pltpu.roll(e, 1, axis=0))
        carry_scr[...] = jnp.broadcast_to(e[TL - 1:TL, :], carry_scr.shape)

    def finalize(after=0.0):
        e_prev = state["e_prev"] + after
        for r in range(NS):
            h = pl_scr[r] * e_prev + hl_scr[r]
            y_ref[r] = jax.nn.gelu(gl_ref[r], approximate=True) * h

    half = NS // 2

    def second_half(after=0.0):
        local_scan(range(half, NS))(after)
        row_scan()

    return [(conv, LRU_WIDTH), (gates, 2 * LRU_WIDTH), (local_scan(range(half)), LRU_WIDTH),
            (second_half, LRU_WIDTH), (finalize, LRU_WIDTH)]


TM4 = NS * TL
NG4 = TM4 // PG
FF_CHUNK = 1024


def _out_body(x_ref, at_ref, xl0_ref, gl0_ref, xln_ref, gln_ref, zero_ref,
              cw_ref, cb_ref, wg_ref, bg_ref, lam_ref,
              perm_ref, ga_ref, gl_ref, wo_ref, g2_ref, wu_ref, wd_ref, gf_ref,
              o_ref,
              lr_scr, tail_scr, carry_scr, xc_scr, g_scr, pl_scr, hl_scr):
    lru_params = (cw_ref, cb_ref, wg_ref, bg_ref, lam_ref)
    lru_state = (tail_scr, carry_scr, xc_scr, g_scr, pl_scr, hl_scr)

    @pl.when(pl.program_id(0) == 0)
    def _():
        tail_scr[...] = jnp.zeros_like(tail_scr)
        carry_scr[...] = jnp.zeros_like(carry_scr)
        for phase, _ in _lru_phases(xl0_ref, gl0_ref, *lru_params, lr_scr, *lru_state):
            phase()

    groups = []
    for g in range(NG4):
        rows = slice(g * RG, (g + 1) * RG)
        attn = jnp.concatenate(
            [jnp.concatenate([at_ref[j, r, rows] for j in range(NSLAB)], axis=1) for r in range(NS)],
            axis=0)
        lru = jnp.concatenate([lr_scr[r, rows] for r in range(NS)], axis=0)
        mixed = jnp.concatenate([_rms(attn, ga_ref[...]), _rms(lru, gl_ref[...])], axis=1)
        groups.append(jnp.dot(perm_ref[...], mixed.astype(BF16),
                              preferred_element_type=F32).astype(BF16))
    mixed = jnp.concatenate(groups, axis=0)
    lru_next = _lru_phases(xln_ref, gln_ref, *lru_params, lr_scr, *lru_state)
    n_chunks = D_FF // FF_CHUNK
    assert len(lru_next) <= n_chunks + 1

    def zero_after(val, width):
        bits = pltpu.bitcast(val[0:1, :width], jnp.int32) & zero_ref[0:1, :width]
        return pltpu.bitcast(bits, F32)

    h = x_ref[...] + jnp.dot(mixed, wo_ref[...], preferred_element_type=F32)
    u = _rms(h, g2_ref[...]).astype(BF16)
    acc = h
    phase, width = lru_next.pop(0)
    phase(zero_after(h, width))
    for c in range(n_chunks):
        if lru_next:
            phase, width = lru_next.pop(0)
            phase(zero_after(acc, width))
        sl = slice(c * FF_CHUNK, (c + 1) * FF_CHUNK)
        f = jnp.dot(u, wu_ref[:, sl], preferred_element_type=F32)
        f = jnp.square(jnp.maximum(f, 0.0)).astype(BF16)
        acc = acc + jnp.dot(f, wd_ref[sl, :], preferred_element_type=F32)
    o_ref[...] = _rms(acc, gf_ref[...])


def _out_mlp(x2, attn, xl, gl_in, lru_params, ga, gl, wo, g2, wu, wd, gf):
    def const(a):
        return pl.BlockSpec(a.shape, lambda i: (0,) * a.ndim, pipeline_mode=pl.Buffered(1))

    n_tiles = SEQ // TM4
    xspec = pl.BlockSpec((TM4, D_MODEL), lambda i: (i, 0))
    first = pl.BlockSpec((NS, TL, LRU_WIDTH), lambda i: (0, 0, 0), pipeline_mode=pl.Buffered(1))
    ahead = pl.BlockSpec((NS, TL, LRU_WIDTH), lambda i: (0, jnp.minimum(i + 1, n_tiles - 1), 0))
    perm = jnp.asarray(_group_perm(), BF16)
    zero = jnp.zeros((SUBLANES, D_MODEL), jnp.int32)
    consts = (zero,) + tuple(lru_params) + (perm, ga, gl, wo, g2, wu, wd, gf)
    return pl.pallas_call(
        _out_body,
        grid=(n_tiles,),
        in_specs=[
            xspec,
            pl.BlockSpec((NSLAB, NS, TL, LANES), lambda i: (0, 0, i, 0)),
            first, first, ahead, ahead,
        ] + [const(a) for a in consts],
        out_specs=xspec,
        out_shape=jax.ShapeDtypeStruct((SEQ, D_MODEL), F32),
        scratch_shapes=[
            pltpu.VMEM((NS, TL, LRU_WIDTH), F32),
            pltpu.VMEM((CONV_WIDTH - 1, SUBLANES, LRU_WIDTH), F32),
            pltpu.VMEM((SUBLANES, LRU_WIDTH), F32),
            pltpu.VMEM((NS, TL, LRU_WIDTH), F32),
            pltpu.VMEM((NS * TL, 2 * LRU_WIDTH), F32),
            pltpu.VMEM((NS, TL, LRU_WIDTH), F32),
            pltpu.VMEM((NS, TL, LRU_WIDTH), F32),
        ],
        compiler_params=pltpu.CompilerParams(
            dimension_semantics=("arbitrary",), vmem_limit_bytes=VMEM_LIMIT),
        name="out_mlp",
    )(x2, attn, xl, gl_in, xl, gl_in, *consts)


@functools.lru_cache(maxsize=None)
def _rope_tables():
    inv_freq = ROPE_THETA ** (-np.arange(0, ROT_DIM, 2, dtype=np.float64) / ROT_DIM)
    pad = HEAD_DIM - ROT_DIM
    reps = LANES // HEAD_DIM

    def lanes(rot_first, rot_second, rest, n):
        head = np.concatenate([rot_first, rot_second, np.full((n, pad), rest)], axis=1)
        return np.tile(head, (1, reps))

    tile_ang = (TM1 * np.arange(SEQ // TM1, dtype=np.float64))[:, None] * inv_freq[None, :]
    tc, ts = np.cos(tile_ang), np.sin(tile_ang)
    tiles = np.concatenate([lanes(tc, tc, 1.0, len(tc)), lanes(ts, ts, 0.0, len(ts))], axis=1)
    off_ang = np.arange(TM1, dtype=np.float64)[:, None] * inv_freq[None, :]
    oc, osn = np.cos(off_ang), np.sin(off_ang)
    offs = np.stack([lanes(oc, oc, 1.0, TM1), lanes(osn, osn, 0.0, TM1),
                     lanes(-oc, oc, 0.0, TM1), lanes(-osn, osn, 0.0, TM1)])
    offs = offs.reshape(4, NG1, RG, NS, LANES).transpose(0, 1, 3, 2, 4).reshape(4, TM1, LANES)
    return (tiles.astype(np.float32)[:, None, :], np.ascontiguousarray(offs.astype(np.float32)))


def _block_diag(w):
    n, b, _ = w.shape
    eye = jnp.eye(n, dtype=w.dtype)
    return (eye[:, None, :, None] * w[:, :, None, :]).reshape(n * b, n * b)


def _gate_tiles(w_r, w_i):
    per = MXU_TILE // (LRU_WIDTH // LRU_BLOCKS)
    tiles = [jnp.concatenate([_block_diag(w_r[t * per:(t + 1) * per]),
                              _block_diag(w_i[t * per:(t + 1) * per])], axis=1)
             for t in range(LRU_WIDTH // MXU_TILE)]
    return jnp.stack(tiles).astype(BF16)


def kernel(x, norm1_g, w_in, conv_w, conv_b, w_rgate, b_rgate, w_igate, b_igate, lru_lambda,
           attn_out_g, lru_out_g, w_out, norm2_g, w_mlp_up, w_mlp_down, final_g):
    assert x.shape == (1, SEQ, D_MODEL) and w_in.shape[0] == 1
    x2 = x.reshape(SEQ, D_MODEL)
    q, k, v, q1, k1, v1, xl, gl = _inproj(x2, norm1_g.reshape(1, D_MODEL), w_in[0])
    attn, wo, wu, wd = _attention(q, k, v, q1, k1, v1, (w_out[0], w_mlp_up[0], w_mlp_down[0]))
    w_gate = _gate_tiles(w_rgate[0], w_igate[0])
    b_gate = jnp.concatenate([b_rgate[0].reshape(1, -1), b_igate[0].reshape(1, -1)], axis=1)
    lru_params = (conv_w[0], conv_b[0].reshape(1, -1), w_gate, b_gate, lru_lambda[0].reshape(1, -1))
    out = _out_mlp(x2, attn, xl, gl, lru_params,
                   attn_out_g[0].reshape(1, -1), lru_out_g[0].reshape(1, -1),
                   wo, norm2_g[0].reshape(1, -1), wu, wd, final_g.reshape(1, -1))
    return out.reshape(1, SEQ, D_MODEL)
```

```python
import functools

import numpy as np
import jax
import jax.numpy as jnp
from jax import lax
from jax.experimental import pallas as pl
from jax.experimental.pallas import tpu as pltpu

F32 = jnp.float32
BF16 = jnp.bfloat16

D_MODEL = 1024
SEQ = 16384
ATTN_HEADS = 8
HEAD_DIM = 64
ATTN_WIDTH = ATTN_HEADS * HEAD_DIM
ROT_DIM = HEAD_DIM // 4
ROPE_THETA = 500000.0
LRU_WIDTH = D_MODEL - ATTN_WIDTH
LRU_BLOCKS = 8
CONV_WIDTH = 4
LRU_C = 8.0
IN_WIDTH = 3 * ATTN_WIDTH + 2 * LRU_WIDTH
D_FF = 4 * D_MODEL
EPS = 1e-6
WINDOW_STEPS = 128

NS = 16
RA = SEQ // NS
LANES = 128
SUBLANES = 8
MXU_TILE = 256
NSLAB = ATTN_WIDTH // LANES
NEG = -np.inf

VMEM_LIMIT = 56 * 1024 * 1024


def _rms(x, g):
    return x * lax.rsqrt(jnp.mean(x * x, axis=-1, keepdims=True) + EPS) * g


TM1 = 1024
PG = NS * NS
RG = PG // NS
NG1 = TM1 // PG
CH = SUBLANES
LOG2E = 1.4426950408889634


@functools.lru_cache(maxsize=None)
def _group_perm():
    n = np.arange(PG)
    p = np.zeros((PG, PG), np.float32)
    p[(n % NS) * RG + n // NS, n] = 1.0
    return p


def _inproj_body(x_ref, g_ref, w_ref, perm_ref, tile_ref, off_ref,
                 q_ref, k_ref, v_ref, q1_ref, k1_ref, v1_ref, xl_ref, gl_ref, w_scr):
    @pl.when(pl.program_id(0) == 0)
    def _():
        w_scr[...] = w_ref[...].astype(BF16)

    u = _rms(x_ref[...], g_ref[...]).astype(BF16)
    u = jnp.concatenate(
        [jnp.dot(perm_ref[...], u[g * PG:(g + 1) * PG], preferred_element_type=F32).astype(BF16)
         for g in range(NG1)], axis=0)
    z = jnp.dot(u, w_scr[...], preferred_element_type=F32)
    tile_cos, tile_sin = tile_ref[:, :LANES], tile_ref[:, LANES:]
    c = tile_cos * off_ref[0] - tile_sin * off_ref[1]
    s = tile_sin * off_ref[2] + tile_cos * off_ref[3]
    c = jnp.concatenate([c] * NSLAB, axis=1)
    s = jnp.concatenate([s] * NSLAB, axis=1)
    lane = lax.broadcasted_iota(jnp.int32, (1, ATTN_WIDTH), 1) % HEAD_DIM
    first_half = lane < ROT_DIM // 2

    def rope(t):
        up = pltpu.roll(t, ATTN_WIDTH - ROT_DIM // 2, axis=1)
        dn = pltpu.roll(t, ROT_DIM // 2, axis=1)
        return t * c + jnp.where(first_half, up, dn) * s

    q = rope(z[:, :ATTN_WIDTH]) * (HEAD_DIM ** -0.5 * LOG2E)
    k = rope(z[:, ATTN_WIDTH:2 * ATTN_WIDTH])
    v = z[:, 2 * ATTN_WIDTH:3 * ATTN_WIDTH]
    xl = z[:, 3 * ATTN_WIDTH:3 * ATTN_WIDTH + LRU_WIDTH]
    gl = z[:, 3 * ATTN_WIDTH + LRU_WIDTH:]
    for g in range(NG1):
        for r in range(NS):
            src = slice(g * PG + r * RG, g * PG + (r + 1) * RG)
            dst = slice(g * RG, (g + 1) * RG)
            for j in range(NSLAB):
                sl = slice(j * LANES, (j + 1) * LANES)
                q_ref[j, r, dst] = q[src, sl].astype(BF16)
                k_ref[j, r, dst] = k[src, sl].astype(BF16)
                v_ref[j, r, dst] = v[src, sl].astype(BF16)
            xl_ref[r, dst] = xl[src]
            gl_ref[r, dst] = gl[src]
    for g in range(NG1):
        for cl in range(RG // CH):
            for r in range(0, NS, 2):
                lo = g * PG + r * RG + cl * CH
                hi = lo + RG
                d0 = (g * (RG // CH) + cl) * NS * CH + r * CH
                dst = slice(d0, d0 + 2 * CH)
                for j in range(NSLAB):
                    sl = slice(j * LANES, (j + 1) * LANES)
                    for src, ref in ((q, q1_ref), (k, k1_ref), (v, v1_ref)):
                        ref[j, dst] = jnp.concatenate(
                            [src[lo:lo + CH, sl], src[hi:hi + CH, sl]], axis=0).astype(BF16)


def _inproj(x2, g, w):
    steps = SEQ // TM1
    rows = TM1 // NS
    slab_spec = pl.BlockSpec((NSLAB, NS, rows, LANES), lambda i: (0, 0, i, 0))
    chunk_spec = pl.BlockSpec((NSLAB, TM1, LANES), lambda i: (0, i, 0))
    row_spec = pl.BlockSpec((NS, rows, LRU_WIDTH), lambda i: (0, i, 0))
    rope_tiles, rope_offs = _rope_tables()
    slab_shape = jax.ShapeDtypeStruct((NSLAB, NS, RA, LANES), BF16)
    chunk_shape = jax.ShapeDtypeStruct((NSLAB, SEQ, LANES), BF16)
    row_shape = jax.ShapeDtypeStruct((NS, RA, LRU_WIDTH), F32)
    perm = jnp.asarray(_group_perm(), BF16)
    return pl.pallas_call(
        _inproj_body,
        grid=(steps,),
        in_specs=[
            pl.BlockSpec((TM1, D_MODEL), lambda i: (i, 0)),
            pl.BlockSpec((1, D_MODEL), lambda i: (0, 0)),
            pl.BlockSpec((D_MODEL, IN_WIDTH), lambda i: (0, 0), pipeline_mode=pl.Buffered(1)),
            pl.BlockSpec((PG, PG), lambda i: (0, 0)),
            pl.BlockSpec((None, 1, 2 * LANES), lambda i: (i, 0, 0)),
            pl.BlockSpec(rope_offs.shape, lambda i: (0, 0, 0)),
        ],
        out_specs=[slab_spec] * 3 + [chunk_spec] * 3 + [row_spec] * 2,
        out_shape=[slab_shape] * 3 + [chunk_shape] * 3 + [row_shape] * 2,
        scratch_shapes=[pltpu.VMEM((D_MODEL, IN_WIDTH), BF16)],
        compiler_params=pltpu.CompilerParams(
            dimension_semantics=("arbitrary",), vmem_limit_bytes=VMEM_LIMIT),
        name="inproj",
    )(x2, g, w, perm, rope_tiles, rope_offs)


TA = WINDOW_STEPS
UQ, UK = WINDOW_STEPS, 2 * WINDOW_STEPS
Q4, K4 = UQ // 4, UK // 4


@functools.lru_cache(maxsize=None)
def _attn_masks():
    def variants(diff, from_prev, per_unit):
        ok = (diff >= 0) & (diff <= WINDOW_STEPS)
        normal = np.where(ok, 0.0, NEG).astype(np.float32)
        first = np.where(ok & ~from_prev, 0.0, NEG).astype(np.float32)
        if per_unit:
            return np.stack([np.stack([first, normal]), np.stack([normal, normal])])
        return np.stack([first, normal])

    iq = np.arange(UQ)[:, None]
    ck = np.arange(UK)[None, :]
    m16 = variants(iq - ck + TA, np.broadcast_to(ck < TA, (UQ, UK)), False)
    cq, i4 = np.divmod(np.arange(UQ), Q4)
    ckk, j4 = np.divmod(np.arange(UK), K4)
    d4 = 4 * (i4[:, None] - j4[None, :] + Q4) + (cq[:, None] - ckk[None, :])
    m4 = variants(d4, np.broadcast_to(j4[None, :] < K4 - Q4, d4.shape), True)
    rq, i1 = np.divmod(np.arange(UQ), CH)
    kc, kin = np.divmod(np.arange(UK), UQ)
    rk, j1 = np.divmod(kin, CH)
    d1 = (NS * i1 + rq)[:, None] - (NS * j1 + rk + UQ * (kc - 1))[None, :]
    m1 = variants(d1, np.broadcast_to(kc[None, :] == 0, d1.shape), True)
    return m16, m4, m1


def _attn_body(q_ref, kc_ref, kp_ref, vc_ref, vp_ref, q1_ref, k1c_ref, k1p_ref, v1c_ref, v1p_ref,
               m16_ref, m4_ref, m1_ref, wo_ref, wu_ref, wd_ref,
               o_ref, wo_bf_ref, wu_bf_ref, wd_bf_ref,
               acc_scr, max_scr, sum_scr):
    wo_bf_ref[...] = wo_ref[...].astype(BF16)
    wu_bf_ref[...] = wu_ref[...].astype(BF16)
    wd_bf_ref[...] = wd_ref[...].astype(BF16)

    lane = lax.broadcasted_iota(jnp.int32, (1, LANES), 1)
    head0 = lane < HEAD_DIM
    zero = jnp.zeros((), BF16)

    def only_head(x, h):
        return jnp.where(head0, x, zero) if h == 0 else jnp.where(head0, zero, x)

    ones = jnp.ones((UK, LANES), BF16)

    def unit16(r):
        def store(o, mx, sm):
            acc_scr[0, r], max_scr[0, r], sum_scr[0, r] = o, mx, sm

        return (lambda h: only_head(q_ref[r], h),
                lambda: jnp.concatenate([kp_ref[r], kc_ref[r]], axis=0),
                lambda: jnp.concatenate([vp_ref[r], vc_ref[r]], axis=0),
                lambda: m16_ref[...], store)

    def unit4(r4, b):
        streams = [r4 + 4 * c for c in range(4)]
        q0, k0 = b * Q4, b * Q4 + Q4 - K4

        def window(cur, prev, st):
            if k0 < 0:
                return [prev[st, TA + k0:TA], cur[st, 0:k0 + K4]]
            return [cur[st, k0:k0 + K4]]

        def store(o, mx, sm):
            for c, st in enumerate(streams):
                rows = slice(c * Q4, (c + 1) * Q4)
                acc_scr[1, st, q0:q0 + Q4] = o[rows]
                max_scr[1, st, q0:q0 + Q4] = mx[rows]
                sum_scr[1, st, q0:q0 + Q4] = sm[rows]

        return (lambda h: only_head(
                    jnp.concatenate([q_ref[st, q0:q0 + Q4] for st in streams], axis=0), h),
                lambda: jnp.concatenate(
                    [x for st in streams for x in window(kc_ref, kp_ref, st)], axis=0),
                lambda: jnp.concatenate(
                    [x for st in streams for x in window(vc_ref, vp_ref, st)], axis=0),
                lambda: m4_ref[min(b, 1)], store)

    def unit1(u):
        off, row = u * UQ, u * CH

        def store(o, mx, sm):
            for r in range(NS):
                rows = slice(r * CH, (r + 1) * CH)
                acc_scr[2, r, row:row + CH] = o[rows]
                max_scr[2, r, row:row + CH] = mx[rows]
                sum_scr[2, r, row:row + CH] = sm[rows]

        def chunks(cur, prev):
            if u == 0:
                return jnp.concatenate([prev[...], cur[0:UQ]], axis=0)
            return cur[off - UQ:off + UQ]

        return (lambda h: only_head(q1_ref[off:off + UQ], h),
                lambda: chunks(k1c_ref, k1p_ref),
                lambda: chunks(v1c_ref, v1p_ref),
                lambda: m1_ref[min(u, 1)], store)

    all_units = ([unit16(n) for n in range(NS)]
                 + [unit4(n // (TA // Q4), n % (TA // Q4)) for n in range(NS)]
                 + [unit1(n) for n in range(NS)])

    for q, k, v, mask, store in all_units:
        scores = lax.dot_general(
            jnp.concatenate([q(0), q(1)], axis=0), k(), (((1,), (1,)), ((), ())),
            preferred_element_type=F32)
        ps, mx = [], []
        for h in range(2):
            s = scores[h * UQ:(h + 1) * UQ] + mask()
            m = jnp.max(s, axis=-1, keepdims=True)
            ps.append(jnp.exp2(s - m).astype(BF16))
            mx.append(m)
        res = jnp.dot(jnp.concatenate(ps, axis=0), jnp.concatenate([v(), ones], axis=1),
                      preferred_element_type=F32)
        top, bot = res[:UQ], res[UQ:]
        store(jnp.where(head0, top[:, :LANES], bot[:, :LANES]),
              jnp.where(head0, mx[0], mx[1]),
              jnp.where(head0, top[:, LANES:], bot[:, LANES:]))

    def comb(r, carry):
        m0, m1, m2 = max_scr[0, r], max_scr[1, r], max_scr[2, r]
        m = jnp.maximum(jnp.maximum(m0, m1), m2)
        w0, w1, w2 = jnp.exp2(m0 - m), jnp.exp2(m1 - m), jnp.exp2(m2 - m)
        num = w0 * acc_scr[0, r] + w1 * acc_scr[1, r] + w2 * acc_scr[2, r]
        den = w0 * sum_scr[0, r] + w1 * sum_scr[1, r] + w2 * sum_scr[2, r]
        o_ref[r] = num / den
        return carry

    lax.fori_loop(0, NS, comb, 0, unroll=4)


def _attention(q, k, v, q1, k1, v1, later_weights):
    m16, m4, m1 = _attn_masks()
    cur = pl.BlockSpec((None, NS, TA, LANES), lambda j, i: (j, 0, i, 0))
    prev = pl.BlockSpec((None, NS, TA, LANES), lambda j, i: (j, 0, jnp.maximum(i - 1, 0), 0))
    cur1 = pl.BlockSpec((None, NS * TA, LANES), lambda j, i: (j, i, 0))
    prev1 = pl.BlockSpec((None, UQ, LANES),
                         lambda j, i: (j, jnp.maximum(i * (NS * TA // UQ) - 1, 0), 0))

    def tile_kind(m):
        return pl.BlockSpec((None,) + m.shape[1:],
                            lambda j, i: (jnp.minimum(i, 1),) + (0,) * (m.ndim - 1))

    n_tiles = RA // TA
    steps = NSLAB * n_tiles
    cast_specs = [pl.BlockSpec((a.shape[0] // steps, a.shape[1]), lambda j, i: (j * n_tiles + i, 0))
                  for a in later_weights]
    cast_shapes = [jax.ShapeDtypeStruct(a.shape, BF16) for a in later_weights]
    return pl.pallas_call(
        _attn_body,
        grid=(NSLAB, n_tiles),
        in_specs=[cur, cur, prev, cur, prev, cur1, cur1, prev1, cur1, prev1,
                  tile_kind(m16), tile_kind(m4), tile_kind(m1)] + cast_specs,
        out_specs=[pl.BlockSpec((None, NS, TA, LANES), lambda j, i: (j, 0, i, 0))] + cast_specs,
        out_shape=[jax.ShapeDtypeStruct((NSLAB, NS, RA, LANES), F32)] + cast_shapes,
        scratch_shapes=[
            pltpu.VMEM((3, NS, TA, LANES), F32),
            pltpu.VMEM((3, NS, TA, LANES), F32),
            pltpu.VMEM((3, NS, TA, LANES), F32),
        ],
        compiler_params=pltpu.CompilerParams(
            dimension_semantics=("arbitrary", "arbitrary"), vmem_limit_bytes=VMEM_LIMIT),
        name="dilated_attn",
    )(q, k, k, v, v, q1, k1, k1, v1, v1, m16, m4, m1, *later_weights)


TL = 32


def _lru_phases(xl_ref, gl_ref, cw_ref, cb_ref, wg_ref, bg_ref, lam_ref, y_ref,
                tail_scr, carry_scr, xc_scr, g_scr, pl_scr, hl_scr):
    row = lax.broadcasted_iota(jnp.int32, (TL, LRU_WIDTH), 0)
    state = {}

    def conv(after=0.0):
        cw = cw_ref[...]
        cb = cb_ref[...] + after

        def tap(r, k):
            st = r - k
            if st >= 0:
                return xl_ref[st]
            st += NS
            prev_last = tail_scr[st - (NS - CONV_WIDTH + 1), SUBLANES - 1:SUBLANES, :]
            return jnp.where(row == 0, prev_last, pltpu.roll(xl_ref[st], 1, axis=0))

        for r in range(NS):
            xc_scr[r] = cb + sum(cw[CONV_WIDTH - 1 - k:CONV_WIDTH - k] * tap(r, k)
                                 for k in range(CONV_WIDTH))
        for n in range(CONV_WIDTH - 1):
            tail_scr[n] = xl_ref[NS - CONV_WIDTH + 1 + n, TL - SUBLANES:TL, :]

    def gates(after=0.0):
        xc_all = xc_scr[...].reshape(NS * TL, LRU_WIDTH).astype(BF16)
        bias = bg_ref[...] + after
        for t in range(LRU_WIDTH // MXU_TILE):
            cols = slice(t * MXU_TILE, (t + 1) * MXU_TILE)
            res = jnp.dot(xc_all[:, cols], wg_ref[t], preferred_element_type=F32)
            for gate in range(2):
                dst = slice(gate * LRU_WIDTH + t * MXU_TILE, gate * LRU_WIDTH + (t + 1) * MXU_TILE)
                g_scr[:, dst] = res[:, gate * MXU_TILE:(gate + 1) * MXU_TILE] + bias[:, dst]

    def local_scan(streams):
        def run(after=0.0):
            neg_lam = after - lam_ref[...]
            softplus = jnp.maximum(neg_lam, 0.0) + jnp.log1p(jnp.exp(-jnp.abs(neg_lam)))
            p_run, h_run = state.get("run", (None, None))
            for r in streams:
                g = g_scr[r * TL:(r + 1) * TL]
                rg = jax.nn.sigmoid(g[:, :LRU_WIDTH])
                ig = jax.nn.sigmoid(g[:, LRU_WIDTH:])
                log_a = -LRU_C * rg * softplus
                a = jnp.exp(log_a)
                th = jnp.tanh(log_a)
                bx = jnp.sqrt(-2.0 * th / (1.0 - th)) * (ig * xc_scr[r])
                if p_run is None:
                    p_run, h_run = a, bx
                else:
                    h_run = a * h_run + bx
                    p_run = a * p_run
                pl_scr[r] = p_run
                hl_scr[r] = h_run
            state["run"] = (p_run, h_run)
        return run

    def row_scan():
        pa, hb = state["run"]
        sft = 1
        while sft < TL:
            keep = row >= sft
            pa_s = jnp.where(keep, pltpu.roll(pa, sft, axis=0), 1.0)
            hb_s = jnp.where(keep, pltpu.roll(hb, sft, axis=0), 0.0)
            hb = pa * hb_s + hb
            pa = pa * pa_s
            sft *= 2
        carry = carry_scr[0:1, :]
        e = pa * carry + hb
        state["e_prev"] = jnp.where(row == 0, carry, pltpu.roll(e, 1, axis=0))
        carry_scr[...] = jnp.broadcast_to(e[TL - 1:TL, :], carry_scr.shape)

    def finalize(after=0.0):
        e_prev = state["e_prev"] + after
        for r in range(NS):
            h = pl_scr[r] * e_prev + hl_scr[r]
            y_ref[r] = jax.nn.gelu(gl_ref[r], approximate=True) * h

    half = NS // 2

    def second_half(after=0.0):
        local_scan(range(half, NS))(after)
        row_scan()

    return [(conv, LRU_WIDTH), (gates, 2 * LRU_WIDTH), (local_scan(range(half)), LRU_WIDTH),
            (second_half, LRU_WIDTH), (finalize, LRU_WIDTH)]


TM4 = NS * TL
NG4 = TM4 // PG
FF_CHUNK = 1024


def _out_body(x_ref, at_ref, xl0_ref, gl0_ref, xln_ref, gln_ref, zero_ref,
              cw_ref, cb_ref, wg_ref, bg_ref, lam_ref,
              perm_ref, ga_ref, gl_ref, wo_ref, g2_ref, wu_ref, wd_ref, gf_ref,
              o_ref,
              lr_scr, tail_scr, carry_scr, xc_scr, g_scr, pl_scr, hl_scr):
    lru_params = (cw_ref, cb_ref, wg_ref, bg_ref, lam_ref)
    lru_state = (tail_scr, carry_scr, xc_scr, g_scr, pl_scr, hl_scr)

    @pl.when(pl.program_id(0) == 0)
    def _():
        tail_scr[...] = jnp.zeros_like(tail_scr)
        carry_scr[...] = jnp.zeros_like(carry_scr)
        for phase, _ in _lru_phases(xl0_ref, gl0_ref, *lru_params, lr_scr, *lru_state):
            phase()

    groups = []
    for g in range(NG4):
        rows = slice(g * RG, (g + 1) * RG)
        attn = jnp.concatenate(
            [jnp.concatenate([at_ref[j, r, rows] for j in range(NSLAB)], axis=1) for r in range(NS)],
            axis=0)
        lru = jnp.concatenate([lr_scr[r, rows] for r in range(NS)], axis=0)
        mixed = jnp.concatenate([_rms(attn, ga_ref[...]), _rms(lru, gl_ref[...])], axis=1)
        groups.append(jnp.dot(perm_ref[...], mixed.astype(BF16),
                              preferred_element_type=F32).astype(BF16))
    mixed = jnp.concatenate(groups, axis=0)
    lru_next = _lru_phases(xln_ref, gln_ref, *lru_params, lr_scr, *lru_state)
    n_chunks = D_FF // FF_CHUNK
    assert len(lru_next) <= n_chunks + 1

    def zero_after(val, width):
        bits = pltpu.bitcast(val[0:1, :width], jnp.int32) & zero_ref[0:1, :width]
        return pltpu.bitcast(bits, F32)

    h = x_ref[...] + jnp.dot(mixed, wo_ref[...], preferred_element_type=F32)
    u = _rms(h, g2_ref[...]).astype(BF16)
    acc = h
    phase, width = lru_next.pop(0)
    phase(zero_after(h, width))
    for c in range(n_chunks):
        if lru_next:
            phase, width = lru_next.pop(0)
            phase(zero_after(acc, width))
        sl = slice(c * FF_CHUNK, (c + 1) * FF_CHUNK)
        f = jnp.dot(u, wu_ref[:, sl], preferred_element_type=F32)
        f = jnp.square(jnp.maximum(f, 0.0)).astype(BF16)
        acc = acc + jnp.dot(f, wd_ref[sl, :], preferred_element_type=F32)
    o_ref[...] = _rms(acc, gf_ref[...])


def _out_mlp(x2, attn, xl, gl_in, lru_params, ga, gl, wo, g2, wu, wd, gf):
    def const(a):
        return pl.BlockSpec(a.shape, lambda i: (0,) * a.ndim, pipeline_mode=pl.Buffered(1))

    n_tiles = SEQ // TM4
    xspec = pl.BlockSpec((TM4, D_MODEL), lambda i: (i, 0))
    first = pl.BlockSpec((NS, TL, LRU_WIDTH), lambda i: (0, 0, 0), pipeline_mode=pl.Buffered(1))
    ahead = pl.BlockSpec((NS, TL, LRU_WIDTH), lambda i: (0, jnp.minimum(i + 1, n_tiles - 1), 0))
    perm = jnp.asarray(_group_perm(), BF16)
    zero = jnp.zeros((SUBLANES, D_MODEL), jnp.int32)
    consts = (zero,) + tuple(lru_params) + (perm, ga, gl, wo, g2, wu, wd, gf)
    return pl.pallas_call(
        _out_body,
        grid=(n_tiles,),
        in_specs=[
            xspec,
            pl.BlockSpec((NSLAB, NS, TL, LANES), lambda i: (0, 0, i, 0)),
            first, first, ahead, ahead,
        ] + [const(a) for a in consts],
        out_specs=xspec,
        out_shape=jax.ShapeDtypeStruct((SEQ, D_MODEL), F32),
        scratch_shapes=[
            pltpu.VMEM((NS, TL, LRU_WIDTH), F32),
            pltpu.VMEM((CONV_WIDTH - 1, SUBLANES, LRU_WIDTH), F32),
            pltpu.VMEM((SUBLANES, LRU_WIDTH), F32),
            pltpu.VMEM((NS, TL, LRU_WIDTH), F32),
            pltpu.VMEM((NS * TL, 2 * LRU_WIDTH), F32),
            pltpu.VMEM((NS, TL, LRU_WIDTH), F32),
            pltpu.VMEM((NS, TL, LRU_WIDTH), F32),
        ],
        compiler_params=pltpu.CompilerParams(
            dimension_semantics=("arbitrary",), vmem_limit_bytes=VMEM_LIMIT),
        name="out_mlp",
    )(x2, attn, xl, gl_in, xl, gl_in, *consts)


@functools.lru_cache(maxsize=None)
def _rope_tables():
    inv_freq = ROPE_THETA ** (-np.arange(0, ROT_DIM, 2, dtype=np.float64) / ROT_DIM)
    pad = HEAD_DIM - ROT_DIM
    reps = LANES // HEAD_DIM

    def lanes(rot_first, rot_second, rest, n):
        head = np.concatenate([rot_first, rot_second, np.full((n, pad), rest)], axis=1)
        return np.tile(head, (1, reps))

    tile_ang = (TM1 * np.arange(SEQ // TM1, dtype=np.float64))[:, None] * inv_freq[None, :]
    tc, ts = np.cos(tile_ang), np.sin(tile_ang)
    tiles = np.concatenate([lanes(tc, tc, 1.0, len(tc)), lanes(ts, ts, 0.0, len(ts))], axis=1)
    off_ang = np.arange(TM1, dtype=np.float64)[:, None] * inv_freq[None, :]
    oc, osn = np.cos(off_ang), np.sin(off_ang)
    offs = np.stack([lanes(oc, oc, 1.0, TM1), lanes(osn, osn, 0.0, TM1),
                     lanes(-oc, oc, 0.0, TM1), lanes(-osn, osn, 0.0, TM1)])
    offs = offs.reshape(4, NG1, RG, NS, LANES).transpose(0, 1, 3, 2, 4).reshape(4, TM1, LANES)
    return (tiles.astype(np.float32)[:, None, :], np.ascontiguousarray(offs.astype(np.float32)))


def _block_diag(w):
    n, b, _ = w.shape
    eye = jnp.eye(n, dtype=w.dtype)
    return (eye[:, None, :, None] * w[:, :, None, :]).reshape(n * b, n * b)


def _gate_tiles(w_r, w_i):
    per = MXU_TILE // (LRU_WIDTH // LRU_BLOCKS)
    tiles = [jnp.concatenate([_block_diag(w_r[t * per:(t + 1) * per]),
                              _block_diag(w_i[t * per:(t + 1) * per])], axis=1)
             for t in range(LRU_WIDTH // MXU_TILE)]
    return jnp.stack(tiles).astype(BF16)


def kernel(x, norm1_g, w_in, conv_w, conv_b, w_rgate, b_rgate, w_igate, b_igate, lru_lambda,
           attn_out_g, lru_out_g, w_out, norm2_g, w_mlp_up, w_mlp_down, final_g):
    assert x.shape == (1, SEQ, D_MODEL) and w_in.shape[0] == 1
    x2 = x.reshape(SEQ, D_MODEL)
    q, k, v, q1, k1, v1, xl, gl = _inproj(x2, norm1_g.reshape(1, D_MODEL), w_in[0])
    attn, wo, wu, wd = _attention(q, k, v, q1, k1, v1, (w_out[0], w_mlp_up[0], w_mlp_down[0]))
    w_gate = _gate_tiles(w_rgate[0], w_igate[0])
    b_gate = jnp.concatenate([b_rgate[0].reshape(1, -1), b_igate[0].reshape(1, -1)], axis=1)
    lru_params = (conv_w[0], conv_b[0].reshape(1, -1), w_gate, b_gate, lru_lambda[0].reshape(1, -1))
    out = _out_mlp(x2, attn, xl, gl, lru_params,
                   attn_out_g[0].reshape(1, -1), lru_out_g[0].reshape(1, -1),
                   wo, norm2_g[0].reshape(1, -1), wu, wd, final_g.reshape(1, -1))
    return out.reshape(1, SEQ, D_MODEL)
```

```python
import functools

import numpy as np
import jax
import jax.numpy as jnp
from jax import lax
from jax.experimental import pallas as pl
from jax.experimental.pallas import tpu as pltpu

F32 = jnp.float32
BF16 = jnp.bfloat16

D_MODEL = 1024
SEQ = 16384
ATTN_HEADS = 8
HEAD_DIM = 64
ATTN_WIDTH = ATTN_HEADS * HEAD_DIM
ROT_DIM = HEAD_DIM // 4
ROPE_THETA = 500000.0
LRU_WIDTH = D_MODEL - ATTN_WIDTH
LRU_BLOCKS = 8
CONV_WIDTH = 4
LRU_C = 8.0
IN_WIDTH = 3 * ATTN_WIDTH + 2 * LRU_WIDTH
D_FF = 4 * D_MODEL
EPS = 1e-6
WINDOW_STEPS = 128

NS = 16
RA = SEQ // NS
LANES = 128
SUBLANES = 8
MXU_TILE = 256
NSLAB = ATTN_WIDTH // LANES
NEG = -np.inf

VMEM_LIMIT = 56 * 1024 * 1024


def _rms(x, g):
    return x * lax.rsqrt(jnp.mean(x * x, axis=-1, keepdims=True) + EPS) * g


TM1 = 1024
PG = NS * NS
RG = PG // NS
NG1 = TM1 // PG
CH = SUBLANES
LOG2E = 1.4426950408889634


@functools.lru_cache(maxsize=None)
def _group_perm():
    n = np.arange(PG)
    p = np.zeros((PG, PG), np.float32)
    p[(n % NS) * RG + n // NS, n] = 1.0
    return p


def _inproj_body(x_ref, g_ref, w_ref, perm_ref, tile_ref, off_ref,
                 q_ref, k_ref, v_ref, q1_ref, k1_ref, v1_ref, xl_ref, gl_ref, w_scr):
    @pl.when(pl.program_id(0) == 0)
    def _():
        w_scr[...] = w_ref[...].astype(BF16)

    u = _rms(x_ref[...], g_ref[...]).astype(BF16)
    u = jnp.concatenate(
        [jnp.dot(perm_ref[...], u[g * PG:(g + 1) * PG], preferred_element_type=F32).astype(BF16)
         for g in range(NG1)], axis=0)
    z = jnp.dot(u, w_scr[...], preferred_element_type=F32)
    tile_cos, tile_sin = tile_ref[:, :LANES], tile_ref[:, LANES:]
    c = tile_cos * off_ref[0] - tile_sin * off_ref[1]
    s = tile_sin * off_ref[2] + tile_cos * off_ref[3]
    c = jnp.concatenate([c] * NSLAB, axis=1)
    s = jnp.concatenate([s] * NSLAB, axis=1)
    lane = lax.broadcasted_iota(jnp.int32, (1, ATTN_WIDTH), 1) % HEAD_DIM
    first_half = lane < ROT_DIM // 2

    def rope(t):
        up = pltpu.roll(t, ATTN_WIDTH - ROT_DIM // 2, axis=1)
        dn = pltpu.roll(t, ROT_DIM // 2, axis=1)
        return t * c + jnp.where(first_half, up, dn) * s

    q = rope(z[:, :ATTN_WIDTH]) * (HEAD_DIM ** -0.5 * LOG2E)
    k = rope(z[:, ATTN_WIDTH:2 * ATTN_WIDTH])
    v = z[:, 2 * ATTN_WIDTH:3 * ATTN_WIDTH]
    xl = z[:, 3 * ATTN_WIDTH:3 * ATTN_WIDTH + LRU_WIDTH]
    gl = z[:, 3 * ATTN_WIDTH + LRU_WIDTH:]
    for g in range(NG1):
        for r in range(NS):
            src = slice(g * PG + r * RG, g * PG + (r + 1) * RG)
            dst = slice(g * RG, (g + 1) * RG)
            for j in range(NSLAB):
                sl = slice(j * LANES, (j + 1) * LANES)
                q_ref[j, r, dst] = q[src, sl].astype(BF16)
                k_ref[j, r, dst] = k[src, sl].astype(BF16)
                v_ref[j, r, dst] = v[src, sl].astype(BF16)
            xl_ref[r, dst] = xl[src]
            gl_ref[r, dst] = gl[src]
    for g in range(NG1):
        for cl in range(RG // CH):
            for r in range(0, NS, 2):
                lo = g * PG + r * RG + cl * CH
                hi = lo + RG
                d0 = (g * (RG // CH) + cl) * NS * CH + r * CH
                dst = slice(d0, d0 + 2 * CH)
                for j in range(NSLAB):
                    sl = slice(j * LANES, (j + 1) * LANES)
                    for src, ref in ((q, q1_ref), (k, k1_ref), (v, v1_ref)):
                        ref[j, dst] = jnp.concatenate(
                            [src[lo:lo + CH, sl], src[hi:hi + CH, sl]], axis=0).astype(BF16)


def _inproj(x2, g, w):
    steps = SEQ // TM1
    rows = TM1 // NS
    slab_spec = pl.BlockSpec((NSLAB, NS, rows, LANES), lambda i: (0, 0, i, 0))
    chunk_spec = pl.BlockSpec((NSLAB, TM1, LANES), lambda i: (0, i, 0))
    row_spec = pl.BlockSpec((NS, rows, LRU_WIDTH), lambda i: (0, i, 0))
    rope_tiles, rope_offs = _rope_tables()
    slab_shape = jax.ShapeDtypeStruct((NSLAB, NS, RA, LANES), BF16)
    chunk_shape = jax.ShapeDtypeStruct((NSLAB, SEQ, LANES), BF16)
    row_shape = jax.ShapeDtypeStruct((NS, RA, LRU_WIDTH), F32)
    perm = jnp.asarray(_group_perm(), BF16)
    return pl.pallas_call(
        _inproj_body,
        grid=(steps,),
        in_specs=[
            pl.BlockSpec((TM1, D_MODEL), lambda i: (i, 0)),
            pl.BlockSpec((1, D_MODEL), lambda i: (0, 0)),
            pl.BlockSpec((D_MODEL, IN_WIDTH), lambda i: (0, 0), pipeline_mode=pl.Buffered(1)),
            pl.BlockSpec((PG, PG), lambda i: (0, 0)),
            pl.BlockSpec((None, 1, 2 * LANES), lambda i: (i, 0, 0)),
            pl.BlockSpec(rope_offs.shape, lambda i: (0, 0, 0)),
        ],
        out_specs=[slab_spec] * 3 + [chunk_spec] * 3 + [row_spec] * 2,
        out_shape=[slab_shape] * 3 + [chunk_shape] * 3 + [row_shape] * 2,
        scratch_shapes=[pltpu.VMEM((D_MODEL, IN_WIDTH), BF16)],
        compiler_params=pltpu.CompilerParams(
            dimension_semantics=("arbitrary",), vmem_limit_bytes=VMEM_LIMIT),
        name="inproj",
    )(x2, g, w, perm, rope_tiles, rope_offs)


TA = WINDOW_STEPS
UQ, UK = WINDOW_STEPS, 2 * WINDOW_STEPS
Q4, K4 = UQ // 4, UK // 4
AHEAD = 4
RING = 2 * AHEAD


@functools.lru_cache(maxsize=None)
def _attn_masks():
    def variants(diff, from_prev, per_unit):
        ok = (diff >= 0) & (diff <= WINDOW_STEPS)
        normal = np.where(ok, 0.0, NEG).astype(np.float32)
        first = np.where(ok & ~from_prev, 0.0, NEG).astype(np.float32)
        if per_unit:
            return np.stack([np.stack([first, normal]), np.stack([normal, normal])])
        return np.stack([first, normal])

    iq = np.arange(UQ)[:, None]
    ck = np.arange(UK)[None, :]
    m16 = variants(iq - ck + TA, np.broadcast_to(ck < TA, (UQ, UK)), False)
    cq, i4 = np.divmod(np.arange(UQ), Q4)
    ckk, j4 = np.divmod(np.arange(UK), K4)
    d4 = 4 * (i4[:, None] - j4[None, :] + Q4) + (cq[:, None] - ckk[None, :])
    m4 = variants(d4, np.broadcast_to(j4[None, :] < K4 - Q4, d4.shape), True)
    rq, i1 = np.divmod(np.arange(UQ), CH)
    kc, kin = np.divmod(np.arange(UK), UQ)
    rk, j1 = np.divmod(kin, CH)
    d1 = (NS * i1 + rq)[:, None] - (NS * j1 + rk + UQ * (kc - 1))[None, :]
    m1 = variants(d1, np.broadcast_to(kc[None, :] == 0, d1.shape), True)
    return m16, m4, m1


def _attn_body(q_ref, kc_ref, kp_ref, vc_ref, vp_ref, q1_ref, k1c_ref, k1p_ref, v1c_ref, v1p_ref,
               m16_ref, m4_ref, m1_ref, wo_ref, wu_ref, wd_ref,
               o_ref, wo_bf_ref, wu_bf_ref, wd_bf_ref,
               s_scr, acc_scr, max_scr, sum_scr):
    wo_bf_ref[...] = wo_ref[...].astype(BF16)
    wu_bf_ref[...] = wu_ref[...].astype(BF16)
    wd_bf_ref[...] = wd_ref[...].astype(BF16)

    lane = lax.broadcasted_iota(jnp.int32, (1, LANES), 1)
    head0 = lane < HEAD_DIM
    zero = jnp.zeros((), BF16)

    def only_head(x, h):
        return jnp.where(head0, x, zero) if h == 0 else jnp.where(head0, zero, x)

    ones = jnp.ones((UK, LANES), BF16)

    def unit16(r):
        def store(o, mx, sm):
            acc_scr[0, r], max_scr[0, r], sum_scr[0, r] = o, mx, sm

        return (lambda h: only_head(q_ref[r], h),
                lambda: jnp.concatenate([kp_ref[r], kc_ref[r]], axis=0),
                lambda: jnp.concatenate([vp_ref[r], vc_ref[r]], axis=0),
                lambda: m16_ref[...], store)

    def unit4(r4, b):
        streams = [r4 + 4 * c for c in range(4)]
        q0, k0 = b * Q4, b * Q4 + Q4 - K4

        def window(cur, prev, st):
            if k0 < 0:
                return [prev[st, TA + k0:TA], cur[st, 0:k0 + K4]]
            return [cur[st, k0:k0 + K4]]

        def store(o, mx, sm):
            for c, st in enumerate(streams):
                rows = slice(c * Q4, (c + 1) * Q4)
                acc_scr[1, st, q0:q0 + Q4] = o[rows]
                max_scr[1, st, q0:q0 + Q4] = mx[rows]
                sum_scr[1, st, q0:q0 + Q4] = sm[rows]

        return (lambda h: only_head(
                    jnp.concatenate([q_ref[st, q0:q0 + Q4] for st in streams], axis=0), h),
                lambda: jnp.concatenate(
                    [x for st in streams for x in window(kc_ref, kp_ref, st)], axis=0),
                lambda: jnp.concatenate(
                    [x for st in streams for x in window(vc_ref, vp_ref, st)], axis=0),
                lambda: m4_ref[min(b, 1)], store)

    def unit1(u):
        off, row = u * UQ, u * CH

        def store(o, mx, sm):
            for r in range(NS):
                rows = slice(r * CH, (r + 1) * CH)
                acc_scr[2, r, row:row + CH] = o[rows]
                max_scr[2, r, row:row + CH] = mx[rows]
                sum_scr[2, r, row:row + CH] = sm[rows]

        def chunks(cur, prev):
            if u == 0:
                return jnp.concatenate([prev[...], cur[0:UQ]], axis=0)
            return cur[off - UQ:off + UQ]

        return (lambda h: only_head(q1_ref[off:off + UQ], h),
                lambda: chunks(k1c_ref, k1p_ref),
                lambda: chunks(v1c_ref, v1p_ref),
                lambda: m1_ref[min(u, 1)], store)

    all_units = ([unit16(n) for n in range(NS)]
                 + [unit4(n // (TA // Q4), n % (TA // Q4)) for n in range(NS)]
                 + [unit1(n) for n in range(NS)])

    def score_unit(n):
        q, k, _, _, _ = all_units[n]
        s_scr[n % RING] = lax.dot_general(
            jnp.concatenate([q(0), q(1)], axis=0), k(), (((1,), (1,)), ((), ())),
            preferred_element_type=F32)

    def value_unit(n):
        _, _, v, mask, store = all_units[n]
        ps, mx = [], []
        for h in range(2):
            s = s_scr[n % RING, h * UQ:(h + 1) * UQ] + mask()
            m = jnp.max(s, axis=-1, keepdims=True)
            ps.append(jnp.exp2(s - m).astype(BF16))
            mx.append(m)
        res = jnp.dot(jnp.concatenate(ps, axis=0), jnp.concatenate([v(), ones], axis=1),
                      preferred_element_type=F32)
        top, bot = res[:UQ], res[UQ:]
        store(jnp.where(head0, top[:, :LANES], bot[:, :LANES]),
              jnp.where(head0, mx[0], mx[1]),
              jnp.where(head0, top[:, LANES:], bot[:, LANES:]))

    for n in range(AHEAD):
        score_unit(n)
    for n in range(len(all_units)):
        if n + AHEAD < len(all_units):
            score_unit(n + AHEAD)
        value_unit(n)

    def comb(r, carry):
        m0, m1, m2 = max_scr[0, r], max_scr[1, r], max_scr[2, r]
        m = jnp.maximum(jnp.maximum(m0, m1), m2)
        w0, w1, w2 = jnp.exp2(m0 - m), jnp.exp2(m1 - m), jnp.exp2(m2 - m)
        num = w0 * acc_scr[0, r] + w1 * acc_scr[1, r] + w2 * acc_scr[2, r]
        den = w0 * sum_scr[0, r] + w1 * sum_scr[1, r] + w2 * sum_scr[2, r]
        o_ref[r] = num / den
        return carry

    lax.fori_loop(0, NS, comb, 0, unroll=4)


def _attention(q, k, v, q1, k1, v1, later_weights):
    m16, m4, m1 = _attn_masks()
    cur = pl.BlockSpec((None, NS, TA, LANES), lambda j, i: (j, 0, i, 0))
    prev = pl.BlockSpec((None, NS, TA, LANES), lambda j, i: (j, 0, jnp.maximum(i - 1, 0), 0))
    cur1 = pl.BlockSpec((None, NS * TA, LANES), lambda j, i: (j, i, 0))
    prev1 = pl.BlockSpec((None, UQ, LANES),
                         lambda j, i: (j, jnp.maximum(i * (NS * TA // UQ) - 1, 0), 0))

    def tile_kind(m):
        return pl.BlockSpec((None,) + m.shape[1:],
                            lambda j, i: (jnp.minimum(i, 1),) + (0,) * (m.ndim - 1))

    n_tiles = RA // TA
    steps = NSLAB * n_tiles
    cast_specs = [pl.BlockSpec((a.shape[0] // steps, a.shape[1]), lambda j, i: (j * n_tiles + i, 0))
                  for a in later_weights]
    cast_shapes = [jax.ShapeDtypeStruct(a.shape, BF16) for a in later_weights]
    return pl.pallas_call(
        _attn_body,
        grid=(NSLAB, n_tiles),
        in_specs=[cur, cur, prev, cur, prev, cur1, cur1, prev1, cur1, prev1,
                  tile_kind(m16), tile_kind(m4), tile_kind(m1)] + cast_specs,
        out_specs=[pl.BlockSpec((None, NS, TA, LANES), lambda j, i: (j, 0, i, 0))] + cast_specs,
        out_shape=[jax.ShapeDtypeStruct((NSLAB, NS, RA, LANES), F32)] + cast_shapes,
        scratch_shapes=[
            pltpu.VMEM((RING, 2 * UQ, UK), F32),
            pltpu.VMEM((3, NS, TA, LANES), F32),
            pltpu.VMEM((3, NS, TA, LANES), F32),
            pltpu.VMEM((3, NS, TA, LANES), F32),
        ],
        compiler_params=pltpu.CompilerParams(
            dimension_semantics=("arbitrary", "arbitrary"), vmem_limit_bytes=VMEM_LIMIT),
        name="dilated_attn",
    )(q, k, k, v, v, q1, k1, k1, v1, v1, m16, m4, m1, *later_weights)


TL = 32


def _lru_phases(xl_ref, gl_ref, cw_ref, cb_ref, wg_ref, bg_ref, lam_ref, y_ref,
                tail_scr, carry_scr, xc_scr, g_scr, pl_scr, hl_scr):
    row = lax.broadcasted_iota(jnp.int32, (TL, LRU_WIDTH), 0)
    state = {}

    def conv(after=0.0):
        cw = cw_ref[...]
        cb = cb_ref[...] + after

        def tap(r, k):
            st = r - k
            if st >= 0:
                return xl_ref[st]
            st += NS
            prev_last = tail_scr[st - (NS - CONV_WIDTH + 1), SUBLANES - 1:SUBLANES, :]
            return jnp.where(row == 0, prev_last, pltpu.roll(xl_ref[st], 1, axis=0))

        for r in range(NS):
            xc_scr[r] = cb + sum(cw[CONV_WIDTH - 1 - k:CONV_WIDTH - k] * tap(r, k)
                                 for k in range(CONV_WIDTH))
        for n in range(CONV_WIDTH - 1):
            tail_scr[n] = xl_ref[NS - CONV_WIDTH + 1 + n, TL - SUBLANES:TL, :]

    def gates(after=0.0):
        xc_all = xc_scr[...].reshape(NS * TL, LRU_WIDTH).astype(BF16)
        bias = bg_ref[...] + after
        for t in range(LRU_WIDTH // MXU_TILE):
            cols = slice(t * MXU_TILE, (t + 1) * MXU_TILE)
            res = jnp.dot(xc_all[:, cols], wg_ref[t], preferred_element_type=F32)
            for gate in range(2):
                dst = slice(gate * LRU_WIDTH + t * MXU_TILE, gate * LRU_WIDTH + (t + 1) * MXU_TILE)
                g_scr[:, dst] = res[:, gate * MXU_TILE:(gate + 1) * MXU_TILE] + bias[:, dst]

    def local_scan(streams):
        def run(after=0.0):
            neg_lam = after - lam_ref[...]
            softplus = jnp.maximum(neg_lam, 0.0) + jnp.log1p(jnp.exp(-jnp.abs(neg_lam)))
            p_run, h_run = state.get("run", (None, None))
            for r in streams:
                g = g_scr[r * TL:(r + 1) * TL]
                rg = jax.nn.sigmoid(g[:, :LRU_WIDTH])
                ig = jax.nn.sigmoid(g[:, LRU_WIDTH:])
                log_a = -LRU_C * rg * softplus
                a = jnp.exp(log_a)
                th = jnp.tanh(log_a)
                bx = jnp.sqrt(-2.0 * th / (1.0 - th)) * (ig * xc_scr[r])
                if p_run is None:
                    p_run, h_run = a, bx
                else:
                    h_run = a * h_run + bx
                    p_run = a * p_run
                pl_scr[r] = p_run
                hl_scr[r] = h_run
            state["run"] = (p_run, h_run)
        return run

    def row_scan():
        pa, hb = state["run"]
        sft = 1
        while sft < TL:
            keep = row >= sft
            pa_s = jnp.where(keep, pltpu.roll(pa, sft, axis=0), 1.0)
            hb_s = jnp.where(keep, pltpu.roll(hb, sft, axis=0), 0.0)
            hb = pa * hb_s + hb
            pa = pa * pa_s
            sft *= 2
        carry = carry_scr[0:1, :]
        e = pa * carry + hb
        state["e_prev"] = jnp.where(row == 0, carry, pltpu.roll(e, 1, axis=0))
        carry_scr[...] = jnp.broadcast_to(e[TL - 1:TL, :], carry_scr.shape)

    def finalize(after=0.0):
        e_prev = state["e_prev"] + after
        for r in range(NS):
            h = pl_scr[r] * e_prev + hl_scr[r]
            y_ref[r] = jax.nn.gelu(gl_ref[r], approximate=True) * h

    half = NS // 2

    def second_half(after=0.0):
        local_scan(range(half, NS))(after)
        row_scan()

    return [(conv, LRU_WIDTH), (gates, 2 * LRU_WIDTH), (local_scan(range(half)), LRU_WIDTH),
            (second_half, LRU_WIDTH), (finalize, LRU_WIDTH)]


TM4 = NS * TL
NG4 = TM4 // PG
FF_CHUNK = 1024


def _out_body(x_ref, at_ref, xl0_ref, gl0_ref, xln_ref, gln_ref, zero_ref,
              cw_ref, cb_ref, wg_ref, bg_ref, lam_ref,
              perm_ref, ga_ref, gl_ref, wo_ref, g2_ref, wu_ref, wd_ref, gf_ref,
              o_ref,
              lr_scr, tail_scr, carry_scr, xc_scr, g_scr, pl_scr, hl_scr):
    lru_params = (cw_ref, cb_ref, wg_ref, bg_ref, lam_ref)
    lru_state = (tail_scr, carry_scr, xc_scr, g_scr, pl_scr, hl_scr)

    @pl.when(pl.program_id(0) == 0)
    def _():
        tail_scr[...] = jnp.zeros_like(tail_scr)
        carry_scr[...] = jnp.zeros_like(carry_scr)
        for phase, _ in _lru_phases(xl0_ref, gl0_ref, *lru_params, lr_scr, *lru_state):
            phase()

    groups = []
    for g in range(NG4):
        rows = slice(g * RG, (g + 1) * RG)
        attn = jnp.concatenate(
            [jnp.concatenate([at_ref[j, r, rows] for j in range(NSLAB)], axis=1) for r in range(NS)],
            axis=0)
        lru = jnp.concatenate([lr_scr[r, rows] for r in range(NS)], axis=0)
        mixed = jnp.concatenate([_rms(attn, ga_ref[...]), _rms(lru, gl_ref[...])], axis=1)
        groups.append(jnp.dot(perm_ref[...], mixed.astype(BF16),
                              preferred_element_type=F32).astype(BF16))
    mixed = jnp.concatenate(groups, axis=0)
    lru_next = _lru_phases(xln_ref, gln_ref, *lru_params, lr_scr, *lru_state)
    n_chunks = D_FF // FF_CHUNK
    assert len(lru_next) <= n_chunks + 1

    def zero_after(val, width):
        bits = pltpu.bitcast(val[0:1, :width], jnp.int32) & zero_ref[0:1, :width]
        return pltpu.bitcast(bits, F32)

    h = x_ref[...] + jnp.dot(mixed, wo_ref[...], preferred_element_type=F32)
    u = _rms(h, g2_ref[...]).astype(BF16)
    acc = h
    phase, width = lru_next.pop(0)
    phase(zero_after(h, width))
    for c in range(n_chunks):
        if lru_next:
            phase, width = lru_next.pop(0)
            phase(zero_after(acc, width))
        sl = slice(c * FF_CHUNK, (c + 1) * FF_CHUNK)
        f = jnp.dot(u, wu_ref[:, sl], preferred_element_type=F32)
        f = jnp.square(jnp.maximum(f, 0.0)).astype(BF16)
        acc = acc + jnp.dot(f, wd_ref[sl, :], preferred_element_type=F32)
    o_ref[...] = _rms(acc, gf_ref[...])


def _out_mlp(x2, attn, xl, gl_in, lru_params, ga, gl, wo, g2, wu, wd, gf):
    def const(a):
        return pl.BlockSpec(a.shape, lambda i: (0,) * a.ndim, pipeline_mode=pl.Buffered(1))

    n_tiles = SEQ // TM4
    xspec = pl.BlockSpec((TM4, D_MODEL), lambda i: (i, 0))
    first = pl.BlockSpec((NS, TL, LRU_WIDTH), lambda i: (0, 0, 0), pipeline_mode=pl.Buffered(1))
    ahead = pl.BlockSpec((NS, TL, LRU_WIDTH), lambda i: (0, jnp.minimum(i + 1, n_tiles - 1), 0))
    perm = jnp.asarray(_group_perm(), BF16)
    zero = jnp.zeros((SUBLANES, D_MODEL), jnp.int32)
    consts = (zero,) + tuple(lru_params) + (perm, ga, gl, wo, g2, wu, wd, gf)
    return pl.pallas_call(
        _out_body,
        grid=(n_tiles,),
        in_specs=[
            xspec,
            pl.BlockSpec((NSLAB, NS, TL, LANES), lambda i: (0, 0, i, 0)),
            first, first, ahead, ahead,
        ] + [const(a) for a in consts],
        out_specs=xspec,
        out_shape=jax.ShapeDtypeStruct((SEQ, D_MODEL), F32),
        scratch_shapes=[
            pltpu.VMEM((NS, TL, LRU_WIDTH), F32),
            pltpu.VMEM((CONV_WIDTH - 1, SUBLANES, LRU_WIDTH), F32),
            pltpu.VMEM((SUBLANES, LRU_WIDTH), F32),
            pltpu.VMEM((NS, TL, LRU_WIDTH), F32),
            pltpu.VMEM((NS * TL, 2 * LRU_WIDTH), F32),
            pltpu.VMEM((NS, TL, LRU_WIDTH), F32),
            pltpu.VMEM((NS, TL, LRU_WIDTH), F32),
        ],
        compiler_params=pltpu.CompilerParams(
            dimension_semantics=("arbitrary",), vmem_limit_bytes=VMEM_LIMIT),
        name="out_mlp",
    )(x2, attn, xl, gl_in, xl, gl_in, *consts)


@functools.lru_cache(maxsize=None)
def _rope_tables():
    inv_freq = ROPE_THETA ** (-np.arange(0, ROT_DIM, 2, dtype=np.float64) / ROT_DIM)
    pad = HEAD_DIM - ROT_DIM
    reps = LANES // HEAD_DIM

    def lanes(rot_first, rot_second, rest, n):
        head = np.concatenate([rot_first, rot_second, np.full((n, pad), rest)], axis=1)
        return np.tile(head, (1, reps))

    tile_ang = (TM1 * np.arange(SEQ // TM1, dtype=np.float64))[:, None] * inv_freq[None, :]
    tc, ts = np.cos(tile_ang), np.sin(tile_ang)
    tiles = np.concatenate([lanes(tc, tc, 1.0, len(tc)), lanes(ts, ts, 0.0, len(ts))], axis=1)
    off_ang = np.arange(TM1, dtype=np.float64)[:, None] * inv_freq[None, :]
    oc, osn = np.cos(off_ang), np.sin(off_ang)
    offs = np.stack([lanes(oc, oc, 1.0, TM1), lanes(osn, osn, 0.0, TM1),
                     lanes(-oc, oc, 0.0, TM1), lanes(-osn, osn, 0.0, TM1)])
    offs = offs.reshape(4, NG1, RG, NS, LANES).transpose(0, 1, 3, 2, 4).reshape(4, TM1, LANES)
    return (tiles.astype(np.float32)[:, None, :], np.ascontiguousarray(offs.astype(np.float32)))


def _block_diag(w):
    n, b, _ = w.shape
    eye = jnp.eye(n, dtype=w.dtype)
    return (eye[:, None, :, None] * w[:, :, None, :]).reshape(n * b, n * b)


def _gate_tiles(w_r, w_i):
    per = MXU_TILE // (LRU_WIDTH // LRU_BLOCKS)
    tiles = [jnp.concatenate([_block_diag(w_r[t * per:(t + 1) * per]),
                              _block_diag(w_i[t * per:(t + 1) * per])], axis=1)
             for t in range(LRU_WIDTH // MXU_TILE)]
    return jnp.stack(tiles).astype(BF16)


def kernel(x, norm1_g, w_in, conv_w, conv_b, w_rgate, b_rgate, w_igate, b_igate, lru_lambda,
           attn_out_g, lru_out_g, w_out, norm2_g, w_mlp_up, w_mlp_down, final_g):
    assert x.shape == (1, SEQ, D_MODEL) and w_in.shape[0] == 1
    x2 = x.reshape(SEQ, D_MODEL)
    q, k, v, q1, k1, v1, xl, gl = _inproj(x2, norm1_g.reshape(1, D_MODEL), w_in[0])
    attn, wo, wu, wd = _attention(q, k, v, q1, k1, v1, (w_out[0], w_mlp_up[0], w_mlp_down[0]))
    w_gate = _gate_tiles(w_rgate[0], w_igate[0])
    b_gate = jnp.concatenate([b_rgate[0].reshape(1, -1), b_igate[0].reshape(1, -1)], axis=1)
    lru_params = (conv_w[0], conv_b[0].reshape(1, -1), w_gate, b_gate, lru_lambda[0].reshape(1, -1))
    out = _out_mlp(x2, attn, xl, gl, lru_params,
                   attn_out_g[0].reshape(1, -1), lru_out_g[0].reshape(1, -1),
                   wo, norm2_g[0].reshape(1, -1), wu, wd, final_g.reshape(1, -1))
    return out.reshape(1, SEQ, D_MODEL)
```

```python
import functools

import numpy as np
import jax
import jax.numpy as jnp
from jax import lax
from jax.experimental import pallas as pl
from jax.experimental.pallas import tpu as pltpu

F32 = jnp.float32
BF16 = jnp.bfloat16

D_MODEL = 1024
SEQ = 16384
ATTN_HEADS = 8
HEAD_DIM = 64
ATTN_WIDTH = ATTN_HEADS * HEAD_DIM
ROT_DIM = HEAD_DIM // 4
ROPE_THETA = 500000.0
LRU_WIDTH = D_MODEL - ATTN_WIDTH
LRU_BLOCKS = 8
CONV_WIDTH = 4
LRU_C = 8.0
IN_WIDTH = 3 * ATTN_WIDTH + 2 * LRU_WIDTH
D_FF = 4 * D_MODEL
EPS = 1e-6
WINDOW_STEPS = 128

NS = 16
RA = SEQ // NS
LANES = 128
SUBLANES = 8
MXU_TILE = 256
NSLAB = ATTN_WIDTH // LANES
NEG = -np.inf

VMEM_LIMIT = 56 * 1024 * 1024


def _rms(x, g):
    return x * lax.rsqrt(jnp.mean(x * x, axis=-1, keepdims=True) + EPS) * g


TM1 = 1024
PG = NS * NS
RG = PG // NS
NG1 = TM1 // PG
CH = SUBLANES
LOG2E = 1.4426950408889634


@functools.lru_cache(maxsize=None)
def _group_perm():
    n = np.arange(PG)
    p = np.zeros((PG, PG), np.float32)
    p[(n % NS) * RG + n // NS, n] = 1.0
    return p


def _inproj_body(x_ref, g_ref, w_ref, perm_ref, tile_ref, off_ref,
                 sm_ref, cm_ref, xl_ref, gl_ref, w_scr):
    k_ref, v_ref, q_ref = (sm_ref.at[n] for n in range(3))
    k1_ref, v1_ref, q1_ref = (cm_ref.at[n] for n in range(3))

    @pl.when(pl.program_id(0) == 0)
    def _():
        w_scr[...] = w_ref[...].astype(BF16)

    u = _rms(x_ref[...], g_ref[...]).astype(BF16)
    u = jnp.concatenate(
        [jnp.dot(perm_ref[...], u[g * PG:(g + 1) * PG], preferred_element_type=F32).astype(BF16)
         for g in range(NG1)], axis=0)
    z = jnp.dot(u, w_scr[...], preferred_element_type=F32)
    tile_cos, tile_sin = tile_ref[:, :LANES], tile_ref[:, LANES:]
    c = tile_cos * off_ref[0] - tile_sin * off_ref[1]
    s = tile_sin * off_ref[2] + tile_cos * off_ref[3]
    c = jnp.concatenate([c] * NSLAB, axis=1)
    s = jnp.concatenate([s] * NSLAB, axis=1)
    lane = lax.broadcasted_iota(jnp.int32, (1, ATTN_WIDTH), 1) % HEAD_DIM
    first_half = lane < ROT_DIM // 2

    def rope(t):
        up = pltpu.roll(t, ATTN_WIDTH - ROT_DIM // 2, axis=1)
        dn = pltpu.roll(t, ROT_DIM // 2, axis=1)
        return t * c + jnp.where(first_half, up, dn) * s

    q = rope(z[:, :ATTN_WIDTH]) * (HEAD_DIM ** -0.5 * LOG2E)
    k = rope(z[:, ATTN_WIDTH:2 * ATTN_WIDTH])
    v = z[:, 2 * ATTN_WIDTH:3 * ATTN_WIDTH]
    xl = z[:, 3 * ATTN_WIDTH:3 * ATTN_WIDTH + LRU_WIDTH]
    gl = z[:, 3 * ATTN_WIDTH + LRU_WIDTH:]
    for g in range(NG1):
        for r in range(NS):
            src = slice(g * PG + r * RG, g * PG + (r + 1) * RG)
            dst = slice(g * RG, (g + 1) * RG)
            for j in range(NSLAB):
                sl = slice(j * LANES, (j + 1) * LANES)
                q_ref[j, r, dst] = q[src, sl].astype(BF16)
                k_ref[j, r, dst] = k[src, sl].astype(BF16)
                v_ref[j, r, dst] = v[src, sl].astype(BF16)
            xl_ref[r, dst] = xl[src]
            gl_ref[r, dst] = gl[src]
    for g in range(NG1):
        for cl in range(RG // CH):
            for r in range(0, NS, 2):
                lo = g * PG + r * RG + cl * CH
                hi = lo + RG
                d0 = (g * (RG // CH) + cl) * NS * CH + r * CH
                dst = slice(d0, d0 + 2 * CH)
                for j in range(NSLAB):
                    sl = slice(j * LANES, (j + 1) * LANES)
                    for src, ref in ((q, q1_ref), (k, k1_ref), (v, v1_ref)):
                        ref[j, dst] = jnp.concatenate(
                            [src[lo:lo + CH, sl], src[hi:hi + CH, sl]], axis=0).astype(BF16)


def _inproj(x2, g, w):
    steps = SEQ // TM1
    rows = TM1 // NS
    slab_spec = pl.BlockSpec((3, NSLAB, NS, rows, LANES), lambda i: (0, 0, 0, i, 0))
    chunk_spec = pl.BlockSpec((3, NSLAB, TM1, LANES), lambda i: (0, 0, i, 0))
    row_spec = pl.BlockSpec((NS, rows, LRU_WIDTH), lambda i: (0, i, 0))
    rope_tiles, rope_offs = _rope_tables()
    slab_shape = jax.ShapeDtypeStruct((3, NSLAB, NS, RA, LANES), BF16)
    chunk_shape = jax.ShapeDtypeStruct((3, NSLAB, SEQ, LANES), BF16)
    row_shape = jax.ShapeDtypeStruct((NS, RA, LRU_WIDTH), F32)
    perm = jnp.asarray(_group_perm(), BF16)
    return pl.pallas_call(
        _inproj_body,
        grid=(steps,),
        in_specs=[
            pl.BlockSpec((TM1, D_MODEL), lambda i: (i, 0)),
            pl.BlockSpec((1, D_MODEL), lambda i: (0, 0)),
            pl.BlockSpec((D_MODEL, IN_WIDTH), lambda i: (0, 0), pipeline_mode=pl.Buffered(1)),
            pl.BlockSpec((PG, PG), lambda i: (0, 0)),
            pl.BlockSpec((None, 1, 2 * LANES), lambda i: (i, 0, 0)),
            pl.BlockSpec(rope_offs.shape, lambda i: (0, 0, 0)),
        ],
        out_specs=[slab_spec, chunk_spec] + [row_spec] * 2,
        out_shape=[slab_shape, chunk_shape] + [row_shape] * 2,
        scratch_shapes=[pltpu.VMEM((D_MODEL, IN_WIDTH), BF16)],
        compiler_params=pltpu.CompilerParams(
            dimension_semantics=("arbitrary",), vmem_limit_bytes=VMEM_LIMIT),
        name="inproj",
    )(x2, g, w, perm, rope_tiles, rope_offs)


TA = WINDOW_STEPS
UQ, UK = WINDOW_STEPS, 2 * WINDOW_STEPS
Q4, K4 = UQ // 4, UK // 4
AHEAD = 2
RING = 2 * AHEAD


@functools.lru_cache(maxsize=None)
def _attn_masks():
    def variants(diff, from_prev, per_unit):
        ok = (diff >= 0) & (diff <= WINDOW_STEPS)
        normal = np.where(ok, 0.0, NEG).astype(np.float32)
        first = np.where(ok & ~from_prev, 0.0, NEG).astype(np.float32)
        if per_unit:
            return np.stack([np.stack([first, normal]), np.stack([normal, normal])])
        return np.stack([first, normal])

    iq = np.arange(UQ)[:, None]
    ck = np.arange(UK)[None, :]
    m16 = variants(iq - ck + TA, np.broadcast_to(ck < TA, (UQ, UK)), False)
    cq, i4 = np.divmod(np.arange(UQ), Q4)
    ckk, j4 = np.divmod(np.arange(UK), K4)
    d4 = 4 * (i4[:, None] - j4[None, :] + Q4) + (cq[:, None] - ckk[None, :])
    m4 = variants(d4, np.broadcast_to(j4[None, :] < K4 - Q4, d4.shape), True)
    rq, i1 = np.divmod(np.arange(UQ), CH)
    kc, kin = np.divmod(np.arange(UK), UQ)
    rk, j1 = np.divmod(kin, CH)
    d1 = (NS * i1 + rq)[:, None] - (NS * j1 + rk + UQ * (kc - 1))[None, :]
    m1 = variants(d1, np.broadcast_to(kc[None, :] == 0, d1.shape), True)
    return m16, m4, m1


def _attn_body(sm_ref, smp_ref, cm_ref, cmp_ref,
               m16_ref, m4_ref, m1_ref, wo_ref, wu_ref, wd_ref,
               o_ref, wo_bf_ref, wu_bf_ref, wd_bf_ref,
               s_scr, acc_scr, max_scr, sum_scr):
    kc_ref, vc_ref, q_ref = (sm_ref.at[n] for n in range(3))
    kp_ref, vp_ref = smp_ref.at[0], smp_ref.at[1]
    k1c_ref, v1c_ref, q1_ref = (cm_ref.at[n] for n in range(3))
    k1p_ref, v1p_ref = cmp_ref.at[0], cmp_ref.at[1]

    wo_bf_ref[...] = wo_ref[...].astype(BF16)
    wu_bf_ref[...] = wu_ref[...].astype(BF16)
    wd_bf_ref[...] = wd_ref[...].astype(BF16)

    lane = lax.broadcasted_iota(jnp.int32, (1, LANES), 1)
    head0 = lane < HEAD_DIM
    zero = jnp.zeros((), BF16)

    def only_head(x, h):
        return jnp.where(head0, x, zero) if h == 0 else jnp.where(head0, zero, x)

    ones = jnp.ones((UK, LANES), BF16)

    def unit16(r):
        def store(o, mx, sm):
            acc_scr[0, r], max_scr[0, r], sum_scr[0, r] = o, mx, sm

        return (lambda h: only_head(q_ref[r], h),
                lambda: jnp.concatenate([kp_ref[r], kc_ref[r]], axis=0),
                lambda: jnp.concatenate([vp_ref[r], vc_ref[r]], axis=0),
                lambda: m16_ref[...], store)

    def unit4(r4, b):
        streams = [r4 + 4 * c for c in range(4)]
        q0, k0 = b * Q4, b * Q4 + Q4 - K4

        def window(cur, prev, st):
            if k0 < 0:
                return [prev[st, TA + k0:TA], cur[st, 0:k0 + K4]]
            return [cur[st, k0:k0 + K4]]

        def store(o, mx, sm):
            for c, st in enumerate(streams):
                rows = slice(c * Q4, (c + 1) * Q4)
                acc_scr[1, st, q0:q0 + Q4] = o[rows]
                max_scr[1, st, q0:q0 + Q4] = mx[rows]
                sum_scr[1, st, q0:q0 + Q4] = sm[rows]

        return (lambda h: only_head(
                    jnp.concatenate([q_ref[st, q0:q0 + Q4] for st in streams], axis=0), h),
                lambda: jnp.concatenate(
                    [x for st in streams for x in window(kc_ref, kp_ref, st)], axis=0),
                lambda: jnp.concatenate(
                    [x for st in streams for x in window(vc_ref, vp_ref, st)], axis=0),
                lambda: m4_ref[min(b, 1)], store)

    def unit1(u):
        off, row = u * UQ, u * CH

        def store(o, mx, sm):
            for r in range(NS):
                rows = slice(r * CH, (r + 1) * CH)
                acc_scr[2, r, row:row + CH] = o[rows]
                max_scr[2, r, row:row + CH] = mx[rows]
                sum_scr[2, r, row:row + CH] = sm[rows]

        def chunks(cur, prev):
            if u == 0:
                return jnp.concatenate([prev[...], cur[0:UQ]], axis=0)
            return cur[off - UQ:off + UQ]

        return (lambda h: only_head(q1_ref[off:off + UQ], h),
                lambda: chunks(k1c_ref, k1p_ref),
                lambda: chunks(v1c_ref, v1p_ref),
                lambda: m1_ref[min(u, 1)], store)

    all_units = ([unit16(n) for n in range(NS)]
                 + [unit4(n // (TA // Q4), n % (TA // Q4)) for n in range(NS)]
                 + [unit1(n) for n in range(NS)])

    def score_unit(n):
        q, k, _, _, _ = all_units[n]
        s_scr[n % RING] = lax.dot_general(
            jnp.concatenate([q(0), q(1)], axis=0), k(), (((1,), (1,)), ((), ())),
            preferred_element_type=F32)

    def value_unit(n):
        _, _, v, mask, store = all_units[n]
        ps, mx = [], []
        for h in range(2):
            s = s_scr[n % RING, h * UQ:(h + 1) * UQ] + mask()
            m = jnp.max(s, axis=-1, keepdims=True)
            ps.append(jnp.exp2(s - m).astype(BF16))
            mx.append(m)
        res = jnp.dot(jnp.concatenate(ps, axis=0), jnp.concatenate([v(), ones], axis=1),
                      preferred_element_type=F32)
        top, bot = res[:UQ], res[UQ:]
        store(jnp.where(head0, top[:, :LANES], bot[:, :LANES]),
              jnp.where(head0, mx[0], mx[1]),
              jnp.where(head0, top[:, LANES:], bot[:, LANES:]))

    for n in range(AHEAD):
        score_unit(n)
    for n in range(len(all_units)):
        if n + AHEAD < len(all_units):
            score_unit(n + AHEAD)
        value_unit(n)

    def comb(r, carry):
        m0, m1, m2 = max_scr[0, r], max_scr[1, r], max_scr[2, r]
        m = jnp.maximum(jnp.maximum(m0, m1), m2)
        w0, w1, w2 = jnp.exp2(m0 - m), jnp.exp2(m1 - m), jnp.exp2(m2 - m)
        num = w0 * acc_scr[0, r] + w1 * acc_scr[1, r] + w2 * acc_scr[2, r]
        den = w0 * sum_scr[0, r] + w1 * sum_scr[1, r] + w2 * sum_scr[2, r]
        o_ref[r] = num / den
        return carry

    lax.fori_loop(0, NS, comb, 0, unroll=4)


def _attention(sm, cm, later_weights):
    m16, m4, m1 = _attn_masks()
    cur = pl.BlockSpec((3, None, NS, TA, LANES), lambda j, i: (0, j, 0, i, 0))
    prev = pl.BlockSpec((2, None, NS, TA, LANES),
                        lambda j, i: (0, j, 0, jnp.maximum(i - 1, 0), 0))
    cur1 = pl.BlockSpec((3, None, NS * TA, LANES), lambda j, i: (0, j, i, 0))
    prev1 = pl.BlockSpec((2, None, UQ, LANES),
                         lambda j, i: (0, j, jnp.maximum(i * (NS * TA // UQ) - 1, 0), 0))

    def tile_kind(m):
        return pl.BlockSpec((None,) + m.shape[1:],
                            lambda j, i: (jnp.minimum(i, 1),) + (0,) * (m.ndim - 1))

    n_tiles = RA // TA
    steps = NSLAB * n_tiles
    cast_specs = [pl.BlockSpec((a.shape[0] // steps, a.shape[1]), lambda j, i: (j * n_tiles + i, 0))
                  for a in later_weights]
    cast_shapes = [jax.ShapeDtypeStruct(a.shape, BF16) for a in later_weights]
    return pl.pallas_call(
        _attn_body,
        grid=(NSLAB, n_tiles),
        in_specs=[cur, prev, cur1, prev1,
                  tile_kind(m16), tile_kind(m4), tile_kind(m1)] + cast_specs,
        out_specs=[pl.BlockSpec((None, NS, TA, LANES), lambda j, i: (j, 0, i, 0))] + cast_specs,
        out_shape=[jax.ShapeDtypeStruct((NSLAB, NS, RA, LANES), F32)] + cast_shapes,
        scratch_shapes=[
            pltpu.VMEM((RING, 2 * UQ, UK), F32),
            pltpu.VMEM((3, NS, TA, LANES), F32),
            pltpu.VMEM((3, NS, TA, LANES), F32),
            pltpu.VMEM((3, NS, TA, LANES), F32),
        ],
        compiler_params=pltpu.CompilerParams(
            dimension_semantics=("arbitrary", "arbitrary"), vmem_limit_bytes=VMEM_LIMIT),
        name="dilated_attn",
    )(sm, sm, cm, cm, m16, m4, m1, *later_weights)


TL = 32


def _lru_phases(xl_ref, gl_ref, cw_ref, cb_ref, wg_ref, bg_ref, lam_ref, y_ref,
                tail_scr, carry_scr, xc_scr, g_scr, pl_scr, hl_scr):
    row = lax.broadcasted_iota(jnp.int32, (TL, LRU_WIDTH), 0)
    state = {}

    def conv(after=0.0):
        cw = cw_ref[...]
        cb = cb_ref[...] + after

        def tap(r, k):
            st = r - k
            if st >= 0:
                return xl_ref[st]
            st += NS
            prev_last = tail_scr[st - (NS - CONV_WIDTH + 1), SUBLANES - 1:SUBLANES, :]
            return jnp.where(row == 0, prev_last, pltpu.roll(xl_ref[st], 1, axis=0))

        for r in range(NS):
            xc_scr[r] = cb + sum(cw[CONV_WIDTH - 1 - k:CONV_WIDTH - k] * tap(r, k)
                                 for k in range(CONV_WIDTH))
        for n in range(CONV_WIDTH - 1):
            tail_scr[n] = xl_ref[NS - CONV_WIDTH + 1 + n, TL - SUBLANES:TL, :]

    def gates(after=0.0):
        xc_all = xc_scr[...].reshape(NS * TL, LRU_WIDTH).astype(BF16)
        bias = bg_ref[...] + after
        for t in range(LRU_WIDTH // MXU_TILE):
            cols = slice(t * MXU_TILE, (t + 1) * MXU_TILE)
            res = jnp.dot(xc_all[:, cols], wg_ref[t], preferred_element_type=F32)
            for gate in range(2):
                dst = slice(gate * LRU_WIDTH + t * MXU_TILE, gate * LRU_WIDTH + (t + 1) * MXU_TILE)
                g_scr[:, dst] = res[:, gate * MXU_TILE:(gate + 1) * MXU_TILE] + bias[:, dst]

    def local_scan(streams):
        def run(after=0.0):
            neg_lam = after - lam_ref[...]
            softplus = jnp.maximum(neg_lam, 0.0) + jnp.log1p(jnp.exp(-jnp.abs(neg_lam)))
            p_run, h_run = state.get("run", (None, None))
            for r in streams:
                g = g_scr[r * TL:(r + 1) * TL]
                rg = jax.nn.sigmoid(g[:, :LRU_WIDTH])
                ig = jax.nn.sigmoid(g[:, LRU_WIDTH:])
                log_a = -LRU_C * rg * softplus
                a = jnp.exp(log_a)
                th = jnp.tanh(log_a)
                bx = jnp.sqrt(-2.0 * th / (1.0 - th)) * (ig * xc_scr[r])
                if p_run is None:
                    p_run, h_run = a, bx
                else:
                    h_run = a * h_run + bx
                    p_run = a * p_run
                pl_scr[r] = p_run
                hl_scr[r] = h_run
            state["run"] = (p_run, h_run)
        return run

    def row_scan():
        pa, hb = state["run"]
        sft = 1
        while sft < TL:
            keep = row >= sft
            pa_s = jnp.where(keep, pltpu.roll(pa, sft, axis=0), 1.0)
            hb_s = jnp.where(keep, pltpu.roll(hb, sft, axis=0), 0.0)
            hb = pa * hb_s + hb
            pa = pa * pa_s
            sft *= 2
        carry = carry_scr[0:1, :]
        e = pa * carry + hb
        state["e_prev"] = jnp.where(row == 0, carry, pltpu.roll(e, 1, axis=0))
        carry_scr[...] = jnp.broadcast_to(e[TL - 1:TL, :], carry_scr.shape)

    def finalize(after=0.0):
        e_prev = state["e_prev"] + after
        for r in range(NS):
            h = pl_scr[r] * e_prev + hl_scr[r]
            y_ref[r] = jax.nn.gelu(gl_ref[r], approximate=True) * h

    half = NS // 2

    def second_half(after=0.0):
        local_scan(range(half, NS))(after)
        row_scan()

    return [(conv, LRU_WIDTH), (gates, 2 * LRU_WIDTH), (local_scan(range(half)), LRU_WIDTH),
            (second_half, LRU_WIDTH), (finalize, LRU_WIDTH)]


TM4 = NS * TL
NG4 = TM4 // PG
FF_CHUNK = 1024


def _out_body(x_ref, at_ref, xl0_ref, gl0_ref, xln_ref, gln_ref, zero_ref,
              cw_ref, cb_ref, wg_ref, bg_ref, lam_ref,
              perm_ref, ga_ref, gl_ref, wo_ref, g2_ref, wu_ref, wd_ref, gf_ref,
              o_ref,
              lr_scr, tail_scr, carry_scr, xc_scr, g_scr, pl_scr, hl_scr):
    lru_params = (cw_ref, cb_ref, wg_ref, bg_ref, lam_ref)
    lru_state = (tail_scr, carry_scr, xc_scr, g_scr, pl_scr, hl_scr)

    @pl.when(pl.program_id(0) == 0)
    def _():
        tail_scr[...] = jnp.zeros_like(tail_scr)
        carry_scr[...] = jnp.zeros_like(carry_scr)
        for phase, _ in _lru_phases(xl0_ref, gl0_ref, *lru_params, lr_scr, *lru_state):
            phase()

    groups = []
    for g in range(NG4):
        rows = slice(g * RG, (g + 1) * RG)
        attn = jnp.concatenate(
            [jnp.concatenate([at_ref[j, r, rows] for j in range(NSLAB)], axis=1) for r in range(NS)],
            axis=0)
        lru = jnp.concatenate([lr_scr[r, rows] for r in range(NS)], axis=0)
        mixed = jnp.concatenate([_rms(attn, ga_ref[...]), _rms(lru, gl_ref[...])], axis=1)
        groups.append(jnp.dot(perm_ref[...], mixed.astype(BF16),
                              preferred_element_type=F32).astype(BF16))
    mixed = jnp.concatenate(groups, axis=0)
    lru_next = _lru_phases(xln_ref, gln_ref, *lru_params, lr_scr, *lru_state)
    n_chunks = D_FF // FF_CHUNK
    assert len(lru_next) <= n_chunks + 1

    def zero_after(val, width):
        bits = pltpu.bitcast(val[0:1, :width], jnp.int32) & zero_ref[0:1, :width]
        return pltpu.bitcast(bits, F32)

    h = x_ref[...] + jnp.dot(mixed, wo_ref[...], preferred_element_type=F32)
    u = _rms(h, g2_ref[...]).astype(BF16)
    acc = h
    phase, width = lru_next.pop(0)
    phase(zero_after(h, width))
    for c in range(n_chunks):
        if lru_next:
            phase, width = lru_next.pop(0)
            phase(zero_after(acc, width))
        sl = slice(c * FF_CHUNK, (c + 1) * FF_CHUNK)
        f = jnp.dot(u, wu_ref[:, sl], preferred_element_type=F32)
        f = jnp.square(jnp.maximum(f, 0.0)).astype(BF16)
        acc = acc + jnp.dot(f, wd_ref[sl, :], preferred_element_type=F32)
    o_ref[...] = _rms(acc, gf_ref[...])


def _out_mlp(x2, attn, xl, gl_in, lru_params, ga, gl, wo, g2, wu, wd, gf):
    def const(a):
        return pl.BlockSpec(a.shape, lambda i: (0,) * a.ndim, pipeline_mode=pl.Buffered(1))

    n_tiles = SEQ // TM4
    xspec = pl.BlockSpec((TM4, D_MODEL), lambda i: (i, 0))
    first = pl.BlockSpec((NS, TL, LRU_WIDTH), lambda i: (0, 0, 0), pipeline_mode=pl.Buffered(1))
    ahead = pl.BlockSpec((NS, TL, LRU_WIDTH), lambda i: (0, jnp.minimum(i + 1, n_tiles - 1), 0))
    perm = jnp.asarray(_group_perm(), BF16)
    zero = jnp.zeros((SUBLANES, D_MODEL), jnp.int32)
    consts = (zero,) + tuple(lru_params) + (perm, ga, gl, wo, g2, wu, wd, gf)
    return pl.pallas_call(
        _out_body,
        grid=(n_tiles,),
        in_specs=[
            xspec,
            pl.BlockSpec((NSLAB, NS, TL, LANES), lambda i: (0, 0, i, 0)),
            first, first, ahead, ahead,
        ] + [const(a) for a in consts],
        out_specs=xspec,
        out_shape=jax.ShapeDtypeStruct((SEQ, D_MODEL), F32),
        scratch_shapes=[
            pltpu.VMEM((NS, TL, LRU_WIDTH), F32),
            pltpu.VMEM((CONV_WIDTH - 1, SUBLANES, LRU_WIDTH), F32),
            pltpu.VMEM((SUBLANES, LRU_WIDTH), F32),
            pltpu.VMEM((NS, TL, LRU_WIDTH), F32),
            pltpu.VMEM((NS * TL, 2 * LRU_WIDTH), F32),
            pltpu.VMEM((NS, TL, LRU_WIDTH), F32),
            pltpu.VMEM((NS, TL, LRU_WIDTH), F32),
        ],
        compiler_params=pltpu.CompilerParams(
            dimension_semantics=("arbitrary",), vmem_limit_bytes=VMEM_LIMIT),
        name="out_mlp",
    )(x2, attn, xl, gl_in, xl, gl_in, *consts)


@functools.lru_cache(maxsize=None)
def _rope_tables():
    inv_freq = ROPE_THETA ** (-np.arange(0, ROT_DIM, 2, dtype=np.float64) / ROT_DIM)
    pad = HEAD_DIM - ROT_DIM
    reps = LANES // HEAD_DIM

    def lanes(rot_first, rot_second, rest, n):
        head = np.concatenate([rot_first, rot_second, np.full((n, pad), rest)], axis=1)
        return np.tile(head, (1, reps))

    tile_ang = (TM1 * np.arange(SEQ // TM1, dtype=np.float64))[:, None] * inv_freq[None, :]
    tc, ts = np.cos(tile_ang), np.sin(tile_ang)
    tiles = np.concatenate([lanes(tc, tc, 1.0, len(tc)), lanes(ts, ts, 0.0, len(ts))], axis=1)
    off_ang = np.arange(TM1, dtype=np.float64)[:, None] * inv_freq[None, :]
    oc, osn = np.cos(off_ang), np.sin(off_ang)
    offs = np.stack([lanes(oc, oc, 1.0, TM1), lanes(osn, osn, 0.0, TM1),
                     lanes(-oc, oc, 0.0, TM1), lanes(-osn, osn, 0.0, TM1)])
    offs = offs.reshape(4, NG1, RG, NS, LANES).transpose(0, 1, 3, 2, 4).reshape(4, TM1, LANES)
    return (tiles.astype(np.float32)[:, None, :], np.ascontiguousarray(offs.astype(np.float32)))


def _block_diag(w):
    n, b, _ = w.shape
    eye = jnp.eye(n, dtype=w.dtype)
    return (eye[:, None, :, None] * w[:, :, None, :]).reshape(n * b, n * b)


def _gate_tiles(w_r, w_i):
    per = MXU_TILE // (LRU_WIDTH // LRU_BLOCKS)
    tiles = [jnp.concatenate([_block_diag(w_r[t * per:(t + 1) * per]),
                              _block_diag(w_i[t * per:(t + 1) * per])], axis=1)
             for t in range(LRU_WIDTH // MXU_TILE)]
    return jnp.stack(tiles).astype(BF16)


def kernel(x, norm1_g, w_in, conv_w, conv_b, w_rgate, b_rgate, w_igate, b_igate, lru_lambda,
           attn_out_g, lru_out_g, w_out, norm2_g, w_mlp_up, w_mlp_down, final_g):
    assert x.shape == (1, SEQ, D_MODEL) and w_in.shape[0] == 1
    x2 = x.reshape(SEQ, D_MODEL)
    sm, cm, xl, gl = _inproj(x2, norm1_g.reshape(1, D_MODEL), w_in[0])
    attn, wo, wu, wd = _attention(sm, cm, (w_out[0], w_mlp_up[0], w_mlp_down[0]))
    w_gate = _gate_tiles(w_rgate[0], w_igate[0])
    b_gate = jnp.concatenate([b_rgate[0].reshape(1, -1), b_igate[0].reshape(1, -1)], axis=1)
    lru_params = (conv_w[0], conv_b[0].reshape(1, -1), w_gate, b_gate, lru_lambda[0].reshape(1, -1))
    out = _out_mlp(x2, attn, xl, gl, lru_params,
                   attn_out_g[0].reshape(1, -1), lru_out_g[0].reshape(1, -1),
                   wo, norm2_g[0].reshape(1, -1), wu, wd, final_g.reshape(1, -1))
    return out.reshape(1, SEQ, D_MODEL)
```

```python
import functools

import numpy as np
import jax
import jax.numpy as jnp
from jax import lax
from jax.experimental import pallas as pl
from jax.experimental.pallas import tpu as pltpu

F32 = jnp.float32
BF16 = jnp.bfloat16

D_MODEL = 1024
SEQ = 16384
ATTN_HEADS = 8
HEAD_DIM = 64
ATTN_WIDTH = ATTN_HEADS * HEAD_DIM
ROT_DIM = HEAD_DIM // 4
ROPE_THETA = 500000.0
LRU_WIDTH = D_MODEL - ATTN_WIDTH
LRU_BLOCKS = 8
CONV_WIDTH = 4
LRU_C = 8.0
IN_WIDTH = 3 * ATTN_WIDTH + 2 * LRU_WIDTH
D_FF = 4 * D_MODEL
EPS = 1e-6
WINDOW_STEPS = 128

NS = 16
RA = SEQ // NS
LANES = 128
SUBLANES = 8
MXU_TILE = 256
NSLAB = ATTN_WIDTH // LANES
NEG = -np.inf

VMEM_LIMIT = 56 * 1024 * 1024


def _rms(x, g):
    return x * lax.rsqrt(jnp.mean(x * x, axis=-1, keepdims=True) + EPS) * g


TM1 = 1024
PG = NS * NS
RG = PG // NS
NG1 = TM1 // PG
CH = SUBLANES
LOG2E = 1.4426950408889634


@functools.lru_cache(maxsize=None)
def _group_perm():
    n = np.arange(PG)
    p = np.zeros((PG, PG), np.float32)
    p[(n % NS) * RG + n // NS, n] = 1.0
    return p


def _inproj_body(x_ref, g_ref, w_ref, perm_ref, tile_ref, off_ref,
                 sm_ref, cm_ref, xg_ref, w_scr):
    xl_ref, gl_ref = xg_ref.at[0], xg_ref.at[1]
    k_ref, v_ref, q_ref = (sm_ref.at[n] for n in range(3))
    k1_ref, v1_ref, q1_ref = (cm_ref.at[n] for n in range(3))

    @pl.when(pl.program_id(0) == 0)
    def _():
        w_scr[...] = w_ref[...].astype(BF16)

    u = _rms(x_ref[...], g_ref[...]).astype(BF16)
    u = jnp.concatenate(
        [jnp.dot(perm_ref[...], u[g * PG:(g + 1) * PG], preferred_element_type=F32).astype(BF16)
         for g in range(NG1)], axis=0)
    z = jnp.dot(u, w_scr[...], preferred_element_type=F32)
    tile_cos, tile_sin = tile_ref[:, :LANES], tile_ref[:, LANES:]
    c = tile_cos * off_ref[0] - tile_sin * off_ref[1]
    s = tile_sin * off_ref[2] + tile_cos * off_ref[3]
    c = jnp.concatenate([c] * NSLAB, axis=1)
    s = jnp.concatenate([s] * NSLAB, axis=1)
    lane = lax.broadcasted_iota(jnp.int32, (1, ATTN_WIDTH), 1) % HEAD_DIM
    first_half = lane < ROT_DIM // 2

    def rope(t):
        up = pltpu.roll(t, ATTN_WIDTH - ROT_DIM // 2, axis=1)
        dn = pltpu.roll(t, ROT_DIM // 2, axis=1)
        return t * c + jnp.where(first_half, up, dn) * s

    q = rope(z[:, :ATTN_WIDTH]) * (HEAD_DIM ** -0.5 * LOG2E)
    k = rope(z[:, ATTN_WIDTH:2 * ATTN_WIDTH])
    v = z[:, 2 * ATTN_WIDTH:3 * ATTN_WIDTH]
    xl = z[:, 3 * ATTN_WIDTH:3 * ATTN_WIDTH + LRU_WIDTH]
    gl = z[:, 3 * ATTN_WIDTH + LRU_WIDTH:]
    for g in range(NG1):
        for r in range(NS):
            src = slice(g * PG + r * RG, g * PG + (r + 1) * RG)
            dst = slice(g * RG, (g + 1) * RG)
            for j in range(NSLAB):
                sl = slice(j * LANES, (j + 1) * LANES)
                q_ref[j, r, dst] = q[src, sl].astype(BF16)
                k_ref[j, r, dst] = k[src, sl].astype(BF16)
                v_ref[j, r, dst] = v[src, sl].astype(BF16)
            xl_ref[r, dst] = xl[src]
            gl_ref[r, dst] = gl[src]
    for g in range(NG1):
        for cl in range(RG // CH):
            for r in range(0, NS, 2):
                lo = g * PG + r * RG + cl * CH
                hi = lo + RG
                d0 = (g * (RG // CH) + cl) * NS * CH + r * CH
                dst = slice(d0, d0 + 2 * CH)
                for j in range(NSLAB):
                    sl = slice(j * LANES, (j + 1) * LANES)
                    for src, ref in ((q, q1_ref), (k, k1_ref), (v, v1_ref)):
                        ref[j, dst] = jnp.concatenate(
                            [src[lo:lo + CH, sl], src[hi:hi + CH, sl]], axis=0).astype(BF16)


def _inproj(x2, g, w):
    steps = SEQ // TM1
    rows = TM1 // NS
    slab_spec = pl.BlockSpec((3, NSLAB, NS, rows, LANES), lambda i: (0, 0, 0, i, 0))
    chunk_spec = pl.BlockSpec((3, NSLAB, TM1, LANES), lambda i: (0, 0, i, 0))
    row_spec = pl.BlockSpec((2, NS, rows, LRU_WIDTH), lambda i: (0, 0, i, 0))
    rope_tiles, rope_offs = _rope_tables()
    slab_shape = jax.ShapeDtypeStruct((3, NSLAB, NS, RA, LANES), BF16)
    chunk_shape = jax.ShapeDtypeStruct((3, NSLAB, SEQ, LANES), BF16)
    row_shape = jax.ShapeDtypeStruct((2, NS, RA, LRU_WIDTH), F32)
    perm = jnp.asarray(_group_perm(), BF16)
    return pl.pallas_call(
        _inproj_body,
        grid=(steps,),
        in_specs=[
            pl.BlockSpec((TM1, D_MODEL), lambda i: (i, 0)),
            pl.BlockSpec((1, D_MODEL), lambda i: (0, 0)),
            pl.BlockSpec((D_MODEL, IN_WIDTH), lambda i: (0, 0), pipeline_mode=pl.Buffered(1)),
            pl.BlockSpec((PG, PG), lambda i: (0, 0)),
            pl.BlockSpec((None, 1, 2 * LANES), lambda i: (i, 0, 0)),
            pl.BlockSpec(rope_offs.shape, lambda i: (0, 0, 0)),
        ],
        out_specs=[slab_spec, chunk_spec, row_spec],
        out_shape=[slab_shape, chunk_shape, row_shape],
        scratch_shapes=[pltpu.VMEM((D_MODEL, IN_WIDTH), BF16)],
        compiler_params=pltpu.CompilerParams(
            dimension_semantics=("arbitrary",), vmem_limit_bytes=VMEM_LIMIT),
        name="inproj",
    )(x2, g, w, perm, rope_tiles, rope_offs)


TA = WINDOW_STEPS
UQ, UK = WINDOW_STEPS, 2 * WINDOW_STEPS
Q4, K4 = UQ // 4, UK // 4
AHEAD = 2
RING = 2 * AHEAD


@functools.lru_cache(maxsize=None)
def _attn_masks():
    def variants(diff, from_prev, per_unit):
        ok = (diff >= 0) & (diff <= WINDOW_STEPS)
        normal = np.where(ok, 0.0, NEG).astype(np.float32)
        first = np.where(ok & ~from_prev, 0.0, NEG).astype(np.float32)
        if per_unit:
            return np.stack([np.stack([first, normal]), np.stack([normal, normal])])
        return np.stack([first, normal])

    iq = np.arange(UQ)[:, None]
    ck = np.arange(UK)[None, :]
    m16 = variants(iq - ck + TA, np.broadcast_to(ck < TA, (UQ, UK)), False)
    cq, i4 = np.divmod(np.arange(UQ), Q4)
    ckk, j4 = np.divmod(np.arange(UK), K4)
    d4 = 4 * (i4[:, None] - j4[None, :] + Q4) + (cq[:, None] - ckk[None, :])
    m4 = variants(d4, np.broadcast_to(j4[None, :] < K4 - Q4, d4.shape), True)
    rq, i1 = np.divmod(np.arange(UQ), CH)
    kc, kin = np.divmod(np.arange(UK), UQ)
    rk, j1 = np.divmod(kin, CH)
    d1 = (NS * i1 + rq)[:, None] - (NS * j1 + rk + UQ * (kc - 1))[None, :]
    m1 = variants(d1, np.broadcast_to(kc[None, :] == 0, d1.shape), True)
    return m16, m4, m1


def _attn_body(sm_ref, smp_ref, cm_ref, cmp_ref,
               m16_ref, m4_ref, m1_ref, wo_ref, wu_ref, wd_ref,
               o_ref, wo_bf_ref, wu_bf_ref, wd_bf_ref,
               s_scr, acc_scr, max_scr, sum_scr):
    kc_ref, vc_ref, q_ref = (sm_ref.at[n] for n in range(3))
    kp_ref, vp_ref = smp_ref.at[0], smp_ref.at[1]
    k1c_ref, v1c_ref, q1_ref = (cm_ref.at[n] for n in range(3))
    k1p_ref, v1p_ref = cmp_ref.at[0], cmp_ref.at[1]

    wo_bf_ref[...] = wo_ref[...].astype(BF16)
    wu_bf_ref[...] = wu_ref[...].astype(BF16)
    wd_bf_ref[...] = wd_ref[...].astype(BF16)

    lane = lax.broadcasted_iota(jnp.int32, (1, LANES), 1)
    head0 = lane < HEAD_DIM
    zero = jnp.zeros((), BF16)

    def only_head(x, h):
        return jnp.where(head0, x, zero) if h == 0 else jnp.where(head0, zero, x)

    ones = jnp.ones((UK, LANES), BF16)

    def unit16(r):
        def store(o, mx, sm):
            acc_scr[0, r], max_scr[0, r], sum_scr[0, r] = o, mx, sm

        return (lambda h: only_head(q_ref[r], h),
                lambda: jnp.concatenate([kp_ref[r], kc_ref[r]], axis=0),
                lambda: jnp.concatenate([vp_ref[r], vc_ref[r]], axis=0),
                lambda: m16_ref[...], store)

    def unit4(r4, b):
        streams = [r4 + 4 * c for c in range(4)]
        q0, k0 = b * Q4, b * Q4 + Q4 - K4

        def window(cur, prev, st):
            if k0 < 0:
                return [prev[st, TA + k0:TA], cur[st, 0:k0 + K4]]
            return [cur[st, k0:k0 + K4]]

        def store(o, mx, sm):
            for c, st in enumerate(streams):
                rows = slice(c * Q4, (c + 1) * Q4)
                acc_scr[1, st, q0:q0 + Q4] = o[rows]
                max_scr[1, st, q0:q0 + Q4] = mx[rows]
                sum_scr[1, st, q0:q0 + Q4] = sm[rows]

        return (lambda h: only_head(
                    jnp.concatenate([q_ref[st, q0:q0 + Q4] for st in streams], axis=0), h),
                lambda: jnp.concatenate(
                    [x for st in streams for x in window(kc_ref, kp_ref, st)], axis=0),
                lambda: jnp.concatenate(
                    [x for st in streams for x in window(vc_ref, vp_ref, st)], axis=0),
                lambda: m4_ref[min(b, 1)], store)

    def unit1(u):
        off, row = u * UQ, u * CH

        def store(o, mx, sm):
            for r in range(NS):
                rows = slice(r * CH, (r + 1) * CH)
                acc_scr[2, r, row:row + CH] = o[rows]
                max_scr[2, r, row:row + CH] = mx[rows]
                sum_scr[2, r, row:row + CH] = sm[rows]

        def chunks(cur, prev):
            if u == 0:
                return jnp.concatenate([prev[...], cur[0:UQ]], axis=0)
            return cur[off - UQ:off + UQ]

        return (lambda h: only_head(q1_ref[off:off + UQ], h),
                lambda: chunks(k1c_ref, k1p_ref),
                lambda: chunks(v1c_ref, v1p_ref),
                lambda: m1_ref[min(u, 1)], store)

    all_units = ([unit16(n) for n in range(NS)]
                 + [unit4(n // (TA // Q4), n % (TA // Q4)) for n in range(NS)]
                 + [unit1(n) for n in range(NS)])

    def score_unit(n):
        q, k, _, _, _ = all_units[n]
        s_scr[n % RING] = lax.dot_general(
            jnp.concatenate([q(0), q(1)], axis=0), k(), (((1,), (1,)), ((), ())),
            preferred_element_type=F32)

    def value_unit(n):
        _, _, v, mask, store = all_units[n]
        ps, mx = [], []
        for h in range(2):
            s = s_scr[n % RING, h * UQ:(h + 1) * UQ] + mask()
            m = jnp.max(s, axis=-1, keepdims=True)
            ps.append(jnp.exp2(s - m).astype(BF16))
            mx.append(m)
        res = jnp.dot(jnp.concatenate(ps, axis=0), jnp.concatenate([v(), ones], axis=1),
                      preferred_element_type=F32)
        top, bot = res[:UQ], res[UQ:]
        store(jnp.where(head0, top[:, :LANES], bot[:, :LANES]),
              jnp.where(head0, mx[0], mx[1]),
              jnp.where(head0, top[:, LANES:], bot[:, LANES:]))

    for n in range(AHEAD):
        score_unit(n)
    for n in range(len(all_units)):
        if n + AHEAD < len(all_units):
            score_unit(n + AHEAD)
        value_unit(n)

    def comb(r, carry):
        m0, m1, m2 = max_scr[0, r], max_scr[1, r], max_scr[2, r]
        m = jnp.maximum(jnp.maximum(m0, m1), m2)
        w0, w1, w2 = jnp.exp2(m0 - m), jnp.exp2(m1 - m), jnp.exp2(m2 - m)
        num = w0 * acc_scr[0, r] + w1 * acc_scr[1, r] + w2 * acc_scr[2, r]
        den = w0 * sum_scr[0, r] + w1 * sum_scr[1, r] + w2 * sum_scr[2, r]
        o_ref[r] = num / den
        return carry

    lax.fori_loop(0, NS, comb, 0, unroll=4)


def _attention(sm, cm, later_weights):
    m16, m4, m1 = _attn_masks()
    cur = pl.BlockSpec((3, None, NS, TA, LANES), lambda j, i: (0, j, 0, i, 0))
    prev = pl.BlockSpec((2, None, NS, TA, LANES),
                        lambda j, i: (0, j, 0, jnp.maximum(i - 1, 0), 0))
    cur1 = pl.BlockSpec((3, None, NS * TA, LANES), lambda j, i: (0, j, i, 0))
    prev1 = pl.BlockSpec((2, None, UQ, LANES),
                         lambda j, i: (0, j, jnp.maximum(i * (NS * TA // UQ) - 1, 0), 0))

    def tile_kind(m):
        return pl.BlockSpec((None,) + m.shape[1:],
                            lambda j, i: (jnp.minimum(i, 1),) + (0,) * (m.ndim - 1))

    n_tiles = RA // TA
    steps = NSLAB * n_tiles
    cast_specs = [pl.BlockSpec((a.shape[0] // steps, a.shape[1]), lambda j, i: (j * n_tiles + i, 0))
                  for a in later_weights]
    cast_shapes = [jax.ShapeDtypeStruct(a.shape, BF16) for a in later_weights]
    return pl.pallas_call(
        _attn_body,
        grid=(NSLAB, n_tiles),
        in_specs=[cur, prev, cur1, prev1,
                  tile_kind(m16), tile_kind(m4), tile_kind(m1)] + cast_specs,
        out_specs=[pl.BlockSpec((None, NS, TA, LANES), lambda j, i: (j, 0, i, 0))] + cast_specs,
        out_shape=[jax.ShapeDtypeStruct((NSLAB, NS, RA, LANES), F32)] + cast_shapes,
        scratch_shapes=[
            pltpu.VMEM((RING, 2 * UQ, UK), F32),
            pltpu.VMEM((3, NS, TA, LANES), F32),
            pltpu.VMEM((3, NS, TA, LANES), F32),
            pltpu.VMEM((3, NS, TA, LANES), F32),
        ],
        compiler_params=pltpu.CompilerParams(
            dimension_semantics=("arbitrary", "arbitrary"), vmem_limit_bytes=VMEM_LIMIT),
        name="dilated_attn",
    )(sm, sm, cm, cm, m16, m4, m1, *later_weights)


TL = 32


def _lru_phases(xl_ref, gl_ref, cw_ref, cb_ref, wg_ref, bg_ref, lam_ref, y_ref,
                tail_scr, carry_scr, xc_scr, g_scr, pl_scr, hl_scr):
    row = lax.broadcasted_iota(jnp.int32, (TL, LRU_WIDTH), 0)
    state = {}

    def conv(after=0.0):
        cw = cw_ref[...]
        cb = cb_ref[...] + after

        def tap(r, k):
            st = r - k
            if st >= 0:
                return xl_ref[st]
            st += NS
            prev_last = tail_scr[st - (NS - CONV_WIDTH + 1), SUBLANES - 1:SUBLANES, :]
            return jnp.where(row == 0, prev_last, pltpu.roll(xl_ref[st], 1, axis=0))

        for r in range(NS):
            xc_scr[r] = cb + sum(cw[CONV_WIDTH - 1 - k:CONV_WIDTH - k] * tap(r, k)
                                 for k in range(CONV_WIDTH))
        for n in range(CONV_WIDTH - 1):
            tail_scr[n] = xl_ref[NS - CONV_WIDTH + 1 + n, TL - SUBLANES:TL, :]

    def gates(after=0.0):
        xc_all = xc_scr[...].reshape(NS * TL, LRU_WIDTH).astype(BF16)
        bias = bg_ref[...] + after
        for t in range(LRU_WIDTH // MXU_TILE):
            cols = slice(t * MXU_TILE, (t + 1) * MXU_TILE)
            res = jnp.dot(xc_all[:, cols], wg_ref[t], preferred_element_type=F32)
            for gate in range(2):
                dst = slice(gate * LRU_WIDTH + t * MXU_TILE, gate * LRU_WIDTH + (t + 1) * MXU_TILE)
                g_scr[:, dst] = res[:, gate * MXU_TILE:(gate + 1) * MXU_TILE] + bias[:, dst]

    def local_scan(streams):
        def run(after=0.0):
            neg_lam = after - lam_ref[...]
            softplus = jnp.maximum(neg_lam, 0.0) + jnp.log1p(jnp.exp(-jnp.abs(neg_lam)))
            p_run, h_run = state.get("run", (None, None))
            for r in streams:
                g = g_scr[r * TL:(r + 1) * TL]
                rg = jax.nn.sigmoid(g[:, :LRU_WIDTH])
                ig = jax.nn.sigmoid(g[:, LRU_WIDTH:])
                log_a = -LRU_C * rg * softplus
                a = jnp.exp(log_a)
                th = jnp.tanh(log_a)
                bx = jnp.sqrt(-2.0 * th / (1.0 - th)) * (ig * xc_scr[r])
                if p_run is None:
                    p_run, h_run = a, bx
                else:
                    h_run = a * h_run + bx
                    p_run = a * p_run
                pl_scr[r] = p_run
                hl_scr[r] = h_run
            state["run"] = (p_run, h_run)
        return run

    def row_scan():
        pa, hb = state["run"]
        sft = 1
        while sft < TL:
            keep = row >= sft
            pa_s = jnp.where(keep, pltpu.roll(pa, sft, axis=0), 1.0)
            hb_s = jnp.where(keep, pltpu.roll(hb, sft, axis=0), 0.0)
            hb = pa * hb_s + hb
            pa = pa * pa_s
            sft *= 2
        carry = carry_scr[0:1, :]
        e = pa * carry + hb
        state["e_prev"] = jnp.where(row == 0, carry, pltpu.roll(e, 1, axis=0))
        carry_scr[...] = jnp.broadcast_to(e[TL - 1:TL, :], carry_scr.shape)

    def finalize(after=0.0):
        e_prev = state["e_prev"] + after
        for r in range(NS):
            h = pl_scr[r] * e_prev + hl_scr[r]
            y_ref[r] = jax.nn.gelu(gl_ref[r], approximate=True) * h

    half = NS // 2

    def second_half(after=0.0):
        local_scan(range(half, NS))(after)
        row_scan()

    return [(conv, LRU_WIDTH), (gates, 2 * LRU_WIDTH), (local_scan(range(half)), LRU_WIDTH),
            (second_half, LRU_WIDTH), (finalize, LRU_WIDTH)]


TM4 = NS * TL
NG4 = TM4 // PG
FF_CHUNK = 1024


def _out_body(x_ref, at_ref, xg0_ref, xgn_ref, zero_ref,
              cw_ref, cb_ref, wg_ref, bg_ref, lam_ref,
              perm_ref, ga_ref, gl_ref, wo_ref, g2_ref, wu_ref, wd_ref, gf_ref,
              o_ref,
              lr_scr, tail_scr, carry_scr, xc_scr, g_scr, pl_scr, hl_scr):
    xl0_ref, gl0_ref = xg0_ref.at[0], xg0_ref.at[1]
    xln_ref, gln_ref = xgn_ref.at[0], xgn_ref.at[1]
    lru_params = (cw_ref, cb_ref, wg_ref, bg_ref, lam_ref)
    lru_state = (tail_scr, carry_scr, xc_scr, g_scr, pl_scr, hl_scr)

    @pl.when(pl.program_id(0) == 0)
    def _():
        tail_scr[...] = jnp.zeros_like(tail_scr)
        carry_scr[...] = jnp.zeros_like(carry_scr)
        for phase, _ in _lru_phases(xl0_ref, gl0_ref, *lru_params, lr_scr, *lru_state):
            phase()

    groups = []
    for g in range(NG4):
        rows = slice(g * RG, (g + 1) * RG)
        attn = jnp.concatenate(
            [jnp.concatenate([at_ref[j, r, rows] for j in range(NSLAB)], axis=1) for r in range(NS)],
            axis=0)
        lru = jnp.concatenate([lr_scr[r, rows] for r in range(NS)], axis=0)
        mixed = jnp.concatenate([_rms(attn, ga_ref[...]), _rms(lru, gl_ref[...])], axis=1)
        groups.append(jnp.dot(perm_ref[...], mixed.astype(BF16),
                              preferred_element_type=F32).astype(BF16))
    mixed = jnp.concatenate(groups, axis=0)
    lru_next = _lru_phases(xln_ref, gln_ref, *lru_params, lr_scr, *lru_state)
    n_chunks = D_FF // FF_CHUNK
    assert len(lru_next) <= n_chunks + 1

    def zero_after(val, width):
        bits = pltpu.bitcast(val[0:1, :width], jnp.int32) & zero_ref[0:1, :width]
        return pltpu.bitcast(bits, F32)

    h = x_ref[...] + jnp.dot(mixed, wo_ref[...], preferred_element_type=F32)
    u = _rms(h, g2_ref[...]).astype(BF16)
    acc = h
    phase, width = lru_next.pop(0)
    phase(zero_after(h, width))
    for c in range(n_chunks):
        if lru_next:
            phase, width = lru_next.pop(0)
            phase(zero_after(acc, width))
        sl = slice(c * FF_CHUNK, (c + 1) * FF_CHUNK)
        f = jnp.dot(u, wu_ref[:, sl], preferred_element_type=F32)
        f = jnp.square(jnp.maximum(f, 0.0)).astype(BF16)
        acc = acc + jnp.dot(f, wd_ref[sl, :], preferred_element_type=F32)
    o_ref[...] = _rms(acc, gf_ref[...])


def _out_mlp(x2, attn, xg, lru_params, ga, gl, wo, g2, wu, wd, gf):
    def const(a):
        return pl.BlockSpec(a.shape, lambda i: (0,) * a.ndim, pipeline_mode=pl.Buffered(1))

    n_tiles = SEQ // TM4
    xspec = pl.BlockSpec((TM4, D_MODEL), lambda i: (i, 0))
    first = pl.BlockSpec((2, NS, TL, LRU_WIDTH), lambda i: (0, 0, 0, 0),
                         pipeline_mode=pl.Buffered(1))
    ahead = pl.BlockSpec((2, NS, TL, LRU_WIDTH),
                         lambda i: (0, 0, jnp.minimum(i + 1, n_tiles - 1), 0))
    perm = jnp.asarray(_group_perm(), BF16)
    zero = jnp.zeros((SUBLANES, D_MODEL), jnp.int32)
    consts = (zero,) + tuple(lru_params) + (perm, ga, gl, wo, g2, wu, wd, gf)
    return pl.pallas_call(
        _out_body,
        grid=(n_tiles,),
        in_specs=[
            xspec,
            pl.BlockSpec((NSLAB, NS, TL, LANES), lambda i: (0, 0, i, 0)),
            first, ahead,
        ] + [const(a) for a in consts],
        out_specs=xspec,
        out_shape=jax.ShapeDtypeStruct((SEQ, D_MODEL), F32),
        scratch_shapes=[
            pltpu.VMEM((NS, TL, LRU_WIDTH), F32),
            pltpu.VMEM((CONV_WIDTH - 1, SUBLANES, LRU_WIDTH), F32),
            pltpu.VMEM((SUBLANES, LRU_WIDTH), F32),
            pltpu.VMEM((NS, TL, LRU_WIDTH), F32),
            pltpu.VMEM((NS * TL, 2 * LRU_WIDTH), F32),
            pltpu.VMEM((NS, TL, LRU_WIDTH), F32),
            pltpu.VMEM((NS, TL, LRU_WIDTH), F32),
        ],
        compiler_params=pltpu.CompilerParams(
            dimension_semantics=("arbitrary",), vmem_limit_bytes=VMEM_LIMIT),
        name="out_mlp",
    )(x2, attn, xg, xg, *consts)


@functools.lru_cache(maxsize=None)
def _rope_tables():
    inv_freq = ROPE_THETA ** (-np.arange(0, ROT_DIM, 2, dtype=np.float64) / ROT_DIM)
    pad = HEAD_DIM - ROT_DIM
    reps = LANES // HEAD_DIM

    def lanes(rot_first, rot_second, rest, n):
        head = np.concatenate([rot_first, rot_second, np.full((n, pad), rest)], axis=1)
        return np.tile(head, (1, reps))

    tile_ang = (TM1 * np.arange(SEQ // TM1, dtype=np.float64))[:, None] * inv_freq[None, :]
    tc, ts = np.cos(tile_ang), np.sin(tile_ang)
    tiles = np.concatenate([lanes(tc, tc, 1.0, len(tc)), lanes(ts, ts, 0.0, len(ts))], axis=1)
    off_ang = np.arange(TM1, dtype=np.float64)[:, None] * inv_freq[None, :]
    oc, osn = np.cos(off_ang), np.sin(off_ang)
    offs = np.stack([lanes(oc, oc, 1.0, TM1), lanes(osn, osn, 0.0, TM1),
                     lanes(-oc, oc, 0.0, TM1), lanes(-osn, osn, 0.0, TM1)])
    offs = offs.reshape(4, NG1, RG, NS, LANES).transpose(0, 1, 3, 2, 4).reshape(4, TM1, LANES)
    return (tiles.astype(np.float32)[:, None, :], np.ascontiguousarray(offs.astype(np.float32)))


def _block_diag(w):
    n, b, _ = w.shape
    eye = jnp.eye(n, dtype=w.dtype)
    return (eye[:, None, :, None] * w[:, :, None, :]).reshape(n * b, n * b)


def _gate_tiles(w_r, w_i):
    per = MXU_TILE // (LRU_WIDTH // LRU_BLOCKS)
    tiles = [jnp.concatenate([_block_diag(w_r[t * per:(t + 1) * per]),
                              _block_diag(w_i[t * per:(t + 1) * per])], axis=1)
             for t in range(LRU_WIDTH // MXU_TILE)]
    return jnp.stack(tiles).astype(BF16)


def kernel(x, norm1_g, w_in, conv_w, conv_b, w_rgate, b_rgate, w_igate, b_igate, lru_lambda,
           attn_out_g, lru_out_g, w_out, norm2_g, w_mlp_up, w_mlp_down, final_g):
    assert x.shape == (1, SEQ, D_MODEL) and w_in.shape[0] == 1
    x2 = x.reshape(SEQ, D_MODEL)
    sm, cm, xg = _inproj(x2, norm1_g.reshape(1, D_MODEL), w_in[0])
    attn, wo, wu, wd = _attention(sm, cm, (w_out[0], w_mlp_up[0], w_mlp_down[0]))
    w_gate = _gate_tiles(w_rgate[0], w_igate[0])
    b_gate = jnp.concatenate([b_rgate[0].reshape(1, -1), b_igate[0].reshape(1, -1)], axis=1)
    lru_params = (conv_w[0], conv_b[0].reshape(1, -1), w_gate, b_gate, lru_lambda[0].reshape(1, -1))
    out = _out_mlp(x2, attn, xg, lru_params,
                   attn_out_g[0].reshape(1, -1), lru_out_g[0].reshape(1, -1),
                   wo, norm2_g[0].reshape(1, -1), wu, wd, final_g.reshape(1, -1))
    return out.reshape(1, SEQ, D_MODEL)
```

```python
import functools

import numpy as np
import jax
import jax.numpy as jnp
from jax import lax
from jax.experimental import pallas as pl
from jax.experimental.pallas import tpu as pltpu

F32 = jnp.float32
BF16 = jnp.bfloat16

D_MODEL = 1024
SEQ = 16384
ATTN_HEADS = 8
HEAD_DIM = 64
ATTN_WIDTH = ATTN_HEADS * HEAD_DIM
ROT_DIM = HEAD_DIM // 4
ROPE_THETA = 500000.0
LRU_WIDTH = D_MODEL - ATTN_WIDTH
LRU_BLOCKS = 8
CONV_WIDTH = 4
LRU_C = 8.0
IN_WIDTH = 3 * ATTN_WIDTH + 2 * LRU_WIDTH
D_FF = 4 * D_MODEL
EPS = 1e-6
WINDOW_STEPS = 128

NS = 16
RA = SEQ // NS
LANES = 128
SUBLANES = 8
MXU_TILE = 256
NSLAB = ATTN_WIDTH // LANES
NEG = -np.inf

VMEM_LIMIT = 56 * 1024 * 1024


def _rms(x, g):
    return x * lax.rsqrt(jnp.mean(x * x, axis=-1, keepdims=True) + EPS) * g


TM1 = 1024
PG = NS * NS
RG = PG // NS
NG1 = TM1 // PG
CH = SUBLANES
LOG2E = 1.4426950408889634


@functools.lru_cache(maxsize=None)
def _group_perm():
    n = np.arange(PG)
    p = np.zeros((PG, PG), np.float32)
    p[(n % NS) * RG + n // NS, n] = 1.0
    return p


def _inproj_body(x_ref, g_ref, w_ref, perm_ref, tile_ref, off_ref,
                 sm_ref, cm_ref, xg_ref, w_scr):
    xl_ref, gl_ref = xg_ref.at[0], xg_ref.at[1]
    k_ref, v_ref, q_ref = (sm_ref.at[n] for n in range(3))
    k1_ref, v1_ref, q1_ref = (cm_ref.at[n] for n in range(3))

    @pl.when(pl.program_id(0) == 0)
    def _():
        w_scr[...] = w_ref[...].astype(BF16)

    u = _rms(x_ref[...], g_ref[...]).astype(BF16)
    u = jnp.concatenate(
        [jnp.dot(perm_ref[...], u[g * PG:(g + 1) * PG], preferred_element_type=F32).astype(BF16)
         for g in range(NG1)], axis=0)
    z = jnp.dot(u, w_scr[...], preferred_element_type=F32)
    tile_cos, tile_sin = tile_ref[:, :LANES], tile_ref[:, LANES:]
    c = tile_cos * off_ref[0] - tile_sin * off_ref[1]
    s = tile_sin * off_ref[2] + tile_cos * off_ref[3]
    c = jnp.concatenate([c] * NSLAB, axis=1)
    s = jnp.concatenate([s] * NSLAB, axis=1)
    lane = lax.broadcasted_iota(jnp.int32, (1, ATTN_WIDTH), 1) % HEAD_DIM
    first_half = lane < ROT_DIM // 2

    def rope(t):
        up = pltpu.roll(t, ATTN_WIDTH - ROT_DIM // 2, axis=1)
        dn = pltpu.roll(t, ROT_DIM // 2, axis=1)
        return t * c + jnp.where(first_half, up, dn) * s

    q = rope(z[:, :ATTN_WIDTH]) * (HEAD_DIM ** -0.5 * LOG2E)
    k = rope(z[:, ATTN_WIDTH:2 * ATTN_WIDTH])
    v = z[:, 2 * ATTN_WIDTH:3 * ATTN_WIDTH]
    xl = z[:, 3 * ATTN_WIDTH:3 * ATTN_WIDTH + LRU_WIDTH]
    gl = z[:, 3 * ATTN_WIDTH + LRU_WIDTH:]
    for g in range(NG1):
        for r in range(NS):
            src = slice(g * PG + r * RG, g * PG + (r + 1) * RG)
            dst = slice(g * RG, (g + 1) * RG)
            for j in range(NSLAB):
                sl = slice(j * LANES, (j + 1) * LANES)
                q_ref[j, r, dst] = q[src, sl].astype(BF16)
                k_ref[j, r, dst] = k[src, sl].astype(BF16)
                v_ref[j, r, dst] = v[src, sl].astype(BF16)
            xl_ref[r, dst] = xl[src]
            gl_ref[r, dst] = gl[src]
    for g in range(NG1):
        for cl in range(RG // CH):
            for r in range(0, NS, 2):
                lo = g * PG + r * RG + cl * CH
                hi = lo + RG
                d0 = (g * (RG // CH) + cl) * NS * CH + r * CH
                dst = slice(d0, d0 + 2 * CH)
                for j in range(NSLAB):
                    sl = slice(j * LANES, (j + 1) * LANES)
                    for src, ref in ((q, q1_ref), (k, k1_ref), (v, v1_ref)):
                        ref[j, dst] = jnp.concatenate(
                            [src[lo:lo + CH, sl], src[hi:hi + CH, sl]], axis=0).astype(BF16)


def _inproj(x2, g, w):
    steps = SEQ // TM1
    rows = TM1 // NS
    slab_spec = pl.BlockSpec((3, NSLAB, NS, rows, LANES), lambda i: (0, 0, 0, i, 0))
    chunk_spec = pl.BlockSpec((3, NSLAB, TM1, LANES), lambda i: (0, 0, i, 0))
    row_spec = pl.BlockSpec((2, NS, rows, LRU_WIDTH), lambda i: (0, 0, i, 0))
    rope_tiles, rope_offs = _rope_tables()
    slab_shape = jax.ShapeDtypeStruct((3, NSLAB, NS, RA, LANES), BF16)
    chunk_shape = jax.ShapeDtypeStruct((3, NSLAB, SEQ, LANES), BF16)
    row_shape = jax.ShapeDtypeStruct((2, NS, RA, LRU_WIDTH), F32)
    perm = jnp.asarray(_group_perm(), BF16)
    return pl.pallas_call(
        _inproj_body,
        grid=(steps,),
        in_specs=[
            pl.BlockSpec((TM1, D_MODEL), lambda i: (i, 0)),
            pl.BlockSpec((1, D_MODEL), lambda i: (0, 0)),
            pl.BlockSpec((D_MODEL, IN_WIDTH), lambda i: (0, 0), pipeline_mode=pl.Buffered(1)),
            pl.BlockSpec((PG, PG), lambda i: (0, 0)),
            pl.BlockSpec((None, 1, 2 * LANES), lambda i: (i, 0, 0)),
            pl.BlockSpec(rope_offs.shape, lambda i: (0, 0, 0)),
        ],
        out_specs=[slab_spec, chunk_spec, row_spec],
        out_shape=[slab_shape, chunk_shape, row_shape],
        scratch_shapes=[pltpu.VMEM((D_MODEL, IN_WIDTH), BF16)],
        compiler_params=pltpu.CompilerParams(
            dimension_semantics=("arbitrary",), vmem_limit_bytes=VMEM_LIMIT),
        name="inproj",
    )(x2, g, w, perm, rope_tiles, rope_offs)


TA = WINDOW_STEPS
UQ, UK = WINDOW_STEPS, 2 * WINDOW_STEPS
Q4, K4 = UQ // 4, UK // 4
AHEAD = 2
RING = 2 * AHEAD


@functools.lru_cache(maxsize=None)
def _attn_masks():
    def variants(diff, from_prev, per_unit):
        ok = (diff >= 0) & (diff <= WINDOW_STEPS)
        normal = np.where(ok, 0.0, NEG).astype(np.float32)
        first = np.where(ok & ~from_prev, 0.0, NEG).astype(np.float32)
        if per_unit:
            return np.stack([np.stack([first, normal]), np.stack([normal, normal])])
        return np.stack([first, normal])

    iq = np.arange(UQ)[:, None]
    ck = np.arange(UK)[None, :]
    m16 = variants(iq - ck + TA, np.broadcast_to(ck < TA, (UQ, UK)), False)
    cq, i4 = np.divmod(np.arange(UQ), Q4)
    ckk, j4 = np.divmod(np.arange(UK), K4)
    d4 = 4 * (i4[:, None] - j4[None, :] + Q4) + (cq[:, None] - ckk[None, :])
    m4 = variants(d4, np.broadcast_to(j4[None, :] < K4 - Q4, d4.shape), True)
    rq, i1 = np.divmod(np.arange(UQ), CH)
    kc, kin = np.divmod(np.arange(UK), UQ)
    rk, j1 = np.divmod(kin, CH)
    d1 = (NS * i1 + rq)[:, None] - (NS * j1 + rk + UQ * (kc - 1))[None, :]
    m1 = variants(d1, np.broadcast_to(kc[None, :] == 0, d1.shape), True)
    return np.concatenate([m16[:, None], m4, m1], axis=1)


def _attn_body(sm_ref, smp_ref, cm_ref, cmp_ref,
               mk_ref, wo_ref, wu_ref, wd_ref,
               o_ref, wo_bf_ref, wu_bf_ref, wd_bf_ref,
               s_scr, acc_scr, max_scr, sum_scr):
    kc_ref, vc_ref, q_ref = (sm_ref.at[n] for n in range(3))
    kp_ref, vp_ref = smp_ref.at[0], smp_ref.at[1]
    k1c_ref, v1c_ref, q1_ref = (cm_ref.at[n] for n in range(3))
    k1p_ref, v1p_ref = cmp_ref.at[0], cmp_ref.at[1]

    wo_bf_ref[...] = wo_ref[...].astype(BF16)
    wu_bf_ref[...] = wu_ref[...].astype(BF16)
    wd_bf_ref[...] = wd_ref[...].astype(BF16)

    lane = lax.broadcasted_iota(jnp.int32, (1, LANES), 1)
    head0 = lane < HEAD_DIM
    zero = jnp.zeros((), BF16)

    def only_head(x, h):
        return jnp.where(head0, x, zero) if h == 0 else jnp.where(head0, zero, x)

    ones = jnp.ones((UK, LANES), BF16)

    def unit16(r):
        def store(o, mx, sm):
            acc_scr[0, r], max_scr[0, r], sum_scr[0, r] = o, mx, sm

        return (lambda h: only_head(q_ref[r], h),
                lambda: jnp.concatenate([kp_ref[r], kc_ref[r]], axis=0),
                lambda: jnp.concatenate([vp_ref[r], vc_ref[r]], axis=0),
                lambda: mk_ref[0], store)

    def unit4(r4, b):
        streams = [r4 + 4 * c for c in range(4)]
        q0, k0 = b * Q4, b * Q4 + Q4 - K4

        def window(cur, prev, st):
            if k0 < 0:
                return [prev[st, TA + k0:TA], cur[st, 0:k0 + K4]]
            return [cur[st, k0:k0 + K4]]

        def store(o, mx, sm):
            for c, st in enumerate(streams):
                rows = slice(c * Q4, (c + 1) * Q4)
                acc_scr[1, st, q0:q0 + Q4] = o[rows]
                max_scr[1, st, q0:q0 + Q4] = mx[rows]
                sum_scr[1, st, q0:q0 + Q4] = sm[rows]

        return (lambda h: only_head(
                    jnp.concatenate([q_ref[st, q0:q0 + Q4] for st in streams], axis=0), h),
                lambda: jnp.concatenate(
                    [x for st in streams for x in window(kc_ref, kp_ref, st)], axis=0),
                lambda: jnp.concatenate(
                    [x for st in streams for x in window(vc_ref, vp_ref, st)], axis=0),
                lambda: mk_ref[1 + min(b, 1)], store)

    def unit1(u):
        off, row = u * UQ, u * CH

        def store(o, mx, sm):
            for r in range(NS):
                rows = slice(r * CH, (r + 1) * CH)
                acc_scr[2, r, row:row + CH] = o[rows]
                max_scr[2, r, row:row + CH] = mx[rows]
                sum_scr[2, r, row:row + CH] = sm[rows]

        def chunks(cur, prev):
            if u == 0:
                return jnp.concatenate([prev[...], cur[0:UQ]], axis=0)
            return cur[off - UQ:off + UQ]

        return (lambda h: only_head(q1_ref[off:off + UQ], h),
                lambda: chunks(k1c_ref, k1p_ref),
                lambda: chunks(v1c_ref, v1p_ref),
                lambda: mk_ref[3 + min(u, 1)], store)

    all_units = ([unit16(n) for n in range(NS)]
                 + [unit4(n // (TA // Q4), n % (TA // Q4)) for n in range(NS)]
                 + [unit1(n) for n in range(NS)])

    def score_unit(n):
        q, k, _, _, _ = all_units[n]
        s_scr[n % RING] = lax.dot_general(
            jnp.concatenate([q(0), q(1)], axis=0), k(), (((1,), (1,)), ((), ())),
            preferred_element_type=F32)

    def value_unit(n):
        _, _, v, mask, store = all_units[n]
        ps, mx = [], []
        for h in range(2):
            s = s_scr[n % RING, h * UQ:(h + 1) * UQ] + mask()
            m = jnp.max(s, axis=-1, keepdims=True)
            ps.append(jnp.exp2(s - m).astype(BF16))
            mx.append(m)
        res = jnp.dot(jnp.concatenate(ps, axis=0), jnp.concatenate([v(), ones], axis=1),
                      preferred_element_type=F32)
        top, bot = res[:UQ], res[UQ:]
        store(jnp.where(head0, top[:, :LANES], bot[:, :LANES]),
              jnp.where(head0, mx[0], mx[1]),
              jnp.where(head0, top[:, LANES:], bot[:, LANES:]))

    for n in range(AHEAD):
        score_unit(n)
    for n in range(len(all_units)):
        if n + AHEAD < len(all_units):
            score_unit(n + AHEAD)
        value_unit(n)

    def comb(r, carry):
        m0, m1, m2 = max_scr[0, r], max_scr[1, r], max_scr[2, r]
        m = jnp.maximum(jnp.maximum(m0, m1), m2)
        w0, w1, w2 = jnp.exp2(m0 - m), jnp.exp2(m1 - m), jnp.exp2(m2 - m)
        num = w0 * acc_scr[0, r] + w1 * acc_scr[1, r] + w2 * acc_scr[2, r]
        den = w0 * sum_scr[0, r] + w1 * sum_scr[1, r] + w2 * sum_scr[2, r]
        o_ref[r] = num / den
        return carry

    lax.fori_loop(0, NS, comb, 0, unroll=4)


def _attention(sm, cm, later_weights):
    masks = _attn_masks()
    cur =pl.BlockSpec((3, None, NS, TA, LANES), lambda j, i: (0, j, 0, i, 0))
    prev = pl.BlockSpec((2, None, NS, TA, LANES),
                        lambda j, i: (0, j, 0, jnp.maximum(i - 1, 0), 0))
    cur1 = pl.BlockSpec((3, None, NS * TA, LANES), lambda j, i: (0, j, i, 0))
    prev1 = pl.BlockSpec((2, None, UQ, LANES),
                         lambda j, i: (0, j, jnp.maximum(i * (NS * TA // UQ) - 1, 0), 0))

    def tile_kind(m):
        return pl.BlockSpec((None,) + m.shape[1:],
                            lambda j, i: (jnp.minimum(i, 1),) + (0,) * (m.ndim - 1))

    n_tiles = RA // TA
    steps = NSLAB * n_tiles
    cast_specs = [pl.BlockSpec((a.shape[0] // steps, a.shape[1]), lambda j, i: (j * n_tiles + i, 0))
                  for a in later_weights]
    cast_shapes = [jax.ShapeDtypeStruct(a.shape, BF16) for a in later_weights]
    return pl.pallas_call(
        _attn_body,
        grid=(NSLAB, n_tiles),
        in_specs=[cur, prev, cur1, prev1, tile_kind(masks)] + cast_specs,
        out_specs=[pl.BlockSpec((None, NS, TA, LANES), lambda j, i: (j, 0, i, 0))] + cast_specs,
        out_shape=[jax.ShapeDtypeStruct((NSLAB, NS, RA, LANES), F32)] + cast_shapes,
        scratch_shapes=[
            pltpu.VMEM((RING, 2 * UQ, UK), F32),
            pltpu.VMEM((3, NS, TA, LANES), F32),
            pltpu.VMEM((3, NS, TA, LANES), F32),
            pltpu.VMEM((3, NS, TA, LANES), F32),
        ],
        compiler_params=pltpu.CompilerParams(
            dimension_semantics=("arbitrary", "arbitrary"), vmem_limit_bytes=VMEM_LIMIT),
        name="dilated_attn",
    )(sm, sm, cm, cm, masks, *later_weights)


TL = 32


def _lru_phases(xl_ref, gl_ref, cw_ref, cb_ref, wg_ref, bg_ref, lam_ref, y_ref,
                tail_scr, carry_scr, xc_scr, g_scr, pl_scr, hl_scr):
    row = lax.broadcasted_iota(jnp.int32, (TL, LRU_WIDTH), 0)
    state = {}

    def conv(after=0.0):
        cw = cw_ref[...]
        cb = cb_ref[...] + after

        def tap(r, k):
            st = r - k
            if st >= 0:
                return xl_ref[st]
            st += NS
            prev_last = tail_scr[st - (NS - CONV_WIDTH + 1), SUBLANES - 1:SUBLANES, :]
            return jnp.where(row == 0, prev_last, pltpu.roll(xl_ref[st], 1, axis=0))

        for r in range(NS):
            xc_scr[r] = cb + sum(cw[CONV_WIDTH - 1 - k:CONV_WIDTH - k] * tap(r, k)
                                 for k in range(CONV_WIDTH))
        for n in range(CONV_WIDTH - 1):
            tail_scr[n] = xl_ref[NS - CONV_WIDTH + 1 + n, TL - SUBLANES:TL, :]

    def gates(after=0.0):
        xc_all = xc_scr[...].reshape(NS * TL, LRU_WIDTH).astype(BF16)
        bias = bg_ref[...] + after
        for t in range(LRU_WIDTH // MXU_TILE):
            cols = slice(t * MXU_TILE, (t + 1) * MXU_TILE)
            res = jnp.dot(xc_all[:, cols], wg_ref[t], preferred_element_type=F32)
            for gate in range(2):
                dst = slice(gate * LRU_WIDTH + t * MXU_TILE, gate * LRU_WIDTH + (t + 1) * MXU_TILE)
                g_scr[:, dst] = res[:, gate * MXU_TILE:(gate + 1) * MXU_TILE] + bias[:, dst]

    def local_scan(streams):
        def run(after=0.0):
            neg_lam = after - lam_ref[...]
            softplus = jnp.maximum(neg_lam, 0.0) + jnp.log1p(jnp.exp(-jnp.abs(neg_lam)))
            p_run, h_run = state.get("run", (None, None))
            for r in streams:
                g = g_scr[r * TL:(r + 1) * TL]
                rg = jax.nn.sigmoid(g[:, :LRU_WIDTH])
                ig = jax.nn.sigmoid(g[:, LRU_WIDTH:])
                log_a = -LRU_C * rg * softplus
                a = jnp.exp(log_a)
                th = jnp.tanh(log_a)
                bx = jnp.sqrt(-2.0 * th / (1.0 - th)) * (ig * xc_scr[r])
                if p_run is None:
                    p_run, h_run = a, bx
                else:
                    h_run = a * h_run + bx
                    p_run = a * p_run
                pl_scr[r] = p_run
                hl_scr[r] = h_run
            state["run"] = (p_run, h_run)
        return run

    def row_scan():
        pa, hb = state["run"]
        sft = 1
        while sft < TL:
            keep = row >= sft
            pa_s = jnp.where(keep, pltpu.roll(pa, sft, axis=0), 1.0)
            hb_s = jnp.where(keep, pltpu.roll(hb, sft, axis=0), 0.0)
            hb = pa * hb_s + hb
            pa = pa * pa_s
            sft *= 2
        carry = carry_scr[0:1, :]
        e = pa * carry + hb
        state["e_prev"] = jnp.where(row == 0, carry, pltpu.roll(e, 1, axis=0))
        carry_scr[...] = jnp.broadcast_to(e[TL - 1:TL, :], carry_scr.shape)

    def finalize(after=0.0):
        e_prev = state["e_prev"] + after
        for r in range(NS):
            h = pl_scr[r] * e_prev + hl_scr[r]
            y_ref[r] = jax.nn.gelu(gl_ref[r], approximate=True) * h

    half = NS // 2

    def second_half(after=0.0):
        local_scan(range(half, NS))(after)
        row_scan()

    return [(conv, LRU_WIDTH), (gates, 2 * LRU_WIDTH), (local_scan(range(half)), LRU_WIDTH),
            (second_half, LRU_WIDTH), (finalize, LRU_WIDTH)]


TM4 = NS * TL
NG4 = TM4 // PG
FF_CHUNK = 1024


def _out_body(x_ref, at_ref, xg0_ref, xgn_ref, zero_ref,
              cw_ref, cb_ref, wg_ref, bg_ref, lam_ref,
              perm_ref, ga_ref, gl_ref, wo_ref, g2_ref, wu_ref, wd_ref, gf_ref,
              o_ref,
              lr_scr, tail_scr, carry_scr, xc_scr, g_scr, pl_scr, hl_scr):
    xl0_ref, gl0_ref = xg0_ref.at[0], xg0_ref.at[1]
    xln_ref, gln_ref = xgn_ref.at[0], xgn_ref.at[1]
    lru_params = (cw_ref, cb_ref, wg_ref, bg_ref, lam_ref)
    lru_state = (tail_scr, carry_scr, xc_scr, g_scr, pl_scr, hl_scr)

    @pl.when(pl.program_id(0) == 0)
    def _():
        tail_scr[...] = jnp.zeros_like(tail_scr)
        carry_scr[...] = jnp.zeros_like(carry_scr)
        for phase, _ in _lru_phases(xl0_ref, gl0_ref, *lru_params, lr_scr, *lru_state):
            phase()

    groups = []
    for g in range(NG4):
        rows = slice(g * RG, (g + 1) * RG)
        attn = jnp.concatenate(
            [jnp.concatenate([at_ref[j, r, rows] for j in range(NSLAB)], axis=1) for r in range(NS)],
            axis=0)
        lru = jnp.concatenate([lr_scr[r, rows] for r in range(NS)], axis=0)
        mixed = jnp.concatenate([_rms(attn, ga_ref[...]), _rms(lru, gl_ref[...])], axis=1)
        groups.append(jnp.dot(perm_ref[...], mixed.astype(BF16),
                              preferred_element_type=F32).astype(BF16))
    mixed = jnp.concatenate(groups, axis=0)
    lru_next = _lru_phases(xln_ref, gln_ref, *lru_params, lr_scr, *lru_state)
    n_chunks = D_FF // FF_CHUNK
    assert len(lru_next) <= n_chunks + 1

    def zero_after(val, width):
        bits = pltpu.bitcast(val[0:1, :width], jnp.int32) & zero_ref[0:1, :width]
        return pltpu.bitcast(bits, F32)

    h = x_ref[...] + jnp.dot(mixed, wo_ref[...], preferred_element_type=F32)
    u = _rms(h, g2_ref[...]).astype(BF16)
    acc = h
    phase, width = lru_next.pop(0)
    phase(zero_after(h, width))
    for c in range(n_chunks):
        if lru_next:
            phase, width = lru_next.pop(0)
            phase(zero_after(acc, width))
        sl = slice(c * FF_CHUNK, (c + 1) * FF_CHUNK)
        f = jnp.dot(u, wu_ref[:, sl], preferred_element_type=F32)
        f = jnp.square(jnp.maximum(f, 0.0)).astype(BF16)
        acc = acc + jnp.dot(f, wd_ref[sl, :], preferred_element_type=F32)
    o_ref[...] = _rms(acc, gf_ref[...])


def _out_mlp(x2, attn, xg, lru_params, ga, gl, wo, g2, wu, wd, gf):
    def const(a):
        return pl.BlockSpec(a.shape, lambda i: (0,) * a.ndim, pipeline_mode=pl.Buffered(1))

    n_tiles = SEQ // TM4
    xspec = pl.BlockSpec((TM4, D_MODEL), lambda i: (i, 0))
    first = pl.BlockSpec((2, NS, TL, LRU_WIDTH), lambda i: (0, 0, 0, 0),
                         pipeline_mode=pl.Buffered(1))
    ahead = pl.BlockSpec((2, NS, TL, LRU_WIDTH),
                         lambda i: (0, 0, jnp.minimum(i + 1, n_tiles - 1), 0))
    perm = jnp.asarray(_group_perm(), BF16)
    zero = jnp.zeros((SUBLANES, D_MODEL), jnp.int32)
    consts = (zero,) + tuple(lru_params) + (perm, ga, gl, wo, g2, wu, wd, gf)
    return pl.pallas_call(
        _out_body,
        grid=(n_tiles,),
        in_specs=[
            xspec,
            pl.BlockSpec((NSLAB, NS, TL, LANES), lambda i: (0, 0, i, 0)),
            first, ahead,
        ] + [const(a) for a in consts],
        out_specs=xspec,
        out_shape=jax.ShapeDtypeStruct((SEQ, D_MODEL), F32),
        scratch_shapes=[
            pltpu.VMEM((NS, TL, LRU_WIDTH), F32),
            pltpu.VMEM((CONV_WIDTH - 1, SUBLANES, LRU_WIDTH), F32),
            pltpu.VMEM((SUBLANES, LRU_WIDTH), F32),
            pltpu.VMEM((NS, TL, LRU_WIDTH), F32),
            pltpu.VMEM((NS * TL, 2 * LRU_WIDTH), F32),
            pltpu.VMEM((NS, TL, LRU_WIDTH), F32),
            pltpu.VMEM((NS, TL, LRU_WIDTH), F32),
        ],
        compiler_params=pltpu.CompilerParams(
            dimension_semantics=("arbitrary",), vmem_limit_bytes=VMEM_LIMIT),
        name="out_mlp",
    )(x2, attn, xg, xg, *consts)


@functools.lru_cache(maxsize=None)
def _rope_tables():
    inv_freq = ROPE_THETA ** (-np.arange(0, ROT_DIM, 2, dtype=np.float64) / ROT_DIM)
    pad = HEAD_DIM - ROT_DIM
    reps = LANES // HEAD_DIM

    def lanes(rot_first, rot_second, rest, n):
        head = np.concatenate([rot_first, rot_second, np.full((n, pad), rest)], axis=1)
        return np.tile(head, (1, reps))

    tile_ang = (TM1 * np.arange(SEQ // TM1, dtype=np.float64))[:, None] * inv_freq[None, :]
    tc, ts = np.cos(tile_ang), np.sin(tile_ang)
    tiles = np.concatenate([lanes(tc, tc, 1.0, len(tc)), lanes(ts, ts, 0.0, len(ts))], axis=1)
    off_ang = np.arange(TM1, dtype=np.float64)[:, None] * inv_freq[None, :]
    oc, osn = np.cos(off_ang), np.sin(off_ang)
    offs = np.stack([lanes(oc, oc, 1.0, TM1), lanes(osn, osn, 0.0, TM1),
                     lanes(-oc, oc, 0.0, TM1), lanes(-osn, osn, 0.0, TM1)])
    offs = offs.reshape(4, NG1, RG, NS, LANES).transpose(0, 1, 3, 2, 4).reshape(4, TM1, LANES)
    return (tiles.astype(np.float32)[:, None, :], np.ascontiguousarray(offs.astype(np.float32)))


def _block_diag(w):
    n, b, _ = w.shape
    eye = jnp.eye(n, dtype=w.dtype)
    return (eye[:, None, :, None] * w[:, :, None, :]).reshape(n * b, n * b)


def _gate_tiles(w_r, w_i):
    per = MXU_TILE // (LRU_WIDTH // LRU_BLOCKS)
    tiles = [jnp.concatenate([_block_diag(w_r[t * per:(t + 1) * per]),
                              _block_diag(w_i[t * per:(t + 1) * per])], axis=1)
             for t in range(LRU_WIDTH // MXU_TILE)]
    return jnp.stack(tiles).astype(BF16)


def kernel(x, norm1_g, w_in, conv_w, conv_b, w_rgate, b_rgate, w_igate, b_igate, lru_lambda,
           attn_out_g, lru_out_g, w_out, norm2_g, w_mlp_up, w_mlp_down, final_g):
    assert x.shape == (1, SEQ, D_MODEL) and w_in.shape[0] == 1
    x2 = x.reshape(SEQ, D_MODEL)
    sm, cm, xg = _inproj(x2, norm1_g.reshape(1, D_MODEL), w_in[0])
    attn, wo, wu, wd = _attention(sm, cm, (w_out[0], w_mlp_up[0], w_mlp_down[0]))
    w_gate = _gate_tiles(w_rgate[0], w_igate[0])
    b_gate = jnp.concatenate([b_rgate[0].reshape(1, -1), b_igate[0].reshape(1, -1)], axis=1)
    lru_params = (conv_w[0], conv_b[0].reshape(1, -1), w_gate, b_gate, lru_lambda[0].reshape(1, -1))
    out = _out_mlp(x2, attn, xg, lru_params,
                   attn_out_g[0].reshape(1, -1), lru_out_g[0].reshape(1, -1),
                   wo, norm2_g[0].reshape(1, -1), wu, wd, final_g.reshape(1, -1))
    return out.reshape(1, SEQ, D_MODEL)
```

```python
import functools

import numpy as np
import jax
import jax.numpy as jnp
from jax import lax
from jax.experimental import pallas as pl
from jax.experimental.pallas import tpu as pltpu

F32 = jnp.float32
BF16 = jnp.bfloat16

D_MODEL = 1024
SEQ = 16384
ATTN_HEADS = 8
HEAD_DIM = 64
ATTN_WIDTH = ATTN_HEADS * HEAD_DIM
ROT_DIM = HEAD_DIM // 4
ROPE_THETA = 500000.0
LRU_WIDTH = D_MODEL - ATTN_WIDTH
LRU_BLOCKS = 8
CONV_WIDTH = 4
LRU_C = 8.0
IN_WIDTH = 3 * ATTN_WIDTH + 2 * LRU_WIDTH
D_FF = 4 * D_MODEL
EPS = 1e-6
WINDOW_STEPS = 128

NS = 16
RA = SEQ // NS
LANES = 128
SUBLANES = 8
MXU_TILE = 256
NSLAB = ATTN_WIDTH // LANES
NEG = -np.inf

VMEM_LIMIT = 56 * 1024 * 1024


def _rms(x, g):
    return x * lax.rsqrt(jnp.mean(x * x, axis=-1, keepdims=True) + EPS) * g


TM1 = 1024
PG = NS * NS
RG = PG // NS
NG1 = TM1 // PG
CH = SUBLANES
LOG2E = 1.4426950408889634


@functools.lru_cache(maxsize=None)
def _group_perm():
    n = np.arange(PG)
    p = np.zeros((PG, PG), np.float32)
    p[(n % NS) * RG + n // NS, n] = 1.0
    return p


def _inproj_body(x_ref, g_ref, w_ref, perm_ref, tile_ref, off_ref,
                 sm_ref, cm_ref, xg_ref, w_scr):
    xl_ref, gl_ref = xg_ref.at[0], xg_ref.at[1]
    k_ref, v_ref, q_ref = (sm_ref.at[n] for n in range(3))
    k1_ref, v1_ref, q1_ref = (cm_ref.at[n] for n in range(3))

    @pl.when(pl.program_id(0) == 0)
    def _():
        w_scr[...] = w_ref[...].astype(BF16)

    u = _rms(x_ref[...], g_ref[...]).astype(BF16)
    u = jnp.concatenate(
        [jnp.dot(perm_ref[...], u[g * PG:(g + 1) * PG], preferred_element_type=F32).astype(BF16)
         for g in range(NG1)], axis=0)
    z = jnp.dot(u, w_scr[...], preferred_element_type=F32)
    tile_cos, tile_sin = tile_ref[:, :LANES], tile_ref[:, LANES:]
    c = tile_cos * off_ref[0] - tile_sin * off_ref[1]
    s = tile_sin * off_ref[2] + tile_cos * off_ref[3]
    c = jnp.concatenate([c] * NSLAB, axis=1)
    s = jnp.concatenate([s] * NSLAB, axis=1)
    lane = lax.broadcasted_iota(jnp.int32, (1, ATTN_WIDTH), 1) % HEAD_DIM
    first_half = lane < ROT_DIM // 2

    def rope(t):
        up = pltpu.roll(t, ATTN_WIDTH - ROT_DIM // 2, axis=1)
        dn = pltpu.roll(t, ROT_DIM // 2, axis=1)
        return t * c + jnp.where(first_half, up, dn) * s

    q = rope(z[:, :ATTN_WIDTH]) * (HEAD_DIM ** -0.5 * LOG2E)
    k = rope(z[:, ATTN_WIDTH:2 * ATTN_WIDTH])
    v = z[:, 2 * ATTN_WIDTH:3 * ATTN_WIDTH]
    xl = z[:, 3 * ATTN_WIDTH:3 * ATTN_WIDTH + LRU_WIDTH]
    gl = z[:, 3 * ATTN_WIDTH + LRU_WIDTH:]
    for g in range(NG1):
        for r in range(NS):
            src = slice(g * PG + r * RG, g * PG + (r + 1) * RG)
            dst = slice(g * RG, (g + 1) * RG)
            for j in range(NSLAB):
                sl = slice(j * LANES, (j + 1) * LANES)
                q_ref[j, r, dst] = q[src, sl].astype(BF16)
                k_ref[j, r, dst] = k[src, sl].astype(BF16)
                v_ref[j, r, dst] = v[src, sl].astype(BF16)
            xl_ref[r, dst] = xl[src]
            gl_ref[r, dst] = gl[src]
    for g in range(NG1):
        for cl in range(RG // CH):
            for r in range(0, NS, 2):
                lo = g * PG + r * RG + cl * CH
                hi = lo + RG
                d0 = (g * (RG // CH) + cl) * NS * CH + r * CH
                dst = slice(d0, d0 + 2 * CH)
                for j in range(NSLAB):
                    sl = slice(j * LANES, (j + 1) * LANES)
                    for src, ref in ((q, q1_ref), (k, k1_ref), (v, v1_ref)):
                        ref[j, dst] = jnp.concatenate(
                            [src[lo:lo + CH, sl], src[hi:hi + CH, sl]], axis=0).astype(BF16)


def _inproj(x2, g, w):
    steps = SEQ // TM1
    rows = TM1 // NS
    slab_spec = pl.BlockSpec((3, NSLAB, NS, rows, LANES), lambda i: (0, 0, 0, i, 0))
    chunk_spec = pl.BlockSpec((3, NSLAB, TM1, LANES), lambda i: (0, 0, i, 0))
    row_spec = pl.BlockSpec((2, NS, rows, LRU_WIDTH), lambda i: (0, 0, i, 0))
    rope_tiles, rope_offs = _rope_tables()
    slab_shape = jax.ShapeDtypeStruct((3, NSLAB, NS, RA, LANES), BF16)
    chunk_shape = jax.ShapeDtypeStruct((3, NSLAB, SEQ, LANES), BF16)
    row_shape = jax.ShapeDtypeStruct((2, NS, RA, LRU_WIDTH), F32)
    perm = jnp.asarray(_group_perm(), BF16)
    return pl.pallas_call(
        _inproj_body,
        grid=(steps,),
        in_specs=[
            pl.BlockSpec((TM1, D_MODEL), lambda i: (i, 0)),
            pl.BlockSpec((1, D_MODEL), lambda i: (0, 0)),
            pl.BlockSpec((D_MODEL, IN_WIDTH), lambda i: (0, 0), pipeline_mode=pl.Buffered(1)),
            pl.BlockSpec((PG, PG), lambda i: (0, 0)),
            pl.BlockSpec((None, 1, 2 * LANES), lambda i: (i, 0, 0)),
            pl.BlockSpec(rope_offs.shape, lambda i: (0, 0, 0)),
        ],
        out_specs=[slab_spec, chunk_spec, row_spec],
        out_shape=[slab_shape, chunk_shape, row_shape],
        scratch_shapes=[pltpu.VMEM((D_MODEL, IN_WIDTH), BF16)],
        compiler_params=pltpu.CompilerParams(
            dimension_semantics=("arbitrary",), vmem_limit_bytes=VMEM_LIMIT),
        name="inproj",
    )(x2, g, w, perm, rope_tiles, rope_offs)


TA = WINDOW_STEPS
UQ, UK = WINDOW_STEPS, 2 * WINDOW_STEPS
Q4, K4 = UQ // 4, UK // 4
AHEAD = 2
RING = 2 * AHEAD


@functools.lru_cache(maxsize=None)
def _attn_masks():
    def variants(diff, from_prev, per_unit):
        ok = (diff >= 0) & (diff <= WINDOW_STEPS)
        normal = np.where(ok, 0.0, NEG).astype(np.float32)
        first = np.where(ok & ~from_prev, 0.0, NEG).astype(np.float32)
        if per_unit:
            return np.stack([np.stack([first, normal]), np.stack([normal, normal])])
        return np.stack([first, normal])

    iq = np.arange(UQ)[:, None]
    ck = np.arange(UK)[None, :]
    m16 = variants(iq - ck + TA, np.broadcast_to(ck < TA, (UQ, UK)), False)
    cq, i4 = np.divmod(np.arange(UQ), Q4)
    ckk, j4 = np.divmod(np.arange(UK), K4)
    d4 = 4 * (i4[:, None] - j4[None, :] + Q4) + (cq[:, None] - ckk[None, :])
    m4 = variants(d4, np.broadcast_to(j4[None, :] < K4 - Q4, d4.shape), True)
    rq, i1 = np.divmod(np.arange(UQ), CH)
    kc, kin = np.divmod(np.arange(UK), UQ)
    rk, j1 = np.divmod(kin, CH)
    d1 = (NS * i1 + rq)[:, None] - (NS * j1 + rk + UQ * (kc - 1))[None, :]
    m1 = variants(d1, np.broadcast_to(kc[None, :] == 0, d1.shape), True)
    return np.concatenate([m16[:, None], m4, m1], axis=1)


def _attn_body(sm_ref, smp_ref, cm_ref, cmp_ref,
               mk_ref, wo_ref, wu_ref, wd_ref,
               o_ref, wo_bf_ref, wu_bf_ref, wd_bf_ref,
               s_scr, acc_scr, max_scr, sum_scr):
    kc_ref, vc_ref, q_ref = (sm_ref.at[n] for n in range(3))
    kp_ref, vp_ref = smp_ref.at[0], smp_ref.at[1]
    k1c_ref, v1c_ref, q1_ref = (cm_ref.at[n] for n in range(3))
    k1p_ref, v1p_ref = cmp_ref.at[0], cmp_ref.at[1]

    wo_bf_ref[...] = wo_ref[...].astype(BF16)
    wu_bf_ref[...] = wu_ref[...].astype(BF16)
    wd_bf_ref[...] = wd_ref[...].astype(BF16)

    lane = lax.broadcasted_iota(jnp.int32, (1, LANES), 1)
    head0 = lane < HEAD_DIM
    zero = jnp.zeros((), BF16)

    def only_head(x, h):
        return jnp.where(head0, x, zero) if h == 0 else jnp.where(head0, zero, x)

    ones = jnp.ones((UK, LANES), BF16)

    def unit16(r):
        def store(o, mx, sm):
            acc_scr[0, r], max_scr[0, r], sum_scr[0, r] = o, mx, sm

        return (lambda h: only_head(q_ref[r], h),
                lambda: jnp.concatenate([kp_ref[r], kc_ref[r]], axis=0),
                lambda: jnp.concatenate([vp_ref[r], vc_ref[r]], axis=0),
                lambda: mk_ref[0], store)

    def unit4(r4, b):
        streams = [r4 + 4 * c for c in range(4)]
        q0, k0 = b * Q4, b * Q4 + Q4 - K4

        def window(cur, prev, st):
            if k0 < 0:
                return [prev[st, TA + k0:TA], cur[st, 0:k0 + K4]]
            return [cur[st, k0:k0 + K4]]

        def store(o, mx, sm):
            for c, st in enumerate(streams):
                rows = slice(c * Q4, (c + 1) * Q4)
                acc_scr[1, st, q0:q0 + Q4] = o[rows]
                max_scr[1, st, q0:q0 + Q4] = mx[rows]
                sum_scr[1, st, q0:q0 + Q4] = sm[rows]

        return (lambda h: only_head(
                    jnp.concatenate([q_ref[st, q0:q0 + Q4] for st in streams], axis=0), h),
                lambda: jnp.concatenate(
                    [x for st in streams for x in window(kc_ref, kp_ref, st)], axis=0),
                lambda: jnp.concatenate(
                    [x for st in streams for x in window(vc_ref, vp_ref, st)], axis=0),
                lambda: mk_ref[1 + min(b, 1)], store)

    def unit1(u):
        off, row = u * UQ, u * CH

        def store(o, mx, sm):
            for r in range(NS):
                rows = slice(r * CH, (r + 1) * CH)
                acc_scr[2, r, row:row + CH] = o[rows]
                max_scr[2, r, row:row + CH] = mx[rows]
                sum_scr[2, r, row:row + CH] = sm[rows]

        def chunks(cur, prev):
            if u == 0:
                return jnp.concatenate([prev[...], cur[0:UQ]], axis=0)
            return cur[off - UQ:off + UQ]

        return (lambda h: only_head(q1_ref[off:off + UQ], h),
                lambda: chunks(k1c_ref, k1p_ref),
                lambda: chunks(v1c_ref, v1p_ref),
                lambda: mk_ref[3 + min(u, 1)], store)

    all_units = ([unit16(n) for n in range(NS)]
                 + [unit4(n // (TA // Q4), n % (TA // Q4)) for n in range(NS)]
                 + [unit1(n) for n in range(NS)])

    def score_unit(n):
        q, k, _, _, _ = all_units[n]
        s_scr[n % RING] = lax.dot_general(
            jnp.concatenate([q(0), q(1)], axis=0), k(), (((1,), (1,)), ((), ())),
            preferred_element_type=F32)

    def value_unit(n):
        _, _, v, mask, store = all_units[n]
        ps, mx = [], []
        for h in range(2):
            s = s_scr[n % RING, h * UQ:(h + 1) * UQ] + mask()
            m = jnp.max(s, axis=-1, keepdims=True)
            ps.append(jnp.exp2(s - m).astype(BF16))
            mx.append(m)
        res = jnp.dot(jnp.concatenate(ps, axis=0), jnp.concatenate([v(), ones], axis=1),
                      preferred_element_type=F32)
        top, bot = res[:UQ], res[UQ:]
        store(jnp.where(head0, top[:, :LANES], bot[:, :LANES]),
              jnp.where(head0, mx[0], mx[1]),
              jnp.where(head0, top[:, LANES:], bot[:, LANES:]))

    for n in range(AHEAD):
        score_unit(n)
    for n in range(len(all_units)):
        if n + AHEAD < len(all_units):
            score_unit(n + AHEAD)
        value_unit(n)

    def comb(r, carry):
        m0, m1, m2 = max_scr[0, r], max_scr[1, r], max_scr[2, r]
        m = jnp.maximum(jnp.maximum(m0, m1), m2)
        w0, w1, w2 = jnp.exp2(m0 - m), jnp.exp2(m1 - m), jnp.exp2(m2 - m)
        num = w0 * acc_scr[0, r] + w1 * acc_scr[1, r] + w2 * acc_scr[2, r]
        den = w0 * sum_scr[0, r] + w1 * sum_scr[1, r] + w2 * sum_scr[2, r]
        o_ref[r] = num / den
        return carry

    lax.fori_loop(0, NS, comb, 0, unroll=4)


def _attention(sm, cm, later_weights):
    masks = _attn_masks()
    cur =pl.BlockSpec((3, None, NS, TA, LANES), lambda j, i: (0, j, 0, i, 0))
    prev = pl.BlockSpec((2, None, NS, TA, LANES),
                        lambda j, i: (0, j, 0, jnp.maximum(i - 1, 0), 0))
    cur1 = pl.BlockSpec((3, None, NS * TA, LANES), lambda j, i: (0, j, i, 0))
    prev1 = pl.BlockSpec((2, None, UQ, LANES),
                         lambda j, i: (0, j, jnp.maximum(i * (NS * TA // UQ) - 1, 0), 0))

    def tile_kind(m):
        return pl.BlockSpec((None,) + m.shape[1:],
                            lambda j, i: (jnp.minimum(i, 1),) + (0,) * (m.ndim - 1))

    n_tiles = RA // TA
    steps = NSLAB * n_tiles
    cast_specs = [pl.BlockSpec((a.shape[0] // steps, a.shape[1]), lambda j, i: (j * n_tiles + i, 0))
                  for a in later_weights]
    cast_shapes = [jax.ShapeDtypeStruct(a.shape, BF16) for a in later_weights]
    return pl.pallas_call(
        _attn_body,
        grid=(NSLAB, n_tiles),
        in_specs=[cur, prev, cur1, prev1, tile_kind(masks)] + cast_specs,
        out_specs=[pl.BlockSpec((None, NS, TA, LANES), lambda j, i: (j, 0, i, 0))] + cast_specs,
        out_shape=[jax.ShapeDtypeStruct((NSLAB, NS, RA, LANES), F32)] + cast_shapes,
        scratch_shapes=[
            pltpu.VMEM((RING, 2 * UQ, UK), F32),
            pltpu.VMEM((3, NS, TA, LANES), F32),
            pltpu.VMEM((3, NS, TA, LANES), F32),
            pltpu.VMEM((3, NS, TA, LANES), F32),
        ],
        compiler_params=pltpu.CompilerParams(
            dimension_semantics=("arbitrary", "arbitrary"), vmem_limit_bytes=VMEM_LIMIT),
        name="dilated_attn",
    )(sm, sm, cm, cm, masks, *later_weights)


TL = 32


def _lru_phases(xl_ref, gl_ref, cw_ref, cb_ref, wg_ref, bg_ref, lam_ref, y_ref,
                tail_scr, carry_scr, xc_scr, g_scr, pl_scr, hl_scr):
    row = lax.broadcasted_iota(jnp.int32, (TL, LRU_WIDTH), 0)
    state = {}

    def conv(after=0.0):
        cw = cw_ref[...]
        cb = cb_ref[...] + after

        def tap(r, k):
            st = r - k
            if st >= 0:
                return xl_ref[st]
            st += NS
            prev_last = tail_scr[st - (NS - CONV_WIDTH + 1), SUBLANES - 1:SUBLANES, :]
            return jnp.where(row == 0, prev_last, pltpu.roll(xl_ref[st], 1, axis=0))

        for r in range(NS):
            xc_scr[r] = cb + sum(cw[CONV_WIDTH - 1 - k:CONV_WIDTH - k] * tap(r, k)
                                 for k in range(CONV_WIDTH))
        for n in range(CONV_WIDTH - 1):
            tail_scr[n] = xl_ref[NS - CONV_WIDTH + 1 + n, TL - SUBLANES:TL, :]

    def gates(after=0.0):
        xc_all = xc_scr[...].reshape(NS * TL, LRU_WIDTH).astype(BF16)
        bias = bg_ref[...] + after
        for t in range(LRU_WIDTH // MXU_TILE):
            cols = slice(t * MXU_TILE, (t + 1) * MXU_TILE)
            res = jnp.dot(xc_all[:, cols], wg_ref[t], preferred_element_type=F32)
            for gate in range(2):
                dst = slice(gate * LRU_WIDTH + t * MXU_TILE, gate * LRU_WIDTH + (t + 1) * MXU_TILE)
                g_scr[:, dst] = res[:, gate * MXU_TILE:(gate + 1) * MXU_TILE] + bias[:, dst]

    def local_scan(streams):
        def run(after=0.0):
            neg_lam = after - lam_ref[...]
            softplus = jnp.maximum(neg_lam, 0.0) + jnp.log1p(jnp.exp(-jnp.abs(neg_lam)))
            p_run, h_run = state.get("run", (None, None))
            for r in streams:
                g = g_scr[r * TL:(r + 1) * TL]
                rg = jax.nn.sigmoid(g[:, :LRU_WIDTH])
                ig = jax.nn.sigmoid(g[:, LRU_WIDTH:])
                log_a = -LRU_C * rg * softplus
                a = jnp.exp(log_a)
                th = jnp.tanh(log_a)
                bx = jnp.sqrt(-2.0 * th / (1.0 - th)) * (ig * xc_scr[r])
                if p_run is None:
                    p_run, h_run = a, bx
                else:
                    h_run = a * h_run + bx
                    p_run = a * p_run
                pl_scr[r] = p_run
                hl_scr[r] = h_run
            state["run"] = (p_run, h_run)
        return run

    def row_scan():
        pa, hb = state["run"]
        sft = 1
        while sft < TL:
            keep = row >= sft
            pa_s = jnp.where(keep, pltpu.roll(pa, sft, axis=0), 1.0)
            hb_s = jnp.where(keep, pltpu.roll(hb, sft, axis=0), 0.0)
            hb = pa * hb_s + hb
            pa = pa * pa_s
            sft *= 2
        carry = carry_scr[0:1, :]
        e = pa * carry + hb
        state["e_prev"] = jnp.where(row == 0, carry, pltpu.roll(e, 1, axis=0))
        carry_scr[...] = jnp.broadcast_to(e[TL - 1:TL, :], carry_scr.shape)

    def finalize(after=0.0):
        e_prev = state["e_prev"] + after
        for r in range(NS):
            h = pl_scr[r] * e_prev + hl_scr[r]
            y_ref[r] = jax.nn.gelu(gl_ref[r], approximate=True) * h

    half = NS // 2

    def second_half(after=0.0):
        local_scan(range(half, NS))(after)
        row_scan()

    return [(conv, LRU_WIDTH), (gates, 2 * LRU_WIDTH), (local_scan(range(half)), LRU_WIDTH),
            (second_half, LRU_WIDTH), (finalize, LRU_WIDTH)]


TM4 = NS * TL
NG4 = TM4 // PG
FF_CHUNK = 1024


def _out_body(x_ref, at_ref, xg0_ref, xgn_ref, zero_ref,
              cw_ref, cb_ref, wg_ref, bg_ref, lam_ref,
              perm_ref, ga_ref, gl_ref, wo_hbm, g2_ref, wu_hbm, wd_hbm, gf_ref,
              o_ref,
              lr_scr, tail_scr, carry_scr, xc_scr, g_scr, pl_scr, hl_scr,
              wo_ref, wu_ref, wd_ref, w_sem):
    xl0_ref, gl0_ref = xg0_ref.at[0], xg0_ref.at[1]
    xln_ref, gln_ref = xgn_ref.at[0], xgn_ref.at[1]
    lru_params = (cw_ref, cb_ref, wg_ref, bg_ref, lam_ref)
    lru_state = (tail_scr, carry_scr, xc_scr, g_scr, pl_scr, hl_scr)

    @pl.when(pl.program_id(0) == 0)
    def _():
        copies = [pltpu.make_async_copy(src, dst, w_sem.at[n])
                  for n, (src, dst) in enumerate(((wo_hbm, wo_ref), (wu_hbm, wu_ref), (wd_hbm, wd_ref)))]
        for c in copies:
            c.start()
        tail_scr[...] = jnp.zeros_like(tail_scr)
        carry_scr[...] = jnp.zeros_like(carry_scr)
        for phase, _ in _lru_phases(xl0_ref, gl0_ref, *lru_params, lr_scr, *lru_state):
            phase()
        for c in copies:
            c.wait()

    groups = []
    for g in range(NG4):
        rows = slice(g * RG, (g + 1) * RG)
        attn = jnp.concatenate(
            [jnp.concatenate([at_ref[j, r, rows] for j in range(NSLAB)], axis=1) for r in range(NS)],
            axis=0)
        lru = jnp.concatenate([lr_scr[r, rows] for r in range(NS)], axis=0)
        mixed = jnp.concatenate([_rms(attn, ga_ref[...]), _rms(lru, gl_ref[...])], axis=1)
        groups.append(jnp.dot(perm_ref[...], mixed.astype(BF16),
                              preferred_element_type=F32).astype(BF16))
    mixed = jnp.concatenate(groups, axis=0)
    lru_next = _lru_phases(xln_ref, gln_ref, *lru_params, lr_scr, *lru_state)
    n_chunks = D_FF // FF_CHUNK
    assert len(lru_next) <= n_chunks + 1

    def zero_after(val, width):
        bits = pltpu.bitcast(val[0:1, :width], jnp.int32) & zero_ref[0:1, :width]
        return pltpu.bitcast(bits, F32)

    h = x_ref[...] + jnp.dot(mixed, wo_ref[...], preferred_element_type=F32)
    u = _rms(h, g2_ref[...]).astype(BF16)
    acc = h
    phase, width = lru_next.pop(0)
    phase(zero_after(h, width))
    for c in range(n_chunks):
        if lru_next:
            phase, width = lru_next.pop(0)
            phase(zero_after(acc, width))
        sl = slice(c * FF_CHUNK, (c + 1) * FF_CHUNK)
        f = jnp.dot(u, wu_ref[:, sl], preferred_element_type=F32)
        f = jnp.square(jnp.maximum(f, 0.0)).astype(BF16)
        acc = acc + jnp.dot(f, wd_ref[sl, :], preferred_element_type=F32)
    o_ref[...] = _rms(acc, gf_ref[...])


def _out_mlp(x2, attn, xg, lru_params, ga, gl, wo, g2, wu, wd, gf):
    def const(a):
        return pl.BlockSpec(a.shape, lambda i: (0,) * a.ndim, pipeline_mode=pl.Buffered(1))

    n_tiles = SEQ // TM4
    xspec = pl.BlockSpec((TM4, D_MODEL), lambda i: (i, 0))
    first = pl.BlockSpec((2, NS, TL, LRU_WIDTH), lambda i: (0, 0, 0, 0),
                         pipeline_mode=pl.Buffered(1))
    ahead = pl.BlockSpec((2, NS, TL, LRU_WIDTH),
                         lambda i: (0, 0, jnp.minimum(i + 1, n_tiles - 1), 0))
    perm = jnp.asarray(_group_perm(), BF16)
    zero = jnp.zeros((SUBLANES, D_MODEL), jnp.int32)
    consts = (zero,) + tuple(lru_params) + (perm, ga, gl, wo, g2, wu, wd, gf)
    return pl.pallas_call(
        _out_body,
        grid=(n_tiles,),
        in_specs=[
            xspec,
            pl.BlockSpec((NSLAB, NS, TL, LANES), lambda i: (0, 0, i, 0)),
            first, ahead,
        ] + [pl.BlockSpec(memory_space=pl.ANY) if any(a is w for w in (wo, wu, wd)) else const(a)
             for a in consts],
        out_specs=xspec,
        out_shape=jax.ShapeDtypeStruct((SEQ, D_MODEL), F32),
        scratch_shapes=[
            pltpu.VMEM((NS, TL, LRU_WIDTH), F32),
            pltpu.VMEM((CONV_WIDTH - 1, SUBLANES, LRU_WIDTH), F32),
            pltpu.VMEM((SUBLANES, LRU_WIDTH), F32),
            pltpu.VMEM((NS, TL, LRU_WIDTH), F32),
            pltpu.VMEM((NS * TL, 2 * LRU_WIDTH), F32),
            pltpu.VMEM((NS, TL, LRU_WIDTH), F32),
            pltpu.VMEM((NS, TL, LRU_WIDTH), F32),
            pltpu.VMEM(wo.shape, BF16),
            pltpu.VMEM(wu.shape, BF16),
            pltpu.VMEM(wd.shape, BF16),
            pltpu.SemaphoreType.DMA((3,)),
        ],
        compiler_params=pltpu.CompilerParams(
            dimension_semantics=("arbitrary",), vmem_limit_bytes=VMEM_LIMIT),
        name="out_mlp",
    )(x2, attn, xg, xg, *consts)


@functools.lru_cache(maxsize=None)
def _rope_tables():
    inv_freq = ROPE_THETA ** (-np.arange(0, ROT_DIM, 2, dtype=np.float64) / ROT_DIM)
    pad = HEAD_DIM - ROT_DIM
    reps = LANES // HEAD_DIM

    def lanes(rot_first, rot_second, rest, n):
        head = np.concatenate([rot_first, rot_second, np.full((n, pad), rest)], axis=1)
        return np.tile(head, (1, reps))

    tile_ang = (TM1 * np.arange(SEQ // TM1, dtype=np.float64))[:, None] * inv_freq[None, :]
    tc, ts = np.cos(tile_ang), np.sin(tile_ang)
    tiles = np.concatenate([lanes(tc, tc, 1.0, len(tc)), lanes(ts, ts, 0.0, len(ts))], axis=1)
    off_ang = np.arange(TM1, dtype=np.float64)[:, None] * inv_freq[None, :]
    oc, osn = np.cos(off_ang), np.sin(off_ang)
    offs = np.stack([lanes(oc, oc, 1.0, TM1), lanes(osn, osn, 0.0, TM1),
                     lanes(-oc, oc, 0.0, TM1), lanes(-osn, osn, 0.0, TM1)])
    offs = offs.reshape(4, NG1, RG, NS, LANES).transpose(0, 1, 3, 2, 4).reshape(4, TM1, LANES)
    return (tiles.astype(np.float32)[:, None, :], np.ascontiguousarray(offs.astype(np.float32)))


def _block_diag(w):
    n, b, _ = w.shape
    eye = jnp.eye(n, dtype=w.dtype)
    return (eye[:, None, :, None] * w[:, :, None, :]).reshape(n * b, n * b)


def _gate_tiles(w_r, w_i):
    per = MXU_TILE // (LRU_WIDTH // LRU_BLOCKS)
    tiles = [jnp.concatenate([_block_diag(w_r[t * per:(t + 1) * per]),
                              _block_diag(w_i[t * per:(t + 1) * per])], axis=1)
             for t in range(LRU_WIDTH // MXU_TILE)]
    return jnp.stack(tiles).astype(BF16)


def kernel(x, norm1_g, w_in, conv_w, conv_b, w_rgate, b_rgate, w_igate, b_igate, lru_lambda,
           attn_out_g, lru_out_g, w_out, norm2_g, w_mlp_up, w_mlp_down, final_g):
    assert x.shape == (1, SEQ, D_MODEL) and w_in.shape[0] == 1
    x2 = x.reshape(SEQ, D_MODEL)
    sm, cm, xg = _inproj(x2, norm1_g.reshape(1, D_MODEL), w_in[0])
    attn, wo, wu, wd = _attention(sm, cm, (w_out[0], w_mlp_up[0], w_mlp_down[0]))
    w_gate = _gate_tiles(w_rgate[0], w_igate[0])
    b_gate = jnp.concatenate([b_rgate[0].reshape(1, -1), b_igate[0].reshape(1, -1)], axis=1)
    lru_params = (conv_w[0], conv_b[0].reshape(1, -1), w_gate, b_gate, lru_lambda[0].reshape(1, -1))
    out = _out_mlp(x2, attn, xg, lru_params,
                   attn_out_g[0].reshape(1, -1), lru_out_g[0].reshape(1, -1),
                   wo, norm2_g[0].reshape(1, -1), wu, wd, final_g.reshape(1, -1))
    return out.reshape(1, SEQ, D_MODEL)
```

```python
import functools

import numpy as np
import jax
import jax.numpy as jnp
from jax import lax
from jax.experimental import pallas as pl
from jax.experimental.pallas import tpu as pltpu

F32 = jnp.float32
BF16 = jnp.bfloat16

D_MODEL = 1024
SEQ = 16384
ATTN_HEADS = 8
HEAD_DIM = 64
ATTN_WIDTH = ATTN_HEADS * HEAD_DIM
ROT_DIM = HEAD_DIM // 4
ROPE_THETA = 500000.0
LRU_WIDTH = D_MODEL - ATTN_WIDTH
LRU_BLOCKS = 8
CONV_WIDTH = 4
LRU_C = 8.0
IN_WIDTH = 3 * ATTN_WIDTH + 2 * LRU_WIDTH
D_FF = 4 * D_MODEL
EPS = 1e-6
WINDOW_STEPS = 128

NS = 16
RA = SEQ // NS
LANES = 128
SUBLANES = 8
MXU_TILE = 256
NSLAB = ATTN_WIDTH // LANES
NEG = -np.inf

VMEM_LIMIT = 56 * 1024 * 1024


def _rms(x, g):
    return x * lax.rsqrt(jnp.mean(x * x, axis=-1, keepdims=True) + EPS) * g


TM1 = 1024
PG = NS * NS
RG = PG // NS
NG1 = TM1 // PG
CH = SUBLANES
LOG2E = 1.4426950408889634


@functools.lru_cache(maxsize=None)
def _group_perm():
    n = np.arange(PG)
    p = np.zeros((PG, PG), np.float32)
    p[(n % NS) * RG + n // NS, n] = 1.0
    return p


W_PIECES = 4


def _inproj_body(x_ref, g_ref, w_ref, perm_ref, tile_ref, off_ref,
                 sm_ref, cm_ref, xg_ref, w_scr, w32_scr, w_sem):
    xl_ref, gl_ref = xg_ref.at[0], xg_ref.at[1]
    k_ref, v_ref, q_ref = (sm_ref.at[n] for n in range(3))
    k1_ref, v1_ref, q1_ref = (cm_ref.at[n] for n in range(3))

    @pl.when(pl.program_id(0) == 0)
    def _():
        rows = D_MODEL // W_PIECES
        copies = [pltpu.make_async_copy(w_ref.at[pl.ds(n * rows, rows)],
                                        w32_scr.at[pl.ds(n * rows, rows)], w_sem.at[n])
                  for n in range(W_PIECES)]
        for c in copies:
            c.start()
        for n, c in enumerate(copies):
            c.wait()
            w_scr[n * rows:(n + 1) * rows] = w32_scr[n * rows:(n + 1) * rows].astype(BF16)

    u = _rms(x_ref[...], g_ref[...]).astype(BF16)
    u = jnp.concatenate(
        [jnp.dot(perm_ref[...], u[g * PG:(g + 1) * PG], preferred_element_type=F32).astype(BF16)
         for g in range(NG1)], axis=0)
    z = jnp.dot(u, w_scr[...], preferred_element_type=F32)
    tile_cos, tile_sin = tile_ref[:, :LANES], tile_ref[:, LANES:]
    c = tile_cos * off_ref[0] - tile_sin * off_ref[1]
    s = tile_sin * off_ref[2] + tile_cos * off_ref[3]
    c = jnp.concatenate([c] * NSLAB, axis=1)
    s = jnp.concatenate([s] * NSLAB, axis=1)
    lane = lax.broadcasted_iota(jnp.int32, (1, ATTN_WIDTH), 1) % HEAD_DIM
    first_half = lane < ROT_DIM // 2

    def rope(t):
        up = pltpu.roll(t, ATTN_WIDTH - ROT_DIM // 2, axis=1)
        dn = pltpu.roll(t, ROT_DIM // 2, axis=1)
        return t * c + jnp.where(first_half, up, dn) * s

    q = rope(z[:, :ATTN_WIDTH]) * (HEAD_DIM ** -0.5 * LOG2E)
    k = rope(z[:, ATTN_WIDTH:2 * ATTN_WIDTH])
    v = z[:, 2 * ATTN_WIDTH:3 * ATTN_WIDTH]
    xl = z[:, 3 * ATTN_WIDTH:3 * ATTN_WIDTH + LRU_WIDTH]
    gl = z[:, 3 * ATTN_WIDTH + LRU_WIDTH:]
    for g in range(NG1):
        for r in range(NS):
            src = slice(g * PG + r * RG, g * PG + (r + 1) * RG)
            dst = slice(g * RG, (g + 1) * RG)
            for j in range(NSLAB):
                sl = slice(j * LANES, (j + 1) * LANES)
                q_ref[j, r, dst] = q[src, sl].astype(BF16)
                k_ref[j, r, dst] = k[src, sl].astype(BF16)
                v_ref[j, r, dst] = v[src, sl].astype(BF16)
            xl_ref[r, dst] = xl[src]
            gl_ref[r, dst] = gl[src]
    for g in range(NG1):
        for cl in range(RG // CH):
            for r in range(0, NS, 2):
                lo = g * PG + r * RG + cl * CH
                hi = lo + RG
                d0 = (g * (RG // CH) + cl) * NS * CH + r * CH
                dst = slice(d0, d0 + 2 * CH)
                for j in range(NSLAB):
                    sl = slice(j * LANES, (j + 1) * LANES)
                    for src, ref in ((q, q1_ref), (k, k1_ref), (v, v1_ref)):
                        ref[j, dst] = jnp.concatenate(
                            [src[lo:lo + CH, sl], src[hi:hi + CH, sl]], axis=0).astype(BF16)


def _inproj(x2, g, w):
    steps = SEQ // TM1
    rows = TM1 // NS
    slab_spec = pl.BlockSpec((3, NSLAB, NS, rows, LANES), lambda i: (0, 0, 0, i, 0))
    chunk_spec = pl.BlockSpec((3, NSLAB, TM1, LANES), lambda i: (0, 0, i, 0))
    row_spec = pl.BlockSpec((2, NS, rows, LRU_WIDTH), lambda i: (0, 0, i, 0))
    rope_tiles, rope_offs = _rope_tables()
    slab_shape = jax.ShapeDtypeStruct((3, NSLAB, NS, RA, LANES), BF16)
    chunk_shape = jax.ShapeDtypeStruct((3, NSLAB, SEQ, LANES), BF16)
    row_shape = jax.ShapeDtypeStruct((2, NS, RA, LRU_WIDTH), F32)
    perm = jnp.asarray(_group_perm(), BF16)
    return pl.pallas_call(
        _inproj_body,
        grid=(steps,),
        in_specs=[
            pl.BlockSpec((TM1, D_MODEL), lambda i: (i, 0)),
            pl.BlockSpec((1, D_MODEL), lambda i: (0, 0)),
            pl.BlockSpec(memory_space=pl.ANY),
            pl.BlockSpec((PG, PG), lambda i: (0, 0)),
            pl.BlockSpec((None, 1, 2 * LANES), lambda i: (i, 0, 0)),
            pl.BlockSpec(rope_offs.shape, lambda i: (0, 0, 0)),
        ],
        out_specs=[slab_spec, chunk_spec, row_spec],
        out_shape=[slab_shape, chunk_shape, row_shape],
        scratch_shapes=[pltpu.VMEM((D_MODEL, IN_WIDTH), BF16),
                        pltpu.VMEM((D_MODEL, IN_WIDTH), F32),
                        pltpu.SemaphoreType.DMA((W_PIECES,))],
        compiler_params=pltpu.CompilerParams(
            dimension_semantics=("arbitrary",), vmem_limit_bytes=VMEM_LIMIT),
        name="inproj",
    )(x2, g, w, perm, rope_tiles, rope_offs)


TA = WINDOW_STEPS
UQ, UK = WINDOW_STEPS, 2 * WINDOW_STEPS
Q4, K4 = UQ // 4, UK // 4
AHEAD = 2
RING = 2 * AHEAD


@functools.lru_cache(maxsize=None)
def _attn_masks():
    def variants(diff, from_prev, per_unit):
        ok = (diff >= 0) & (diff <= WINDOW_STEPS)
        normal = np.where(ok, 0.0, NEG).astype(np.float32)
        first = np.where(ok & ~from_prev, 0.0, NEG).astype(np.float32)
        if per_unit:
            return np.stack([np.stack([first, normal]), np.stack([normal, normal])])
        return np.stack([first, normal])

    iq = np.arange(UQ)[:, None]
    ck = np.arange(UK)[None, :]
    m16 = variants(iq - ck + TA, np.broadcast_to(ck < TA, (UQ, UK)), False)
    cq, i4 = np.divmod(np.arange(UQ), Q4)
    ckk, j4 = np.divmod(np.arange(UK), K4)
    d4 = 4 * (i4[:, None] - j4[None, :] + Q4) + (cq[:, None] - ckk[None, :])
    m4 = variants(d4, np.broadcast_to(j4[None, :] < K4 - Q4, d4.shape), True)
    rq, i1 = np.divmod(np.arange(UQ), CH)
    kc, kin = np.divmod(np.arange(UK), UQ)
    rk, j1 = np.divmod(kin, CH)
    d1 = (NS * i1 + rq)[:, None] - (NS * j1 + rk + UQ * (kc - 1))[None, :]
    m1 = variants(d1, np.broadcast_to(kc[None, :] == 0, d1.shape), True)
    return np.concatenate([m16[:, None], m4, m1], axis=1)


def _attn_body(sm_ref, smp_ref, cm_ref, cmp_ref,
               mk_ref, wo_ref, wu_ref, wd_ref,
               o_ref, wo_bf_ref, wu_bf_ref, wd_bf_ref,
               s_scr, acc_scr, max_scr, sum_scr):
    kc_ref, vc_ref, q_ref = (sm_ref.at[n] for n in range(3))
    kp_ref, vp_ref = smp_ref.at[0], smp_ref.at[1]
    k1c_ref, v1c_ref, q1_ref = (cm_ref.at[n] for n in range(3))
    k1p_ref, v1p_ref = cmp_ref.at[0], cmp_ref.at[1]

    wo_bf_ref[...] = wo_ref[...].astype(BF16)
    wu_bf_ref[...] = wu_ref[...].astype(BF16)
    wd_bf_ref[...] = wd_ref[...].astype(BF16)

    lane = lax.broadcasted_iota(jnp.int32, (1, LANES), 1)
    head0 = lane < HEAD_DIM
    zero = jnp.zeros((), BF16)

    def only_head(x, h):
        return jnp.where(head0, x, zero) if h == 0 else jnp.where(head0, zero, x)

    ones = jnp.ones((UK, LANES), BF16)

    def unit16(r):
        def store(o, mx, sm):
            acc_scr[0, r], max_scr[0, r], sum_scr[0, r] = o, mx, sm

        return (lambda h: only_head(q_ref[r], h),
                lambda: jnp.concatenate([kp_ref[r], kc_ref[r]], axis=0),
                lambda: jnp.concatenate([vp_ref[r], vc_ref[r]], axis=0),
                lambda: mk_ref[0], store)

    def unit4(r4, b):
        streams = [r4 + 4 * c for c in range(4)]
        q0, k0 = b * Q4, b * Q4 + Q4 - K4

        def window(cur, prev, st):
            if k0 < 0:
                return [prev[st, TA + k0:TA], cur[st, 0:k0 + K4]]
            return [cur[st, k0:k0 + K4]]

        def store(o, mx, sm):
            for c, st in enumerate(streams):
                rows = slice(c * Q4, (c + 1) * Q4)
                acc_scr[1, st, q0:q0 + Q4] = o[rows]
                max_scr[1, st, q0:q0 + Q4] = mx[rows]
                sum_scr[1, st, q0:q0 + Q4] = sm[rows]

        return (lambda h: only_head(
                    jnp.concatenate([q_ref[st, q0:q0 + Q4] for st in streams], axis=0), h),
                lambda: jnp.concatenate(
                    [x for st in streams for x in window(kc_ref, kp_ref, st)], axis=0),
                lambda: jnp.concatenate(
                    [x for st in streams for x in window(vc_ref, vp_ref, st)], axis=0),
                lambda: mk_ref[1 + min(b, 1)], store)

    def unit1(u):
        off, row = u * UQ, u * CH

        def store(o, mx, sm):
            for r in range(NS):
                rows = slice(r * CH, (r + 1) * CH)
                acc_scr[2, r, row:row + CH] = o[rows]
                max_scr[2, r, row:row + CH] = mx[rows]
                sum_scr[2, r, row:row + CH] = sm[rows]

        def chunks(cur, prev):
            if u == 0:
                return jnp.concatenate([prev[...], cur[0:UQ]], axis=0)
            return cur[off - UQ:off + UQ]

        return (lambda h: only_head(q1_ref[off:off + UQ], h),
                lambda: chunks(k1c_ref, k1p_ref),
                lambda: chunks(v1c_ref, v1p_ref),
                lambda: mk_ref[3 + min(u, 1)], store)

    all_units = ([unit16(n) for n in range(NS)]
                 + [unit4(n // (TA // Q4), n % (TA // Q4)) for n in range(NS)]
                 + [unit1(n) for n in range(NS)])

    def score_unit(n):
        q, k, _, _, _ = all_units[n]
        s_scr[n % RING] = lax.dot_general(
            jnp.concatenate([q(0), q(1)], axis=0), k(), (((1,), (1,)), ((), ())),
            preferred_element_type=F32)

    def value_unit(n):
        _, _, v, mask, store = all_units[n]
        ps, mx = [], []
        for h in range(2):
            s = s_scr[n % RING, h * UQ:(h + 1) * UQ] + mask()
            m = jnp.max(s, axis=-1, keepdims=True)
            ps.append(jnp.exp2(s - m).astype(BF16))
            mx.append(m)
        res = jnp.dot(jnp.concatenate(ps, axis=0), jnp.concatenate([v(), ones], axis=1),
                      preferred_element_type=F32)
        top, bot = res[:UQ], res[UQ:]
        store(jnp.where(head0, top[:, :LANES], bot[:, :LANES]),
              jnp.where(head0, mx[0], mx[1]),
              jnp.where(head0, top[:, LANES:], bot[:, LANES:]))

    for n in range(AHEAD):
        score_unit(n)
    for n in range(len(all_units)):
        if n + AHEAD < len(all_units):
            score_unit(n + AHEAD)
        value_unit(n)

    def comb(r, carry):
        m0, m1, m2 = max_scr[0, r], max_scr[1, r], max_scr[2, r]
        m = jnp.maximum(jnp.maximum(m0, m1), m2)
        w0, w1, w2 = jnp.exp2(m0 - m), jnp.exp2(m1 - m), jnp.exp2(m2 - m)
        num = w0 * acc_scr[0, r] + w1 * acc_scr[1, r] + w2 * acc_scr[2, r]
        den = w0 * sum_scr[0, r] + w1 * sum_scr[1, r] + w2 * sum_scr[2, r]
        o_ref[r] = num / den
        return carry

    lax.fori_loop(0, NS, comb, 0, unroll=4)


def _attention(sm, cm, later_weights):
    masks = _attn_masks()
    cur =pl.BlockSpec((3, None, NS, TA, LANES), lambda j, i: (0, j, 0, i, 0))
    prev = pl.BlockSpec((2, None, NS, TA, LANES),
                        lambda j, i: (0, j, 0, jnp.maximum(i - 1, 0), 0))
    cur1 = pl.BlockSpec((3, None, NS * TA, LANES), lambda j, i: (0, j, i, 0))
    prev1 = pl.BlockSpec((2, None, UQ, LANES),
                         lambda j, i: (0, j, jnp.maximum(i * (NS * TA // UQ) - 1, 0), 0))

    def tile_kind(m):
        return pl.BlockSpec((None,) + m.shape[1:],
                            lambda j, i: (jnp.minimum(i, 1),) + (0,) * (m.ndim - 1))

    n_tiles = RA // TA
    steps = NSLAB * n_tiles
    cast_specs = [pl.BlockSpec((a.shape[0] // steps, a.shape[1]), lambda j, i: (j * n_tiles + i, 0))
                  for a in later_weights]
    cast_shapes = [jax.ShapeDtypeStruct(a.shape, BF16) for a in later_weights]
    return pl.pallas_call(
        _attn_body,
        grid=(NSLAB, n_tiles),
        in_specs=[cur, prev, cur1, prev1, tile_kind(masks)] + cast_specs,
        out_specs=[pl.BlockSpec((None, NS, TA, LANES), lambda j, i: (j, 0, i, 0))] + cast_specs,
        out_shape=[jax.ShapeDtypeStruct((NSLAB, NS, RA, LANES), F32)] + cast_shapes,
        scratch_shapes=[
            pltpu.VMEM((RING, 2 * UQ, UK), F32),
            pltpu.VMEM((3, NS, TA, LANES), F32),
            pltpu.VMEM((3, NS, TA, LANES), F32),
            pltpu.VMEM((3, NS, TA, LANES), F32),
        ],
        compiler_params=pltpu.CompilerParams(
            dimension_semantics=("arbitrary", "arbitrary"), vmem_limit_bytes=VMEM_LIMIT),
        name="dilated_attn",
    )(sm, sm, cm, cm, masks, *later_weights)


TL = 32


def _lru_phases(xl_ref, gl_ref, cw_ref, cb_ref, wg_ref, bg_ref, lam_ref, y_ref,
                tail_scr, carry_scr, xc_scr, g_scr, pl_scr, hl_scr):
    row = lax.broadcasted_iota(jnp.int32, (TL, LRU_WIDTH), 0)
    state = {}

    def conv(after=0.0):
        cw = cw_ref[...]
        cb = cb_ref[...] + after

        def tap(r, k):
            st = r - k
            if st >= 0:
                return xl_ref[st]
            st += NS
            prev_last = tail_scr[st - (NS - CONV_WIDTH + 1), SUBLANES - 1:SUBLANES, :]
            return jnp.where(row == 0, prev_last, pltpu.roll(xl_ref[st], 1, axis=0))

        for r in range(NS):
            xc_scr[r] = cb + sum(cw[CONV_WIDTH - 1 - k:CONV_WIDTH - k] * tap(r, k)
                                 for k in range(CONV_WIDTH))
        for n in range(CONV_WIDTH - 1):
            tail_scr[n] = xl_ref[NS - CONV_WIDTH + 1 + n, TL - SUBLANES:TL, :]

    def gates(after=0.0):
        xc_all = xc_scr[...].reshape(NS * TL, LRU_WIDTH).astype(BF16)
        bias = bg_ref[...] + after
        for t in range(LRU_WIDTH // MXU_TILE):
            cols = slice(t * MXU_TILE, (t + 1) * MXU_TILE)
            res = jnp.dot(xc_all[:, cols], wg_ref[t], preferred_element_type=F32)
            for gate in range(2):
                dst = slice(gate * LRU_WIDTH + t * MXU_TILE, gate * LRU_WIDTH + (t + 1) * MXU_TILE)
                g_scr[:, dst] = res[:, gate * MXU_TILE:(gate + 1) * MXU_TILE] + bias[:, dst]

    def local_scan(streams):
        def run(after=0.0):
            neg_lam = after - lam_ref[...]
            softplus = jnp.maximum(neg_lam, 0.0) + jnp.log1p(jnp.exp(-jnp.abs(neg_lam)))
            p_run, h_run = state.get("run", (None, None))
            for r in streams:
                g = g_scr[r * TL:(r + 1) * TL]
                rg = jax.nn.sigmoid(g[:, :LRU_WIDTH])
                ig = jax.nn.sigmoid(g[:, LRU_WIDTH:])
                log_a = -LRU_C * rg * softplus
                a = jnp.exp(log_a)
                th = jnp.tanh(log_a)
                bx = jnp.sqrt(-2.0 * th / (1.0 - th)) * (ig * xc_scr[r])
                if p_run is None:
                    p_run, h_run = a, bx
                else:
                    h_run = a * h_run + bx
                    p_run = a * p_run
                pl_scr[r] = p_run
                hl_scr[r] = h_run
            state["run"] = (p_run, h_run)
        return run

    def row_scan():
        pa, hb = state["run"]
        sft = 1
        while sft < TL:
            keep = row >= sft
            pa_s = jnp.where(keep, pltpu.roll(pa, sft, axis=0), 1.0)
            hb_s = jnp.where(keep, pltpu.roll(hb, sft, axis=0), 0.0)
            hb = pa * hb_s + hb
            pa = pa * pa_s
            sft *= 2
        carry = carry_scr[0:1, :]
        e = pa * carry + hb
        state["e_prev"] = jnp.where(row == 0, carry, pltpu.roll(e, 1, axis=0))
        carry_scr[...] = jnp.broadcast_to(e[TL - 1:TL, :], carry_scr.shape)

    def finalize(after=0.0):
        e_prev = state["e_prev"] + after
        for r in range(NS):
            h = pl_scr[r] * e_prev + hl_scr[r]
            y_ref[r] = jax.nn.gelu(gl_ref[r], approximate=True) * h

    half = NS // 2

    def second_half(after=0.0):
        local_scan(range(half, NS))(after)
        row_scan()

    return [(conv, LRU_WIDTH), (gates, 2 * LRU_WIDTH), (local_scan(range(half)), LRU_WIDTH),
            (second_half, LRU_WIDTH), (finalize, LRU_WIDTH)]


TM4 = NS * TL
NG4 = TM4 // PG
FF_CHUNK = 1024


def _out_body(x_ref, at_ref, xg0_ref, xgn_ref, zero_ref,
              cw_ref, cb_ref, wg_ref, bg_ref, lam_ref,
              perm_ref, ga_ref, gl_ref, wo_hbm, g2_ref, wu_hbm, wd_hbm, gf_ref,
              o_ref,
              lr_scr, tail_scr, carry_scr, xc_scr, g_scr, pl_scr, hl_scr,
              wo_ref, wu_ref, wd_ref, w_sem):
    xl0_ref, gl0_ref = xg0_ref.at[0], xg0_ref.at[1]
    xln_ref, gln_ref = xgn_ref.at[0], xgn_ref.at[1]
    lru_params = (cw_ref, cb_ref, wg_ref, bg_ref, lam_ref)
    lru_state = (tail_scr, carry_scr, xc_scr, g_scr, pl_scr, hl_scr)

    @pl.when(pl.program_id(0) == 0)
    def _():
        copies = [pltpu.make_async_copy(src, dst, w_sem.at[n])
                  for n, (src, dst) in enumerate(((wo_hbm, wo_ref), (wu_hbm, wu_ref), (wd_hbm, wd_ref)))]
        for c in copies:
            c.start()
        tail_scr[...] = jnp.zeros_like(tail_scr)
        carry_scr[...] = jnp.zeros_like(carry_scr)
        for phase, _ in _lru_phases(xl0_ref, gl0_ref, *lru_params, lr_scr, *lru_state):
            phase()
        for c in copies:
            c.wait()

    groups = []
    for g in range(NG4):
        rows = slice(g * RG, (g + 1) * RG)
        attn = jnp.concatenate(
            [jnp.concatenate([at_ref[j, r, rows] for j in range(NSLAB)], axis=1) for r in range(NS)],
            axis=0)
        lru = jnp.concatenate([lr_scr[r, rows] for r in range(NS)], axis=0)
        mixed = jnp.concatenate([_rms(attn, ga_ref[...]), _rms(lru, gl_ref[...])], axis=1)
        groups.append(jnp.dot(perm_ref[...], mixed.astype(BF16),
                              preferred_element_type=F32).astype(BF16))
    mixed = jnp.concatenate(groups, axis=0)
    lru_next = _lru_phases(xln_ref, gln_ref, *lru_params, lr_scr, *lru_state)
    n_chunks = D_FF // FF_CHUNK
    assert len(lru_next) <= n_chunks + 1

    def zero_after(val, width):
        bits = pltpu.bitcast(val[0:1, :width], jnp.int32) & zero_ref[0:1, :width]
        return pltpu.bitcast(bits, F32)

    h = x_ref[...] + jnp.dot(mixed, wo_ref[...], preferred_element_type=F32)
    u = _rms(h, g2_ref[...]).astype(BF16)
    acc = h
    phase, width = lru_next.pop(0)
    phase(zero_after(h, width))
    for c in range(n_chunks):
        if lru_next:
            phase, width = lru_next.pop(0)
            phase(zero_after(acc, width))
        sl = slice(c * FF_CHUNK, (c + 1) * FF_CHUNK)
        f = jnp.dot(u, wu_ref[:, sl], preferred_element_type=F32)
        f = jnp.square(jnp.maximum(f, 0.0)).astype(BF16)
        acc = acc + jnp.dot(f, wd_ref[sl, :], preferred_element_type=F32)
    o_ref[...] = _rms(acc, gf_ref[...])


def _out_mlp(x2, attn, xg, lru_params, ga, gl, wo, g2, wu, wd, gf):
    def const(a):
        return pl.BlockSpec(a.shape, lambda i: (0,) * a.ndim, pipeline_mode=pl.Buffered(1))

    n_tiles = SEQ // TM4
    xspec = pl.BlockSpec((TM4, D_MODEL), lambda i: (i, 0))
    first = pl.BlockSpec((2, NS, TL, LRU_WIDTH), lambda i: (0, 0, 0, 0),
                         pipeline_mode=pl.Buffered(1))
    ahead = pl.BlockSpec((2, NS, TL, LRU_WIDTH),
                         lambda i: (0, 0, jnp.minimum(i + 1, n_tiles - 1), 0))
    perm = jnp.asarray(_group_perm(), BF16)
    zero = jnp.zeros((SUBLANES, D_MODEL), jnp.int32)
    consts = (zero,) + tuple(lru_params) + (perm, ga, gl, wo, g2, wu, wd, gf)
    return pl.pallas_call(
        _out_body,
        grid=(n_tiles,),
        in_specs=[
            xspec,
            pl.BlockSpec((NSLAB, NS, TL, LANES), lambda i: (0, 0, i, 0)),
            first, ahead,
        ] + [pl.BlockSpec(memory_space=pl.ANY) if any(a is w for w in (wo, wu, wd)) else const(a)
             for a in consts],
        out_specs=xspec,
        out_shape=jax.ShapeDtypeStruct((SEQ, D_MODEL), F32),
        scratch_shapes=[
            pltpu.VMEM((NS, TL, LRU_WIDTH), F32),
            pltpu.VMEM((CONV_WIDTH - 1, SUBLANES, LRU_WIDTH), F32),
            pltpu.VMEM((SUBLANES, LRU_WIDTH), F32),
            pltpu.VMEM((NS, TL, LRU_WIDTH), F32),
            pltpu.VMEM((NS * TL, 2 * LRU_WIDTH), F32),
            pltpu.VMEM((NS, TL, LRU_WIDTH), F32),
            pltpu.VMEM((NS, TL, LRU_WIDTH), F32),
            pltpu.VMEM(wo.shape, BF16),
            pltpu.VMEM(wu.shape, BF16),
            pltpu.VMEM(wd.shape, BF16),
            pltpu.SemaphoreType.DMA((3,)),
        ],
        compiler_params=pltpu.CompilerParams(
            dimension_semantics=("arbitrary",), vmem_limit_bytes=VMEM_LIMIT),
        name="out_mlp",
    )(x2, attn, xg, xg, *consts)


@functools.lru_cache(maxsize=None)
def _rope_tables():
    inv_freq = ROPE_THETA ** (-np.arange(0, ROT_DIM, 2, dtype=np.float64) / ROT_DIM)
    pad = HEAD_DIM - ROT_DIM
    reps = LANES // HEAD_DIM

    def lanes(rot_first, rot_second, rest, n):
        head = np.concatenate([rot_first, rot_second, np.full((n, pad), rest)], axis=1)
        return np.tile(head, (1, reps))

    tile_ang = (TM1 * np.arange(SEQ // TM1, dtype=np.float64))[:, None] * inv_freq[None, :]
    tc, ts = np.cos(tile_ang), np.sin(tile_ang)
    tiles = np.concatenate([lanes(tc, tc, 1.0, len(tc)), lanes(ts, ts, 0.0, len(ts))], axis=1)
    off_ang = np.arange(TM1, dtype=np.float64)[:, None] * inv_freq[None, :]
    oc, osn = np.cos(off_ang), np.sin(off_ang)
    offs = np.stack([lanes(oc, oc, 1.0, TM1), lanes(osn, osn, 0.0, TM1),
                     lanes(-oc, oc, 0.0, TM1), lanes(-osn, osn, 0.0, TM1)])
    offs = offs.reshape(4, NG1, RG, NS, LANES).transpose(0, 1, 3, 2, 4).reshape(4, TM1, LANES)
    return (tiles.astype(np.float32)[:, None, :], np.ascontiguousarray(offs.astype(np.float32)))


def _block_diag(w):
    n, b, _ = w.shape
    eye = jnp.eye(n, dtype=w.dtype)
    return (eye[:, None, :, None] * w[:, :, None, :]).reshape(n * b, n * b)


def _gate_tiles(w_r, w_i):
    per = MXU_TILE // (LRU_WIDTH // LRU_BLOCKS)
    tiles = [jnp.concatenate([_block_diag(w_r[t * per:(t + 1) * per]),
                              _block_diag(w_i[t * per:(t + 1) * per])], axis=1)
             for t in range(LRU_WIDTH // MXU_TILE)]
    return jnp.stack(tiles).astype(BF16)


def kernel(x, norm1_g, w_in, conv_w, conv_b, w_rgate, b_rgate, w_igate, b_igate, lru_lambda,
           attn_out_g, lru_out_g, w_out, norm2_g, w_mlp_up, w_mlp_down, final_g):
    assert x.shape == (1, SEQ, D_MODEL) and w_in.shape[0] == 1
    x2 = x.reshape(SEQ, D_MODEL)
    sm, cm, xg = _inproj(x2, norm1_g.reshape(1, D_MODEL), w_in[0])
    attn, wo, wu, wd = _attention(sm, cm, (w_out[0], w_mlp_up[0], w_mlp_down[0]))
    w_gate = _gate_tiles(w_rgate[0], w_igate[0])
    b_gate = jnp.concatenate([b_rgate[0].reshape(1, -1), b_igate[0].reshape(1, -1)], axis=1)
    lru_params = (conv_w[0], conv_b[0].reshape(1, -1), w_gate, b_gate, lru_lambda[0].reshape(1, -1))
    out = _out_mlp(x2, attn, xg, lru_params,
                   attn_out_g[0].reshape(1, -1), lru_out_g[0].reshape(1, -1),
                   wo, norm2_g[0].reshape(1, -1), wu, wd, final_g.reshape(1, -1))
    return out.reshape(1, SEQ, D_MODEL)
```
